```python
import math
import jax, jax.numpy as jnp
from jax import lax
import numpy as np

D_MODEL = 1024
BATCH = 8
SEQ = 2048
DEPTH = 1
DEC_BATCH = 128
DEC_SEQ = 4
PAST_LEN = 16384
PAGE_SIZE = 128

D_MIX = 2 * D_MODEL
SSD_WIDTH = D_MIX // 2
SSD_HEAD_DIM = 64
SSD_HEADS = SSD_WIDTH // SSD_HEAD_DIM
SSD_GROUPS = 2
D_STATE = 128
SSD_CONV_W = 4
SSD_CONV_DIM = SSD_WIDTH + 2 * SSD_GROUPS * D_STATE
SSD_CHUNK = 128
SC_WIDTH = D_MIX - SSD_WIDTH
SC_GROUPS = 16
SC_CONV_W = 3
IN_DIM = SSD_WIDTH + SSD_CONV_DIM + SSD_HEADS + 3 * SC_WIDTH
N_EXPERTS = 256
TOP_K = 8
N_EXPERT_GROUPS = 8
TOPK_GROUPS = 4
D_EXPERT = 256
D_SHARED = 256
ROUTED_SCALE = 2.5
MOE_BLOCK = 128
EPS = 1e-6
DT_MIN = 0.001
DT_MAX = 0.1

kernel_name = 'hybrid_ssd_shortconv_moe_step'


def rmsnorm(x, g, groups=1):
    shp = x.shape
    x32 = x.astype(jnp.float32).reshape(shp[:-1] + (groups, shp[-1] // groups))
    x32 = x32 * lax.rsqrt(jnp.mean(x32 * x32, axis=-1, keepdims=True) + EPS)
    return (x32.reshape(shp) * g.astype(jnp.float32)).astype(x.dtype)


def causal_dwconv(seq_full, w, length):
    out = seq_full[:, 0:length] * w[0]
    for k in range(1, w.shape[0]):
        out = out + seq_full[:, k:k + length] * w[k]
    return out


def segsum(a):
    q = a.shape[-1]
    rep = jnp.broadcast_to(a[..., :, None], a.shape + (q,))
    strict = jnp.tril(jnp.ones((q, q), dtype=bool), -1)
    cs = jnp.cumsum(jnp.where(strict, rep, 0.0), axis=-2)
    return jnp.where(jnp.tril(jnp.ones((q, q), dtype=bool)), cs, -jnp.inf)


def ssd_scan(x, dt, a_head, bm, cm, h0):
    b, l, nh, p = x.shape
    g, n = bm.shape[2], bm.shape[3]
    r = nh // g
    q = min(SSD_CHUNK, l)
    nc = -(-l // q)
    lp = nc * q
    f32 = jnp.float32
    x32, b32, c32 = x.astype(f32), bm.astype(f32), cm.astype(f32)
    if lp != l:
        padw = ((0, 0), (0, lp - l), (0, 0), (0, 0))
        x32 = jnp.pad(x32, padw)
        b32 = jnp.pad(b32, padw)
        c32 = jnp.pad(c32, padw)
        dt = jnp.pad(dt, ((0, 0), (0, lp - l), (0, 0)))
    xdt = (x32 * dt[..., None]).reshape(b, nc, q, g, r, p)
    a = (dt * a_head).reshape(b, nc, q, g, r).transpose(0, 3, 4, 1, 2)
    bc = b32.reshape(b, nc, q, g, n)
    cc = c32.reshape(b, nc, q, g, n)
    a_cum = jnp.cumsum(a, axis=-1)
    decay_in = jnp.exp(segsum(a))
    cb = jnp.einsum('bclgn,bcsgn->bgcls', cc, bc)
    y_diag = jnp.einsum('bgcls,bgrcls,bcsgrp->bclgrp', cb, decay_in, xdt)
    decay_to_end = jnp.exp(a_cum[..., -1:] - a_cum)
    chunk_states = jnp.einsum('bclgn,bgrcl,bclgrp->cbgrpn', bc, decay_to_end, xdt)
    chunk_decay = jnp.exp(a_cum[..., -1]).transpose(3, 0, 1, 2)

    def step(h, inp):
        s, dec = inp
        return h * dec[..., None, None] + s, h

    h_last, h_prev = lax.scan(step, h0.astype(f32).reshape(b, g, r, p, n),
                              (chunk_states, chunk_decay))
    y_off = jnp.einsum('bclgn,cbgrpn,bgrcl->bclgrp', cc, h_prev, jnp.exp(a_cum))
    y = (y_diag + y_off).reshape(b, lp, nh, p)[:, :l]
    return y.astype(x.dtype), h_last.reshape(b, nh, p, n).astype(h0.dtype)


def hybrid_mixer(u, ssm0, ssd_conv0, sc_conv0, w_in, ssd_conv_w, ssd_conv_b, dt_bias,
                 a_log, d_skip, g_ssd_norm, sc_conv_w, g_sc_norm, w_out):
    b, l, _ = u.shape
    proj = u @ w_in
    cuts = np.cumsum([SSD_WIDTH, SSD_CONV_DIM, SSD_HEADS, SC_WIDTH, SC_WIDTH]).tolist()
    z, xbc, dt, sc_b, sc_c, sc_h = jnp.split(proj, cuts, axis=-1)
    xbc_full = jnp.concatenate([ssd_conv0.astype(u.dtype), xbc], axis=1)
    ssd_conv1 = xbc_full[:, xbc_full.shape[1] - (SSD_CONV_W - 1):]
    xbc = jax.nn.silu(causal_dwconv(xbc_full, ssd_conv_w, l) + ssd_conv_b)
    xs, bm, cm = jnp.split(xbc, [SSD_WIDTH, SSD_WIDTH + SSD_GROUPS * D_STATE], axis=-1)
    xs = xs.reshape(b, l, SSD_HEADS, SSD_HEAD_DIM)
    bm = bm.reshape(b, l, SSD_GROUPS, D_STATE)
    cm = cm.reshape(b, l, SSD_GROUPS, D_STATE)
    dt = jax.nn.softplus(dt.astype(jnp.float32) + dt_bias.astype(jnp.float32))
    a_head = -jnp.exp(a_log.astype(jnp.float32))
    y, ssm1 = ssd_scan(xs, dt, a_head, bm, cm, ssm0)
    y = (y + xs * d_skip[:, None]).reshape(b, l, SSD_WIDTH) * jax.nn.silu(z)
    y_ssd = rmsnorm(y, g_ssd_norm, SSD_GROUPS)
    v_full = jnp.concatenate([sc_conv0.astype(u.dtype), sc_c * sc_h], axis=1)
    sc_conv1 = v_full[:, v_full.shape[1] - (SC_CONV_W - 1):]
    y_sc = rmsnorm(sc_b * causal_dwconv(v_full, sc_conv_w, l), g_sc_norm, SC_GROUPS)
    out = jnp.concatenate([y_ssd, y_sc], axis=-1) @ w_out
    return out, ssm1, ssd_conv1, sc_conv1


def route(u, w_router, router_bias):
    t = u.shape[0]
    scores = jax.nn.sigmoid(u.astype(jnp.float32) @ w_router.astype(jnp.float32))
    biased = scores + router_bias.astype(jnp.float32)
    per_group = N_EXPERTS // N_EXPERT_GROUPS
    group_score = lax.top_k(biased.reshape(t, N_EXPERT_GROUPS, per_group), 2)[0].sum(-1)
    _, gidx = lax.top_k(group_score, TOPK_GROUPS)
    gmask = (gidx[..., None] == jnp.arange(N_EXPERT_GROUPS)).any(axis=-2)
    emask = jnp.repeat(gmask, per_group, axis=-1)
    _, idx = lax.top_k(jnp.where(emask, biased, -jnp.inf), TOP_K)
    w = jnp.take_along_axis(scores, idx, axis=-1)
    w = w / jnp.sum(w, axis=-1, keepdims=True) * ROUTED_SCALE
    return idx, w


def routed_experts(u, idx, wts, w_gate, w_up, w_down):
    t, d = u.shape
    n_assign = t * TOP_K
    e_flat = idx.reshape(-1)
    t_flat = jnp.arange(n_assign, dtype=jnp.int32) // TOP_K
    w_flat = wts.reshape(-1).astype(u.dtype)
    order = jnp.argsort(e_flat)
    e_s, t_s, w_s = e_flat[order], t_flat[order], w_flat[order]
    counts = jnp.bincount(e_flat, length=N_EXPERTS)
    padded = (counts + MOE_BLOCK - 1) // MOE_BLOCK * MOE_BLOCK
    pad_end = jnp.cumsum(padded)
    pad_start = pad_end - padded
    cnt_start = jnp.cumsum(counts) - counts
    pos = pad_start[e_s] + jnp.arange(n_assign) - cnt_start[e_s]
    n_blocks = -(-n_assign // MOE_BLOCK) + N_EXPERTS
    n_rows = n_blocks * MOE_BLOCK
    row_tok = jnp.full((n_rows,), t, jnp.int32).at[pos].set(t_s)
    row_w = jnp.zeros((n_rows,), u.dtype).at[pos].set(w_s)
    block_e = jnp.minimum(jnp.searchsorted(pad_end, jnp.arange(n_blocks) * MOE_BLOCK,
                                           side='right'), N_EXPERTS - 1)
    u_ext = jnp.concatenate([u, jnp.zeros((1, d), u.dtype)], axis=0)

    def block_ffn(args):
        tok, wrow, e = args
        xb = u_ext[tok]
        hb = jax.nn.silu(xb @ w_gate[e]) * (xb @ w_up[e])
        return (hb @ w_down[e]) * wrow[:, None]

    y_rows = lax.map(block_ffn, (row_tok.reshape(n_blocks, MOE_BLOCK),
                                 row_w.reshape(n_blocks, MOE_BLOCK), block_e))
    return jax.ops.segment_sum(y_rows.reshape(n_rows, d), row_tok, num_segments=t + 1)[:t]


def moe_ffn(u, w_router, router_bias, w_exp_gate, w_exp_up, w_exp_down,
            w_sh_gate, w_sh_up, w_sh_down):
    idx, wts = route(u, w_router, router_bias)
    routed = routed_experts(u, idx, wts, w_exp_gate, w_exp_up, w_exp_down)
    shared = (jax.nn.silu(u @ w_sh_gate) * (u @ w_sh_up)) @ w_sh_down
    return routed + shared


def trunk_layer(x, c, ssm0, ssd_conv0, sc_conv0, w_ada, b_ada, g_pre_mix, g_post_mix,
                g_pre_ffn, g_post_ffn, w_in, ssd_conv_w, ssd_conv_b, dt_bias, a_log, d_skip,
                g_ssd_norm, sc_conv_w, g_sc_norm, w_out, w_router, router_bias,
                w_exp_gate, w_exp_up, w_exp_down, w_sh_gate, w_sh_up, w_sh_down):
    b, l, d = x.shape
    mod = (jax.nn.silu(c) @ w_ada + b_ada).reshape(b, 6, d)[:, None]
    shift1, scale1, gate1 = mod[:, :, 0], mod[:, :, 1], mod[:, :, 2]
    shift2, scale2, gate2 = mod[:, :, 3], mod[:, :, 4], mod[:, :, 5]
    u = rmsnorm(x, g_pre_mix) * (1 + scale1) + shift1
    m, ssm1, ssd_conv1, sc_conv1 = hybrid_mixer(u, ssm0, ssd_conv0, sc_conv0, w_in, ssd_conv_w,
                                                ssd_conv_b, dt_bias, a_log, d_skip, g_ssd_norm,
                                                sc_conv_w, g_sc_norm, w_out)
    x = x + gate1 * rmsnorm(m, g_post_mix)
    u = rmsnorm(x, g_pre_ffn) * (1 + scale2) + shift2
    f = moe_ffn(u.reshape(b * l, d), w_router, router_bias, w_exp_gate, w_exp_up, w_exp_down,
                w_sh_gate, w_sh_up, w_sh_down).reshape(b, l, d)
    x = x + gate2 * rmsnorm(f, g_post_ffn)
    return x, ssm1, ssd_conv1, sc_conv1


def setup_inputs(seed: int = 0) -> dict:
    key = jax.random.key(seed)
    ks = jax.random.split(key, 40)
    f32 = jnp.float32

    def nrm(k, shape, scale):
        return jax.random.normal(k, shape, f32) * scale

    L = DEPTH
    dt0 = jnp.exp(jax.random.uniform(ks[20], (L, SSD_HEADS), f32)
                  * (math.log(DT_MAX) - math.log(DT_MIN)) + math.log(DT_MIN))
    return {
        'x_prompt': nrm(ks[0], (BATCH, SEQ, D_MODEL), 1.0),
        'x_sample': nrm(ks[1], (DEC_BATCH, DEC_SEQ, D_MODEL), 1.0),
        'c_prompt': nrm(ks[2], (BATCH, D_MODEL), 1.0),
        'c_sample': nrm(ks[3], (DEC_BATCH, D_MODEL), 1.0),
        'state_ssm': nrm(ks[4], (L, DEC_BATCH, SSD_HEADS, SSD_HEAD_DIM, D_STATE), 0.5),
        'state_ssd_conv': nrm(ks[5], (L, DEC_BATCH, SSD_CONV_W - 1, SSD_CONV_DIM), 1.0),
        'state_short_conv': nrm(ks[6], (L, DEC_BATCH, SC_CONV_W - 1, SC_WIDTH), 1.0),
        'w_ada': nrm(ks[7], (L, D_MODEL, 6 * D_MODEL), 0.5 * D_MODEL ** -0.5),
        'b_ada': nrm(ks[8], (L, 6 * D_MODEL), 0.02),
        'g_pre_mix': 1.0 + nrm(ks[9], (L, D_MODEL), 0.02),
        'g_post_mix': 1.0 + nrm(ks[10], (L, D_MODEL), 0.02),
        'g_pre_ffn': 1.0 + nrm(ks[11], (L, D_MODEL), 0.02),
        'g_post_ffn': 1.0 + nrm(ks[12], (L, D_MODEL), 0.02),
        'w_in': nrm(ks[13], (L, D_MODEL, IN_DIM), D_MODEL ** -0.5),
        'ssd_conv_w': nrm(ks[14], (L, SSD_CONV_W, SSD_CONV_DIM), SSD_CONV_W ** -0.5),
        'ssd_conv_b': nrm(ks[15], (L, SSD_CONV_DIM), 0.02),
        'dt_bias': dt0 + jnp.log(-jnp.expm1(-dt0)),
        'a_log': jnp.log(jax.random.uniform(ks[21], (L, SSD_HEADS), f32, minval=1.0, maxval=16.0)),
        'd_skip': 1.0 + nrm(ks[22], (L, SSD_HEADS), 0.02),
        'g_ssd_norm': 1.0 + nrm(ks[23], (L, SSD_WIDTH), 0.02),
        'sc_conv_w': nrm(ks[24], (L, SC_CONV_W, SC_WIDTH), SC_CONV_W ** -0.5),
        'g_sc_norm': 1.0 + nrm(ks[25], (L, SC_WIDTH), 0.02),
        'w_out': nrm(ks[26], (L, D_MIX, D_MODEL), D_MIX ** -0.5),
        'w_router': nrm(ks[27], (L, D_MODEL, N_EXPERTS), D_MODEL ** -0.5),
        'router_bias': nrm(ks[28], (L, N_EXPERTS), 0.01),
        'w_exp_gate': nrm(ks[29], (L, N_EXPERTS, D_MODEL, D_EXPERT), D_MODEL ** -0.5),
        'w_exp_up': nrm(ks[30], (L, N_EXPERTS, D_MODEL, D_EXPERT), D_MODEL ** -0.5),
        'w_exp_down': nrm(ks[31], (L, N_EXPERTS, D_EXPERT, D_MODEL), D_EXPERT ** -0.5),
        'w_sh_gate': nrm(ks[32], (L, D_MODEL, D_SHARED), D_MODEL ** -0.5),
        'w_sh_up': nrm(ks[33], (L, D_MODEL, D_SHARED), D_MODEL ** -0.5),
        'w_sh_down': nrm(ks[34], (L, D_SHARED, D_MODEL), D_SHARED ** -0.5),
    }


def reference(x_prompt, x_sample, c_prompt, c_sample, state_ssm, state_ssd_conv,
              state_short_conv, w_ada, b_ada, g_pre_mix, g_post_mix, g_pre_ffn, g_post_ffn,
              w_in, ssd_conv_w, ssd_conv_b, dt_bias, a_log, d_skip, g_ssd_norm, sc_conv_w,
              g_sc_norm, w_out, w_router, router_bias, w_exp_gate, w_exp_up, w_exp_down,
              w_sh_gate, w_sh_up, w_sh_down):
    bp = x_prompt.shape[0]
    dtype = x_prompt.dtype
    ssm_zero = jnp.zeros((bp, SSD_HEADS, SSD_HEAD_DIM, D_STATE), dtype)
    ssd_conv_zero = jnp.zeros((bp, SSD_CONV_W - 1, SSD_CONV_DIM), dtype)
    sc_conv_zero = jnp.zeros((bp, SC_CONV_W - 1, SC_WIDTH), dtype)
    yp, ys = x_prompt, x_sample
    ssm_p, ssdc_p, scc_p, ssm_s, ssdc_s, scc_s = [], [], [], [], [], []
    for layer in range(DEPTH):
        lw = (w_ada[layer], b_ada[layer], g_pre_mix[layer], g_post_mix[layer], g_pre_ffn[layer],
              g_post_ffn[layer], w_in[layer], ssd_conv_w[layer], ssd_conv_b[layer],
              dt_bias[layer], a_log[layer], d_skip[layer], g_ssd_norm[layer], sc_conv_w[layer],
              g_sc_norm[layer], w_out[layer], w_router[layer], router_bias[layer],
              w_exp_gate[layer], w_exp_up[layer], w_exp_down[layer], w_sh_gate[layer],
              w_sh_up[layer], w_sh_down[layer])
        yp, a1, a2, a3 = trunk_layer(yp, c_prompt, ssm_zero, ssd_conv_zero, sc_conv_zero, *lw)
        ys, b1, b2, b3 = trunk_layer(ys, c_sample, state_ssm[layer], state_ssd_conv[layer],
                                     state_short_conv[layer], *lw)
        ssm_p.append(a1)
        ssdc_p.append(a2)
        scc_p.append(a3)
        ssm_s.append(b1)
        ssdc_s.append(b2)
        scc_s.append(b3)
    return (yp, ys, jnp.stack(ssm_p), jnp.stack(ssdc_p), jnp.stack(scc_p),
            jnp.stack(ssm_s), jnp.stack(ssdc_s), jnp.stack(scc_s))
```

```python
import functools

import jax
import jax.numpy as jnp
import numpy as np
from jax import lax
from jax.experimental import pallas as pl
from jax.experimental.pallas import tpu as pltpu

F32 = jnp.float32
BF16 = jnp.bfloat16
I32 = jnp.int32
HI = lax.Precision.HIGHEST

SSD_HEAD_DIM = 64
SSD_GROUPS = 2
D_STATE = 128
SSD_CONV_W = 4
SC_GROUPS = 16
SC_CONV_W = 3
TOP_K = 8
N_EXPERT_GROUPS = 8
TOPK_GROUPS = 4
ROUTED_SCALE = 2.5
EPS = 1e-6

LANES = 128
SUBLANES = 8
CHUNK = 128
TOK_TILE = 512
ROW_TILE = 256
GATHER_TILE = 256
VMEM_LIMIT = 56 * 1024 * 1024

NT = (((1,), (1,)), ((), ()))
TN = (((0,), (0,)), ((), ()))


def _sigmoid(x):
    return 1.0 / (1.0 + jnp.exp(-x))


def _silu(x):
    return x * _sigmoid(x)


def _softplus(x):
    return jnp.maximum(x, 0.0) + jnp.log1p(jnp.exp(-jnp.abs(x)))


def _rms(x, eps=EPS):
    return x * lax.rsqrt(jnp.mean(x * x, axis=-1, keepdims=True) + eps)


def _params(sem=None):
    return pltpu.CompilerParams(dimension_semantics=sem, vmem_limit_bytes=VMEM_LIMIT)


def _const_spec(shape, single=False):
    nd = len(shape)
    mode = dict(pipeline_mode=pl.Buffered(1)) if single else {}
    return pl.BlockSpec(shape, lambda *_: (0,) * nd, **mode)


def _ada_kernel(c_ref, w_ref, b_ref, o_ref):
    c = c_ref[...]
    s = _silu(c).astype(BF16)
    o_ref[...] = jnp.dot(s, w_ref[...].astype(BF16), preferred_element_type=F32) + b_ref[...]


def _ada(c, w_ada, b_ada):
    m, d = c.shape
    n = w_ada.shape[1]
    tn = 512
    return pl.pallas_call(
        _ada_kernel,
        out_shape=jax.ShapeDtypeStruct((m, n), F32),
        grid=(n // tn,),
        in_specs=[_const_spec((m, d)),
                  pl.BlockSpec((d, tn), lambda j: (0, j)),
                  pl.BlockSpec((1, tn), lambda j: (0, j))],
        out_specs=pl.BlockSpec((m, tn), lambda j: (0, j)),
        compiler_params=_params(("arbitrary",)),
        name="ada",
    )(c, w_ada, b_ada.reshape(1, n))


def _mod_rows(mod_ref, per_row, rows_per_batch, tile, d):
    n = mod_ref.shape[1] // d
    if per_row:
        return [mod_ref[:, k * d:(k + 1) * d] for k in range(n)]
    b = (pl.program_id(0) * tile) // rows_per_batch
    return [mod_ref[pl.ds(b, 1), k * d:(k + 1) * d] for k in range(n)]


def _mod_spec(mod, per_row, tile):
    if per_row:
        return pl.BlockSpec((tile, mod.shape[1]), lambda i: (i, 0))
    return _const_spec(mod.shape)


def _in_kernel(x_ref, mod_ref, g_ref, w_ref, *out_refs, per_row, rows_per_batch, widths):
    tile, d = x_ref.shape
    shift, scale = _mod_rows(mod_ref, per_row, rows_per_batch, tile, d)
    u = (_rms(x_ref[...]) * g_ref[...]) * (1.0 + scale) + shift
    u = u.astype(BF16)
    col = 0
    for ref, width in zip(out_refs, widths):
        for a in range(0, width, 512):
            bw = min(512, width - a)
            r = jnp.dot(u, w_ref[:, col + a:col + a + bw], preferred_element_type=F32)
            ref[:, a:a + bw] = r.astype(ref.dtype)
        col += width


def _in_proj(x, mod, g, w_bf16, widths, out_dtype, per_row, rows_per_batch, tile):
    t, d = x.shape
    n = w_bf16.shape[1]
    mod_spec = _mod_spec(mod, per_row, tile)
    dts = [out_dtype] * (len(widths) - 1) + [F32]
    return pl.pallas_call(
        functools.partial(_in_kernel, per_row=per_row, rows_per_batch=rows_per_batch, widths=widths),
        out_shape=[jax.ShapeDtypeStruct((t, wd), dt) for wd, dt in zip(widths, dts)],
        grid=(t // tile,),
        in_specs=[pl.BlockSpec((tile, d), lambda i: (i, 0)), mod_spec,
                  _const_spec((1, d)), _const_spec((d, n), single=True)],
        out_specs=[pl.BlockSpec((tile, wd), lambda i: (i, 0)) for wd in widths],
        compiler_params=_params(("arbitrary",)),
        name="in_proj",
    )(x, mod, g.reshape(1, d), w_bf16)


def _ssd_kernel(*refs, rows_in, q_valid, has_init, heads):
    it = iter(refs)
    z_ref, xbc_ref, scb_ref, scc_ref, sch_ref, dt_ref = (next(it) for _ in range(6))
    if has_init:
        ssm0_ref, cst0_ref, scst0_ref = (next(it) for _ in range(3))
    (cw_ref, cb_ref, dtb_ref, alog_ref, dsk_ref, gssd_ref, scw_ref, gsc_ref,
     tril_ref, e_ref, et_ref) = (next(it) for _ in range(11))
    ymix_ref, ssm_ref, cst_ref, scst_ref = (next(it) for _ in range(4))
    h_scr, ext_scr, extv_scr = (next(it) for _ in range(3))
    stage = [next(it) for _ in range(5)] if rows_in < CHUNK else None

    c = pl.program_id(1)
    nc = pl.num_programs(1)
    ssd_w = dsk_ref.shape[1]
    gw = ssd_w // SSD_GROUPS
    hpg = heads // SSD_GROUPS
    n_state = D_STATE
    head0 = SUBLANES - (SSD_CONV_W - 1)
    headv = SUBLANES - (SC_CONV_W - 1)

    first = jnp.logical_and(pl.program_id(0) == 0, c == 0)
    if stage is not None:
        @pl.when(first)
        def _():
            for s in stage:
                s[...] = jnp.zeros(s.shape, s.dtype)
            ext_scr[...] = jnp.zeros(ext_scr.shape, F32)
            extv_scr[...] = jnp.zeros(extv_scr.shape, F32)

    @pl.when(c == 0)
    def _():
        if has_init:
            h_scr[...] = ssm0_ref[0]
            ext_scr[head0:SUBLANES, :] = cst0_ref[0]
            extv_scr[headv:SUBLANES, :] = scst0_ref[0]
        else:
            h_scr[...] = jnp.zeros(h_scr.shape, F32)
            ext_scr[0:SUBLANES, :] = jnp.zeros((SUBLANES, ext_scr.shape[1]), F32)
            extv_scr[0:SUBLANES, :] = jnp.zeros((SUBLANES, extv_scr.shape[1]), F32)

    def load(ref, k):
        if stage is None:
            return ref[...].astype(F32)
        stage[k][0:rows_in, :] = ref[...].astype(F32)
        return stage[k][...]

    z = load(z_ref, 0)
    scb = load(scb_ref, 1)
    scc = load(scc_ref, 2)
    sch = load(sch_ref, 3)
    dt_raw = load(dt_ref, 4)

    ext_scr[SUBLANES:SUBLANES + rows_in, :] = xbc_ref[...].astype(F32)
    conv = cb_ref[...]
    for k in range(SSD_CONV_W):
        conv = conv + cw_ref[k:k + 1, :] * ext_scr[head0 + k:head0 + k + CHUNK, :]
    xc = _silu(conv)
    xs = xc[:, :ssd_w]
    bm = xc[:, ssd_w:ssd_w + SSD_GROUPS * n_state]
    cm = xc[:, ssd_w + SSD_GROUPS * n_state:]

    lane = lax.broadcasted_iota(I32, (CHUNK, LANES), 1)
    row = lax.broadcasted_iota(I32, (CHUNK, LANES), 0)
    dt = _softplus(dt_raw + dtb_ref[...])
    dt = jnp.where(jnp.logical_and(lane < heads, row < q_valid), dt, 0.0)
    a = dt * (-jnp.exp(alog_ref[...]))
    acum = jnp.dot(tril_ref[...], a, precision=HI, preferred_element_type=F32)
    acum_t = acum.T
    a_last = acum[CHUNK - 1:CHUNK, :]
    e = e_ref[...]
    expand = lambda v: jnp.dot(v.astype(BF16), e, preferred_element_type=F32)
    dt_e = expand(dt)
    dtdte_e = expand(dt * jnp.exp(a_last - acum))
    exa_e = expand(jnp.exp(acum))
    xdt = xs * dt_e
    xw_b = (xs * dtdte_e).astype(BF16)
    dlast = jnp.broadcast_to(jnp.exp(acum_t[:, CHUNK - 1:CHUNK]), (LANES, n_state))
    dcol = jnp.dot(et_ref[...], dlast, precision=HI, preferred_element_type=F32)

    tri = row >= lane
    y_groups = []
    for g in range(SSD_GROUPS):
        bm_g = bm[:, g * n_state:(g + 1) * n_state].astype(BF16)
        cm_g = cm[:, g * n_state:(g + 1) * n_state].astype(BF16)
        cb = lax.dot_general(cm_g, bm_g, NT, preferred_element_type=F32)
        h_g = h_scr[g * gw:(g + 1) * gw, :]
        y_off = lax.dot_general(cm_g, h_g.astype(BF16), NT, preferred_element_type=F32)
        parts = []
        for pair in range(hpg // 2):
            lo = (g * hpg + 2 * pair) * SSD_HEAD_DIM
            x_pair = xdt[:, lo:lo + LANES]
            acc = None
            for half in range(2):
                h = g * hpg + 2 * pair + half
                ci = jnp.broadcast_to(acum[:, h:h + 1], (CHUNK, CHUNK))
                rj = jnp.broadcast_to(acum_t[h:h + 1, :], (CHUNK, CHUNK))
                dec = jnp.where(tri, jnp.exp(ci - rj), 0.0)
                m = (cb * dec).astype(BF16)
                own = (lane >= SSD_HEAD_DIM) if half else (lane < SSD_HEAD_DIM)
                y_h = jnp.dot(m, jnp.where(own, x_pair, 0.0).astype(BF16), preferred_element_type=F32)
                acc = y_h if acc is None else acc + y_h
            parts.append(acc)
        y_diag = jnp.concatenate(parts, axis=1)
        y_groups.append(y_diag + y_off * exa_e[:, g * gw:(g + 1) * gw])
        upd = lax.dot_general(xw_b[:, g * gw:(g + 1) * gw], bm_g, TN, preferred_element_type=F32)
        h_scr[g * gw:(g + 1) * gw, :] = h_g * dcol[g * gw:(g + 1) * gw, :] + upd

    y = (jnp.concatenate(y_groups, axis=1) + xs * dsk_ref[...]) * _silu(z)
    y = jnp.concatenate([_rms(y[:, g * gw:(g + 1) * gw]) for g in range(SSD_GROUPS)], axis=1)
    y_ssd = y * gssd_ref[...]

    extv_scr[SUBLANES:SUBLANES + CHUNK, :] = scc * sch
    cv = scw_ref[0:1, :] * extv_scr[headv:headv + CHUNK, :]
    for k in range(1, SC_CONV_W):
        cv = cv + scw_ref[k:k + 1, :] * extv_scr[headv + k:headv + k + CHUNK, :]
    t = scb * cv
    sc_per_group = t.shape[1] // SC_GROUPS
    gsum = jnp.dot((t * t).astype(BF16), et_ref[...].astype(BF16), preferred_element_type=F32)
    rs = lax.rsqrt(gsum * (1.0 / sc_per_group) + EPS)
    y_sc = t * expand(rs) * gsc_ref[...]

    ymix_ref[:, :ssd_w] = y_ssd[0:rows_in].astype(ymix_ref.dtype)
    ymix_ref[:, ssd_w:] = y_sc[0:rows_in].astype(ymix_ref.dtype)

    @pl.when(c == nc - 1)
    def _():
        ssm_ref[0] = h_scr[...]
        cst_ref[0] = ext_scr[SUBLANES + q_valid - (SSD_CONV_W - 1):SUBLANES + q_valid, :]
        scst_ref[0] = extv_scr[SUBLANES + q_valid - (SC_CONV_W - 1):SUBLANES + q_valid, :]

    ext_scr[0:SUBLANES, :] = ext_scr[CHUNK:CHUNK + SUBLANES, :]
    extv_scr[0:SUBLANES, :] = extv_scr[CHUNK:CHUNK + SUBLANES, :]


def _ssd(proj, init, consts, nb, nc, rows_in, q_valid, ymix_dtype):
    z, xbc, scb, scc, sch, dt = proj
    heads = consts["heads"]
    ssd_w, conv_dim, sc_w = z.shape[1], xbc.shape[1], scb.shape[1]
    has_init = init is not None
    row_spec = lambda w: pl.BlockSpec((rows_in, w), lambda b, c: (b * nc + c, 0))
    in_specs = [row_spec(ssd_w), row_spec(conv_dim), row_spec(sc_w), row_spec(sc_w), row_spec(sc_w),
                row_spec(LANES)]
    args = [z, xbc, scb, scc, sch, dt]
    if has_init:
        ssm0, cst0, scst0 = init
        in_specs += [pl.BlockSpec((1,) + ssm0.shape[1:], lambda b, c: (b, 0, 0)),
                     pl.BlockSpec((1,) + cst0.shape[1:], lambda b, c: (b, 0, 0)),
                     pl.BlockSpec((1,) + scst0.shape[1:], lambda b, c: (b, 0, 0))]
        args += [ssm0, cst0, scst0]
    weights = [consts[k] for k in ("conv_w", "conv_b", "dt_bias", "a_log", "d_skip_e", "g_ssd",
                                   "sc_w", "g_sc", "tril", "e", "et")]
    in_specs += [_const_spec(w.shape) for w in weights]
    args += weights
    n_state = D_STATE
    out_shape = [jax.ShapeDtypeStruct((nb * nc * rows_in, ssd_w + sc_w), ymix_dtype),
                 jax.ShapeDtypeStruct((nb, ssd_w, n_state), F32),
                 jax.ShapeDtypeStruct((nb, SSD_CONV_W - 1, conv_dim), F32),
                 jax.ShapeDtypeStruct((nb, SC_CONV_W - 1, sc_w), F32)]
    out_specs = [pl.BlockSpec((rows_in, ssd_w + sc_w), lambda b, c: (b * nc + c, 0)),
                 pl.BlockSpec((1, ssd_w, n_state), lambda b, c: (b, 0, 0)),
                 pl.BlockSpec((1, SSD_CONV_W - 1, conv_dim), lambda b, c: (b, 0, 0)),
                 pl.BlockSpec((1, SC_CONV_W - 1, sc_w), lambda b, c: (b, 0, 0))]
    scratch = [pltpu.VMEM((ssd_w, n_state), F32),
               pltpu.VMEM((CHUNK + SUBLANES, conv_dim), F32),
               pltpu.VMEM((CHUNK + SUBLANES, sc_w), F32)]
    if rows_in < CHUNK:
        scratch += [pltpu.VMEM((CHUNK, w), F32) for w in (ssd_w, sc_w, sc_w, sc_w, LANES)]
    return pl.pallas_call(
        functools.partial(_ssd_kernel, rows_in=rows_in, q_valid=q_valid, has_init=has_init, heads=heads),
        out_shape=out_shape, grid=(nb, nc), in_specs=in_specs, out_specs=out_specs,
        scratch_shapes=scratch,
        compiler_params=_params(("arbitrary", "arbitrary")),
        name="ssd",
    )(*args)


def _out_kernel(ymix_ref, x_ref, mod_ref, wout_ref, gpost_ref, gpre_ref, wr_hi_ref, wr_lo_ref,
                *rest, per_row, rows_per_batch, has_alias):
    x1_ref, u2_ref, lg_ref = rest[3:6] if has_alias else rest[0:3]
    tile, d = x_ref.shape
    gate1, shift2, scale2 = _mod_rows(mod_ref, per_row, rows_per_batch, tile, d)
    m = jnp.dot(ymix_ref[...].astype(BF16), wout_ref[...], preferred_element_type=F32)
    x1 = x_ref[...] + gate1 * (_rms(m) * gpost_ref[...])
    u2 = (_rms(x1) * gpre_ref[...]) * (1.0 + scale2) + shift2
    x1_ref[...] = x1
    u2_ref[...] = u2
    u_hi = u2.astype(BF16)
    u_lo = (u2 - u_hi.astype(F32)).astype(BF16)
    lg = lax.dot_general(wr_hi_ref[...], u_hi, NT, preferred_element_type=F32)
    lg = lg + lax.dot_general(wr_hi_ref[...], u_lo, NT, preferred_element_type=F32)
    lg = lg + lax.dot_general(wr_lo_ref[...], u_hi, NT, preferred_element_type=F32)
    lg_ref[...] = lg


def _out_proj(ymix, x, mod, consts, per_row, rows_per_batch, tile, t_total, tile_off, prev):
    t, d = x.shape
    dm = ymix.shape[1]
    ne = consts["wr_hi_t"].shape[0]
    in_specs = [pl.BlockSpec((tile, dm), lambda i: (i, 0)),
                pl.BlockSpec((tile, d), lambda i: (i, 0)), _mod_spec(mod, per_row, tile),
                _const_spec((dm, d)), _const_spec((1, d)), _const_spec((1, d)),
                _const_spec((ne, d)), _const_spec((ne, d))]
    args = [ymix, x, mod, consts["w_out"], consts["g_post_mix"], consts["g_pre_ffn"],
            consts["wr_hi_t"], consts["wr_lo_t"]]
    aliases = {}
    if prev is not None:
        in_specs += [pl.BlockSpec(memory_space=pl.ANY)] * 3
        aliases = {len(args) + k: k for k in range(3)}
        args += list(prev)
    return pl.pallas_call(
        functools.partial(_out_kernel, per_row=per_row, rows_per_batch=rows_per_batch,
                          has_alias=prev is not None),
        out_shape=[jax.ShapeDtypeStruct((t_total, d), F32), jax.ShapeDtypeStruct((t_total, d), F32),
                   jax.ShapeDtypeStruct((ne, t_total), F32)],
        grid=(t // tile,), in_specs=in_specs,
        out_specs=[pl.BlockSpec((tile, d), lambda i: (i + tile_off, 0)),
                   pl.BlockSpec((tile, d), lambda i: (i + tile_off, 0)),
                   pl.BlockSpec((ne, tile), lambda i: (0, i + tile_off))],
        input_output_aliases=aliases,
        compiler_params=_params(("arbitrary",)),
        name="out_proj",
    )(*args)


def _route_kernel(lg_ref, bias_ref, upper_ref, idx_ref, rank_ref, wtok_ref, cnt_ref, carry_scr):
    i = pl.program_id(0)
    ne, tm = lg_ref.shape
    per_group = ne // N_EXPERT_GROUPS
    neg = -jnp.inf

    @pl.when(i == 0)
    def _():
        carry_scr[...] = jnp.zeros(carry_scr.shape, F32)

    s = _sigmoid(lg_ref[...])
    biased = s + bias_ref[...]
    gl = []
    io_g = lax.broadcasted_iota(I32, (per_group, tm), 0).astype(F32)
    for g in range(N_EXPERT_GROUPS):
        blk = biased[g * per_group:(g + 1) * per_group, :]
        m1 = jnp.max(blk, axis=0, keepdims=True)
        f1 = jnp.min(jnp.where(blk == m1, io_g, float(per_group)), axis=0, keepdims=True)
        m2 = jnp.max(jnp.where(io_g == f1, neg, blk), axis=0, keepdims=True)
        gl.append(m1 + m2)
    gscore = jnp.concatenate(gl, axis=0)
    io8 = lax.broadcasted_iota(I32, (N_EXPERT_GROUPS, tm), 0).astype(F32)
    gsel = jnp.zeros((N_EXPERT_GROUPS, tm), F32)
    for _ in range(TOPK_GROUPS):
        m = jnp.max(gscore, axis=0, keepdims=True)
        f = jnp.min(jnp.where(gscore == m, io8, float(N_EXPERT_GROUPS)), axis=0, keepdims=True)
        hit = io8 == f
        gsel = jnp.where(hit, 1.0, gsel)
        gscore = jnp.where(hit, neg, gscore)
    emask = jnp.concatenate(
        [jnp.broadcast_to(gsel[g:g + 1, :], (per_group, tm)) for g in range(N_EXPERT_GROUPS)], axis=0)
    cand = jnp.where(emask > 0.5, biased, neg)
    io_e = lax.broadcasted_iota(I32, (ne, tm), 0).astype(F32)
    msel = jnp.zeros((ne, tm), F32)
    idxs, wts = [], []
    for _ in range(TOP_K):
        m = jnp.max(cand, axis=0, keepdims=True)
        f = jnp.min(jnp.where(cand == m, io_e, float(ne)), axis=0, keepdims=True)
        hit = io_e == f
        wts.append(jnp.sum(jnp.where(hit, s, 0.0), axis=0, keepdims=True))
        idxs.append(f)
        msel = jnp.where(hit, 1.0, msel)
        cand = jnp.where(hit, neg, cand)
    pref = jnp.dot(msel.astype(BF16), upper_ref[...], preferred_element_type=F32) + carry_scr[:, 0:1]
    ranks = [jnp.sum(jnp.where(io_e == f, pref, 0.0), axis=0, keepdims=True) for f in idxs]
    carry_scr[...] = carry_scr[...] + jnp.sum(msel, axis=1, keepdims=True)
    cnt_ref[...] = carry_scr[...].astype(I32)
    idx_ref[...] = jnp.concatenate(idxs, axis=0).astype(I32)
    rank_ref[...] = jnp.concatenate(ranks, axis=0).astype(I32)
    wsum = wts[0]
    for w in wts[1:]:
        wsum = wsum + w
    wn = jnp.concatenate([w / wsum * ROUTED_SCALE for w in wts]
                         + [jnp.zeros((LANES - TOP_K, tm), F32)], axis=0)
    for j in range(tm // LANES):
        wtok_ref[j * LANES:(j + 1) * LANES, :] = wn[:, j * LANES:(j + 1) * LANES].T


def _route(logits_t, bias, tile):
    ne, t = logits_t.shape
    upper = jnp.triu(jnp.ones((tile, tile), F32), 1).astype(BF16)
    return pl.pallas_call(
        _route_kernel,
        out_shape=[jax.ShapeDtypeStruct((TOP_K, t), I32), jax.ShapeDtypeStruct((TOP_K, t), I32),
                   jax.ShapeDtypeStruct((t, LANES), F32), jax.ShapeDtypeStruct((ne, LANES), I32)],
        grid=(t // tile,),
        in_specs=[pl.BlockSpec((ne, tile), lambda i: (0, i)), _const_spec((ne, 1)),
                  _const_spec((tile, tile))],
        out_specs=[pl.BlockSpec((TOP_K, tile), lambda i: (0, i)),
                   pl.BlockSpec((TOP_K, tile), lambda i: (0, i)),
                   pl.BlockSpec((tile, LANES), lambda i: (i, 0)),
                   _const_spec((ne, LANES))],
        scratch_shapes=[pltpu.VMEM((ne, LANES), F32)],
        compiler_params=_params(("arbitrary",)),
        name="route",
    )(logits_t, bias.reshape(ne, 1), upper)


def _pos_kernel(idx_ref, rank_ref, start_ref, pos_ref):
    ne = start_ref.shape[0]
    tm = idx_ref.shape[1]
    io_e = lax.broadcasted_iota(I32, (ne, tm), 0)
    start = start_ref[...].astype(F32)
    rows = []
    for k in range(TOP_K):
        hit = io_e == idx_ref[k:k + 1, :]
        rows.append(jnp.sum(jnp.where(hit, start, 0.0), axis=0, keepdims=True))
    pos_ref[...] = jnp.concatenate(rows, axis=0).astype(I32) + rank_ref[...]


def _positions(idx_t, rank_t, pad_start, tile):
    k, t = idx_t.shape
    ne = pad_start.shape[0]
    return pl.pallas_call(
        _pos_kernel,
        out_shape=jax.ShapeDtypeStruct((k, t), I32),
        grid=(t // tile,),
        in_specs=[pl.BlockSpec((k, tile), lambda i: (0, i)), pl.BlockSpec((k, tile), lambda i: (0, i)),
                  _const_spec((ne, 1))],
        out_specs=pl.BlockSpec((k, tile), lambda i: (0, i)),
        compiler_params=_params(("arbitrary",)),
        name="positions",
    )(idx_t, rank_t, pad_start.reshape(ne, 1))


ZERO_BITS = (32, 16, 8, 4, 2, 1)


def _zero_copy(zbuf, xs_ref, sem, off, bit):
    rows = bit * SUBLANES
    return pltpu.make_async_copy(zbuf.at[pl.ds(0, rows)], xs_ref.at[pl.ds(pl.multiple_of(off, SUBLANES), rows)], sem)


def _zero_kernel(first_ref, units_ref, xs_ref, zbuf, sem):
    zbuf[...] = jnp.zeros(zbuf.shape, zbuf.dtype)
    ne = first_ref.shape[0]

    def each(e, start):
        off = first_ref[e]
        units = units_ref[e]
        for bit in ZERO_BITS:
            on = (units & bit) != 0

            @pl.when(on)
            def _():
                cp = _zero_copy(zbuf, xs_ref, sem, off, bit)
                cp.start() if start else cp.wait()
            off = off + jnp.where(on, bit * SUBLANES, 0)
        return start

    lax.fori_loop(0, ne, lambda e, c: (each(e, True), c)[1], 0)
    lax.fori_loop(0, ne, lambda e, c: (each(e, False), c)[1], 0)


def _zero_rows(first, units, n_rows, d):
    return pl.pallas_call(
        _zero_kernel,
        out_shape=jax.ShapeDtypeStruct((n_rows, d), F32),
        in_specs=[pl.BlockSpec(memory_space=pltpu.SMEM), pl.BlockSpec(memory_space=pltpu.SMEM)],
        out_specs=pl.BlockSpec(memory_space=pl.ANY),
        scratch_shapes=[pltpu.VMEM((ZERO_BITS[0] * SUBLANES, d), F32), pltpu.SemaphoreType.DMA],
        compiler_params=_params(),
        name="zero_rows",
    )(first, units)


def _row_copy(src_ref, dst_ref, sem, s, d):
    return pltpu.make_async_copy(src_ref.at[pl.ds(s, 1)], dst_ref.at[pl.ds(d, 1)], sem)


def _load_positions(pos_hbm, pos_smem, psem, tile_idx):
    per = pos_smem.shape[0]
    cp = pltpu.make_async_copy(pos_hbm.at[pl.ds(pl.multiple_of(tile_idx * per, per), per)], pos_smem, psem)
    cp.start()
    cp.wait()


def _dispatch_kernel(pos_hbm, u_ref, xs_in, xs_ref, pos_smem, psem, sem):
    del xs_in
    tm = u_ref.shape[0]
    _load_positions(pos_hbm, pos_smem, psem, pl.program_id(0))

    def issue(t, carry):
        for k in range(TOP_K):
            _row_copy(u_ref, xs_ref, sem, t, pos_smem[k * tm + t]).start()
        return carry

    lax.fori_loop(0, tm, issue, 0)
    for k in range(TOP_K):
        pltpu.make_async_copy(u_ref, xs_ref.at[pl.ds(0, tm)], sem).wait()


def _dispatch(pos_flat, u2, xs, tm):
    per = TOP_K * tm
    nt = pos_flat.shape[0] // per
    d = u2.shape[1]
    return pl.pallas_call(
        _dispatch_kernel,
        out_shape=jax.ShapeDtypeStruct(xs.shape, xs.dtype),
        grid=(nt,),
        in_specs=[pl.BlockSpec(memory_space=pl.ANY), pl.BlockSpec((tm, d), lambda i: (i, 0)),
                  pl.BlockSpec(memory_space=pl.ANY)],
        out_specs=pl.BlockSpec(memory_space=pl.ANY),
        scratch_shapes=[pltpu.SMEM((per,), I32), pltpu.SemaphoreType.DMA, pltpu.SemaphoreType.DMA],
        input_output_aliases={2: 0},
        compiler_params=_params(("arbitrary",)),
        name="dispatch",
    )(pos_flat, u2, xs)


def _expert_kernel(te_ref, nu_ref, xs_ref, wg_ref, wu_ref, wd_ref, y_ref):
    del te_ref

    @pl.when(pl.program_id(0) < nu_ref[0])
    def _():
        x = xs_ref[...].astype(BF16)
        g = jnp.dot(x, wg_ref[0].astype(BF16), preferred_element_type=F32)
        u = jnp.dot(x, wu_ref[0].astype(BF16), preferred_element_type=F32)
        h = (_silu(g) * u).astype(BF16)
        y_ref[...] = jnp.dot(h, wd_ref[0].astype(BF16), preferred_element_type=F32)


def _experts(tile_expert, n_used, xs, w_gate, w_up, w_down):
    n_rows, d = xs.shape
    de = w_gate.shape[2]
    n_tiles = n_rows // ROW_TILE
    row_map = lambda i, te, nu: (jnp.minimum(i, nu[0] - 1), 0)
    return pl.pallas_call(
        _expert_kernel,
        out_shape=jax.ShapeDtypeStruct((n_rows, d), F32),
        grid_spec=pltpu.PrefetchScalarGridSpec(
            num_scalar_prefetch=2, grid=(n_tiles,),
            in_specs=[pl.BlockSpec((ROW_TILE, d), row_map),
                      pl.BlockSpec((1, d, de), lambda i, te, nu: (te[i], 0, 0)),
                      pl.BlockSpec((1, d, de), lambda i, te, nu: (te[i], 0, 0)),
                      pl.BlockSpec((1, de, d), lambda i, te, nu: (te[i], 0, 0))],
            out_specs=pl.BlockSpec((ROW_TILE, d), row_map)),
        compiler_params=_params(("arbitrary",)),
        name="experts",
    )(tile_expert, n_used, xs, w_gate, w_up, w_down)


def _combine_kernel(pos_hbm, y_hbm, wtok_ref, u_ref, x1_ref, mod_ref, wsg_ref, wsu_ref, wsd_ref, gpost_ref,
                    o_ref, pos_smem, buf, psem, sem, *, per_row, rows_per_batch, tile_off):
    tm, d = u_ref.shape
    _load_positions(pos_hbm, pos_smem, psem, pl.program_id(0) + tile_off)

    def issue(t, carry):
        for k in range(TOP_K):
            _row_copy(y_hbm, buf.at[k], sem, pos_smem[k * tm + t], t).start()
        return carry

    lax.fori_loop(0, tm, issue, 0)

    ub = u_ref[...].astype(BF16)
    hs = _silu(jnp.dot(ub, wsg_ref[...], preferred_element_type=F32)) * jnp.dot(
        ub, wsu_ref[...], preferred_element_type=F32)
    f = jnp.dot(hs.astype(BF16), wsd_ref[...], preferred_element_type=F32)

    for k in range(TOP_K):
        pltpu.make_async_copy(y_hbm.at[pl.ds(0, tm)], buf.at[k], sem).wait()
    w = wtok_ref[...]
    for k in range(TOP_K):
        f = f + w[:, k:k + 1] * buf[k]
    (gate2,) = _mod_rows(mod_ref, per_row, rows_per_batch, tm, d)
    o_ref[...] = x1_ref[...] + gate2 * (_rms(f) * gpost_ref[...])


def _combine(pos_flat, y_rows, wtok, u2, x1, mod, consts, per_row, rows_per_batch, tile_off, n_tok, tm):
    per = TOP_K * tm
    d = u2.shape[1]
    ds_ = consts["w_sh_gate"].shape[1]
    tok_spec = lambda w: pl.BlockSpec((tm, w), lambda i: (i + tile_off, 0))
    mod_spec = _mod_spec(mod, per_row, tm)
    return pl.pallas_call(
        functools.partial(_combine_kernel, per_row=per_row, rows_per_batch=rows_per_batch, tile_off=tile_off),
        out_shape=jax.ShapeDtypeStruct((n_tok, d), F32),
        grid=(n_tok // tm,),
        in_specs=[pl.BlockSpec(memory_space=pl.ANY), pl.BlockSpec(memory_space=pl.ANY),
                  tok_spec(LANES), tok_spec(d), tok_spec(d), mod_spec,
                  _const_spec((d, ds_)), _const_spec((d, ds_)), _const_spec((ds_, d)), _const_spec((1, d))],
        out_specs=pl.BlockSpec((tm, d), lambda i: (i, 0)),
        scratch_shapes=[pltpu.SMEM((per,), I32), pltpu.VMEM((TOP_K, tm, d), F32),
                        pltpu.SemaphoreType.DMA, pltpu.SemaphoreType.DMA],
        compiler_params=_params(("arbitrary",)),
        name="combine",
    )(pos_flat, y_rows, wtok, u2, x1, mod, consts["w_sh_gate"], consts["w_sh_up"], consts["w_sh_down"],
      consts["g_post_ffn"])


def _tile_major(pos_t, tile):
    k, t = pos_t.shape
    return pos_t.reshape(k, t // tile, tile).transpose(1, 0, 2).reshape(-1)


def kernel(x_prompt, x_sample, c_prompt, c_sample, state_ssm, state_ssd_conv, state_short_conv, w_ada, b_ada, g_pre_mix, g_post_mix, g_pre_ffn, g_post_ffn, w_in, ssd_conv_w, ssd_conv_b, dt_bias, a_log, d_skip, g_ssd_norm, sc_conv_w, g_sc_norm, w_out, w_router, router_bias, w_exp_gate, w_exp_up, w_exp_down, w_sh_gate, w_sh_up, w_sh_down):
    depth = w_ada.shape[0]
    bp, seq, d = x_prompt.shape
    bs, dseq, _ = x_sample.shape
    heads = dt_bias.shape[1]
    ssd_w = heads * SSD_HEAD_DIM
    conv_dim = ssd_conv_w.shape[2]
    sc_w = sc_conv_w.shape[2]
    ne = w_router.shape[2]
    tp, ts = bp * seq, bs * dseq
    t_all = tp + ts

    head_of = jnp.arange(ssd_w, dtype=I32) // SSD_HEAD_DIM
    e_ind = (jnp.arange(LANES, dtype=I32)[:, None] == head_of[None, :])
    tril = jnp.tril(jnp.ones((CHUNK, CHUNK), F32))

    xp = x_prompt.reshape(tp, d)
    xs_pad = jnp.pad(x_sample, ((0, 0), (0, SUBLANES - dseq), (0, 0))).reshape(bs * SUBLANES, d)
    xs_tok = x_sample.reshape(ts, d)
    outs = {k: [] for k in ("ssm_p", "cst_p", "scst_p", "ssm_s", "cst_s", "scst_s")}

    for layer in range(depth):
        cuts = np.cumsum([0, ssd_w, conv_dim, heads, sc_w, sc_w, sc_w]).tolist()
        wi = w_in[layer]
        seg = lambda k: wi[:, cuts[k]:cuts[k + 1]]
        w_in_r = jnp.concatenate([seg(0), seg(1), seg(3), seg(4), seg(5),
                                  jnp.pad(seg(2), ((0, 0), (0, LANES - heads)))], axis=1).astype(BF16)
        widths = (ssd_w, conv_dim, sc_w, sc_w, sc_w, LANES)
        pad_h = lambda v: jnp.pad(v.reshape(1, heads), ((0, 0), (0, LANES - heads)))
        wr = w_router[layer].T
        wr_hi = wr.astype(BF16)
        consts = dict(
            heads=heads,
            conv_w=ssd_conv_w[layer], conv_b=ssd_conv_b[layer].reshape(1, conv_dim),
            dt_bias=pad_h(dt_bias[layer]), a_log=pad_h(a_log[layer]),
            d_skip_e=jnp.repeat(d_skip[layer], SSD_HEAD_DIM).reshape(1, ssd_w),
            g_ssd=g_ssd_norm[layer].reshape(1, ssd_w), sc_w=sc_conv_w[layer],
            g_sc=g_sc_norm[layer].reshape(1, sc_w),
            tril=tril, e=e_ind.astype(BF16), et=e_ind.T.astype(F32),
            w_out=w_out[layer].astype(BF16), g_post_mix=g_post_mix[layer].reshape(1, d),
            g_pre_ffn=g_pre_ffn[layer].reshape(1, d),
            wr_hi_t=wr_hi, wr_lo_t=(wr - wr_hi.astype(F32)).astype(BF16),
            w_sh_gate=w_sh_gate[layer].astype(BF16), w_sh_up=w_sh_up[layer].astype(BF16),
            w_sh_down=w_sh_down[layer].astype(BF16), g_post_ffn=g_post_ffn[layer].reshape(1, d))

        c_all = jnp.concatenate([c_prompt, c_sample], axis=0)
        m_rows = -(-c_all.shape[0] // 16) * 16
        mod = _ada(jnp.pad(c_all, ((0, m_rows - c_all.shape[0]), (0, 0))), w_ada[layer], b_ada[layer])
        mod_p = mod[:bp]
        mod_s = mod[bp:bp + bs]
        mod_s_pad = jnp.repeat(mod_s[:, :2 * d], SUBLANES, axis=0)
        mod_s_tok = jnp.repeat(mod_s, dseq, axis=0)

        proj_p = _in_proj(xp, mod_p[:, :2 * d], g_pre_mix[layer], w_in_r, widths, BF16, False, seq, TOK_TILE)
        proj_s = _in_proj(xs_pad, mod_s_pad, g_pre_mix[layer], w_in_r, widths, F32, True, 1, TOK_TILE)
        ymix_p, ssm_p, cst_p, scst_p = _ssd(proj_p, None, consts, bp, seq // CHUNK, CHUNK, CHUNK, BF16)
        init = (state_ssm[layer].reshape(bs, ssd_w, D_STATE), state_ssd_conv[layer], state_short_conv[layer])
        ymix_s, ssm_s, cst_s, scst_s = _ssd(proj_s, init, consts, bs, 1, SUBLANES, dseq, F32)
        ymix_s = ymix_s.reshape(bs, SUBLANES, ssd_w + sc_w)[:, :dseq].reshape(ts, ssd_w + sc_w)

        merged = _out_proj(ymix_p, xp, mod_p[:, 2 * d:5 * d], consts, False, seq, TOK_TILE, t_all, 0, None)
        x1, u2, logits_t = _out_proj(ymix_s, xs_tok, mod_s_tok[:, 2 * d:5 * d], consts, True, 1, TOK_TILE,
                                     t_all, tp // TOK_TILE, merged)

        idx_t, rank_t, wtok, counts = _route(logits_t, router_bias[layer], TOK_TILE)
        counts = counts[:, 0]
        padded = (counts + ROW_TILE - 1) // ROW_TILE * ROW_TILE
        pad_end = jnp.cumsum(padded)
        pad_start = pad_end - padded
        n_tiles = t_all * TOP_K // ROW_TILE + ne
        n_used = (pad_end[-1] // ROW_TILE).astype(I32)
        tile_ids = jnp.minimum(jnp.arange(n_tiles, dtype=I32), n_used - 1)
        tile_expert = jnp.minimum(jnp.searchsorted(pad_end, tile_ids * ROW_TILE, side="right"), ne - 1).astype(I32)
        first_pad = (pad_start + counts) // SUBLANES * SUBLANES
        units = (pad_end - first_pad) // SUBLANES
        pos_t = _positions(idx_t, rank_t, pad_start.astype(I32), TOK_TILE)
        pos_flat = _tile_major(pos_t, GATHER_TILE)

        xs_rows = _zero_rows(first_pad.astype(I32), units.astype(I32), n_tiles * ROW_TILE, d)
        xs_rows = _dispatch(pos_flat, u2, xs_rows, GATHER_TILE)
        y_rows = _experts(tile_expert, n_used.reshape(1), xs_rows, w_exp_gate[layer], w_exp_up[layer],
                          w_exp_down[layer])
        xp = _combine(pos_flat, y_rows, wtok, u2, x1, mod_p[:, 5 * d:], consts, False, seq, 0, tp, GATHER_TILE)
        xs_tok = _combine(pos_flat, y_rows, wtok, u2, x1, mod_s_tok[:, 5 * d:], consts, True, 1,
                          tp // GATHER_TILE, ts, GATHER_TILE)
        xs_pad = jnp.pad(xs_tok.reshape(bs, dseq, d), ((0, 0), (0, SUBLANES - dseq), (0, 0))).reshape(
            bs * SUBLANES, d)

        outs["ssm_p"].append(ssm_p.reshape(bp, heads, SSD_HEAD_DIM, D_STATE))
        outs["cst_p"].append(cst_p)
        outs["scst_p"].append(scst_p)
        outs["ssm_s"].append(ssm_s.reshape(bs, heads, SSD_HEAD_DIM, D_STATE))
        outs["cst_s"].append(cst_s)
        outs["scst_s"].append(scst_s)

    return (xp.reshape(bp, seq, d), xs_tok.reshape(bs, dseq, d),
            jnp.stack(outs["ssm_p"]), jnp.stack(outs["cst_p"]), jnp.stack(outs["scst_p"]),
            jnp.stack(outs["ssm_s"]), jnp.stack(outs["cst_s"]), jnp.stack(outs["scst_s"]))
```

```python
import functools

import jax
import jax.numpy as jnp
import numpy as np
from jax import lax
from jax.experimental import pallas as pl
from jax.experimental.pallas import tpu as pltpu

F32 = jnp.float32
BF16 = jnp.bfloat16
I32 = jnp.int32
HI = lax.Precision.HIGHEST

SSD_HEAD_DIM = 64
SSD_GROUPS = 2
D_STATE = 128
SSD_CONV_W = 4
SC_GROUPS = 16
SC_CONV_W = 3
TOP_K = 8
N_EXPERT_GROUPS = 8
TOPK_GROUPS = 4
ROUTED_SCALE = 2.5
EPS = 1e-6

LANES = 128
SUBLANES = 8
CHUNK = 128
TOK_TILE = 512
ROW_TILE = 256
GATHER_TILE = 256
VMEM_LIMIT = 56 * 1024 * 1024
DMA_THREADS = 2

NT = (((1,), (1,)), ((), ()))
TN = (((0,), (0,)), ((), ()))


def _sigmoid(x):
    return 1.0 / (1.0 + jnp.exp(-x))


def _silu(x):
    return x * _sigmoid(x)


def _softplus(x):
    return jnp.maximum(x, 0.0) + jnp.log1p(jnp.exp(-jnp.abs(x)))


def _rms(x, eps=EPS):
    return x * lax.rsqrt(jnp.mean(x * x, axis=-1, keepdims=True) + eps)


def _params(sem=None):
    return pltpu.CompilerParams(dimension_semantics=sem, vmem_limit_bytes=VMEM_LIMIT)


def _const_spec(shape, single=False):
    nd = len(shape)
    mode = dict(pipeline_mode=pl.Buffered(1)) if single else {}
    return pl.BlockSpec(shape, lambda *_: (0,) * nd, **mode)


def _ada_kernel(c_ref, w_ref, b_ref, o_ref):
    c = c_ref[...]
    s = _silu(c).astype(BF16)
    o_ref[...] = jnp.dot(s, w_ref[...].astype(BF16), preferred_element_type=F32) + b_ref[...]


def _ada(c, w_ada, b_ada):
    m, d = c.shape
    n = w_ada.shape[1]
    tn = 512
    return pl.pallas_call(
        _ada_kernel,
        out_shape=jax.ShapeDtypeStruct((m, n), F32),
        grid=(n // tn,),
        in_specs=[_const_spec((m, d)),
                  pl.BlockSpec((d, tn), lambda j: (0, j)),
                  pl.BlockSpec((1, tn), lambda j: (0, j))],
        out_specs=pl.BlockSpec((m, tn), lambda j: (0, j)),
        compiler_params=_params(("arbitrary",)),
        name="ada",
    )(c, w_ada, b_ada.reshape(1, n))


def _mod_rows(mod_ref, per_row, rows_per_batch, tile, d):
    n = mod_ref.shape[1] // d
    if per_row:
        return [mod_ref[:, k * d:(k + 1) * d] for k in range(n)]
    b = (pl.program_id(0) * tile) // rows_per_batch
    return [mod_ref[pl.ds(b, 1), k * d:(k + 1) * d] for k in range(n)]


def _mod_spec(mod, per_row, tile):
    if per_row:
        return pl.BlockSpec((tile, mod.shape[1]), lambda i: (i, 0))
    return _const_spec(mod.shape)


def _in_kernel(x_ref, mod_ref, g_ref, w_ref, *out_refs, per_row, rows_per_batch, widths):
    tile, d = x_ref.shape
    shift, scale = _mod_rows(mod_ref, per_row, rows_per_batch, tile, d)
    u = (_rms(x_ref[...]) * g_ref[...]) * (1.0 + scale) + shift
    u = u.astype(BF16)
    col = 0
    for ref, width in zip(out_refs, widths):
        for a in range(0, width, 512):
            bw = min(512, width - a)
            r = jnp.dot(u, w_ref[:, col + a:col + a + bw], preferred_element_type=F32)
            ref[:, a:a + bw] = r.astype(ref.dtype)
        col += width


def _in_proj(x, mod, g, w_bf16, widths, out_dtype, per_row, rows_per_batch, tile):
    t, d = x.shape
    n = w_bf16.shape[1]
    mod_spec = _mod_spec(mod, per_row, tile)
    dts = [out_dtype] * (len(widths) - 1) + [F32]
    return pl.pallas_call(
        functools.partial(_in_kernel, per_row=per_row, rows_per_batch=rows_per_batch, widths=widths),
        out_shape=[jax.ShapeDtypeStruct((t, wd), dt) for wd, dt in zip(widths, dts)],
        grid=(t // tile,),
        in_specs=[pl.BlockSpec((tile, d), lambda i: (i, 0)), mod_spec,
                  _const_spec((1, d)), _const_spec((d, n), single=True)],
        out_specs=[pl.BlockSpec((tile, wd), lambda i: (i, 0)) for wd in widths],
        compiler_params=_params(("arbitrary",)),
        name="in_proj",
    )(x, mod, g.reshape(1, d), w_bf16)


def _ssd_kernel(*refs, rows_in, q_valid, has_init, heads):
    it = iter(refs)
    z_ref, xbc_ref, scb_ref, scc_ref, sch_ref, dt_ref = (next(it) for _ in range(6))
    if has_init:
        ssm0_ref, cst0_ref, scst0_ref = (next(it) for _ in range(3))
    (cw_ref, cb_ref, dtb_ref, alog_ref, dsk_ref, gssd_ref, scw_ref, gsc_ref,
     tril_ref, e_ref, et_ref) = (next(it) for _ in range(11))
    ymix_ref, ssm_ref, cst_ref, scst_ref = (next(it) for _ in range(4))
    h_scr, ext_scr, extv_scr = (next(it) for _ in range(3))
    stage = [next(it) for _ in range(5)] if rows_in < CHUNK else None

    c = pl.program_id(1)
    nc = pl.num_programs(1)
    ssd_w = dsk_ref.shape[1]
    gw = ssd_w // SSD_GROUPS
    hpg = heads // SSD_GROUPS
    n_state = D_STATE
    head0 = SUBLANES - (SSD_CONV_W - 1)
    headv = SUBLANES - (SC_CONV_W - 1)

    first = jnp.logical_and(pl.program_id(0) == 0, c == 0)
    if stage is not None:
        @pl.when(first)
        def _():
            for s in stage:
                s[...] = jnp.zeros(s.shape, s.dtype)
            ext_scr[...] = jnp.zeros(ext_scr.shape, F32)
            extv_scr[...] = jnp.zeros(extv_scr.shape, F32)

    @pl.when(c == 0)
    def _():
        if has_init:
            h_scr[...] = ssm0_ref[0]
            ext_scr[head0:SUBLANES, :] = cst0_ref[0]
            extv_scr[headv:SUBLANES, :] = scst0_ref[0]
        else:
            h_scr[...] = jnp.zeros(h_scr.shape, F32)
            ext_scr[0:SUBLANES, :] = jnp.zeros((SUBLANES, ext_scr.shape[1]), F32)
            extv_scr[0:SUBLANES, :] = jnp.zeros((SUBLANES, extv_scr.shape[1]), F32)

    def load(ref, k):
        if stage is None:
            return ref[...].astype(F32)
        stage[k][0:rows_in, :] = ref[...].astype(F32)
        return stage[k][...]

    z = load(z_ref, 0)
    scb = load(scb_ref, 1)
    scc = load(scc_ref, 2)
    sch = load(sch_ref, 3)
    dt_raw = load(dt_ref, 4)

    ext_scr[SUBLANES:SUBLANES + rows_in, :] = xbc_ref[...].astype(F32)
    conv = cb_ref[...]
    for k in range(SSD_CONV_W):
        conv = conv + cw_ref[k:k + 1, :] * ext_scr[head0 + k:head0 + k + CHUNK, :]
    xc = _silu(conv)
    xs = xc[:, :ssd_w]
    bm = xc[:, ssd_w:ssd_w + SSD_GROUPS * n_state]
    cm = xc[:, ssd_w + SSD_GROUPS * n_state:]

    lane = lax.broadcasted_iota(I32, (CHUNK, LANES), 1)
    row = lax.broadcasted_iota(I32, (CHUNK, LANES), 0)
    dt = _softplus(dt_raw + dtb_ref[...])
    dt = jnp.where(jnp.logical_and(lane < heads, row < q_valid), dt, 0.0)
    a = dt * (-jnp.exp(alog_ref[...]))
    acum = jnp.dot(tril_ref[...], a, precision=HI, preferred_element_type=F32)
    acum_t = acum.T
    a_last = acum[CHUNK - 1:CHUNK, :]
    e = e_ref[...]
    expand = lambda v: jnp.dot(v.astype(BF16), e, preferred_element_type=F32)
    dt_e = expand(dt)
    dtdte_e = expand(dt * jnp.exp(a_last - acum))
    exa_e = expand(jnp.exp(acum))
    xdt = xs * dt_e
    xw_b = (xs * dtdte_e).astype(BF16)
    dlast = jnp.broadcast_to(jnp.exp(acum_t[:, CHUNK - 1:CHUNK]), (LANES, n_state))
    dcol = jnp.dot(et_ref[...], dlast, precision=HI, preferred_element_type=F32)

    tri = row >= lane
    y_groups = []
    for g in range(SSD_GROUPS):
        bm_g = bm[:, g * n_state:(g + 1) * n_state].astype(BF16)
        cm_g = cm[:, g * n_state:(g + 1) * n_state].astype(BF16)
        cb = lax.dot_general(cm_g, bm_g, NT, preferred_element_type=F32)
        h_g = h_scr[g * gw:(g + 1) * gw, :]
        y_off = lax.dot_general(cm_g, h_g.astype(BF16), NT, preferred_element_type=F32)
        parts = []
        for pair in range(hpg // 2):
            lo = (g * hpg + 2 * pair) * SSD_HEAD_DIM
            x_pair = xdt[:, lo:lo + LANES]
            acc = None
            for half in range(2):
                h = g * hpg + 2 * pair + half
                ci = jnp.broadcast_to(acum[:, h:h + 1], (CHUNK, CHUNK))
                rj = jnp.broadcast_to(acum_t[h:h + 1, :], (CHUNK, CHUNK))
                dec = jnp.where(tri, jnp.exp(ci - rj), 0.0)
                m = (cb * dec).astype(BF16)
                own = (lane >= SSD_HEAD_DIM) if half else (lane < SSD_HEAD_DIM)
                y_h = jnp.dot(m, jnp.where(own, x_pair, 0.0).astype(BF16), preferred_element_type=F32)
                acc = y_h if acc is None else acc + y_h
            parts.append(acc)
        y_diag = jnp.concatenate(parts, axis=1)
        y_groups.append(y_diag + y_off * exa_e[:, g * gw:(g + 1) * gw])
        upd = lax.dot_general(xw_b[:, g * gw:(g + 1) * gw], bm_g, TN, preferred_element_type=F32)
        h_scr[g * gw:(g + 1) * gw, :] = h_g * dcol[g * gw:(g + 1) * gw, :] + upd

    y = (jnp.concatenate(y_groups, axis=1) + xs * dsk_ref[...]) * _silu(z)
    y = jnp.concatenate([_rms(y[:, g * gw:(g + 1) * gw]) for g in range(SSD_GROUPS)], axis=1)
    y_ssd = y * gssd_ref[...]

    extv_scr[SUBLANES:SUBLANES + CHUNK, :] = scc * sch
    cv = scw_ref[0:1, :] * extv_scr[headv:headv + CHUNK, :]
    for k in range(1, SC_CONV_W):
        cv = cv + scw_ref[k:k + 1, :] * extv_scr[headv + k:headv + k + CHUNK, :]
    t = scb * cv
    sc_per_group = t.shape[1] // SC_GROUPS
    gsum = jnp.dot((t * t).astype(BF16), et_ref[...].astype(BF16), preferred_element_type=F32)
    rs = lax.rsqrt(gsum * (1.0 / sc_per_group) + EPS)
    y_sc = t * expand(rs) * gsc_ref[...]

    ymix_ref[:, :ssd_w] = y_ssd[0:rows_in].astype(ymix_ref.dtype)
    ymix_ref[:, ssd_w:] = y_sc[0:rows_in].astype(ymix_ref.dtype)

    @pl.when(c == nc - 1)
    def _():
        ssm_ref[0] = h_scr[...]
        cst_ref[0] = ext_scr[SUBLANES + q_valid - (SSD_CONV_W - 1):SUBLANES + q_valid, :]
        scst_ref[0] = extv_scr[SUBLANES + q_valid - (SC_CONV_W - 1):SUBLANES + q_valid, :]

    ext_scr[0:SUBLANES, :] = ext_scr[CHUNK:CHUNK + SUBLANES, :]
    extv_scr[0:SUBLANES, :] = extv_scr[CHUNK:CHUNK + SUBLANES, :]


def _ssd(proj, init, consts, nb, nc, rows_in, q_valid, ymix_dtype):
    z, xbc, scb, scc, sch, dt = proj
    heads = consts["heads"]
    ssd_w, conv_dim, sc_w = z.shape[1], xbc.shape[1], scb.shape[1]
    has_init = init is not None
    row_spec = lambda w: pl.BlockSpec((rows_in, w), lambda b, c: (b * nc + c, 0))
    in_specs = [row_spec(ssd_w), row_spec(conv_dim), row_spec(sc_w), row_spec(sc_w), row_spec(sc_w),
                row_spec(LANES)]
    args = [z, xbc, scb, scc, sch, dt]
    if has_init:
        ssm0, cst0, scst0 = init
        in_specs += [pl.BlockSpec((1,) + ssm0.shape[1:], lambda b, c: (b, 0, 0)),
                     pl.BlockSpec((1,) + cst0.shape[1:], lambda b, c: (b, 0, 0)),
                     pl.BlockSpec((1,) + scst0.shape[1:], lambda b, c: (b, 0, 0))]
        args += [ssm0, cst0, scst0]
    weights = [consts[k] for k in ("conv_w", "conv_b", "dt_bias", "a_log", "d_skip_e", "g_ssd",
                                   "sc_w", "g_sc", "tril", "e", "et")]
    in_specs += [_const_spec(w.shape) for w in weights]
    args += weights
    n_state = D_STATE
    out_shape = [jax.ShapeDtypeStruct((nb * nc * rows_in, ssd_w + sc_w), ymix_dtype),
                 jax.ShapeDtypeStruct((nb, ssd_w, n_state), F32),
                 jax.ShapeDtypeStruct((nb, SSD_CONV_W - 1, conv_dim), F32),
                 jax.ShapeDtypeStruct((nb, SC_CONV_W - 1, sc_w), F32)]
    out_specs = [pl.BlockSpec((rows_in, ssd_w + sc_w), lambda b, c: (b * nc + c, 0)),
                 pl.BlockSpec((1, ssd_w, n_state), lambda b, c: (b, 0, 0)),
                 pl.BlockSpec((1, SSD_CONV_W - 1, conv_dim), lambda b, c: (b, 0, 0)),
                 pl.BlockSpec((1, SC_CONV_W - 1, sc_w), lambda b, c: (b, 0, 0))]
    scratch = [pltpu.VMEM((ssd_w, n_state), F32),
               pltpu.VMEM((CHUNK + SUBLANES, conv_dim), F32),
               pltpu.VMEM((CHUNK + SUBLANES, sc_w), F32)]
    if rows_in < CHUNK:
        scratch += [pltpu.VMEM((CHUNK, w), F32) for w in (ssd_w, sc_w, sc_w, sc_w, LANES)]
    return pl.pallas_call(
        functools.partial(_ssd_kernel, rows_in=rows_in, q_valid=q_valid, has_init=has_init, heads=heads),
        out_shape=out_shape, grid=(nb, nc), in_specs=in_specs, out_specs=out_specs,
        scratch_shapes=scratch,
        compiler_params=_params(("arbitrary", "arbitrary")),
        name="ssd",
    )(*args)


def _out_kernel(ymix_ref, x_ref, mod_ref, wout_ref, gpost_ref, gpre_ref, wr_hi_ref, wr_lo_ref,
                *rest, per_row, rows_per_batch, has_alias):
    x1_ref, u2_ref, lg_ref = rest[3:6] if has_alias else rest[0:3]
    tile, d = x_ref.shape
    gate1, shift2, scale2 = _mod_rows(mod_ref, per_row, rows_per_batch, tile, d)
    m = jnp.dot(ymix_ref[...].astype(BF16), wout_ref[...], preferred_element_type=F32)
    x1 = x_ref[...] + gate1 * (_rms(m) * gpost_ref[...])
    u2 = (_rms(x1) * gpre_ref[...]) * (1.0 + scale2) + shift2
    x1_ref[...] = x1
    u2_ref[...] = u2
    u_hi = u2.astype(BF16)
    u_lo = (u2 - u_hi.astype(F32)).astype(BF16)
    lg = lax.dot_general(wr_hi_ref[...], u_hi, NT, preferred_element_type=F32)
    lg = lg + lax.dot_general(wr_hi_ref[...], u_lo, NT, preferred_element_type=F32)
    lg = lg + lax.dot_general(wr_lo_ref[...], u_hi, NT, preferred_element_type=F32)
    lg_ref[...] = lg


def _out_proj(ymix, x, mod, consts, per_row, rows_per_batch, tile, t_total, tile_off, prev):
    t, d = x.shape
    dm = ymix.shape[1]
    ne = consts["wr_hi_t"].shape[0]
    in_specs = [pl.BlockSpec((tile, dm), lambda i: (i, 0)),
                pl.BlockSpec((tile, d), lambda i: (i, 0)), _mod_spec(mod, per_row, tile),
                _const_spec((dm, d)), _const_spec((1, d)), _const_spec((1, d)),
                _const_spec((ne, d)), _const_spec((ne, d))]
    args = [ymix, x, mod, consts["w_out"], consts["g_post_mix"], consts["g_pre_ffn"],
            consts["wr_hi_t"], consts["wr_lo_t"]]
    aliases = {}
    if prev is not None:
        in_specs += [pl.BlockSpec(memory_space=pl.ANY)] * 3
        aliases = {len(args) + k: k for k in range(3)}
        args += list(prev)
    return pl.pallas_call(
        functools.partial(_out_kernel, per_row=per_row, rows_per_batch=rows_per_batch,
                          has_alias=prev is not None),
        out_shape=[jax.ShapeDtypeStruct((t_total, d), F32), jax.ShapeDtypeStruct((t_total, d), F32),
                   jax.ShapeDtypeStruct((ne, t_total), F32)],
        grid=(t // tile,), in_specs=in_specs,
        out_specs=[pl.BlockSpec((tile, d), lambda i: (i + tile_off, 0)),
                   pl.BlockSpec((tile, d), lambda i: (i + tile_off, 0)),
                   pl.BlockSpec((ne, tile), lambda i: (0, i + tile_off))],
        input_output_aliases=aliases,
        compiler_params=_params(("arbitrary",)),
        name="out_proj",
    )(*args)


def _route_kernel(lg_ref, bias_ref, upper_ref, idx_ref, rank_ref, wtok_ref, cnt_ref, carry_scr):
    i = pl.program_id(0)
    ne, tm = lg_ref.shape
    per_group = ne // N_EXPERT_GROUPS
    neg = -jnp.inf

    @pl.when(i == 0)
    def _():
        carry_scr[...] = jnp.zeros(carry_scr.shape, F32)

    s = _sigmoid(lg_ref[...])
    biased = s + bias_ref[...]
    gl = []
    io_g = lax.broadcasted_iota(I32, (per_group, tm), 0).astype(F32)
    for g in range(N_EXPERT_GROUPS):
        blk = biased[g * per_group:(g + 1) * per_group, :]
        m1 = jnp.max(blk, axis=0, keepdims=True)
        f1 = jnp.min(jnp.where(blk == m1, io_g, float(per_group)), axis=0, keepdims=True)
        m2 = jnp.max(jnp.where(io_g == f1, neg, blk), axis=0, keepdims=True)
        gl.append(m1 + m2)
    gscore = jnp.concatenate(gl, axis=0)
    io8 = lax.broadcasted_iota(I32, (N_EXPERT_GROUPS, tm), 0).astype(F32)
    gsel = jnp.zeros((N_EXPERT_GROUPS, tm), F32)
    for _ in range(TOPK_GROUPS):
        m = jnp.max(gscore, axis=0, keepdims=True)
        f = jnp.min(jnp.where(gscore == m, io8, float(N_EXPERT_GROUPS)), axis=0, keepdims=True)
        hit = io8 == f
        gsel = jnp.where(hit, 1.0, gsel)
        gscore = jnp.where(hit, neg, gscore)
    emask = jnp.concatenate(
        [jnp.broadcast_to(gsel[g:g + 1, :], (per_group, tm)) for g in range(N_EXPERT_GROUPS)], axis=0)
    cand = jnp.where(emask > 0.5, biased, neg)
    io_e = lax.broadcasted_iota(I32, (ne, tm), 0).astype(F32)
    msel = jnp.zeros((ne, tm), F32)
    idxs, wts = [], []
    for _ in range(TOP_K):
        m = jnp.max(cand, axis=0, keepdims=True)
        f = jnp.min(jnp.where(cand == m, io_e, float(ne)), axis=0, keepdims=True)
        hit = io_e == f
        wts.append(jnp.sum(jnp.where(hit, s, 0.0), axis=0, keepdims=True))
        idxs.append(f)
        msel = jnp.where(hit, 1.0, msel)
        cand = jnp.where(hit, neg, cand)
    pref = jnp.dot(msel.astype(BF16), upper_ref[...], preferred_element_type=F32) + carry_scr[:, 0:1]
    ranks = [jnp.sum(jnp.where(io_e == f, pref, 0.0), axis=0, keepdims=True) for f in idxs]
    carry_scr[...] = carry_scr[...] + jnp.sum(msel, axis=1, keepdims=True)
    cnt_ref[...] = carry_scr[...].astype(I32)
    idx_ref[...] = jnp.concatenate(idxs, axis=0).astype(I32)
    rank_ref[...] = jnp.concatenate(ranks, axis=0).astype(I32)
    wsum = wts[0]
    for w in wts[1:]:
        wsum = wsum + w
    wn = jnp.concatenate([w / wsum * ROUTED_SCALE for w in wts]
                         + [jnp.zeros((LANES - TOP_K, tm), F32)], axis=0)
    for j in range(tm // LANES):
        wtok_ref[j * LANES:(j + 1) * LANES, :] = wn[:, j * LANES:(j + 1) * LANES].T


def _route(logits_t, bias, tile):
    ne, t = logits_t.shape
    upper = jnp.triu(jnp.ones((tile, tile), F32), 1).astype(BF16)
    return pl.pallas_call(
        _route_kernel,
        out_shape=[jax.ShapeDtypeStruct((TOP_K, t), I32), jax.ShapeDtypeStruct((TOP_K, t), I32),
                   jax.ShapeDtypeStruct((t, LANES), F32), jax.ShapeDtypeStruct((ne, LANES), I32)],
        grid=(t // tile,),
        in_specs=[pl.BlockSpec((ne, tile), lambda i: (0, i)), _const_spec((ne, 1)),
                  _const_spec((tile, tile))],
        out_specs=[pl.BlockSpec((TOP_K, tile), lambda i: (0, i)),
                   pl.BlockSpec((TOP_K, tile), lambda i: (0, i)),
                   pl.BlockSpec((tile, LANES), lambda i: (i, 0)),
                   _const_spec((ne, LANES))],
        scratch_shapes=[pltpu.VMEM((ne, LANES), F32)],
        compiler_params=_params(("arbitrary",)),
        name="route",
    )(logits_t, bias.reshape(ne, 1), upper)


def _pos_kernel(idx_ref, rank_ref, start_ref, pos_ref):
    ne = start_ref.shape[0]
    tm = idx_ref.shape[1]
    io_e = lax.broadcasted_iota(I32, (ne, tm), 0)
    start = start_ref[...].astype(F32)
    rows = []
    for k in range(TOP_K):
        hit = io_e == idx_ref[k:k + 1, :]
        rows.append(jnp.sum(jnp.where(hit, start, 0.0), axis=0, keepdims=True))
    pos_ref[...] = jnp.concatenate(rows, axis=0).astype(I32) + rank_ref[...]


def _positions(idx_t, rank_t, pad_start, tile):
    k, t = idx_t.shape
    ne = pad_start.shape[0]
    return pl.pallas_call(
        _pos_kernel,
        out_shape=jax.ShapeDtypeStruct((k, t), I32),
        grid=(t // tile,),
        in_specs=[pl.BlockSpec((k, tile), lambda i: (0, i)), pl.BlockSpec((k, tile), lambda i: (0, i)),
                  _const_spec((ne, 1))],
        out_specs=pl.BlockSpec((k, tile), lambda i: (0, i)),
        compiler_params=_params(("arbitrary",)),
        name="positions",
    )(idx_t, rank_t, pad_start.reshape(ne, 1))


ZERO_BITS = (32, 16, 8, 4, 2, 1)


def _zero_copy(zbuf, xs_ref, sem, off, bit):
    rows = bit * SUBLANES
    return pltpu.make_async_copy(zbuf.at[pl.ds(0, rows)], xs_ref.at[pl.ds(pl.multiple_of(off, SUBLANES), rows)], sem)


def _zero_kernel(first_ref, units_ref, xs_ref, zbuf, sem):
    zbuf[...] = jnp.zeros(zbuf.shape, zbuf.dtype)
    ne = first_ref.shape[0]

    def each(e, start):
        off = first_ref[e]
        units = units_ref[e]
        for bit in ZERO_BITS:
            on = (units & bit) != 0

            @pl.when(on)
            def _():
                cp = _zero_copy(zbuf, xs_ref, sem, off, bit)
                cp.start() if start else cp.wait()
            off = off + jnp.where(on, bit * SUBLANES, 0)
        return start

    lax.fori_loop(0, ne, lambda e, c: (each(e, True), c)[1], 0)
    lax.fori_loop(0, ne, lambda e, c: (each(e, False), c)[1], 0)


def _zero_rows(first, units, n_rows, d):
    return pl.pallas_call(
        _zero_kernel,
        out_shape=jax.ShapeDtypeStruct((n_rows, d), F32),
        in_specs=[pl.BlockSpec(memory_space=pltpu.SMEM), pl.BlockSpec(memory_space=pltpu.SMEM)],
        out_specs=pl.BlockSpec(memory_space=pl.ANY),
        scratch_shapes=[pltpu.VMEM((ZERO_BITS[0] * SUBLANES, d), F32), pltpu.SemaphoreType.DMA],
        compiler_params=_params(),
        name="zero_rows",
    )(first, units)


def _row_copy(src_ref, dst_ref, sem, s, d):
    return pltpu.make_async_copy(src_ref.at[pl.ds(s, 1)], dst_ref.at[pl.ds(d, 1)], sem)


def _load_positions(pos_hbm, pos_smem, psem, tile_idx):
    per = pos_smem.shape[0]
    cp = pltpu.make_async_copy(pos_hbm.at[pl.ds(pl.multiple_of(tile_idx * per, per), per)], pos_smem, psem)
    cp.start()
    cp.wait()


def _dispatch_kernel(pos_hbm, u_ref, xs_in, xs_ref, pos_smem, psem, sem):
    del xs_in
    tm = u_ref.shape[0]
    _load_positions(pos_hbm, pos_smem, psem, pl.program_id(0))

    def issue(t, carry):
        for k in range(TOP_K):
            _row_copy(u_ref, xs_ref, sem, t, pos_smem[k * tm + t]).start(priority=k % DMA_THREADS)
        return carry

    lax.fori_loop(0, tm, issue, 0)
    for k in range(TOP_K):
        pltpu.make_async_copy(u_ref, xs_ref.at[pl.ds(0, tm)], sem).wait()


def _dispatch(pos_flat, u2, xs, tm):
    per = TOP_K * tm
    nt = pos_flat.shape[0] // per
    d = u2.shape[1]
    return pl.pallas_call(
        _dispatch_kernel,
        out_shape=jax.ShapeDtypeStruct(xs.shape, xs.dtype),
        grid=(nt,),
        in_specs=[pl.BlockSpec(memory_space=pl.ANY), pl.BlockSpec((tm, d), lambda i: (i, 0)),
                  pl.BlockSpec(memory_space=pl.ANY)],
        out_specs=pl.BlockSpec(memory_space=pl.ANY),
        scratch_shapes=[pltpu.SMEM((per,), I32), pltpu.SemaphoreType.DMA, pltpu.SemaphoreType.DMA],
        input_output_aliases={2: 0},
        compiler_params=_params(("arbitrary",)),
        name="dispatch",
    )(pos_flat, u2, xs)


def _expert_kernel(te_ref, nu_ref, xs_ref, wg_ref, wu_ref, wd_ref, y_ref):
    del te_ref

    @pl.when(pl.program_id(0) < nu_ref[0])
    def _():
        x = xs_ref[...].astype(BF16)
        g = jnp.dot(x, wg_ref[0].astype(BF16), preferred_element_type=F32)
        u = jnp.dot(x, wu_ref[0].astype(BF16), preferred_element_type=F32)
        h = (_silu(g) * u).astype(BF16)
        y_ref[...] = jnp.dot(h, wd_ref[0].astype(BF16), preferred_element_type=F32)


def _experts(tile_expert, n_used, xs, w_gate, w_up, w_down):
    n_rows, d = xs.shape
    de = w_gate.shape[2]
    n_tiles = n_rows // ROW_TILE
    row_map = lambda i, te, nu: (jnp.minimum(i, nu[0] - 1), 0)
    return pl.pallas_call(
        _expert_kernel,
        out_shape=jax.ShapeDtypeStruct((n_rows, d), F32),
        grid_spec=pltpu.PrefetchScalarGridSpec(
            num_scalar_prefetch=2, grid=(n_tiles,),
            in_specs=[pl.BlockSpec((ROW_TILE, d), row_map),
                      pl.BlockSpec((1, d, de), lambda i, te, nu: (te[i], 0, 0)),
                      pl.BlockSpec((1, d, de), lambda i, te, nu: (te[i], 0, 0)),
                      pl.BlockSpec((1, de, d), lambda i, te, nu: (te[i], 0, 0))],
            out_specs=pl.BlockSpec((ROW_TILE, d), row_map)),
        compiler_params=_params(("arbitrary",)),
        name="experts",
    )(tile_expert, n_used, xs, w_gate, w_up, w_down)


def _combine_kernel(pos_hbm, y_hbm, wtok_ref, u_ref, x1_ref, mod_ref, wsg_ref, wsu_ref, wsd_ref, gpost_ref,
                    o_ref, pos_smem, buf, psem, sem, *, per_row, rows_per_batch, tile_off):
    tm, d = u_ref.shape
    _load_positions(pos_hbm, pos_smem, psem, pl.program_id(0) + tile_off)

    def issue(t, carry):
        for k in range(TOP_K):
            _row_copy(y_hbm, buf.at[k], sem, pos_smem[k * tm + t], t).start(priority=k % DMA_THREADS)
        return carry

    lax.fori_loop(0, tm, issue, 0)

    ub = u_ref[...].astype(BF16)
    hs = _silu(jnp.dot(ub, wsg_ref[...], preferred_element_type=F32)) * jnp.dot(
        ub, wsu_ref[...], preferred_element_type=F32)
    f = jnp.dot(hs.astype(BF16), wsd_ref[...], preferred_element_type=F32)

    for k in range(TOP_K):
        pltpu.make_async_copy(y_hbm.at[pl.ds(0, tm)], buf.at[k], sem).wait()
    w = wtok_ref[...]
    for k in range(TOP_K):
        f = f + w[:, k:k + 1] * buf[k]
    (gate2,) = _mod_rows(mod_ref, per_row, rows_per_batch, tm, d)
    o_ref[...] = x1_ref[...] + gate2 * (_rms(f) * gpost_ref[...])


def _combine(pos_flat, y_rows, wtok, u2, x1, mod, consts, per_row, rows_per_batch, tile_off, n_tok, tm):
    per = TOP_K * tm
    d = u2.shape[1]
    ds_ = consts["w_sh_gate"].shape[1]
    tok_spec = lambda w: pl.BlockSpec((tm, w), lambda i: (i + tile_off, 0))
    mod_spec = _mod_spec(mod, per_row, tm)
    return pl.pallas_call(
        functools.partial(_combine_kernel, per_row=per_row, rows_per_batch=rows_per_batch, tile_off=tile_off),
        out_shape=jax.ShapeDtypeStruct((n_tok, d), F32),
        grid=(n_tok // tm,),
        in_specs=[pl.BlockSpec(memory_space=pl.ANY), pl.BlockSpec(memory_space=pl.ANY),
                  tok_spec(LANES), tok_spec(d), tok_spec(d), mod_spec,
                  _const_spec((d, ds_)), _const_spec((d, ds_)), _const_spec((ds_, d)), _const_spec((1, d))],
        out_specs=pl.BlockSpec((tm, d), lambda i: (i, 0)),
        scratch_shapes=[pltpu.SMEM((per,), I32), pltpu.VMEM((TOP_K, tm, d), F32),
                        pltpu.SemaphoreType.DMA, pltpu.SemaphoreType.DMA],
        compiler_params=_params(("arbitrary",)),
        name="combine",
    )(pos_flat, y_rows, wtok, u2, x1, mod, consts["w_sh_gate"], consts["w_sh_up"], consts["w_sh_down"],
      consts["g_post_ffn"])


def _repeat_rows(x, n):
    r, c = x.shape
    return jnp.broadcast_to(x[:, None, :], (r, n, c)).reshape(r * n, c)


def _tile_major(pos_t, tile):
    k, t = pos_t.shape
    return pos_t.reshape(k, t // tile, tile).transpose(1, 0, 2).reshape(-1)


def kernel(x_prompt, x_sample, c_prompt, c_sample, state_ssm, state_ssd_conv, state_short_conv, w_ada, b_ada, g_pre_mix, g_post_mix, g_pre_ffn, g_post_ffn, w_in, ssd_conv_w, ssd_conv_b, dt_bias, a_log, d_skip, g_ssd_norm, sc_conv_w, g_sc_norm, w_out, w_router, router_bias, w_exp_gate, w_exp_up, w_exp_down, w_sh_gate, w_sh_up, w_sh_down):
    depth = w_ada.shape[0]
    bp, seq, d = x_prompt.shape
    bs, dseq, _ = x_sample.shape
    heads = dt_bias.shape[1]
    ssd_w = heads * SSD_HEAD_DIM
    conv_dim = ssd_conv_w.shape[2]
    sc_w = sc_conv_w.shape[2]
    ne = w_router.shape[2]
    tp, ts = bp * seq, bs * dseq
    t_all = tp + ts

    head_of = jnp.arange(ssd_w, dtype=I32) // SSD_HEAD_DIM
    e_ind = (jnp.arange(LANES, dtype=I32)[:, None] == head_of[None, :])
    tril = jnp.tril(jnp.ones((CHUNK, CHUNK), F32))

    xp = x_prompt.reshape(tp, d)
    xs_pad = jnp.pad(x_sample, ((0, 0), (0, SUBLANES - dseq), (0, 0))).reshape(bs * SUBLANES, d)
    xs_tok = x_sample.reshape(ts, d)
    outs = {k: [] for k in ("ssm_p", "cst_p", "scst_p", "ssm_s", "cst_s", "scst_s")}

    for layer in range(depth):
        cuts = np.cumsum([0, ssd_w, conv_dim, heads, sc_w, sc_w, sc_w]).tolist()
        wi = w_in[layer]
        seg = lambda k: wi[:, cuts[k]:cuts[k + 1]]
        w_in_r = jnp.concatenate([seg(0), seg(1), seg(3), seg(4), seg(5),
                                  jnp.pad(seg(2), ((0, 0), (0, LANES - heads)))], axis=1).astype(BF16)
        widths = (ssd_w, conv_dim, sc_w, sc_w, sc_w, LANES)
        pad_h = lambda v: jnp.pad(v.reshape(1, heads), ((0, 0), (0, LANES - heads)))
        wr = w_router[layer].T
        wr_hi = wr.astype(BF16)
        consts = dict(
            heads=heads,
            conv_w=ssd_conv_w[layer], conv_b=ssd_conv_b[layer].reshape(1, conv_dim),
            dt_bias=pad_h(dt_bias[layer]), a_log=pad_h(a_log[layer]),
            d_skip_e=jnp.broadcast_to(d_skip[layer][:, None], (heads, SSD_HEAD_DIM)).reshape(1, ssd_w),
            g_ssd=g_ssd_norm[layer].reshape(1, ssd_w), sc_w=sc_conv_w[layer],
            g_sc=g_sc_norm[layer].reshape(1, sc_w),
            tril=tril, e=e_ind.astype(BF16), et=e_ind.T.astype(F32),
            w_out=w_out[layer].astype(BF16), g_post_mix=g_post_mix[layer].reshape(1, d),
            g_pre_ffn=g_pre_ffn[layer].reshape(1, d),
            wr_hi_t=wr_hi, wr_lo_t=(wr - wr_hi.astype(F32)).astype(BF16),
            w_sh_gate=w_sh_gate[layer].astype(BF16), w_sh_up=w_sh_up[layer].astype(BF16),
            w_sh_down=w_sh_down[layer].astype(BF16), g_post_ffn=g_post_ffn[layer].reshape(1, d))

        c_all = jnp.concatenate([c_prompt, c_sample], axis=0)
        m_rows = -(-c_all.shape[0] // 16) * 16
        mod = _ada(jnp.pad(c_all, ((0, m_rows - c_all.shape[0]), (0, 0))), w_ada[layer], b_ada[layer])
        mod_p = mod[:bp]
        mod_s = mod[bp:bp + bs]
        mod_s_pad = _repeat_rows(mod_s[:, :2 * d], SUBLANES)
        mod_s_tok = _repeat_rows(mod_s, dseq)

        proj_p = _in_proj(xp, mod_p[:, :2 * d], g_pre_mix[layer], w_in_r, widths, BF16, False, seq, TOK_TILE)
        proj_s = _in_proj(xs_pad, mod_s_pad, g_pre_mix[layer], w_in_r, widths, F32, True, 1, TOK_TILE)
        ymix_p, ssm_p, cst_p, scst_p = _ssd(proj_p, None, consts, bp, seq // CHUNK, CHUNK, CHUNK, BF16)
        init = (state_ssm[layer].reshape(bs, ssd_w, D_STATE), state_ssd_conv[layer], state_short_conv[layer])
        ymix_s, ssm_s, cst_s, scst_s = _ssd(proj_s, init, consts, bs, 1, SUBLANES, dseq, F32)
        ymix_s = ymix_s.reshape(bs, SUBLANES, ssd_w + sc_w)[:, :dseq].reshape(ts, ssd_w + sc_w)

        merged = _out_proj(ymix_p, xp, mod_p[:, 2 * d:5 * d], consts, False, seq, TOK_TILE, t_all, 0, None)
        x1, u2, logits_t = _out_proj(ymix_s, xs_tok, mod_s_tok[:, 2 * d:5 * d], consts, True, 1, TOK_TILE,
                                     t_all, tp // TOK_TILE, merged)

        idx_t, rank_t, wtok, counts = _route(logits_t, router_bias[layer], TOK_TILE)
        counts = counts[:, 0]
        padded = (counts + ROW_TILE - 1) // ROW_TILE * ROW_TILE
        pad_end = jnp.cumsum(padded)
        pad_start = pad_end - padded
        n_tiles = t_all * TOP_K // ROW_TILE + ne
        n_used = (pad_end[-1] // ROW_TILE).astype(I32)
        tile_ids = jnp.minimum(jnp.arange(n_tiles, dtype=I32), n_used - 1)
        tile_expert = jnp.minimum(jnp.sum(pad_end[None, :] <= (tile_ids * ROW_TILE)[:, None], axis=1), ne - 1).astype(I32)
        first_pad = (pad_start + counts) // SUBLANES * SUBLANES
        units = (pad_end - first_pad) // SUBLANES
        pos_t = _positions(idx_t, rank_t, pad_start.astype(I32), TOK_TILE)
        pos_flat = _tile_major(pos_t, GATHER_TILE)

        xs_rows = _zero_rows(first_pad.astype(I32), units.astype(I32), n_tiles * ROW_TILE, d)
        xs_rows = _dispatch(pos_flat, u2, xs_rows, GATHER_TILE)
        y_rows = _experts(tile_expert, n_used.reshape(1), xs_rows, w_exp_gate[layer], w_exp_up[layer],
                          w_exp_down[layer])
        xp = _combine(pos_flat, y_rows, wtok, u2, x1, mod_p[:, 5 * d:], consts, False, seq, 0, tp, GATHER_TILE)
        xs_tok = _combine(pos_flat, y_rows, wtok, u2, x1, mod_s_tok[:, 5 * d:], consts, True, 1,
                          tp // GATHER_TILE, ts, GATHER_TILE)
        xs_pad = jnp.pad(xs_tok.reshape(bs, dseq, d), ((0, 0), (0, SUBLANES - dseq), (0, 0))).reshape(
            bs * SUBLANES, d)

        outs["ssm_p"].append(ssm_p.reshape(bp, heads, SSD_HEAD_DIM, D_STATE))
        outs["cst_p"].append(cst_p)
        outs["scst_p"].append(scst_p)
        outs["ssm_s"].append(ssm_s.reshape(bs, heads, SSD_HEAD_DIM, D_STATE))
        outs["cst_s"].append(cst_s)
        outs["scst_s"].append(scst_s)

    return (xp.reshape(bp, seq, d), xs_tok.reshape(bs, dseq, d),
            jnp.stack(outs["ssm_p"]), jnp.stack(outs["cst_p"]), jnp.stack(outs["scst_p"]),
            jnp.stack(outs["ssm_s"]), jnp.stack(outs["cst_s"]), jnp.stack(outs["scst_s"]))
```

```python
import functools

import jax
import jax.numpy as jnp
import numpy as np
from jax import lax
from jax.experimental import pallas as pl
from jax.experimental.pallas import tpu as pltpu

F32 = jnp.float32
BF16 = jnp.bfloat16
I32 = jnp.int32
U32 = jnp.uint32
HI = lax.Precision.HIGHEST

SSD_HEAD_DIM = 64
SSD_GROUPS = 2
D_STATE = 128
SSD_CONV_W = 4
SC_GROUPS = 16
SC_CONV_W = 3
TOP_K = 8
N_EXPERT_GROUPS = 8
TOPK_GROUPS = 4
ROUTED_SCALE = 2.5
EPS = 1e-6

LANES = 128
SUBLANES = 8
CHUNK = 128
SAMPLE_CHUNK = 16
TOK_TILE = 512
ROW_TILE = 192
GATHER_TILE = 256
VMEM_LIMIT = 56 * 1024 * 1024
DMA_THREADS = 2

NT = (((1,), (1,)), ((), ()))
TN = (((0,), (0,)), ((), ()))


def _sigmoid(x):
    return 1.0 / (1.0 + jnp.exp(-x))


def _silu(x):
    return x * _sigmoid(x)


def _softplus(x):
    return jnp.maximum(x, 0.0) + jnp.log1p(jnp.exp(-jnp.abs(x)))


def _rms(x, eps=EPS):
    return x * lax.rsqrt(jnp.mean(x * x, axis=-1, keepdims=True) + eps)


def _params(sem=None):
    return pltpu.CompilerParams(dimension_semantics=sem, vmem_limit_bytes=VMEM_LIMIT)


def _const_spec(shape, single=False):
    nd = len(shape)
    mode = dict(pipeline_mode=pl.Buffered(1)) if single else {}
    return pl.BlockSpec(shape, lambda *_: (0,) * nd, **mode)


def _ada_kernel(c_ref, w_ref, b_ref, o_ref):
    c = c_ref[...]
    s = _silu(c).astype(BF16)
    o_ref[...] = jnp.dot(s, w_ref[...].astype(BF16), preferred_element_type=F32) + b_ref[...]


def _ada(c, w_ada, b_ada):
    m, d = c.shape
    n = w_ada.shape[1]
    tn = 512
    return pl.pallas_call(
        _ada_kernel,
        out_shape=jax.ShapeDtypeStruct((m, n), F32),
        grid=(n // tn,),
        in_specs=[_const_spec((m, d)),
                  pl.BlockSpec((d, tn), lambda j: (0, j)),
                  pl.BlockSpec((1, tn), lambda j: (0, j))],
        out_specs=pl.BlockSpec((m, tn), lambda j: (0, j)),
        compiler_params=_params(("arbitrary",)),
        name="ada",
    )(c, w_ada, b_ada.reshape(1, n))


def _mod_rows(mod_ref, per_row, rows_per_batch, tile, d):
    n = mod_ref.shape[1] // d
    if per_row:
        return [mod_ref[:, k * d:(k + 1) * d] for k in range(n)]
    b = (pl.program_id(0) * tile) // rows_per_batch
    return [mod_ref[pl.ds(b, 1), k * d:(k + 1) * d] for k in range(n)]


def _mod_spec(mod, per_row, tile):
    if per_row:
        return pl.BlockSpec((tile, mod.shape[1]), lambda i: (i, 0))
    return _const_spec(mod.shape)


def _in_kernel(x_ref, mod_ref, g_ref, w_ref, *out_refs, per_row, rows_per_batch, widths):
    tile, d = x_ref.shape
    shift, scale = _mod_rows(mod_ref, per_row, rows_per_batch, tile, d)
    u = (_rms(x_ref[...]) * g_ref[...]) * (1.0 + scale) + shift
    u = u.astype(BF16)
    col = 0
    for ref, width in zip(out_refs, widths):
        for a in range(0, width, 512):
            bw = min(512, width - a)
            r = jnp.dot(u, w_ref[:, col + a:col + a + bw], preferred_element_type=F32)
            ref[:, a:a + bw] = r.astype(ref.dtype)
        col += width


def _in_proj(x, mod, g, w_bf16, widths, out_dtype, per_row, rows_per_batch, tile):
    t, d = x.shape
    n = w_bf16.shape[1]
    mod_spec = _mod_spec(mod, per_row, tile)
    dts = [out_dtype] * (len(widths) - 1) + [F32]
    return pl.pallas_call(
        functools.partial(_in_kernel, per_row=per_row, rows_per_batch=rows_per_batch, widths=widths),
        out_shape=[jax.ShapeDtypeStruct((t, wd), dt) for wd, dt in zip(widths, dts)],
        grid=(t // tile,),
        in_specs=[pl.BlockSpec((tile, d), lambda i: (i, 0)), mod_spec,
                  _const_spec((1, d)), _const_spec((d, n), single=True)],
        out_specs=[pl.BlockSpec((tile, wd), lambda i: (i, 0)) for wd in widths],
        compiler_params=_params(("arbitrary",)),
        name="in_proj",
    )(x, mod, g.reshape(1, d), w_bf16)


def _ssd_kernel(*refs, q, rows_in, q_valid, has_init, heads):
    it = iter(refs)
    z_ref, xbc_ref, scb_ref, scc_ref, sch_ref, dt_ref = (next(it) for _ in range(6))
    if has_init:
        ssm0_ref, cst0_ref, scst0_ref = (next(it) for _ in range(3))
    (cw_ref, cb_ref, dtb_ref, alog_ref, dsk_ref, gssd_ref, scw_ref, gsc_ref,
     tril_ref, e_ref, et_ref) = (next(it) for _ in range(11))
    ymix_ref, ssm_ref, cst_ref, scst_ref = (next(it) for _ in range(4))
    h_scr, ext_scr, extv_scr = (next(it) for _ in range(3))

    c = pl.program_id(1)
    nc = pl.num_programs(1)
    ssd_w = dsk_ref.shape[1]
    gw = ssd_w // SSD_GROUPS
    hpg = heads // SSD_GROUPS
    n_state = D_STATE
    head0 = SUBLANES - (SSD_CONV_W - 1)
    headv = SUBLANES - (SC_CONV_W - 1)

    @pl.when(c == 0)
    def _():
        if has_init:
            h_scr[...] = ssm0_ref[0]
            ext_scr[head0:SUBLANES, :] = cst0_ref[0]
            extv_scr[headv:SUBLANES, :] = scst0_ref[0]
        else:
            h_scr[...] = jnp.zeros(h_scr.shape, F32)
            ext_scr[0:SUBLANES, :] = jnp.zeros((SUBLANES, ext_scr.shape[1]), F32)
            extv_scr[0:SUBLANES, :] = jnp.zeros((SUBLANES, extv_scr.shape[1]), F32)

    def pad_rows(v, n):
        if v.shape[0] == n:
            return v
        return jnp.concatenate([v, jnp.zeros((n - v.shape[0], v.shape[1]), v.dtype)], axis=0)

    load = lambda ref: pad_rows(ref[...].astype(F32), q)
    pad_t = lambda v: pad_rows(v, LANES)
    z, scb, scc, sch, dt_raw = load(z_ref), load(scb_ref), load(scc_ref), load(sch_ref), load(dt_ref)

    ext_scr[SUBLANES:SUBLANES + q, :] = load(xbc_ref)
    conv = cb_ref[...]
    for k in range(SSD_CONV_W):
        conv = conv + cw_ref[k:k + 1, :] * ext_scr[head0 + k:head0 + k + q, :]
    xc = _silu(conv)
    xs = xc[:, :ssd_w]
    bm = xc[:, ssd_w:ssd_w + SSD_GROUPS * n_state]
    cm = xc[:, ssd_w + SSD_GROUPS * n_state:]

    lane = lax.broadcasted_iota(I32, (q, LANES), 1)
    row = lax.broadcasted_iota(I32, (q, LANES), 0)
    dt = _softplus(dt_raw + dtb_ref[...])
    dt = jnp.where(jnp.logical_and(lane < heads, row < q_valid), dt, 0.0)
    a = dt * (-jnp.exp(alog_ref[...]))
    acum = jnp.dot(tril_ref[...], pad_t(a), precision=HI, preferred_element_type=F32)
    acum_t = pad_t(acum).T
    a_last = acum[q - 1:q, :]
    e = e_ref[...]
    expand = lambda v: jnp.dot(v.astype(BF16), e, preferred_element_type=F32)
    dt_e = expand(dt)
    dtdte_e = expand(dt * jnp.exp(a_last - acum))
    exa_e = expand(jnp.exp(acum))
    xdt = xs * dt_e
    xw_b = (xs * dtdte_e).astype(BF16)
    dlast = jnp.broadcast_to(jnp.exp(acum_t[:, q - 1:q]), (LANES, n_state))
    dcol = jnp.dot(et_ref[...], dlast, precision=HI, preferred_element_type=F32)

    tri = row >= lane
    xdt_t = pad_t(xdt)
    xw_t = pad_t(xw_b)
    lane_t = lax.broadcasted_iota(I32, (LANES, LANES), 1)
    y_groups = []
    for g in range(SSD_GROUPS):
        bm_g = bm[:, g * n_state:(g + 1) * n_state].astype(BF16)
        cm_g = cm[:, g * n_state:(g + 1) * n_state].astype(BF16)
        bm_t = pad_t(bm_g)
        cb = lax.dot_general(cm_g, bm_t, NT, preferred_element_type=F32)
        h_g = h_scr[g * gw:(g + 1) * gw, :]
        y_off = lax.dot_general(cm_g, h_g.astype(BF16), NT, preferred_element_type=F32)
        parts = []
        for pair in range(hpg // 2):
            lo = (g * hpg + 2 * pair) * SSD_HEAD_DIM
            x_pair = xdt_t[:, lo:lo + LANES]
            acc = None
            for half in range(2):
                h = g * hpg + 2 * pair + half
                ci = jnp.broadcast_to(acum[:, h:h + 1], (q, LANES))
                rj = jnp.broadcast_to(acum_t[h:h + 1, :], (q, LANES))
                dec = jnp.where(tri, jnp.exp(ci - rj), 0.0)
                m = (cb * dec).astype(BF16)
                own = (lane_t >= SSD_HEAD_DIM) if half else (lane_t < SSD_HEAD_DIM)
                y_h = jnp.dot(m, jnp.where(own, x_pair, 0.0).astype(BF16), preferred_element_type=F32)
                acc = y_h if acc is None else acc + y_h
            parts.append(acc)
        y_diag = jnp.concatenate(parts, axis=1)
        y_groups.append(y_diag + y_off * exa_e[:, g * gw:(g + 1) * gw])
        upd = lax.dot_general(xw_t[:, g * gw:(g + 1) * gw], bm_t, TN, preferred_element_type=F32)
        h_scr[g * gw:(g + 1) * gw, :] = h_g * dcol[g * gw:(g + 1) * gw, :] + upd

    y = (jnp.concatenate(y_groups, axis=1) + xs * dsk_ref[...]) * _silu(z)
    y = jnp.concatenate([_rms(y[:, g * gw:(g + 1) * gw]) for g in range(SSD_GROUPS)], axis=1)
    y_ssd = y * gssd_ref[...]

    extv_scr[SUBLANES:SUBLANES + q, :] = scc * sch
    cv = scw_ref[0:1, :] * extv_scr[headv:headv + q, :]
    for k in range(1, SC_CONV_W):
        cv = cv + scw_ref[k:k + 1, :] * extv_scr[headv + k:headv + k + q, :]
    t = scb * cv
    sc_per_group = t.shape[1] // SC_GROUPS
    gsum = jnp.dot((t * t).astype(BF16), et_ref[...].astype(BF16), preferred_element_type=F32)
    rs = lax.rsqrt(gsum * (1.0 / sc_per_group) + EPS)
    y_sc = t * expand(rs) * gsc_ref[...]

    ymix_ref[:, :ssd_w] = y_ssd[0:rows_in].astype(ymix_ref.dtype)
    ymix_ref[:, ssd_w:] = y_sc[0:rows_in].astype(ymix_ref.dtype)

    @pl.when(c == nc - 1)
    def _():
        ssm_ref[0] = h_scr[...]
        cst_ref[0] = ext_scr[SUBLANES + q_valid - (SSD_CONV_W - 1):SUBLANES + q_valid, :]
        scst_ref[0] = extv_scr[SUBLANES + q_valid - (SC_CONV_W - 1):SUBLANES + q_valid, :]

    ext_scr[0:SUBLANES, :] = ext_scr[q:q + SUBLANES, :]
    extv_scr[0:SUBLANES, :] = extv_scr[q:q + SUBLANES, :]


def _ssd(proj, init, consts, nb, nc, q, rows_in, q_valid, ymix_dtype):
    z, xbc, scb, scc, sch, dt = proj
    heads = consts["heads"]
    tril = jnp.tril(jnp.ones((q, LANES), F32))
    ssd_w, conv_dim, sc_w = z.shape[1], xbc.shape[1], scb.shape[1]
    has_init = init is not None
    row_spec = lambda w: pl.BlockSpec((rows_in, w), lambda b, c: (b * nc + c, 0))
    in_specs = [row_spec(ssd_w), row_spec(conv_dim), row_spec(sc_w), row_spec(sc_w), row_spec(sc_w),
                row_spec(LANES)]
    args = [z, xbc, scb, scc, sch, dt]
    if has_init:
        ssm0, cst0, scst0 = init
        in_specs += [pl.BlockSpec((1,) + ssm0.shape[1:], lambda b, c: (b, 0, 0)),
                     pl.BlockSpec((1,) + cst0.shape[1:], lambda b, c: (b, 0, 0)),
                     pl.BlockSpec((1,) + scst0.shape[1:], lambda b, c: (b, 0, 0))]
        args += [ssm0, cst0, scst0]
    weights = [consts[k] for k in ("conv_w", "conv_b", "dt_bias", "a_log", "d_skip_e", "g_ssd",
                                   "sc_w", "g_sc")] + [tril, consts["e"], consts["et"]]
    in_specs += [_const_spec(w.shape) for w in weights]
    args += weights
    n_state = D_STATE
    out_shape = [jax.ShapeDtypeStruct((nb * nc * rows_in, ssd_w + sc_w), ymix_dtype),
                 jax.ShapeDtypeStruct((nb, ssd_w, n_state), F32),
                 jax.ShapeDtypeStruct((nb, SSD_CONV_W - 1, conv_dim), F32),
                 jax.ShapeDtypeStruct((nb, SC_CONV_W - 1, sc_w), F32)]
    out_specs = [pl.BlockSpec((rows_in, ssd_w + sc_w), lambda b, c: (b * nc + c, 0)),
                 pl.BlockSpec((1, ssd_w, n_state), lambda b, c: (b, 0, 0)),
                 pl.BlockSpec((1, SSD_CONV_W - 1, conv_dim), lambda b, c: (b, 0, 0)),
                 pl.BlockSpec((1, SC_CONV_W - 1, sc_w), lambda b, c: (b, 0, 0))]
    scratch = [pltpu.VMEM((ssd_w, n_state), F32),
               pltpu.VMEM((q + SUBLANES, conv_dim), F32),
               pltpu.VMEM((q + SUBLANES, sc_w), F32)]
    return pl.pallas_call(
        functools.partial(_ssd_kernel, q=q, rows_in=rows_in, q_valid=q_valid, has_init=has_init, heads=heads),
        out_shape=out_shape, grid=(nb, nc), in_specs=in_specs, out_specs=out_specs,
        scratch_shapes=scratch,
        compiler_params=_params(("arbitrary", "arbitrary")),
        name="ssd",
    )(*args)


def _out_kernel(ymix_ref, x_ref, mod_ref, wout_ref, gpost_ref, gpre_ref, wr_hi_ref, wr_lo_ref,
                *rest, per_row, rows_per_batch, has_alias):
    x1_ref, u2_ref, lg_ref = rest[3:6] if has_alias else rest[0:3]
    tile, d = x_ref.shape
    gate1, shift2, scale2 = _mod_rows(mod_ref, per_row, rows_per_batch, tile, d)
    m = jnp.dot(ymix_ref[...].astype(BF16), wout_ref[...], preferred_element_type=F32)
    x1 = x_ref[...] + gate1 * (_rms(m) * gpost_ref[...])
    u2 = (_rms(x1) * gpre_ref[...]) * (1.0 + scale2) + shift2
    x1_ref[...] = x1
    u2_ref[...] = u2
    u_hi = u2.astype(BF16)
    u_lo = (u2 - u_hi.astype(F32)).astype(BF16)
    lg = lax.dot_general(wr_hi_ref[...], u_hi, NT, preferred_element_type=F32)
    lg = lg + lax.dot_general(wr_hi_ref[...], u_lo, NT, preferred_element_type=F32)
    lg = lg + lax.dot_general(wr_lo_ref[...], u_hi, NT, preferred_element_type=F32)
    lg_ref[...] = lg


def _out_proj(ymix, x, mod, consts, per_row, rows_per_batch, tile, t_total, tile_off, prev):
    t, d = x.shape
    dm = ymix.shape[1]
    ne = consts["wr_hi_t"].shape[0]
    in_specs = [pl.BlockSpec((tile, dm), lambda i: (i, 0)),
                pl.BlockSpec((tile, d), lambda i: (i, 0)), _mod_spec(mod, per_row, tile),
                _const_spec((dm, d)), _const_spec((1, d)), _const_spec((1, d)),
                _const_spec((ne, d)), _const_spec((ne, d))]
    args = [ymix, x, mod, consts["w_out"], consts["g_post_mix"], consts["g_pre_ffn"],
            consts["wr_hi_t"], consts["wr_lo_t"]]
    aliases = {}
    if prev is not None:
        in_specs += [pl.BlockSpec(memory_space=pl.ANY)] * 3
        aliases = {len(args) + k: k for k in range(3)}
        args += list(prev)
    return pl.pallas_call(
        functools.partial(_out_kernel, per_row=per_row, rows_per_batch=rows_per_batch,
                          has_alias=prev is not None),
        out_shape=[jax.ShapeDtypeStruct((t_total, d), F32), jax.ShapeDtypeStruct((t_total, d), F32),
                   jax.ShapeDtypeStruct((ne, t_total), F32)],
        grid=(t // tile,), in_specs=in_specs,
        out_specs=[pl.BlockSpec((tile, d), lambda i: (i + tile_off, 0)),
                   pl.BlockSpec((tile, d), lambda i: (i + tile_off, 0)),
                   pl.BlockSpec((ne, tile), lambda i: (0, i + tile_off))],
        input_output_aliases=aliases,
        compiler_params=_params(("arbitrary",)),
        name="out_proj",
    )(*args)


def _route_kernel(lg_ref, bias_ref, upper_ref, idx_ref, rank_ref, wtok_ref, cnt_ref, carry_scr):
    i = pl.program_id(0)
    ne, tm = lg_ref.shape
    per_group = ne // N_EXPERT_GROUPS
    neg = -jnp.inf

    @pl.when(i == 0)
    def _():
        carry_scr[...] = jnp.zeros(carry_scr.shape, F32)

    s = _sigmoid(lg_ref[...])
    biased = s + bias_ref[...]
    gl = []
    io_g = lax.broadcasted_iota(I32, (per_group, tm), 0).astype(F32)
    for g in range(N_EXPERT_GROUPS):
        blk = biased[g * per_group:(g + 1) * per_group, :]
        m1 = jnp.max(blk, axis=0, keepdims=True)
        f1 = jnp.min(jnp.where(blk == m1, io_g, float(per_group)), axis=0, keepdims=True)
        m2 = jnp.max(jnp.where(io_g == f1, neg, blk), axis=0, keepdims=True)
        gl.append(m1 + m2)
    gscore = jnp.concatenate(gl, axis=0)
    io8 = lax.broadcasted_iota(I32, (N_EXPERT_GROUPS, tm), 0).astype(F32)
    gsel = jnp.zeros((N_EXPERT_GROUPS, tm), F32)
    for _ in range(TOPK_GROUPS):
        m = jnp.max(gscore, axis=0, keepdims=True)
        f = jnp.min(jnp.where(gscore == m, io8, float(N_EXPERT_GROUPS)), axis=0, keepdims=True)
        hit = io8 == f
        gsel = jnp.where(hit, 1.0, gsel)
        gscore = jnp.where(hit, neg, gscore)
    emask = jnp.concatenate(
        [jnp.broadcast_to(gsel[g:g + 1, :], (per_group, tm)) for g in range(N_EXPERT_GROUPS)], axis=0)
    cand = jnp.where(emask > 0.5, biased, neg)
    io_e = lax.broadcasted_iota(I32, (ne, tm), 0).astype(F32)
    msel = jnp.zeros((ne, tm), F32)
    idxs, wts = [], []
    for _ in range(TOP_K):
        m = jnp.max(cand, axis=0, keepdims=True)
        f = jnp.min(jnp.where(cand == m, io_e, float(ne)), axis=0, keepdims=True)
        hit = io_e == f
        wts.append(jnp.sum(jnp.where(hit, s, 0.0), axis=0, keepdims=True))
        idxs.append(f)
        msel = jnp.where(hit, 1.0, msel)
        cand = jnp.where(hit, neg, cand)
    pref = jnp.dot(msel.astype(BF16), upper_ref[...], preferred_element_type=F32) + carry_scr[:, 0:1]
    ranks = [jnp.sum(jnp.where(io_e == f, pref, 0.0), axis=0, keepdims=True) for f in idxs]
    carry_scr[...] = carry_scr[...] + jnp.sum(msel, axis=1, keepdims=True)
    cnt_ref[...] = carry_scr[...].astype(I32)
    idx_ref[...] = jnp.concatenate(idxs, axis=0).astype(I32)
    rank_ref[...] = jnp.concatenate(ranks, axis=0).astype(I32)
    wsum = wts[0]
    for w in wts[1:]:
        wsum = wsum + w
    wn = jnp.concatenate([w / wsum * ROUTED_SCALE for w in wts]
                         + [jnp.zeros((LANES - TOP_K, tm), F32)], axis=0)
    for j in range(tm // LANES):
        wtok_ref[j * LANES:(j + 1) * LANES, :] = wn[:, j * LANES:(j + 1) * LANES].T


def _route(logits_t, bias, tile):
    ne, t = logits_t.shape
    upper = jnp.triu(jnp.ones((tile, tile), F32), 1).astype(BF16)
    return pl.pallas_call(
        _route_kernel,
        out_shape=[jax.ShapeDtypeStruct((TOP_K, t), I32), jax.ShapeDtypeStruct((TOP_K, t), I32),
                   jax.ShapeDtypeStruct((t, LANES), F32), jax.ShapeDtypeStruct((ne, LANES), I32)],
        grid=(t // tile,),
        in_specs=[pl.BlockSpec((ne, tile), lambda i: (0, i)), _const_spec((ne, 1)),
                  _const_spec((tile, tile))],
        out_specs=[pl.BlockSpec((TOP_K, tile), lambda i: (0, i)),
                   pl.BlockSpec((TOP_K, tile), lambda i: (0, i)),
                   pl.BlockSpec((tile, LANES), lambda i: (i, 0)),
                   _const_spec((ne, LANES))],
        scratch_shapes=[pltpu.VMEM((ne, LANES), F32)],
        compiler_params=_params(("arbitrary",)),
        name="route",
    )(logits_t, bias.reshape(ne, 1), upper)


def _pos_kernel(idx_ref, rank_ref, start_ref, pos_ref):
    ne = start_ref.shape[0]
    tm = idx_ref.shape[1]
    io_e = lax.broadcasted_iota(I32, (ne, tm), 0)
    start = start_ref[...].astype(F32)
    rows = []
    for k in range(TOP_K):
        hit = io_e == idx_ref[k:k + 1, :]
        rows.append(jnp.sum(jnp.where(hit, start, 0.0), axis=0, keepdims=True))
    pos_ref[...] = jnp.concatenate(rows, axis=0).astype(I32) + rank_ref[...]


def _positions(idx_t, rank_t, pad_start, tile):
    k, t = idx_t.shape
    ne = pad_start.shape[0]
    return pl.pallas_call(
        _pos_kernel,
        out_shape=jax.ShapeDtypeStruct((k, t), I32),
        grid=(t // tile,),
        in_specs=[pl.BlockSpec((k, tile), lambda i: (0, i)), pl.BlockSpec((k, tile), lambda i: (0, i)),
                  _const_spec((ne, 1))],
        out_specs=pl.BlockSpec((k, tile), lambda i: (0, i)),
        compiler_params=_params(("arbitrary",)),
        name="positions",
    )(idx_t, rank_t, pad_start.reshape(ne, 1))


ZERO_BITS = (32, 16, 8, 4, 2, 1)


def _zero_copy(zbuf, xs_ref, sem, off, bit):
    rows = bit * SUBLANES
    return pltpu.make_async_copy(zbuf.at[pl.ds(0, rows)], xs_ref.at[pl.ds(pl.multiple_of(off, SUBLANES), rows)], sem)


def _zero_kernel(first_ref, units_ref, xs_ref, zbuf, sem):
    zbuf[...] = jnp.zeros(zbuf.shape, zbuf.dtype)
    ne = first_ref.shape[0]

    def each(e, start):
        off = first_ref[e]
        units = units_ref[e]
        for bit in ZERO_BITS:
            on = (units & bit) != 0

            @pl.when(on)
            def _():
                cp = _zero_copy(zbuf, xs_ref, sem, off, bit)
                cp.start() if start else cp.wait()
            off = off + jnp.where(on, bit * SUBLANES, 0)
        return start

    lax.fori_loop(0, ne, lambda e, c: (each(e, True), c)[1], 0)
    lax.fori_loop(0, ne, lambda e, c: (each(e, False), c)[1], 0)


def _zero_rows(first, units, n_rows, width):
    return pl.pallas_call(
        _zero_kernel,
        out_shape=jax.ShapeDtypeStruct((n_rows, width), F32),
        in_specs=[pl.BlockSpec(memory_space=pltpu.SMEM), pl.BlockSpec(memory_space=pltpu.SMEM)],
        out_specs=pl.BlockSpec(memory_space=pl.ANY),
        scratch_shapes=[pltpu.VMEM((ZERO_BITS[0] * SUBLANES, width), F32), pltpu.SemaphoreType.DMA],
        compiler_params=_params(),
        name="zero_rows",
    )(first, units)


def _row_copy(src_ref, dst_ref, sem, s, d):
    return pltpu.make_async_copy(src_ref.at[pl.ds(s, 1)], dst_ref.at[pl.ds(d, 1)], sem)


def _load_positions(pos_hbm, pos_smem, psem, tile_idx):
    per = pos_smem.shape[0]
    cp = pltpu.make_async_copy(pos_hbm.at[pl.ds(pl.multiple_of(tile_idx * per, per), per)], pos_smem, psem)
    cp.start()
    cp.wait()


def _dispatch_kernel(pos_hbm, u_ref, xs_in, xs_ref, pos_smem, psem, sem):
    del xs_in
    tm = u_ref.shape[0]
    _load_positions(pos_hbm, pos_smem, psem, pl.program_id(0))

    def issue(t, carry):
        for k in range(TOP_K):
            _row_copy(u_ref, xs_ref, sem, t, pos_smem[k * tm + t]).start(priority=k % DMA_THREADS)
        return carry

    lax.fori_loop(0, tm, issue, 0)
    for k in range(TOP_K):
        pltpu.make_async_copy(u_ref, xs_ref.at[pl.ds(0, tm)], sem).wait()


def _dispatch(pos_flat, u2, xs, tm):
    per = TOP_K * tm
    nt = pos_flat.shape[0] // per
    d = u2.shape[1]
    return pl.pallas_call(
        _dispatch_kernel,
        out_shape=jax.ShapeDtypeStruct(xs.shape, xs.dtype),
        grid=(nt,),
        in_specs=[pl.BlockSpec(memory_space=pl.ANY), pl.BlockSpec((tm, d), lambda i: (i, 0)),
                  pl.BlockSpec(memory_space=pl.ANY)],
        out_specs=pl.BlockSpec(memory_space=pl.ANY),
        scratch_shapes=[pltpu.SMEM((per,), I32), pltpu.SemaphoreType.DMA, pltpu.SemaphoreType.DMA],
        input_output_aliases={2: 0},
        compiler_params=_params(("arbitrary",)),
        name="dispatch",
    )(pos_flat, u2, xs)


def _expert_kernel(te_ref, nu_ref, xs_ref, wg_ref, wu_ref, wd_ref, y_ref):
    del te_ref

    @pl.when(pl.program_id(0) < nu_ref[0])
    def _():
        x = xs_ref[...].astype(BF16)
        g = jnp.dot(x, wg_ref[0].astype(BF16), preferred_element_type=F32)
        u = jnp.dot(x, wu_ref[0].astype(BF16), preferred_element_type=F32)
        h = (_silu(g) * u).astype(BF16)
        y_ref[...] = jnp.dot(h, wd_ref[0].astype(BF16), preferred_element_type=F32)


def _experts(tile_expert, n_used, xs, w_gate, w_up, w_down):
    n_rows, d = xs.shape
    de = w_gate.shape[2]
    n_tiles = n_rows // ROW_TILE
    row_map = lambda i, te, nu: (jnp.minimum(i, nu[0] - 1), 0)
    return pl.pallas_call(
        _expert_kernel,
        out_shape=jax.ShapeDtypeStruct((n_rows, d), F32),
        grid_spec=pltpu.PrefetchScalarGridSpec(
            num_scalar_prefetch=2, grid=(n_tiles,),
            in_specs=[pl.BlockSpec((ROW_TILE, d), row_map),
                      pl.BlockSpec((1, d, de), lambda i, te, nu: (te[i], 0, 0)),
                      pl.BlockSpec((1, d, de), lambda i, te, nu: (te[i], 0, 0)),
                      pl.BlockSpec((1, de, d), lambda i, te, nu: (te[i], 0, 0))],
            out_specs=pl.BlockSpec((ROW_TILE, d), row_map)),
        compiler_params=_params(("arbitrary",)),
        name="experts",
    )(tile_expert, n_used, xs, w_gate, w_up, w_down)


def _combine_kernel(pos_hbm, y_hbm, wtok_ref, u_ref, x1_ref, mod_ref, wsg_ref, wsu_ref, wsd_ref, gpost_ref,
                    o_ref, pos_smem, buf, psem, sem, *, per_row, rows_per_batch, tile_off):
    tm, d = u_ref.shape
    _load_positions(pos_hbm, pos_smem, psem, pl.program_id(0) + tile_off)

    def issue(t, carry):
        for k in range(TOP_K):
            _row_copy(y_hbm, buf.at[k], sem, pos_smem[k * tm + t], t).start(priority=k % DMA_THREADS)
        return carry

    lax.fori_loop(0, tm, issue, 0)

    ub = u_ref[...].astype(BF16)
    hs = _silu(jnp.dot(ub, wsg_ref[...], preferred_element_type=F32)) * jnp.dot(
        ub, wsu_ref[...], preferred_element_type=F32)
    f = jnp.dot(hs.astype(BF16), wsd_ref[...], preferred_element_type=F32)

    for k in range(TOP_K):
        pltpu.make_async_copy(y_hbm.at[pl.ds(0, tm)], buf.at[k], sem).wait()
    w = wtok_ref[...]
    for k in range(TOP_K):
        f = f + w[:, k:k + 1] * buf[k]
    (gate2,) = _mod_rows(mod_ref, per_row, rows_per_batch, tm, d)
    o_ref[...] = x1_ref[...] + gate2 * (_rms(f) * gpost_ref[...])


def _combine(pos_flat, y_rows, wtok, u2, x1, mod, consts, per_row, rows_per_batch, tile_off, n_tok, tm):
    per = TOP_K * tm
    d = u2.shape[1]
    ds_ = consts["w_sh_gate"].shape[1]
    tok_spec = lambda w: pl.BlockSpec((tm, w), lambda i: (i + tile_off, 0))
    mod_spec = _mod_spec(mod, per_row, tm)
    return pl.pallas_call(
        functools.partial(_combine_kernel, per_row=per_row, rows_per_batch=rows_per_batch, tile_off=tile_off),
        out_shape=jax.ShapeDtypeStruct((n_tok, d), F32),
        grid=(n_tok // tm,),
        in_specs=[pl.BlockSpec(memory_space=pl.ANY), pl.BlockSpec(memory_space=pl.ANY),
                  tok_spec(LANES), tok_spec(d), tok_spec(d), mod_spec,
                  _const_spec((d, ds_)), _const_spec((d, ds_)), _const_spec((ds_, d)), _const_spec((1, d))],
        out_specs=pl.BlockSpec((tm, d), lambda i: (i, 0)),
        scratch_shapes=[pltpu.SMEM((per,), I32), pltpu.VMEM((TOP_K, tm, d), F32),
                        pltpu.SemaphoreType.DMA, pltpu.SemaphoreType.DMA],
        compiler_params=_params(("arbitrary",)),
        name="combine",
    )(pos_flat, y_rows, wtok, u2, x1, mod, consts["w_sh_gate"], consts["w_sh_up"], consts["w_sh_down"],
      consts["g_post_ffn"])


def _repeat_rows(x, n):
    r, c = x.shape
    return jnp.broadcast_to(x[:, None, :], (r, n, c)).reshape(r * n, c)


def _tile_major(pos_t, tile):
    k, t = pos_t.shape
    return pos_t.reshape(k, t // tile, tile).transpose(1, 0, 2).reshape(-1)


def kernel(x_prompt, x_sample, c_prompt, c_sample, state_ssm, state_ssd_conv, state_short_conv, w_ada, b_ada, g_pre_mix, g_post_mix, g_pre_ffn, g_post_ffn, w_in, ssd_conv_w, ssd_conv_b, dt_bias, a_log, d_skip, g_ssd_norm, sc_conv_w, g_sc_norm, w_out, w_router, router_bias, w_exp_gate, w_exp_up, w_exp_down, w_sh_gate, w_sh_up, w_sh_down):
    depth = w_ada.shape[0]
    bp, seq, d = x_prompt.shape
    bs, dseq, _ = x_sample.shape
    heads = dt_bias.shape[1]
    ssd_w = heads * SSD_HEAD_DIM
    conv_dim = ssd_conv_w.shape[2]
    sc_w = sc_conv_w.shape[2]
    ne = w_router.shape[2]
    tp, ts = bp * seq, bs * dseq
    t_all = tp + ts

    assert sc_w // SC_GROUPS == SSD_HEAD_DIM and heads == SC_GROUPS
    head_of = jnp.arange(ssd_w, dtype=I32) // SSD_HEAD_DIM
    e_ind = (jnp.arange(LANES, dtype=I32)[:, None] == head_of[None, :])

    xp = x_prompt.reshape(tp, d)
    xs_pad = jnp.pad(x_sample, ((0, 0), (0, SUBLANES - dseq), (0, 0))).reshape(bs * SUBLANES, d)
    xs_tok = x_sample.reshape(ts, d)
    outs = {k: [] for k in ("ssm_p", "cst_p", "scst_p", "ssm_s", "cst_s", "scst_s")}

    for layer in range(depth):
        cuts = np.cumsum([0, ssd_w, conv_dim, heads, sc_w, sc_w, sc_w]).tolist()
        wi = w_in[layer]
        seg = lambda k: wi[:, cuts[k]:cuts[k + 1]]
        w_in_r = jnp.concatenate([seg(0), seg(1), seg(3), seg(4), seg(5),
                                  jnp.pad(seg(2), ((0, 0), (0, LANES - heads)))], axis=1).astype(BF16)
        widths = (ssd_w, conv_dim, sc_w, sc_w, sc_w, LANES)
        pad_h = lambda v: jnp.pad(v.reshape(1, heads), ((0, 0), (0, LANES - heads)))
        wr = w_router[layer].T
        wr_hi = wr.astype(BF16)
        consts = dict(
            heads=heads,
            conv_w=ssd_conv_w[layer], conv_b=ssd_conv_b[layer].reshape(1, conv_dim),
            dt_bias=pad_h(dt_bias[layer]), a_log=pad_h(a_log[layer]),
            d_skip_e=jnp.broadcast_to(d_skip[layer][:, None], (heads, SSD_HEAD_DIM)).reshape(1, ssd_w),
            g_ssd=g_ssd_norm[layer].reshape(1, ssd_w), sc_w=sc_conv_w[layer],
            g_sc=g_sc_norm[layer].reshape(1, sc_w),
            e=e_ind.astype(BF16), et=e_ind.T.astype(F32),
            w_out=w_out[layer].astype(BF16), g_post_mix=g_post_mix[layer].reshape(1, d),
            g_pre_ffn=g_pre_ffn[layer].reshape(1, d),
            wr_hi_t=wr_hi, wr_lo_t=(wr - wr_hi.astype(F32)).astype(BF16),
            w_sh_gate=w_sh_gate[layer].astype(BF16), w_sh_up=w_sh_up[layer].astype(BF16),
            w_sh_down=w_sh_down[layer].astype(BF16), g_post_ffn=g_post_ffn[layer].reshape(1, d))

        c_all = jnp.concatenate([c_prompt, c_sample], axis=0)
        m_rows = -(-c_all.shape[0] // 16) * 16
        mod = _ada(jnp.pad(c_all, ((0, m_rows - c_all.shape[0]), (0, 0))), w_ada[layer], b_ada[layer])
        mod_p = mod[:bp]
        mod_s = mod[bp:bp + bs]
        mod_s_pad = _repeat_rows(mod_s[:, :2 * d], SUBLANES)
        mod_s_tok = _repeat_rows(mod_s, dseq)

        proj_p = _in_proj(xp, mod_p[:, :2 * d], g_pre_mix[layer], w_in_r, widths, BF16, False, seq, TOK_TILE)
        proj_s = _in_proj(xs_pad, mod_s_pad, g_pre_mix[layer], w_in_r, widths, F32, True, 1, TOK_TILE)
        ymix_p, ssm_p, cst_p, scst_p = _ssd(proj_p, None, consts, bp, seq // CHUNK, CHUNK, CHUNK, CHUNK, BF16)
        init = (state_ssm[layer].reshape(bs, ssd_w, D_STATE), state_ssd_conv[layer], state_short_conv[layer])
        ymix_s, ssm_s, cst_s, scst_s = _ssd(proj_s, init, consts, bs, 1, SAMPLE_CHUNK, SUBLANES, dseq, F32)
        ymix_s = ymix_s.reshape(bs, SUBLANES, ssd_w + sc_w)[:, :dseq].reshape(ts, ssd_w + sc_w)

        merged = _out_proj(ymix_p, xp, mod_p[:, 2 * d:5 * d], consts, False, seq, TOK_TILE, t_all, 0, None)
        x1, u2, logits_t = _out_proj(ymix_s, xs_tok, mod_s_tok[:, 2 * d:5 * d], consts, True, 1, TOK_TILE,
                                     t_all, tp // TOK_TILE, merged)

        idx_t, rank_t, wtok, counts = _route(logits_t, router_bias[layer], TOK_TILE)
        counts = counts[:, 0]
        padded = (counts + ROW_TILE - 1) // ROW_TILE * ROW_TILE
        pad_end = jnp.cumsum(padded)
        pad_start = pad_end - padded
        n_tiles = -(-(t_all * TOP_K) // ROW_TILE) + ne
        n_used = (pad_end[-1] // ROW_TILE).astype(I32)
        tile_ids = jnp.minimum(jnp.arange(n_tiles, dtype=I32), n_used - 1)
        tile_expert = jnp.minimum(jnp.sum(pad_end[None, :] <= (tile_ids * ROW_TILE)[:, None], axis=1), ne - 1).astype(I32)
        first_pad = (pad_start + counts) // SUBLANES * SUBLANES
        units = (pad_end - first_pad) // SUBLANES
        pos_t = _positions(idx_t, rank_t, pad_start.astype(I32), TOK_TILE)
        pos_flat = _tile_major(pos_t, GATHER_TILE)

        xs_rows = _zero_rows(first_pad.astype(I32), units.astype(I32), n_tiles * ROW_TILE, d)
        xs_rows = _dispatch(pos_flat, u2, xs_rows, GATHER_TILE)
        y_rows = _experts(tile_expert, n_used.reshape(1), xs_rows, w_exp_gate[layer], w_exp_up[layer],
                          w_exp_down[layer])
        xp = _combine(pos_flat, y_rows, wtok, u2, x1, mod_p[:, 5 * d:], consts, False, seq, 0, tp, GATHER_TILE)
        xs_tok = _combine(pos_flat, y_rows, wtok, u2, x1, mod_s_tok[:, 5 * d:], consts, True, 1,
                          tp // GATHER_TILE, ts, GATHER_TILE)
        xs_pad = jnp.pad(xs_tok.reshape(bs, dseq, d), ((0, 0), (0, SUBLANES - dseq), (0, 0))).reshape(
            bs * SUBLANES, d)

        outs["ssm_p"].append(ssm_p.reshape(bp, heads, SSD_HEAD_DIM, D_STATE))
        outs["cst_p"].append(cst_p)
        outs["scst_p"].append(scst_p)
        outs["ssm_s"].append(ssm_s.reshape(bs, heads, SSD_HEAD_DIM, D_STATE))
        outs["cst_s"].append(cst_s)
        outs["scst_s"].append(scst_s)

    return (xp.reshape(bp, seq, d), xs_tok.reshape(bs, dseq, d),
            jnp.stack(outs["ssm_p"]), jnp.stack(outs["cst_p"]), jnp.stack(outs["scst_p"]),
            jnp.stack(outs["ssm_s"]), jnp.stack(outs["cst_s"]), jnp.stack(outs["scst_s"]))
```

```python
import functools

import jax
import jax.numpy as jnp
import numpy as np
from jax import lax
from jax.experimental import pallas as pl
from jax.experimental.pallas import tpu as pltpu

F32 = jnp.float32
BF16 = jnp.bfloat16
I32 = jnp.int32
U32 = jnp.uint32
HI = lax.Precision.HIGHEST

SSD_HEAD_DIM = 64
SSD_GROUPS = 2
D_STATE = 128
SSD_CONV_W = 4
SC_GROUPS = 16
SC_CONV_W = 3
TOP_K = 8
N_EXPERT_GROUPS = 8
TOPK_GROUPS = 4
ROUTED_SCALE = 2.5
EPS = 1e-6

LANES = 128
SUBLANES = 8
CHUNK = 128
SAMPLE_CHUNK = 16
TOK_TILE = 512
ROW_TILE = 192
GATHER_TILE = 256
VMEM_LIMIT = 56 * 1024 * 1024
DMA_THREADS = 2

NT = (((1,), (1,)), ((), ()))
TN = (((0,), (0,)), ((), ()))


def _sigmoid(x):
    return 1.0 / (1.0 + jnp.exp(-x))


def _silu(x):
    return x * _sigmoid(x)


def _softplus(x):
    return jnp.maximum(x, 0.0) + jnp.log1p(jnp.exp(-jnp.abs(x)))


def _rms(x, eps=EPS):
    return x * lax.rsqrt(jnp.mean(x * x, axis=-1, keepdims=True) + eps)


def _params(sem=None):
    return pltpu.CompilerParams(dimension_semantics=sem, vmem_limit_bytes=VMEM_LIMIT)


def _const_spec(shape, single=False):
    nd = len(shape)
    mode = dict(pipeline_mode=pl.Buffered(1)) if single else {}
    return pl.BlockSpec(shape, lambda *_: (0,) * nd, **mode)


def _ada_kernel(c_ref, w_ref, b_ref, o_ref):
    c = c_ref[...]
    s = _silu(c).astype(BF16)
    o_ref[...] = jnp.dot(s, w_ref[...].astype(BF16), preferred_element_type=F32) + b_ref[...]


def _ada(c, w_ada, b_ada):
    m, d = c.shape
    n = w_ada.shape[1]
    tn = 512
    return pl.pallas_call(
        _ada_kernel,
        out_shape=jax.ShapeDtypeStruct((m, n), F32),
        grid=(n // tn,),
        in_specs=[_const_spec((m, d)),
                  pl.BlockSpec((d, tn), lambda j: (0, j)),
                  pl.BlockSpec((1, tn), lambda j: (0, j))],
        out_specs=pl.BlockSpec((m, tn), lambda j: (0, j)),
        compiler_params=_params(("arbitrary",)),
        name="ada",
    )(c, w_ada, b_ada.reshape(1, n))


def _mod_rows(mod_ref, per_row, rows_per_batch, tile, d):
    n = mod_ref.shape[1] // d
    if per_row:
        return [mod_ref[:, k * d:(k + 1) * d] for k in range(n)]
    b = (pl.program_id(0) * tile) // rows_per_batch
    return [mod_ref[pl.ds(b, 1), k * d:(k + 1) * d] for k in range(n)]


def _mod_spec(mod, per_row, tile):
    if per_row:
        return pl.BlockSpec((tile, mod.shape[1]), lambda i: (i, 0))
    return _const_spec(mod.shape)


def _in_kernel(x_ref, mod_ref, g_ref, w_ref, *out_refs, per_row, rows_per_batch, widths):
    tile, d = x_ref.shape
    shift, scale = _mod_rows(mod_ref, per_row, rows_per_batch, tile, d)
    u = (_rms(x_ref[...]) * g_ref[...]) * (1.0 + scale) + shift
    u = u.astype(BF16)
    col = 0
    for ref, width in zip(out_refs, widths):
        for a in range(0, width, 512):
            bw = min(512, width - a)
            r = jnp.dot(u, w_ref[:, col + a:col + a + bw], preferred_element_type=F32)
            ref[:, a:a + bw] = r.astype(ref.dtype)
        col += width


def _in_proj(x, mod, g, w_bf16, widths, out_dtype, per_row, rows_per_batch, tile):
    t, d = x.shape
    n = w_bf16.shape[1]
    mod_spec = _mod_spec(mod, per_row, tile)
    dts = [out_dtype] * (len(widths) - 1) + [F32]
    return pl.pallas_call(
        functools.partial(_in_kernel, per_row=per_row, rows_per_batch=rows_per_batch, widths=widths),
        out_shape=[jax.ShapeDtypeStruct((t, wd), dt) for wd, dt in zip(widths, dts)],
        grid=(t // tile,),
        in_specs=[pl.BlockSpec((tile, d), lambda i: (i, 0)), mod_spec,
                  _const_spec((1, d)), _const_spec((d, n), single=True)],
        out_specs=[pl.BlockSpec((tile, wd), lambda i: (i, 0)) for wd in widths],
        compiler_params=_params(("arbitrary",)),
        name="in_proj",
    )(x, mod, g.reshape(1, d), w_bf16)


def _ssd_kernel(*refs, q, rows_in, q_valid, has_init, heads):
    it = iter(refs)
    z_ref, xbc_ref, scb_ref, scc_ref, sch_ref, dt_ref = (next(it) for _ in range(6))
    if has_init:
        ssm0_ref, cst0_ref, scst0_ref = (next(it) for _ in range(3))
    (cw_ref, cb_ref, dtb_ref, alog_ref, dsk_ref, gssd_ref, scw_ref, gsc_ref,
     tril_ref, e_ref, et_ref) = (next(it) for _ in range(11))
    ymix_ref, ssm_ref, cst_ref, scst_ref = (next(it) for _ in range(4))
    h_scr, ext_scr, extv_scr = (next(it) for _ in range(3))

    c = pl.program_id(1)
    nc = pl.num_programs(1)
    ssd_w = dsk_ref.shape[1]
    gw = ssd_w // SSD_GROUPS
    hpg = heads // SSD_GROUPS
    n_state = D_STATE
    head0 = SUBLANES - (SSD_CONV_W - 1)
    headv = SUBLANES - (SC_CONV_W - 1)

    @pl.when(c == 0)
    def _():
        if has_init:
            h_scr[...] = ssm0_ref[0]
            ext_scr[head0:SUBLANES, :] = cst0_ref[0]
            extv_scr[headv:SUBLANES, :] = scst0_ref[0]
        else:
            h_scr[...] = jnp.zeros(h_scr.shape, F32)
            ext_scr[0:SUBLANES, :] = jnp.zeros((SUBLANES, ext_scr.shape[1]), F32)
            extv_scr[0:SUBLANES, :] = jnp.zeros((SUBLANES, extv_scr.shape[1]), F32)

    def pad_rows(v, n):
        if v.shape[0] == n:
            return v
        return jnp.concatenate([v, jnp.zeros((n - v.shape[0], v.shape[1]), v.dtype)], axis=0)

    load = lambda ref: pad_rows(ref[...].astype(F32), q)
    pad_t = lambda v: pad_rows(v, LANES)
    z, scb, scc, sch, dt_raw = load(z_ref), load(scb_ref), load(scc_ref), load(sch_ref), load(dt_ref)

    ext_scr[SUBLANES:SUBLANES + q, :] = load(xbc_ref)
    conv = cb_ref[...]
    for k in range(SSD_CONV_W):
        conv = conv + cw_ref[k:k + 1, :] * ext_scr[head0 + k:head0 + k + q, :]
    xc = _silu(conv)
    xs = xc[:, :ssd_w]
    bm = xc[:, ssd_w:ssd_w + SSD_GROUPS * n_state]
    cm = xc[:, ssd_w + SSD_GROUPS * n_state:]

    lane = lax.broadcasted_iota(I32, (q, LANES), 1)
    row = lax.broadcasted_iota(I32, (q, LANES), 0)
    dt = _softplus(dt_raw + dtb_ref[...])
    dt = jnp.where(jnp.logical_and(lane < heads, row < q_valid), dt, 0.0)
    a = dt * (-jnp.exp(alog_ref[...]))
    acum = jnp.dot(tril_ref[...], pad_t(a), precision=HI, preferred_element_type=F32)
    acum_t = pad_t(acum).T
    a_last = acum[q - 1:q, :]
    e = e_ref[...]
    expand = lambda v: jnp.dot(v.astype(BF16), e, preferred_element_type=F32)
    dt_e = expand(dt)
    dtdte_e = expand(dt * jnp.exp(a_last - acum))
    exa_e = expand(jnp.exp(acum))
    xdt = xs * dt_e
    xw_b = (xs * dtdte_e).astype(BF16)
    dlast = jnp.broadcast_to(jnp.exp(acum_t[:, q - 1:q]), (LANES, n_state))
    dcol = jnp.dot(et_ref[...], dlast, precision=HI, preferred_element_type=F32)

    tri = row >= lane
    xdt_t = pad_t(xdt)
    xw_t = pad_t(xw_b)
    lane_t = lax.broadcasted_iota(I32, (LANES, LANES), 1)
    y_groups = []
    for g in range(SSD_GROUPS):
        bm_g = bm[:, g * n_state:(g + 1) * n_state].astype(BF16)
        cm_g = cm[:, g * n_state:(g + 1) * n_state].astype(BF16)
        bm_t = pad_t(bm_g)
        cb = lax.dot_general(cm_g, bm_t, NT, preferred_element_type=F32)
        h_g = h_scr[g * gw:(g + 1) * gw, :]
        y_off = lax.dot_general(cm_g, h_g.astype(BF16), NT, preferred_element_type=F32)
        parts = []
        for pair in range(hpg // 2):
            lo = (g * hpg + 2 * pair) * SSD_HEAD_DIM
            x_pair = xdt_t[:, lo:lo + LANES]
            acc = None
            for half in range(2):
                h = g * hpg + 2 * pair + half
                ci = jnp.broadcast_to(acum[:, h:h + 1], (q, LANES))
                rj = jnp.broadcast_to(acum_t[h:h + 1, :], (q, LANES))
                dec = jnp.where(tri, jnp.exp(ci - rj), 0.0)
                m = (cb * dec).astype(BF16)
                own = (lane_t >= SSD_HEAD_DIM) if half else (lane_t < SSD_HEAD_DIM)
                y_h = jnp.dot(m, jnp.where(own, x_pair, 0.0).astype(BF16), preferred_element_type=F32)
                acc = y_h if acc is None else acc + y_h
            parts.append(acc)
        y_diag = jnp.concatenate(parts, axis=1)
        y_groups.append(y_diag + y_off * exa_e[:, g * gw:(g + 1) * gw])
        upd = lax.dot_general(xw_t[:, g * gw:(g + 1) * gw], bm_t, TN, preferred_element_type=F32)
        h_scr[g * gw:(g + 1) * gw, :] = h_g * dcol[g * gw:(g + 1) * gw, :] + upd

    y = (jnp.concatenate(y_groups, axis=1) + xs * dsk_ref[...]) * _silu(z)
    y = jnp.concatenate([_rms(y[:, g * gw:(g + 1) * gw]) for g in range(SSD_GROUPS)], axis=1)
    y_ssd = y * gssd_ref[...]

    extv_scr[SUBLANES:SUBLANES + q, :] = scc * sch
    cv = scw_ref[0:1, :] * extv_scr[headv:headv + q, :]
    for k in range(1, SC_CONV_W):
        cv = cv + scw_ref[k:k + 1, :] * extv_scr[headv + k:headv + k + q, :]
    t = scb * cv
    sc_per_group = t.shape[1] // SC_GROUPS
    gsum = jnp.dot((t * t).astype(BF16), et_ref[...].astype(BF16), preferred_element_type=F32)
    rs = lax.rsqrt(gsum * (1.0 / sc_per_group) + EPS)
    y_sc = t * expand(rs) * gsc_ref[...]

    ymix_ref[:, :ssd_w] = y_ssd[0:rows_in].astype(ymix_ref.dtype)
    ymix_ref[:, ssd_w:] = y_sc[0:rows_in].astype(ymix_ref.dtype)

    @pl.when(c == nc - 1)
    def _():
        ssm_ref[0] = h_scr[...]
        cst_ref[0] = ext_scr[SUBLANES + q_valid - (SSD_CONV_W - 1):SUBLANES + q_valid, :]
        scst_ref[0] = extv_scr[SUBLANES + q_valid - (SC_CONV_W - 1):SUBLANES + q_valid, :]

    ext_scr[0:SUBLANES, :] = ext_scr[q:q + SUBLANES, :]
    extv_scr[0:SUBLANES, :] = extv_scr[q:q + SUBLANES, :]


def _ssd(proj, init, consts, nb, nc, q, rows_in, q_valid, ymix_dtype):
    z, xbc, scb, scc, sch, dt = proj
    heads = consts["heads"]
    tril = jnp.tril(jnp.ones((q, LANES), F32))
    ssd_w, conv_dim, sc_w = z.shape[1], xbc.shape[1], scb.shape[1]
    has_init = init is not None
    row_spec = lambda w: pl.BlockSpec((rows_in, w), lambda b, c: (b * nc + c, 0))
    in_specs = [row_spec(ssd_w), row_spec(conv_dim), row_spec(sc_w), row_spec(sc_w), row_spec(sc_w),
                row_spec(LANES)]
    args = [z, xbc, scb, scc, sch, dt]
    if has_init:
        ssm0, cst0, scst0 = init
        in_specs += [pl.BlockSpec((1,) + ssm0.shape[1:], lambda b, c: (b, 0, 0)),
                     pl.BlockSpec((1,) + cst0.shape[1:], lambda b, c: (b, 0, 0)),
                     pl.BlockSpec((1,) + scst0.shape[1:], lambda b, c: (b, 0, 0))]
        args += [ssm0, cst0, scst0]
    weights = [consts[k] for k in ("conv_w", "conv_b", "dt_bias", "a_log", "d_skip_e", "g_ssd",
                                   "sc_w", "g_sc")] + [tril, consts["e"], consts["et"]]
    in_specs += [_const_spec(w.shape) for w in weights]
    args += weights
    n_state = D_STATE
    out_shape = [jax.ShapeDtypeStruct((nb * nc * rows_in, ssd_w + sc_w), ymix_dtype),
                 jax.ShapeDtypeStruct((nb, ssd_w, n_state), F32),
                 jax.ShapeDtypeStruct((nb, SSD_CONV_W - 1, conv_dim), F32),
                 jax.ShapeDtypeStruct((nb, SC_CONV_W - 1, sc_w), F32)]
    out_specs = [pl.BlockSpec((rows_in, ssd_w + sc_w), lambda b, c: (b * nc + c, 0)),
                 pl.BlockSpec((1, ssd_w, n_state), lambda b, c: (b, 0, 0)),
                 pl.BlockSpec((1, SSD_CONV_W - 1, conv_dim), lambda b, c: (b, 0, 0)),
                 pl.BlockSpec((1, SC_CONV_W - 1, sc_w), lambda b, c: (b, 0, 0))]
    scratch = [pltpu.VMEM((ssd_w, n_state), F32),
               pltpu.VMEM((q + SUBLANES, conv_dim), F32),
               pltpu.VMEM((q + SUBLANES, sc_w), F32)]
    return pl.pallas_call(
        functools.partial(_ssd_kernel, q=q, rows_in=rows_in, q_valid=q_valid, has_init=has_init, heads=heads),
        out_shape=out_shape, grid=(nb, nc), in_specs=in_specs, out_specs=out_specs,
        scratch_shapes=scratch,
        compiler_params=_params(("arbitrary", "arbitrary")),
        name="ssd",
    )(*args)


def _out_kernel(ymix_ref, x_ref, mod_ref, wout_ref, gpost_ref, gpre_ref, wr_hi_ref, wr_lo_ref,
                *rest, per_row, rows_per_batch, has_alias):
    x1_ref, u2_ref, lg_ref = rest[3:6] if has_alias else rest[0:3]
    tile, d = x_ref.shape
    gate1, shift2, scale2 = _mod_rows(mod_ref, per_row, rows_per_batch, tile, d)
    m = jnp.dot(ymix_ref[...].astype(BF16), wout_ref[...], preferred_element_type=F32)
    x1 = x_ref[...] + gate1 * (_rms(m) * gpost_ref[...])
    u2 = (_rms(x1) * gpre_ref[...]) * (1.0 + scale2) + shift2
    x1_ref[...] = x1
    u2_ref[...] = u2
    u_hi = u2.astype(BF16)
    u_lo = (u2 - u_hi.astype(F32)).astype(BF16)
    lg = lax.dot_general(wr_hi_ref[...], u_hi, NT, preferred_element_type=F32)
    lg = lg + lax.dot_general(wr_hi_ref[...], u_lo, NT, preferred_element_type=F32)
    lg = lg + lax.dot_general(wr_lo_ref[...], u_hi, NT, preferred_element_type=F32)
    lg_ref[...] = lg


def _out_proj(ymix, x, mod, consts, per_row, rows_per_batch, tile, t_total, tile_off, prev):
    t, d = x.shape
    dm = ymix.shape[1]
    ne = consts["wr_hi_t"].shape[0]
    in_specs = [pl.BlockSpec((tile, dm), lambda i: (i, 0)),
                pl.BlockSpec((tile, d), lambda i: (i, 0)), _mod_spec(mod, per_row, tile),
                _const_spec((dm, d)), _const_spec((1, d)), _const_spec((1, d)),
                _const_spec((ne, d)), _const_spec((ne, d))]
    args = [ymix, x, mod, consts["w_out"], consts["g_post_mix"], consts["g_pre_ffn"],
            consts["wr_hi_t"], consts["wr_lo_t"]]
    aliases = {}
    if prev is not None:
        in_specs += [pl.BlockSpec(memory_space=pl.ANY)] * 3
        aliases = {len(args) + k: k for k in range(3)}
        args += list(prev)
    return pl.pallas_call(
        functools.partial(_out_kernel, per_row=per_row, rows_per_batch=rows_per_batch,
                          has_alias=prev is not None),
        out_shape=[jax.ShapeDtypeStruct((t_total, d), F32), jax.ShapeDtypeStruct((t_total, d), F32),
                   jax.ShapeDtypeStruct((ne, t_total), F32)],
        grid=(t // tile,), in_specs=in_specs,
        out_specs=[pl.BlockSpec((tile, d), lambda i: (i + tile_off, 0)),
                   pl.BlockSpec((tile, d), lambda i: (i + tile_off, 0)),
                   pl.BlockSpec((ne, tile), lambda i: (0, i + tile_off))],
        input_output_aliases=aliases,
        compiler_params=_params(("arbitrary",)),
        name="out_proj",
    )(*args)


def _route_kernel(lg_ref, bias_ref, upper_ref, idx_ref, rank_ref, wtok_ref, cnt_ref, carry_scr):
    i = pl.program_id(0)
    ne, tm = lg_ref.shape
    per_group = ne // N_EXPERT_GROUPS
    neg = -jnp.inf

    @pl.when(i == 0)
    def _():
        carry_scr[...] = jnp.zeros(carry_scr.shape, F32)

    s = _sigmoid(lg_ref[...])
    biased = s + bias_ref[...]
    gl = []
    io_g = lax.broadcasted_iota(I32, (per_group, tm), 0).astype(F32)
    for g in range(N_EXPERT_GROUPS):
        blk = biased[g * per_group:(g + 1) * per_group, :]
        m1 = jnp.max(blk, axis=0, keepdims=True)
        f1 = jnp.min(jnp.where(blk == m1, io_g, float(per_group)), axis=0, keepdims=True)
        m2 = jnp.max(jnp.where(io_g == f1, neg, blk), axis=0, keepdims=True)
        gl.append(m1 + m2)
    gscore = jnp.concatenate(gl, axis=0)
    io8 = lax.broadcasted_iota(I32, (N_EXPERT_GROUPS, tm), 0).astype(F32)
    gsel = jnp.zeros((N_EXPERT_GROUPS, tm), F32)
    for _ in range(TOPK_GROUPS):
        m = jnp.max(gscore, axis=0, keepdims=True)
        f = jnp.min(jnp.where(gscore == m, io8, float(N_EXPERT_GROUPS)), axis=0, keepdims=True)
        hit = io8 == f
        gsel = jnp.where(hit, 1.0, gsel)
        gscore = jnp.where(hit, neg, gscore)
    emask = jnp.concatenate(
        [jnp.broadcast_to(gsel[g:g + 1, :], (per_group, tm)) for g in range(N_EXPERT_GROUPS)], axis=0)
    cand = jnp.where(emask > 0.5, biased, neg)
    io_e = lax.broadcasted_iota(I32, (ne, tm), 0).astype(F32)
    msel = jnp.zeros((ne, tm), F32)
    idxs, wts = [], []
    for _ in range(TOP_K):
        m = jnp.max(cand, axis=0, keepdims=True)
        f = jnp.min(jnp.where(cand == m, io_e, float(ne)), axis=0, keepdims=True)
        hit = io_e == f
        wts.append(jnp.sum(jnp.where(hit, s, 0.0), axis=0, keepdims=True))
        idxs.append(f)
        msel = jnp.where(hit, 1.0, msel)
        cand = jnp.where(hit, neg, cand)
    pref = jnp.dot(msel.astype(BF16), upper_ref[...], preferred_element_type=F32) + carry_scr[:, 0:1]
    ranks = [jnp.sum(jnp.where(io_e == f, pref, 0.0), axis=0, keepdims=True) for f in idxs]
    carry_scr[...] = carry_scr[...] + jnp.sum(msel, axis=1, keepdims=True)
    cnt_ref[...] = carry_scr[...].astype(I32)
    idx_ref[...] = jnp.concatenate(idxs, axis=0).astype(I32)
    rank_ref[...] = jnp.concatenate(ranks, axis=0).astype(I32)
    wsum = wts[0]
    for w in wts[1:]:
        wsum = wsum + w
    wn = jnp.concatenate([w / wsum * ROUTED_SCALE for w in wts]
                         + [jnp.zeros((LANES - TOP_K, tm), F32)], axis=0)
    for j in range(tm // LANES):
        wtok_ref[j * LANES:(j + 1) * LANES, :] = wn[:, j * LANES:(j + 1) * LANES].T


def _route(logits_t, bias, tile):
    ne, t = logits_t.shape
    upper = jnp.triu(jnp.ones((tile, tile), F32), 1).astype(BF16)
    return pl.pallas_call(
        _route_kernel,
        out_shape=[jax.ShapeDtypeStruct((TOP_K, t), I32), jax.ShapeDtypeStruct((TOP_K, t), I32),
                   jax.ShapeDtypeStruct((t, LANES), F32), jax.ShapeDtypeStruct((ne, LANES), I32)],
        grid=(t // tile,),
        in_specs=[pl.BlockSpec((ne, tile), lambda i: (0, i)), _const_spec((ne, 1)),
                  _const_spec((tile, tile))],
        out_specs=[pl.BlockSpec((TOP_K, tile), lambda i: (0, i)),
                   pl.BlockSpec((TOP_K, tile), lambda i: (0, i)),
                   pl.BlockSpec((tile, LANES), lambda i: (i, 0)),
                   _const_spec((ne, LANES))],
        scratch_shapes=[pltpu.VMEM((ne, LANES), F32)],
        compiler_params=_params(("arbitrary",)),
        name="route",
    )(logits_t, bias.reshape(ne, 1), upper)


def _pos_kernel(idx_ref, rank_ref, start_ref, pos_ref):
    ne = start_ref.shape[0]
    tm = idx_ref.shape[1]
    io_e = lax.broadcasted_iota(I32, (ne, tm), 0)
    start = start_ref[...].astype(F32)
    rows = []
    for k in range(TOP_K):
        hit = io_e == idx_ref[k:k + 1, :]
        rows.append(jnp.sum(jnp.where(hit, start, 0.0), axis=0, keepdims=True))
    pos_ref[...] = jnp.concatenate(rows, axis=0).astype(I32) + rank_ref[...]


def _positions(idx_t, rank_t, pad_start, tile):
    k, t = idx_t.shape
    ne = pad_start.shape[0]
    return pl.pallas_call(
        _pos_kernel,
        out_shape=jax.ShapeDtypeStruct((k, t), I32),
        grid=(t // tile,),
        in_specs=[pl.BlockSpec((k, tile), lambda i: (0, i)), pl.BlockSpec((k, tile), lambda i: (0, i)),
                  _const_spec((ne, 1))],
        out_specs=pl.BlockSpec((k, tile), lambda i: (0, i)),
        compiler_params=_params(("arbitrary",)),
        name="positions",
    )(idx_t, rank_t, pad_start.reshape(ne, 1))


ZERO_BITS = (32, 16, 8, 4, 2, 1)


def _zero_copy(zbuf, xs_ref, sem, off, bit):
    rows = bit * SUBLANES
    return pltpu.make_async_copy(zbuf.at[pl.ds(0, rows)], xs_ref.at[pl.ds(pl.multiple_of(off, SUBLANES), rows)], sem)


def _zero_kernel(first_ref, units_ref, xs_ref, zbuf, sem):
    zbuf[...] = jnp.zeros(zbuf.shape, zbuf.dtype)
    ne = first_ref.shape[0]

    def each(e, start):
        off = first_ref[e]
        units = units_ref[e]
        for bit in ZERO_BITS:
            on = (units & bit) != 0

            @pl.when(on)
            def _():
                cp = _zero_copy(zbuf, xs_ref, sem, off, bit)
                cp.start() if start else cp.wait()
            off = off + jnp.where(on, bit * SUBLANES, 0)
        return start

    lax.fori_loop(0, ne, lambda e, c: (each(e, True), c)[1], 0)
    lax.fori_loop(0, ne, lambda e, c: (each(e, False), c)[1], 0)


def _zero_rows(first, units, n_rows, width):
    return pl.pallas_call(
        _zero_kernel,
        out_shape=jax.ShapeDtypeStruct((n_rows, width), F32),
        in_specs=[pl.BlockSpec(memory_space=pltpu.SMEM), pl.BlockSpec(memory_space=pltpu.SMEM)],
        out_specs=pl.BlockSpec(memory_space=pl.ANY),
        scratch_shapes=[pltpu.VMEM((ZERO_BITS[0] * SUBLANES, width), F32), pltpu.SemaphoreType.DMA],
        compiler_params=_params(),
        name="zero_rows",
    )(first, units)


def _row_copy(src_ref, dst_ref, sem, s, d):
    return pltpu.make_async_copy(src_ref.at[pl.ds(s, 1)], dst_ref.at[pl.ds(d, 1)], sem)


def _load_positions(pos_hbm, pos_smem, psem, tile_idx):
    per = pos_smem.shape[0]
    cp = pltpu.make_async_copy(pos_hbm.at[pl.ds(pl.multiple_of(tile_idx * per, per), per)], pos_smem, psem)
    cp.start()
    cp.wait()


def _dispatch_kernel(pos_hbm, u_ref, xs_in, xs_ref, pos_smem, psem, sem):
    del xs_in
    tm = u_ref.shape[0]
    _load_positions(pos_hbm, pos_smem, psem, pl.program_id(0))

    def issue(t, carry):
        for k in range(TOP_K):
            _row_copy(u_ref, xs_ref, sem, t, pos_smem[k * tm + t]).start(priority=k % DMA_THREADS)
        return carry

    lax.fori_loop(0, tm, issue, 0)
    for k in range(TOP_K):
        pltpu.make_async_copy(u_ref, xs_ref.at[pl.ds(0, tm)], sem).wait()


def _dispatch(pos_flat, u2, xs, tm):
    per = TOP_K * tm
    nt = pos_flat.shape[0] // per
    d = u2.shape[1]
    return pl.pallas_call(
        _dispatch_kernel,
        out_shape=jax.ShapeDtypeStruct(xs.shape, xs.dtype),
        grid=(nt,),
        in_specs=[pl.BlockSpec(memory_space=pl.ANY), pl.BlockSpec((tm, d), lambda i: (i, 0)),
                  pl.BlockSpec(memory_space=pl.ANY)],
        out_specs=pl.BlockSpec(memory_space=pl.ANY),
        scratch_shapes=[pltpu.SMEM((per,), I32), pltpu.SemaphoreType.DMA, pltpu.SemaphoreType.DMA],
        input_output_aliases={2: 0},
        compiler_params=_params(("arbitrary",)),
        name="dispatch",
    )(pos_flat, u2, xs)


def _expert_kernel(start_ref, ntile_ref, xs_hbm, wg_ref, wu_ref, wd_ref, y_hbm,
                   xbuf, ybuf, wg_b, wu_b, wd_b, sem_in, sem_out):
    e = pl.program_id(0)
    n = ntile_ref[e]
    base = start_ref[e]
    rows = xbuf.shape[1]

    def tile_rows(j):
        return pl.ds(pl.multiple_of(base + j * rows, SUBLANES), rows)

    def x_copy(j, slot):
        return pltpu.make_async_copy(xs_hbm.at[tile_rows(j)], xbuf.at[slot], sem_in.at[slot])

    def y_copy(j, slot):
        return pltpu.make_async_copy(ybuf.at[slot], y_hbm.at[tile_rows(j)], sem_out.at[slot])

    @pl.when(n > 0)
    def _():
        x_copy(0, 0).start()

    wg_b[...] = wg_ref[0].astype(BF16)
    wu_b[...] = wu_ref[0].astype(BF16)
    wd_b[...] = wd_ref[0].astype(BF16)

    def body(j, carry):
        slot = j % 2
        x_copy(j, slot).wait()

        @pl.when(j + 1 < n)
        def _():
            x_copy(j + 1, 1 - slot).start()

        @pl.when(j >= 2)
        def _():
            y_copy(j - 2, slot).wait()

        x = xbuf[slot].astype(BF16)
        g = jnp.dot(x, wg_b[...], preferred_element_type=F32)
        u = jnp.dot(x, wu_b[...], preferred_element_type=F32)
        h = (_silu(g) * u).astype(BF16)
        ybuf[slot] = jnp.dot(h, wd_b[...], preferred_element_type=F32)
        y_copy(j, slot).start()
        return carry

    lax.fori_loop(0, n, body, 0)

    @pl.when(n >= 2)
    def _():
        y_copy(n - 2, n % 2).wait()

    @pl.when(n >= 1)
    def _():
        y_copy(n - 1, (n - 1) % 2).wait()


def _experts(pad_start, tiles_per_expert, xs, w_gate, w_up, w_down):
    n_rows, d = xs.shape
    ne, _, de = w_gate.shape
    return pl.pallas_call(
        _expert_kernel,
        out_shape=jax.ShapeDtypeStruct((n_rows, d), F32),
        grid_spec=pltpu.PrefetchScalarGridSpec(
            num_scalar_prefetch=2, grid=(ne,),
            in_specs=[pl.BlockSpec(memory_space=pl.ANY),
                      pl.BlockSpec((1, d, de), lambda e, s, n: (e, 0, 0)),
                      pl.BlockSpec((1, d, de), lambda e, s, n: (e, 0, 0)),
                      pl.BlockSpec((1, de, d), lambda e, s, n: (e, 0, 0))],
            out_specs=pl.BlockSpec(memory_space=pl.ANY),
            scratch_shapes=[pltpu.VMEM((2, ROW_TILE, d), F32), pltpu.VMEM((2, ROW_TILE, d), F32),
                            pltpu.VMEM((d, de), BF16), pltpu.VMEM((d, de), BF16), pltpu.VMEM((de, d), BF16),
                            pltpu.SemaphoreType.DMA((2,)), pltpu.SemaphoreType.DMA((2,))]),
        compiler_params=_params(("arbitrary",)),
        name="experts",
    )(pad_start, tiles_per_expert, xs, w_gate, w_up, w_down)


def _combine_kernel(pos_hbm, y_hbm, wtok_ref, u_ref, x1_ref, mod_ref, wsg_ref, wsu_ref, wsd_ref, gpost_ref,
                    o_ref, pos_smem, buf, psem, sem, *, per_row, rows_per_batch, tile_off):
    tm, d = u_ref.shape
    _load_positions(pos_hbm, pos_smem, psem, pl.program_id(0) + tile_off)

    def issue(t, carry):
        for k in range(TOP_K):
            _row_copy(y_hbm, buf.at[k], sem, pos_smem[k * tm + t], t).start(priority=k % DMA_THREADS)
        return carry

    lax.fori_loop(0, tm, issue, 0)

    ub = u_ref[...].astype(BF16)
    hs = _silu(jnp.dot(ub, wsg_ref[...], preferred_element_type=F32)) * jnp.dot(
        ub, wsu_ref[...], preferred_element_type=F32)
    f = jnp.dot(hs.astype(BF16), wsd_ref[...], preferred_element_type=F32)

    for k in range(TOP_K):
        pltpu.make_async_copy(y_hbm.at[pl.ds(0, tm)], buf.at[k], sem).wait()
    w = wtok_ref[...]
    for k in range(TOP_K):
        f = f + w[:, k:k + 1] * buf[k]
    (gate2,) = _mod_rows(mod_ref, per_row, rows_per_batch, tm, d)
    o_ref[...] = x1_ref[...] + gate2 * (_rms(f) * gpost_ref[...])


def _combine(pos_flat, y_rows, wtok, u2, x1, mod, consts, per_row, rows_per_batch, tile_off, n_tok, tm):
    per = TOP_K * tm
    d = u2.shape[1]
    ds_ = consts["w_sh_gate"].shape[1]
    tok_spec = lambda w: pl.BlockSpec((tm, w), lambda i: (i + tile_off, 0))
    mod_spec = _mod_spec(mod, per_row, tm)
    return pl.pallas_call(
        functools.partial(_combine_kernel, per_row=per_row, rows_per_batch=rows_per_batch, tile_off=tile_off),
        out_shape=jax.ShapeDtypeStruct((n_tok, d), F32),
        grid=(n_tok // tm,),
        in_specs=[pl.BlockSpec(memory_space=pl.ANY), pl.BlockSpec(memory_space=pl.ANY),
                  tok_spec(LANES), tok_spec(d), tok_spec(d), mod_spec,
                  _const_spec((d, ds_)), _const_spec((d, ds_)), _const_spec((ds_, d)), _const_spec((1, d))],
        out_specs=pl.BlockSpec((tm, d), lambda i: (i, 0)),
        scratch_shapes=[pltpu.SMEM((per,), I32), pltpu.VMEM((TOP_K, tm, d), F32),
                        pltpu.SemaphoreType.DMA, pltpu.SemaphoreType.DMA],
        compiler_params=_params(("arbitrary",)),
        name="combine",
    )(pos_flat, y_rows, wtok, u2, x1, mod, consts["w_sh_gate"], consts["w_sh_up"], consts["w_sh_down"],
      consts["g_post_ffn"])


def _repeat_rows(x, n):
    r, c = x.shape
    return jnp.broadcast_to(x[:, None, :], (r, n, c)).reshape(r * n, c)


def _tile_major(pos_t, tile):
    k, t = pos_t.shape
    return pos_t.reshape(k, t // tile, tile).transpose(1, 0, 2).reshape(-1)


def kernel(x_prompt, x_sample, c_prompt, c_sample, state_ssm, state_ssd_conv, state_short_conv, w_ada, b_ada, g_pre_mix, g_post_mix, g_pre_ffn, g_post_ffn, w_in, ssd_conv_w, ssd_conv_b, dt_bias, a_log, d_skip, g_ssd_norm, sc_conv_w, g_sc_norm, w_out, w_router, router_bias, w_exp_gate, w_exp_up, w_exp_down, w_sh_gate, w_sh_up, w_sh_down):
    depth = w_ada.shape[0]
    bp, seq, d = x_prompt.shape
    bs, dseq, _ = x_sample.shape
    heads = dt_bias.shape[1]
    ssd_w = heads * SSD_HEAD_DIM
    conv_dim = ssd_conv_w.shape[2]
    sc_w = sc_conv_w.shape[2]
    ne = w_router.shape[2]
    tp, ts = bp * seq, bs * dseq
    t_all = tp + ts

    assert sc_w // SC_GROUPS == SSD_HEAD_DIM and heads == SC_GROUPS
    head_of = jnp.arange(ssd_w, dtype=I32) // SSD_HEAD_DIM
    e_ind = (jnp.arange(LANES, dtype=I32)[:, None] == head_of[None, :])

    xp = x_prompt.reshape(tp, d)
    xs_pad = jnp.pad(x_sample, ((0, 0), (0, SUBLANES - dseq), (0, 0))).reshape(bs * SUBLANES, d)
    xs_tok = x_sample.reshape(ts, d)
    outs = {k: [] for k in ("ssm_p", "cst_p", "scst_p", "ssm_s", "cst_s", "scst_s")}

    for layer in range(depth):
        cuts = np.cumsum([0, ssd_w, conv_dim, heads, sc_w, sc_w, sc_w]).tolist()
        wi = w_in[layer]
        seg = lambda k: wi[:, cuts[k]:cuts[k + 1]]
        w_in_r = jnp.concatenate([seg(0), seg(1), seg(3), seg(4), seg(5),
                                  jnp.pad(seg(2), ((0, 0), (0, LANES - heads)))], axis=1).astype(BF16)
        widths = (ssd_w, conv_dim, sc_w, sc_w, sc_w, LANES)
        pad_h = lambda v: jnp.pad(v.reshape(1, heads), ((0, 0), (0, LANES - heads)))
        wr = w_router[layer].T
        wr_hi = wr.astype(BF16)
        consts = dict(
            heads=heads,
            conv_w=ssd_conv_w[layer], conv_b=ssd_conv_b[layer].reshape(1, conv_dim),
            dt_bias=pad_h(dt_bias[layer]), a_log=pad_h(a_log[layer]),
            d_skip_e=jnp.broadcast_to(d_skip[layer][:, None], (heads, SSD_HEAD_DIM)).reshape(1, ssd_w),
            g_ssd=g_ssd_norm[layer].reshape(1, ssd_w), sc_w=sc_conv_w[layer],
            g_sc=g_sc_norm[layer].reshape(1, sc_w),
            e=e_ind.astype(BF16), et=e_ind.T.astype(F32),
            w_out=w_out[layer].astype(BF16), g_post_mix=g_post_mix[layer].reshape(1, d),
            g_pre_ffn=g_pre_ffn[layer].reshape(1, d),
            wr_hi_t=wr_hi, wr_lo_t=(wr - wr_hi.astype(F32)).astype(BF16),
            w_sh_gate=w_sh_gate[layer].astype(BF16), w_sh_up=w_sh_up[layer].astype(BF16),
            w_sh_down=w_sh_down[layer].astype(BF16), g_post_ffn=g_post_ffn[layer].reshape(1, d))

        c_all = jnp.concatenate([c_prompt, c_sample], axis=0)
        m_rows = -(-c_all.shape[0] // 16) * 16
        mod = _ada(jnp.pad(c_all, ((0, m_rows - c_all.shape[0]), (0, 0))), w_ada[layer], b_ada[layer])
        mod_p = mod[:bp]
        mod_s = mod[bp:bp + bs]
        mod_s_pad = _repeat_rows(mod_s[:, :2 * d], SUBLANES)
        mod_s_tok = _repeat_rows(mod_s, dseq)

        proj_p = _in_proj(xp, mod_p[:, :2 * d], g_pre_mix[layer], w_in_r, widths, BF16, False, seq, TOK_TILE)
        proj_s = _in_proj(xs_pad, mod_s_pad, g_pre_mix[layer], w_in_r, widths, F32, True, 1, TOK_TILE)
        ymix_p, ssm_p, cst_p, scst_p = _ssd(proj_p, None, consts, bp, seq // CHUNK, CHUNK, CHUNK, CHUNK, BF16)
        init = (state_ssm[layer].reshape(bs, ssd_w, D_STATE), state_ssd_conv[layer], state_short_conv[layer])
        ymix_s, ssm_s, cst_s, scst_s = _ssd(proj_s, init, consts, bs, 1, SAMPLE_CHUNK, SUBLANES, dseq, F32)
        ymix_s = ymix_s.reshape(bs, SUBLANES, ssd_w + sc_w)[:, :dseq].reshape(ts, ssd_w + sc_w)

        merged = _out_proj(ymix_p, xp, mod_p[:, 2 * d:5 * d], consts, False, seq, TOK_TILE, t_all, 0, None)
        x1, u2, logits_t = _out_proj(ymix_s, xs_tok, mod_s_tok[:, 2 * d:5 * d], consts, True, 1, TOK_TILE,
                                     t_all, tp // TOK_TILE, merged)

        idx_t, rank_t, wtok, counts = _route(logits_t, router_bias[layer], TOK_TILE)
        counts = counts[:, 0]
        padded = (counts + ROW_TILE - 1) // ROW_TILE * ROW_TILE
        pad_end = jnp.cumsum(padded)
        pad_start = pad_end - padded
        n_tiles = -(-(t_all * TOP_K) // ROW_TILE) + ne
        first_pad = (pad_start + counts) // SUBLANES * SUBLANES
        units = (pad_end - first_pad) // SUBLANES
        pos_t = _positions(idx_t, rank_t, pad_start.astype(I32), TOK_TILE)
        pos_flat = _tile_major(pos_t, GATHER_TILE)

        xs_rows = _zero_rows(first_pad.astype(I32), units.astype(I32), n_tiles * ROW_TILE, d)
        xs_rows = _dispatch(pos_flat, u2, xs_rows, GATHER_TILE)
        y_rows = _experts(pad_start.astype(I32), (padded // ROW_TILE).astype(I32), xs_rows,
                          w_exp_gate[layer], w_exp_up[layer], w_exp_down[layer])
        xp = _combine(pos_flat, y_rows, wtok, u2, x1, mod_p[:, 5 * d:], consts, False, seq, 0, tp, GATHER_TILE)
        xs_tok = _combine(pos_flat, y_rows, wtok, u2, x1, mod_s_tok[:, 5 * d:], consts, True, 1,
                          tp // GATHER_TILE, ts, GATHER_TILE)
        xs_pad = jnp.pad(xs_tok.reshape(bs, dseq, d), ((0, 0), (0, SUBLANES - dseq), (0, 0))).reshape(
            bs * SUBLANES, d)

        outs["ssm_p"].append(ssm_p.reshape(bp, heads, SSD_HEAD_DIM, D_STATE))
        outs["cst_p"].append(cst_p)
        outs["scst_p"].append(scst_p)
        outs["ssm_s"].append(ssm_s.reshape(bs, heads, SSD_HEAD_DIM, D_STATE))
        outs["cst_s"].append(cst_s)
        outs["scst_s"].append(scst_s)

    return (xp.reshape(bp, seq, d), xs_tok.reshape(bs, dseq, d),
            jnp.stack(outs["ssm_p"]), jnp.stack(outs["cst_p"]), jnp.stack(outs["scst_p"]),
            jnp.stack(outs["ssm_s"]), jnp.stack(outs["cst_s"]), jnp.stack(outs["scst_s"]))
```

```python
import functools

import jax
import jax.numpy as jnp
import numpy as np
from jax import lax
from jax.experimental import pallas as pl
from jax.experimental.pallas import tpu as pltpu

F32 = jnp.float32
BF16 = jnp.bfloat16
I32 = jnp.int32
U32 = jnp.uint32
HI = lax.Precision.HIGHEST

SSD_HEAD_DIM = 64
SSD_GROUPS = 2
D_STATE = 128
SSD_CONV_W = 4
SC_GROUPS = 16
SC_CONV_W = 3
TOP_K = 8
N_EXPERT_GROUPS = 8
TOPK_GROUPS = 4
ROUTED_SCALE = 2.5
EPS = 1e-6

LANES = 128
SUBLANES = 8
CHUNK = 128
SAMPLE_CHUNK = 16
TOK_TILE = 512
ROW_TILE = 576
GATHER_TILE = 256
VMEM_LIMIT = 56 * 1024 * 1024
DMA_THREADS = 2

NT = (((1,), (1,)), ((), ()))
TN = (((0,), (0,)), ((), ()))


def _sigmoid(x):
    return 1.0 / (1.0 + jnp.exp(-x))


def _silu(x):
    return x * _sigmoid(x)


def _softplus(x):
    return jnp.maximum(x, 0.0) + jnp.log1p(jnp.exp(-jnp.abs(x)))


def _rms(x, eps=EPS):
    return x * lax.rsqrt(jnp.mean(x * x, axis=-1, keepdims=True) + eps)


def _params(sem=None):
    return pltpu.CompilerParams(dimension_semantics=sem, vmem_limit_bytes=VMEM_LIMIT)


def _const_spec(shape, single=False):
    nd = len(shape)
    mode = dict(pipeline_mode=pl.Buffered(1)) if single else {}
    return pl.BlockSpec(shape, lambda *_: (0,) * nd, **mode)


def _ada_kernel(c_ref, w_ref, b_ref, o_ref):
    c = c_ref[...]
    s = _silu(c).astype(BF16)
    o_ref[...] = jnp.dot(s, w_ref[...].astype(BF16), preferred_element_type=F32) + b_ref[...]


def _ada(c, w_ada, b_ada):
    m, d = c.shape
    n = w_ada.shape[1]
    tn = 512
    return pl.pallas_call(
        _ada_kernel,
        out_shape=jax.ShapeDtypeStruct((m, n), F32),
        grid=(n // tn,),
        in_specs=[_const_spec((m, d)),
                  pl.BlockSpec((d, tn), lambda j: (0, j)),
                  pl.BlockSpec((1, tn), lambda j: (0, j))],
        out_specs=pl.BlockSpec((m, tn), lambda j: (0, j)),
        compiler_params=_params(("arbitrary",)),
        name="ada",
    )(c, w_ada, b_ada.reshape(1, n))


def _mod_rows(mod_ref, per_row, rows_per_batch, tile, d):
    n = mod_ref.shape[1] // d
    if per_row:
        return [mod_ref[:, k * d:(k + 1) * d] for k in range(n)]
    b = (pl.program_id(0) * tile) // rows_per_batch
    return [mod_ref[pl.ds(b, 1), k * d:(k + 1) * d] for k in range(n)]


def _mod_spec(mod, per_row, tile):
    if per_row:
        return pl.BlockSpec((tile, mod.shape[1]), lambda i: (i, 0))
    return _const_spec(mod.shape)


def _in_kernel(x_ref, mod_ref, g_ref, w_ref, *out_refs, per_row, rows_per_batch, widths):
    tile, d = x_ref.shape
    shift, scale = _mod_rows(mod_ref, per_row, rows_per_batch, tile, d)
    u = (_rms(x_ref[...]) * g_ref[...]) * (1.0 + scale) + shift
    u = u.astype(BF16)
    col = 0
    for ref, width in zip(out_refs, widths):
        for a in range(0, width, 512):
            bw = min(512, width - a)
            r = jnp.dot(u, w_ref[:, col + a:col + a + bw], preferred_element_type=F32)
            ref[:, a:a + bw] = r.astype(ref.dtype)
        col += width


def _in_proj(x, mod, g, w_bf16, widths, out_dtype, per_row, rows_per_batch, tile):
    t, d = x.shape
    n = w_bf16.shape[1]
    mod_spec = _mod_spec(mod, per_row, tile)
    dts = [out_dtype] * (len(widths) - 1) + [F32]
    return pl.pallas_call(
        functools.partial(_in_kernel, per_row=per_row, rows_per_batch=rows_per_batch, widths=widths),
        out_shape=[jax.ShapeDtypeStruct((t, wd), dt) for wd, dt in zip(widths, dts)],
        grid=(t // tile,),
        in_specs=[pl.BlockSpec((tile, d), lambda i: (i, 0)), mod_spec,
                  _const_spec((1, d)), _const_spec((d, n), single=True)],
        out_specs=[pl.BlockSpec((tile, wd), lambda i: (i, 0)) for wd in widths],
        compiler_params=_params(("arbitrary",)),
        name="in_proj",
    )(x, mod, g.reshape(1, d), w_bf16)


def _ssd_kernel(*refs, q, rows_in, q_valid, has_init, heads):
    it = iter(refs)
    z_ref, xbc_ref, scb_ref, scc_ref, sch_ref, dt_ref = (next(it) for _ in range(6))
    if has_init:
        ssm0_ref, cst0_ref, scst0_ref = (next(it) for _ in range(3))
    (cw_ref, cb_ref, dtb_ref, alog_ref, dsk_ref, gssd_ref, scw_ref, gsc_ref,
     tril_ref, e_ref, et_ref) = (next(it) for _ in range(11))
    ymix_ref, ssm_ref, cst_ref, scst_ref = (next(it) for _ in range(4))
    h_scr, ext_scr, extv_scr = (next(it) for _ in range(3))

    c = pl.program_id(1)
    nc = pl.num_programs(1)
    ssd_w = dsk_ref.shape[1]
    gw = ssd_w // SSD_GROUPS
    hpg = heads // SSD_GROUPS
    n_state = D_STATE
    head0 = SUBLANES - (SSD_CONV_W - 1)
    headv = SUBLANES - (SC_CONV_W - 1)

    @pl.when(c == 0)
    def _():
        if has_init:
            h_scr[...] = ssm0_ref[0]
            ext_scr[head0:SUBLANES, :] = cst0_ref[0]
            extv_scr[headv:SUBLANES, :] = scst0_ref[0]
        else:
            h_scr[...] = jnp.zeros(h_scr.shape, F32)
            ext_scr[0:SUBLANES, :] = jnp.zeros((SUBLANES, ext_scr.shape[1]), F32)
            extv_scr[0:SUBLANES, :] = jnp.zeros((SUBLANES, extv_scr.shape[1]), F32)

    def pad_rows(v, n):
        if v.shape[0] == n:
            return v
        return jnp.concatenate([v, jnp.zeros((n - v.shape[0], v.shape[1]), v.dtype)], axis=0)

    load = lambda ref: pad_rows(ref[...].astype(F32), q)
    pad_t = lambda v: pad_rows(v, LANES)
    z, scb, scc, sch, dt_raw = load(z_ref), load(scb_ref), load(scc_ref), load(sch_ref), load(dt_ref)

    ext_scr[SUBLANES:SUBLANES + q, :] = load(xbc_ref)
    conv = cb_ref[...]
    for k in range(SSD_CONV_W):
        conv = conv + cw_ref[k:k + 1, :] * ext_scr[head0 + k:head0 + k + q, :]
    xc = _silu(conv)
    xs = xc[:, :ssd_w]
    bm = xc[:, ssd_w:ssd_w + SSD_GROUPS * n_state]
    cm = xc[:, ssd_w + SSD_GROUPS * n_state:]

    lane = lax.broadcasted_iota(I32, (q, LANES), 1)
    row = lax.broadcasted_iota(I32, (q, LANES), 0)
    dt = _softplus(dt_raw + dtb_ref[...])
    dt = jnp.where(jnp.logical_and(lane < heads, row < q_valid), dt, 0.0)
    a = dt * (-jnp.exp(alog_ref[...]))
    acum = jnp.dot(tril_ref[...], pad_t(a), precision=HI, preferred_element_type=F32)
    acum_t = pad_t(acum).T
    a_last = acum[q - 1:q, :]
    e = e_ref[...]
    expand = lambda v: jnp.dot(v.astype(BF16), e, preferred_element_type=F32)
    dt_e = expand(dt)
    dtdte_e = expand(dt * jnp.exp(a_last - acum))
    exa_e = expand(jnp.exp(acum))
    xdt = xs * dt_e
    xw_b = (xs * dtdte_e).astype(BF16)
    dlast = jnp.broadcast_to(jnp.exp(acum_t[:, q - 1:q]), (LANES, n_state))
    dcol = jnp.dot(et_ref[...], dlast, precision=HI, preferred_element_type=F32)

    tri = row >= lane
    xdt_t = pad_t(xdt)
    xw_t = pad_t(xw_b)
    lane_t = lax.broadcasted_iota(I32, (LANES, LANES), 1)
    y_groups = []
    for g in range(SSD_GROUPS):
        bm_g = bm[:, g * n_state:(g + 1) * n_state].astype(BF16)
        cm_g = cm[:, g * n_state:(g + 1) * n_state].astype(BF16)
        bm_t = pad_t(bm_g)
        cb = lax.dot_general(cm_g, bm_t, NT, preferred_element_type=F32)
        h_g = h_scr[g * gw:(g + 1) * gw, :]
        y_off = lax.dot_general(cm_g, h_g.astype(BF16), NT, preferred_element_type=F32)
        parts = []
        for pair in range(hpg // 2):
            lo = (g * hpg + 2 * pair) * SSD_HEAD_DIM
            x_pair = xdt_t[:, lo:lo + LANES]
            acc = None
            for half in range(2):
                h = g * hpg + 2 * pair + half
                ci = jnp.broadcast_to(acum[:, h:h + 1], (q, LANES))
                rj = jnp.broadcast_to(acum_t[h:h + 1, :], (q, LANES))
                dec = jnp.where(tri, jnp.exp(ci - rj), 0.0)
                m = (cb * dec).astype(BF16)
                own = (lane_t >= SSD_HEAD_DIM) if half else (lane_t < SSD_HEAD_DIM)
                y_h = jnp.dot(m, jnp.where(own, x_pair, 0.0).astype(BF16), preferred_element_type=F32)
                acc = y_h if acc is None else acc + y_h
            parts.append(acc)
        y_diag = jnp.concatenate(parts, axis=1)
        y_groups.append(y_diag + y_off * exa_e[:, g * gw:(g + 1) * gw])
        upd = lax.dot_general(xw_t[:, g * gw:(g + 1) * gw], bm_t, TN, preferred_element_type=F32)
        h_scr[g * gw:(g + 1) * gw, :] = h_g * dcol[g * gw:(g + 1) * gw, :] + upd

    y = (jnp.concatenate(y_groups, axis=1) + xs * dsk_ref[...]) * _silu(z)
    y = jnp.concatenate([_rms(y[:, g * gw:(g + 1) * gw]) for g in range(SSD_GROUPS)], axis=1)
    y_ssd = y * gssd_ref[...]

    extv_scr[SUBLANES:SUBLANES + q, :] = scc * sch
    cv = scw_ref[0:1, :] * extv_scr[headv:headv + q, :]
    for k in range(1, SC_CONV_W):
        cv = cv + scw_ref[k:k + 1, :] * extv_scr[headv + k:headv + k + q, :]
    t = scb * cv
    sc_per_group = t.shape[1] // SC_GROUPS
    gsum = jnp.dot((t * t).astype(BF16), et_ref[...].astype(BF16), preferred_element_type=F32)
    rs = lax.rsqrt(gsum * (1.0 / sc_per_group) + EPS)
    y_sc = t * expand(rs) * gsc_ref[...]

    ymix_ref[:, :ssd_w] = y_ssd[0:rows_in].astype(ymix_ref.dtype)
    ymix_ref[:, ssd_w:] = y_sc[0:rows_in].astype(ymix_ref.dtype)

    @pl.when(c == nc - 1)
    def _():
        ssm_ref[0] = h_scr[...]
        cst_ref[0] = ext_scr[SUBLANES + q_valid - (SSD_CONV_W - 1):SUBLANES + q_valid, :]
        scst_ref[0] = extv_scr[SUBLANES + q_valid - (SC_CONV_W - 1):SUBLANES + q_valid, :]

    ext_scr[0:SUBLANES, :] = ext_scr[q:q + SUBLANES, :]
    extv_scr[0:SUBLANES, :] = extv_scr[q:q + SUBLANES, :]


def _ssd(proj, init, consts, nb, nc, q, rows_in, q_valid, ymix_dtype):
    z, xbc, scb, scc, sch, dt = proj
    heads = consts["heads"]
    tril = jnp.tril(jnp.ones((q, LANES), F32))
    ssd_w, conv_dim, sc_w = z.shape[1], xbc.shape[1], scb.shape[1]
    has_init = init is not None
    row_spec = lambda w: pl.BlockSpec((rows_in, w), lambda b, c: (b * nc + c, 0))
    in_specs = [row_spec(ssd_w), row_spec(conv_dim), row_spec(sc_w), row_spec(sc_w), row_spec(sc_w),
                row_spec(LANES)]
    args = [z, xbc, scb, scc, sch, dt]
    if has_init:
        ssm0, cst0, scst0 = init
        in_specs += [pl.BlockSpec((1,) + ssm0.shape[1:], lambda b, c: (b, 0, 0)),
                     pl.BlockSpec((1,) + cst0.shape[1:], lambda b, c: (b, 0, 0)),
                     pl.BlockSpec((1,) + scst0.shape[1:], lambda b, c: (b, 0, 0))]
        args += [ssm0, cst0, scst0]
    weights = [consts[k] for k in ("conv_w", "conv_b", "dt_bias", "a_log", "d_skip_e", "g_ssd",
                                   "sc_w", "g_sc")] + [tril, consts["e"], consts["et"]]
    in_specs += [_const_spec(w.shape) for w in weights]
    args += weights
    n_state = D_STATE
    out_shape = [jax.ShapeDtypeStruct((nb * nc * rows_in, ssd_w + sc_w), ymix_dtype),
                 jax.ShapeDtypeStruct((nb, ssd_w, n_state), F32),
                 jax.ShapeDtypeStruct((nb, SSD_CONV_W - 1, conv_dim), F32),
                 jax.ShapeDtypeStruct((nb, SC_CONV_W - 1, sc_w), F32)]
    out_specs = [pl.BlockSpec((rows_in, ssd_w + sc_w), lambda b, c: (b * nc + c, 0)),
                 pl.BlockSpec((1, ssd_w, n_state), lambda b, c: (b, 0, 0)),
                 pl.BlockSpec((1, SSD_CONV_W - 1, conv_dim), lambda b, c: (b, 0, 0)),
                 pl.BlockSpec((1, SC_CONV_W - 1, sc_w), lambda b, c: (b, 0, 0))]
    scratch = [pltpu.VMEM((ssd_w, n_state), F32),
               pltpu.VMEM((q + SUBLANES, conv_dim), F32),
               pltpu.VMEM((q + SUBLANES, sc_w), F32)]
    return pl.pallas_call(
        functools.partial(_ssd_kernel, q=q, rows_in=rows_in, q_valid=q_valid, has_init=has_init, heads=heads),
        out_shape=out_shape, grid=(nb, nc), in_specs=in_specs, out_specs=out_specs,
        scratch_shapes=scratch,
        compiler_params=_params(("arbitrary", "arbitrary")),
        name="ssd",
    )(*args)


def _out_kernel(ymix_ref, x_ref, mod_ref, wout_ref, gpost_ref, gpre_ref, wr_hi_ref, wr_lo_ref,
                *rest, per_row, rows_per_batch, has_alias):
    x1_ref, u2_ref, lg_ref = rest[3:6] if has_alias else rest[0:3]
    tile, d = x_ref.shape
    gate1, shift2, scale2 = _mod_rows(mod_ref, per_row, rows_per_batch, tile, d)
    m = jnp.dot(ymix_ref[...].astype(BF16), wout_ref[...], preferred_element_type=F32)
    x1 = x_ref[...] + gate1 * (_rms(m) * gpost_ref[...])
    u2 = (_rms(x1) * gpre_ref[...]) * (1.0 + scale2) + shift2
    x1_ref[...] = x1
    u2_ref[...] = u2
    u_hi = u2.astype(BF16)
    u_lo = (u2 - u_hi.astype(F32)).astype(BF16)
    lg = lax.dot_general(wr_hi_ref[...], u_hi, NT, preferred_element_type=F32)
    lg = lg + lax.dot_general(wr_hi_ref[...], u_lo, NT, preferred_element_type=F32)
    lg = lg + lax.dot_general(wr_lo_ref[...], u_hi, NT, preferred_element_type=F32)
    lg_ref[...] = lg


def _out_proj(ymix, x, mod, consts, per_row, rows_per_batch, tile, t_total, tile_off, prev):
    t, d = x.shape
    dm = ymix.shape[1]
    ne = consts["wr_hi_t"].shape[0]
    in_specs = [pl.BlockSpec((tile, dm), lambda i: (i, 0)),
                pl.BlockSpec((tile, d), lambda i: (i, 0)), _mod_spec(mod, per_row, tile),
                _const_spec((dm, d)), _const_spec((1, d)), _const_spec((1, d)),
                _const_spec((ne, d)), _const_spec((ne, d))]
    args = [ymix, x, mod, consts["w_out"], consts["g_post_mix"], consts["g_pre_ffn"],
            consts["wr_hi_t"], consts["wr_lo_t"]]
    aliases = {}
    if prev is not None:
        in_specs += [pl.BlockSpec(memory_space=pl.ANY)] * 3
        aliases = {len(args) + k: k for k in range(3)}
        args += list(prev)
    return pl.pallas_call(
        functools.partial(_out_kernel, per_row=per_row, rows_per_batch=rows_per_batch,
                          has_alias=prev is not None),
        out_shape=[jax.ShapeDtypeStruct((t_total, d), F32), jax.ShapeDtypeStruct((t_total, d), F32),
                   jax.ShapeDtypeStruct((ne, t_total), F32)],
        grid=(t // tile,), in_specs=in_specs,
        out_specs=[pl.BlockSpec((tile, d), lambda i: (i + tile_off, 0)),
                   pl.BlockSpec((tile, d), lambda i: (i + tile_off, 0)),
                   pl.BlockSpec((ne, tile), lambda i: (0, i + tile_off))],
        input_output_aliases=aliases,
        compiler_params=_params(("arbitrary",)),
        name="out_proj",
    )(*args)


def _route_kernel(lg_ref, bias_ref, upper_ref, idx_ref, rank_ref, wtok_ref, cnt_ref, carry_scr):
    i = pl.program_id(0)
    ne, tm = lg_ref.shape
    per_group = ne // N_EXPERT_GROUPS
    neg = -jnp.inf

    @pl.when(i == 0)
    def _():
        carry_scr[...] = jnp.zeros(carry_scr.shape, F32)

    s = _sigmoid(lg_ref[...])
    biased = s + bias_ref[...]
    gl = []
    io_g = lax.broadcasted_iota(I32, (per_group, tm), 0).astype(F32)
    for g in range(N_EXPERT_GROUPS):
        blk = biased[g * per_group:(g + 1) * per_group, :]
        m1 = jnp.max(blk, axis=0, keepdims=True)
        f1 = jnp.min(jnp.where(blk == m1, io_g, float(per_group)), axis=0, keepdims=True)
        m2 = jnp.max(jnp.where(io_g == f1, neg, blk), axis=0, keepdims=True)
        gl.append(m1 + m2)
    gscore = jnp.concatenate(gl, axis=0)
    io8 = lax.broadcasted_iota(I32, (N_EXPERT_GROUPS, tm), 0).astype(F32)
    gsel = jnp.zeros((N_EXPERT_GROUPS, tm), F32)
    for _ in range(TOPK_GROUPS):
        m = jnp.max(gscore, axis=0, keepdims=True)
        f = jnp.min(jnp.where(gscore == m, io8, float(N_EXPERT_GROUPS)), axis=0, keepdims=True)
        hit = io8 == f
        gsel = jnp.where(hit, 1.0, gsel)
        gscore = jnp.where(hit, neg, gscore)
    emask = jnp.concatenate(
        [jnp.broadcast_to(gsel[g:g + 1, :], (per_group, tm)) for g in range(N_EXPERT_GROUPS)], axis=0)
    cand = jnp.where(emask > 0.5, biased, neg)
    io_e = lax.broadcasted_iota(I32, (ne, tm), 0).astype(F32)
    msel = jnp.zeros((ne, tm), F32)
    idxs, wts = [], []
    for _ in range(TOP_K):
        m = jnp.max(cand, axis=0, keepdims=True)
        f = jnp.min(jnp.where(cand == m, io_e, float(ne)), axis=0, keepdims=True)
        hit = io_e == f
        wts.append(jnp.sum(jnp.where(hit, s, 0.0), axis=0, keepdims=True))
        idxs.append(f)
        msel = jnp.where(hit, 1.0, msel)
        cand = jnp.where(hit, neg, cand)
    pref = jnp.dot(msel.astype(BF16), upper_ref[...], preferred_element_type=F32) + carry_scr[:, 0:1]
    ranks = [jnp.sum(jnp.where(io_e == f, pref, 0.0), axis=0, keepdims=True) for f in idxs]
    carry_scr[...] = carry_scr[...] + jnp.sum(msel, axis=1, keepdims=True)
    cnt_ref[...] = carry_scr[...].astype(I32)
    idx_ref[...] = jnp.concatenate(idxs, axis=0).astype(I32)
    rank_ref[...] = jnp.concatenate(ranks, axis=0).astype(I32)
    wsum = wts[0]
    for w in wts[1:]:
        wsum = wsum + w
    wn = jnp.concatenate([w / wsum * ROUTED_SCALE for w in wts]
                         + [jnp.zeros((LANES - TOP_K, tm), F32)], axis=0)
    for j in range(tm // LANES):
        wtok_ref[j * LANES:(j + 1) * LANES, :] = wn[:, j * LANES:(j + 1) * LANES].T


def _route(logits_t, bias, tile):
    ne, t = logits_t.shape
    upper = jnp.triu(jnp.ones((tile, tile), F32), 1).astype(BF16)
    return pl.pallas_call(
        _route_kernel,
        out_shape=[jax.ShapeDtypeStruct((TOP_K, t), I32), jax.ShapeDtypeStruct((TOP_K, t), I32),
                   jax.ShapeDtypeStruct((t, LANES), F32), jax.ShapeDtypeStruct((ne, LANES), I32)],
        grid=(t // tile,),
        in_specs=[pl.BlockSpec((ne, tile), lambda i: (0, i)), _const_spec((ne, 1)),
                  _const_spec((tile, tile))],
        out_specs=[pl.BlockSpec((TOP_K, tile), lambda i: (0, i)),
                   pl.BlockSpec((TOP_K, tile), lambda i: (0, i)),
                   pl.BlockSpec((tile, LANES), lambda i: (i, 0)),
                   _const_spec((ne, LANES))],
        scratch_shapes=[pltpu.VMEM((ne, LANES), F32)],
        compiler_params=_params(("arbitrary",)),
        name="route",
    )(logits_t, bias.reshape(ne, 1), upper)


def _pos_kernel(idx_ref, rank_ref, start_ref, pos_ref):
    ne = start_ref.shape[0]
    tm = idx_ref.shape[1]
    io_e = lax.broadcasted_iota(I32, (ne, tm), 0)
    start = start_ref[...].astype(F32)
    rows = []
    for k in range(TOP_K):
        hit = io_e == idx_ref[k:k + 1, :]
        rows.append(jnp.sum(jnp.where(hit, start, 0.0), axis=0, keepdims=True))
    pos_ref[...] = jnp.concatenate(rows, axis=0).astype(I32) + rank_ref[...]


def _positions(idx_t, rank_t, pad_start, tile):
    k, t = idx_t.shape
    ne = pad_start.shape[0]
    return pl.pallas_call(
        _pos_kernel,
        out_shape=jax.ShapeDtypeStruct((k, t), I32),
        grid=(t // tile,),
        in_specs=[pl.BlockSpec((k, tile), lambda i: (0, i)), pl.BlockSpec((k, tile), lambda i: (0, i)),
                  _const_spec((ne, 1))],
        out_specs=pl.BlockSpec((k, tile), lambda i: (0, i)),
        compiler_params=_params(("arbitrary",)),
        name="positions",
    )(idx_t, rank_t, pad_start.reshape(ne, 1))


ZERO_BITS = (64, 32, 16, 8, 4, 2, 1)
assert sum(ZERO_BITS) * SUBLANES >= ROW_TILE


def _zero_copy(zbuf, xs_ref, sem, off, bit):
    rows = bit * SUBLANES
    return pltpu.make_async_copy(zbuf.at[pl.ds(0, rows)], xs_ref.at[pl.ds(pl.multiple_of(off, SUBLANES), rows)], sem)


def _zero_kernel(first_ref, units_ref, xs_ref, zbuf, sem):
    zbuf[...] = jnp.zeros(zbuf.shape, zbuf.dtype)
    ne = first_ref.shape[0]

    def each(e, start):
        off = first_ref[e]
        units = units_ref[e]
        for bit in ZERO_BITS:
            on = (units & bit) != 0

            @pl.when(on)
            def _():
                cp = _zero_copy(zbuf, xs_ref, sem, off, bit)
                cp.start() if start else cp.wait()
            off = off + jnp.where(on, bit * SUBLANES, 0)
        return start

    lax.fori_loop(0, ne, lambda e, c: (each(e, True), c)[1], 0)
    lax.fori_loop(0, ne, lambda e, c: (each(e, False), c)[1], 0)


def _zero_rows(first, units, n_rows, width):
    return pl.pallas_call(
        _zero_kernel,
        out_shape=jax.ShapeDtypeStruct((n_rows, width), F32),
        in_specs=[pl.BlockSpec(memory_space=pltpu.SMEM), pl.BlockSpec(memory_space=pltpu.SMEM)],
        out_specs=pl.BlockSpec(memory_space=pl.ANY),
        scratch_shapes=[pltpu.VMEM((ZERO_BITS[0] * SUBLANES, width), F32), pltpu.SemaphoreType.DMA],
        compiler_params=_params(),
        name="zero_rows",
    )(first, units)


def _row_copy(src_ref, dst_ref, sem, s, d):
    return pltpu.make_async_copy(src_ref.at[pl.ds(s, 1)], dst_ref.at[pl.ds(d, 1)], sem)


def _load_positions(pos_hbm, pos_smem, psem, tile_idx):
    per = pos_smem.shape[0]
    cp = pltpu.make_async_copy(pos_hbm.at[pl.ds(pl.multiple_of(tile_idx * per, per), per)], pos_smem, psem)
    cp.start()
    cp.wait()


def _dispatch_kernel(pos_hbm, u_ref, xs_in, xs_ref, pos_smem, psem, sem):
    del xs_in
    tm = u_ref.shape[0]
    _load_positions(pos_hbm, pos_smem, psem, pl.program_id(0))

    def issue(t, carry):
        for k in range(TOP_K):
            _row_copy(u_ref, xs_ref, sem, t, pos_smem[k * tm + t]).start(priority=k % DMA_THREADS)
        return carry

    lax.fori_loop(0, tm, issue, 0)
    for k in range(TOP_K):
        pltpu.make_async_copy(u_ref, xs_ref.at[pl.ds(0, tm)], sem).wait()


def _dispatch(pos_flat, u2, xs, tm):
    per = TOP_K * tm
    nt = pos_flat.shape[0] // per
    d = u2.shape[1]
    return pl.pallas_call(
        _dispatch_kernel,
        out_shape=jax.ShapeDtypeStruct(xs.shape, xs.dtype),
        grid=(nt,),
        in_specs=[pl.BlockSpec(memory_space=pl.ANY), pl.BlockSpec((tm, d), lambda i: (i, 0)),
                  pl.BlockSpec(memory_space=pl.ANY)],
        out_specs=pl.BlockSpec(memory_space=pl.ANY),
        scratch_shapes=[pltpu.SMEM((per,), I32), pltpu.SemaphoreType.DMA, pltpu.SemaphoreType.DMA],
        input_output_aliases={2: 0},
        compiler_params=_params(("arbitrary",)),
        name="dispatch",
    )(pos_flat, u2, xs)


def _expert_kernel(te_ref, nu_ref, xs_ref, wg_ref, wu_ref, wd_ref, y_ref):
    del te_ref

    @pl.when(pl.program_id(0) < nu_ref[0])
    def _():
        x = xs_ref[...].astype(BF16)
        g = jnp.dot(x, wg_ref[0].astype(BF16), preferred_element_type=F32)
        u = jnp.dot(x, wu_ref[0].astype(BF16), preferred_element_type=F32)
        h = (_silu(g) * u).astype(BF16)
        y_ref[...] = jnp.dot(h, wd_ref[0].astype(BF16), preferred_element_type=F32)


def _experts(tile_expert, n_used, xs, w_gate, w_up, w_down):
    n_rows, d = xs.shape
    de = w_gate.shape[2]
    n_tiles = n_rows // ROW_TILE
    row_map = lambda i, te, nu: (jnp.minimum(i, nu[0] - 1), 0)
    return pl.pallas_call(
        _expert_kernel,
        out_shape=jax.ShapeDtypeStruct((n_rows, d), F32),
        grid_spec=pltpu.PrefetchScalarGridSpec(
            num_scalar_prefetch=2, grid=(n_tiles,),
            in_specs=[pl.BlockSpec((ROW_TILE, d), row_map),
                      pl.BlockSpec((1, d, de), lambda i, te, nu: (te[i], 0, 0)),
                      pl.BlockSpec((1, d, de), lambda i, te, nu: (te[i], 0, 0)),
                      pl.BlockSpec((1, de, d), lambda i, te, nu: (te[i], 0, 0))],
            out_specs=pl.BlockSpec((ROW_TILE, d), row_map)),
        compiler_params=_params(("arbitrary",)),
        name="experts",
    )(tile_expert, n_used, xs, w_gate, w_up, w_down)


def _combine_kernel(pos_hbm, y_hbm, wtok_ref, u_ref, x1_ref, mod_ref, wsg_ref, wsu_ref, wsd_ref, gpost_ref,
                    o_ref, pos_smem, buf, psem, sem, *, per_row, rows_per_batch, tile_off):
    tm, d = u_ref.shape
    _load_positions(pos_hbm, pos_smem, psem, pl.program_id(0) + tile_off)

    def issue(t, carry):
        for k in range(TOP_K):
            _row_copy(y_hbm, buf.at[k], sem, pos_smem[k * tm + t], t).start(priority=k % DMA_THREADS)
        return carry

    lax.fori_loop(0, tm, issue, 0)

    ub = u_ref[...].astype(BF16)
    hs = _silu(jnp.dot(ub, wsg_ref[...], preferred_element_type=F32)) * jnp.dot(
        ub, wsu_ref[...], preferred_element_type=F32)
    f = jnp.dot(hs.astype(BF16), wsd_ref[...], preferred_element_type=F32)

    for k in range(TOP_K):
        pltpu.make_async_copy(y_hbm.at[pl.ds(0, tm)], buf.at[k], sem).wait()
    w = wtok_ref[...]
    for k in range(TOP_K):
        f = f + w[:, k:k + 1] * buf[k]
    (gate2,) = _mod_rows(mod_ref, per_row, rows_per_batch, tm, d)
    o_ref[...] = x1_ref[...] + gate2 * (_rms(f) * gpost_ref[...])


def _combine(pos_flat, y_rows, wtok, u2, x1, mod, consts, per_row, rows_per_batch, tile_off, n_tok, tm):
    per = TOP_K * tm
    d = u2.shape[1]
    ds_ = consts["w_sh_gate"].shape[1]
    tok_spec = lambda w: pl.BlockSpec((tm, w), lambda i: (i + tile_off, 0))
    mod_spec = _mod_spec(mod, per_row, tm)
    return pl.pallas_call(
        functools.partial(_combine_kernel, per_row=per_row, rows_per_batch=rows_per_batch, tile_off=tile_off),
        out_shape=jax.ShapeDtypeStruct((n_tok, d), F32),
        grid=(n_tok // tm,),
        in_specs=[pl.BlockSpec(memory_space=pl.ANY), pl.BlockSpec(memory_space=pl.ANY),
                  tok_spec(LANES), tok_spec(d), tok_spec(d), mod_spec,
                  _const_spec((d, ds_)), _const_spec((d, ds_)), _const_spec((ds_, d)), _const_spec((1, d))],
        out_specs=pl.BlockSpec((tm, d), lambda i: (i, 0)),
        scratch_shapes=[pltpu.SMEM((per,), I32), pltpu.VMEM((TOP_K, tm, d), F32),
                        pltpu.SemaphoreType.DMA, pltpu.SemaphoreType.DMA],
        compiler_params=_params(("arbitrary",)),
        name="combine",
    )(pos_flat, y_rows, wtok, u2, x1, mod, consts["w_sh_gate"], consts["w_sh_up"], consts["w_sh_down"],
      consts["g_post_ffn"])


def _repeat_rows(x, n):
    r, c = x.shape
    return jnp.broadcast_to(x[:, None, :], (r, n, c)).reshape(r * n, c)


def _tile_major(pos_t, tile):
    k, t = pos_t.shape
    return pos_t.reshape(k, t // tile, tile).transpose(1, 0, 2).reshape(-1)


def kernel(x_prompt, x_sample, c_prompt, c_sample, state_ssm, state_ssd_conv, state_short_conv, w_ada, b_ada, g_pre_mix, g_post_mix, g_pre_ffn, g_post_ffn, w_in, ssd_conv_w, ssd_conv_b, dt_bias, a_log, d_skip, g_ssd_norm, sc_conv_w, g_sc_norm, w_out, w_router, router_bias, w_exp_gate, w_exp_up, w_exp_down, w_sh_gate, w_sh_up, w_sh_down):
    depth = w_ada.shape[0]
    bp, seq, d = x_prompt.shape
    bs, dseq, _ = x_sample.shape
    heads = dt_bias.shape[1]
    ssd_w = heads * SSD_HEAD_DIM
    conv_dim = ssd_conv_w.shape[2]
    sc_w = sc_conv_w.shape[2]
    ne = w_router.shape[2]
    tp, ts = bp * seq, bs * dseq
    t_all = tp + ts

    assert sc_w // SC_GROUPS == SSD_HEAD_DIM and heads == SC_GROUPS
    head_of = jnp.arange(ssd_w, dtype=I32) // SSD_HEAD_DIM
    e_ind = (jnp.arange(LANES, dtype=I32)[:, None] == head_of[None, :])

    xp = x_prompt.reshape(tp, d)
    xs_pad = jnp.pad(x_sample, ((0, 0), (0, SUBLANES - dseq), (0, 0))).reshape(bs * SUBLANES, d)
    xs_tok = x_sample.reshape(ts, d)
    outs = {k: [] for k in ("ssm_p", "cst_p", "scst_p", "ssm_s", "cst_s", "scst_s")}

    for layer in range(depth):
        cuts = np.cumsum([0, ssd_w, conv_dim, heads, sc_w, sc_w, sc_w]).tolist()
        wi = w_in[layer]
        seg = lambda k: wi[:, cuts[k]:cuts[k + 1]]
        w_in_r = jnp.concatenate([seg(0), seg(1), seg(3), seg(4), seg(5),
                                  jnp.pad(seg(2), ((0, 0), (0, LANES - heads)))], axis=1).astype(BF16)
        widths = (ssd_w, conv_dim, sc_w, sc_w, sc_w, LANES)
        pad_h = lambda v: jnp.pad(v.reshape(1, heads), ((0, 0), (0, LANES - heads)))
        wr = w_router[layer].T
        wr_hi = wr.astype(BF16)
        consts = dict(
            heads=heads,
            conv_w=ssd_conv_w[layer], conv_b=ssd_conv_b[layer].reshape(1, conv_dim),
            dt_bias=pad_h(dt_bias[layer]), a_log=pad_h(a_log[layer]),
            d_skip_e=jnp.broadcast_to(d_skip[layer][:, None], (heads, SSD_HEAD_DIM)).reshape(1, ssd_w),
            g_ssd=g_ssd_norm[layer].reshape(1, ssd_w), sc_w=sc_conv_w[layer],
            g_sc=g_sc_norm[layer].reshape(1, sc_w),
            e=e_ind.astype(BF16), et=e_ind.T.astype(F32),
            w_out=w_out[layer].astype(BF16), g_post_mix=g_post_mix[layer].reshape(1, d),
            g_pre_ffn=g_pre_ffn[layer].reshape(1, d),
            wr_hi_t=wr_hi, wr_lo_t=(wr - wr_hi.astype(F32)).astype(BF16),
            w_sh_gate=w_sh_gate[layer].astype(BF16), w_sh_up=w_sh_up[layer].astype(BF16),
            w_sh_down=w_sh_down[layer].astype(BF16), g_post_ffn=g_post_ffn[layer].reshape(1, d))

        c_all = jnp.concatenate([c_prompt, c_sample], axis=0)
        m_rows = -(-c_all.shape[0] // 16) * 16
        mod = _ada(jnp.pad(c_all, ((0, m_rows - c_all.shape[0]), (0, 0))), w_ada[layer], b_ada[layer])
        mod_p = mod[:bp]
        mod_s = mod[bp:bp + bs]
        mod_s_pad = _repeat_rows(mod_s[:, :2 * d], SUBLANES)
        mod_s_tok = _repeat_rows(mod_s, dseq)

        proj_p = _in_proj(xp, mod_p[:, :2 * d], g_pre_mix[layer], w_in_r, widths, BF16, False, seq, TOK_TILE)
        proj_s = _in_proj(xs_pad, mod_s_pad, g_pre_mix[layer], w_in_r, widths, F32, True, 1, TOK_TILE)
        ymix_p, ssm_p, cst_p, scst_p = _ssd(proj_p, None, consts, bp, seq // CHUNK, CHUNK, CHUNK, CHUNK, BF16)
        init = (state_ssm[layer].reshape(bs, ssd_w, D_STATE), state_ssd_conv[layer], state_short_conv[layer])
        ymix_s, ssm_s, cst_s, scst_s = _ssd(proj_s, init, consts, bs, 1, SAMPLE_CHUNK, SUBLANES, dseq, F32)
        ymix_s = ymix_s.reshape(bs, SUBLANES, ssd_w + sc_w)[:, :dseq].reshape(ts, ssd_w + sc_w)

        merged = _out_proj(ymix_p, xp, mod_p[:, 2 * d:5 * d], consts, False, seq, TOK_TILE, t_all, 0, None)
        x1, u2, logits_t = _out_proj(ymix_s, xs_tok, mod_s_tok[:, 2 * d:5 * d], consts, True, 1, TOK_TILE,
                                     t_all, tp // TOK_TILE, merged)

        idx_t, rank_t, wtok, counts = _route(logits_t, router_bias[layer], TOK_TILE)
        counts = counts[:, 0]
        padded = (counts + ROW_TILE - 1) // ROW_TILE * ROW_TILE
        pad_end = jnp.cumsum(padded)
        pad_start = pad_end - padded
        n_tiles = -(-(t_all * TOP_K) // ROW_TILE) + ne
        n_used = (pad_end[-1] // ROW_TILE).astype(I32)
        tile_ids = jnp.minimum(jnp.arange(n_tiles, dtype=I32), n_used - 1)
        tile_expert = jnp.minimum(jnp.sum(pad_end[None, :] <= (tile_ids * ROW_TILE)[:, None], axis=1), ne - 1).astype(I32)
        first_pad = (pad_start + counts) // SUBLANES * SUBLANES
        units = (pad_end - first_pad) // SUBLANES
        pos_t = _positions(idx_t, rank_t, pad_start.astype(I32), TOK_TILE)
        pos_flat = _tile_major(pos_t, GATHER_TILE)

        xs_rows = _zero_rows(first_pad.astype(I32), units.astype(I32), n_tiles * ROW_TILE, d)
        xs_rows = _dispatch(pos_flat, u2, xs_rows, GATHER_TILE)
        y_rows = _experts(tile_expert, n_used.reshape(1), xs_rows, w_exp_gate[layer], w_exp_up[layer],
                          w_exp_down[layer])
        xp = _combine(pos_flat, y_rows, wtok, u2, x1, mod_p[:, 5 * d:], consts, False, seq, 0, tp, GATHER_TILE)
        xs_tok = _combine(pos_flat, y_rows, wtok, u2, x1, mod_s_tok[:, 5 * d:], consts, True, 1,
                          tp // GATHER_TILE, ts, GATHER_TILE)
        xs_pad = jnp.pad(xs_tok.reshape(bs, dseq, d), ((0, 0), (0, SUBLANES - dseq), (0, 0))).reshape(
            bs * SUBLANES, d)

        outs["ssm_p"].append(ssm_p.reshape(bp, heads, SSD_HEAD_DIM, D_STATE))
        outs["cst_p"].append(cst_p)
        outs["scst_p"].append(scst_p)
        outs["ssm_s"].append(ssm_s.reshape(bs, heads, SSD_HEAD_DIM, D_STATE))
        outs["cst_s"].append(cst_s)
        outs["scst_s"].append(scst_s)

    return (xp.reshape(bp, seq, d), xs_tok.reshape(bs, dseq, d),
            jnp.stack(outs["ssm_p"]), jnp.stack(outs["cst_p"]), jnp.stack(outs["scst_p"]),
            jnp.stack(outs["ssm_s"]), jnp.stack(outs["cst_s"]), jnp.stack(outs["scst_s"]))
```

```python
import functools

import jax
import jax.numpy as jnp
import numpy as np
from jax import lax
from jax.experimental import pallas as pl
from jax.experimental.pallas import tpu as pltpu

F32 = jnp.float32
BF16 = jnp.bfloat16
I32 = jnp.int32
U32 = jnp.uint32
HI = lax.Precision.HIGHEST

SSD_HEAD_DIM = 64
SSD_GROUPS = 2
D_STATE = 128
SSD_CONV_W = 4
SC_GROUPS = 16
SC_CONV_W = 3
TOP_K = 8
N_EXPERT_GROUPS = 8
TOPK_GROUPS = 4
ROUTED_SCALE = 2.5
EPS = 1e-6

LANES = 128
SUBLANES = 8
CHUNK = 128
SAMPLE_CHUNK = 16
TOK_TILE = 512
ROW_TILE = 576
GATHER_TILE = 256
VMEM_LIMIT = 56 * 1024 * 1024
DMA_THREADS = 2

NT = (((1,), (1,)), ((), ()))
TN = (((0,), (0,)), ((), ()))


def _sigmoid(x):
    return 1.0 / (1.0 + jnp.exp(-x))


def _silu(x):
    return x * _sigmoid(x)


def _softplus(x):
    return jnp.maximum(x, 0.0) + jnp.log1p(jnp.exp(-jnp.abs(x)))


def _rms(x, eps=EPS):
    return x * lax.rsqrt(jnp.mean(x * x, axis=-1, keepdims=True) + eps)


def _params(sem=None):
    return pltpu.CompilerParams(dimension_semantics=sem, vmem_limit_bytes=VMEM_LIMIT)


def _const_spec(shape, single=False):
    nd = len(shape)
    mode = dict(pipeline_mode=pl.Buffered(1)) if single else {}
    return pl.BlockSpec(shape, lambda *_: (0,) * nd, **mode)


def _ada_kernel(c_ref, w_ref, b_ref, o_ref):
    c = c_ref[...]
    s = _silu(c).astype(BF16)
    o_ref[...] = jnp.dot(s, w_ref[...].astype(BF16), preferred_element_type=F32) + b_ref[...]


def _ada(c, w_ada, b_ada):
    m, d = c.shape
    n = w_ada.shape[1]
    tn = 512
    return pl.pallas_call(
        _ada_kernel,
        out_shape=jax.ShapeDtypeStruct((m, n), F32),
        grid=(n // tn,),
        in_specs=[_const_spec((m, d)),
                  pl.BlockSpec((d, tn), lambda j: (0, j)),
                  pl.BlockSpec((1, tn), lambda j: (0, j))],
        out_specs=pl.BlockSpec((m, tn), lambda j: (0, j)),
        compiler_params=_params(("arbitrary",)),
        name="ada",
    )(c, w_ada, b_ada.reshape(1, n))


def _mod_rows(mod_ref, per_row, rows_per_batch, tile, d):
    n = mod_ref.shape[1] // d
    if per_row:
        return [mod_ref[:, k * d:(k + 1) * d] for k in range(n)]
    b = (pl.program_id(0) * tile) // rows_per_batch
    return [mod_ref[pl.ds(b, 1), k * d:(k + 1) * d] for k in range(n)]


def _mod_spec(mod, per_row, tile):
    if per_row:
        return pl.BlockSpec((tile, mod.shape[1]), lambda i: (i, 0))
    return _const_spec(mod.shape)


def _in_kernel(x_ref, mod_ref, g_ref, w_ref, *out_refs, per_row, rows_per_batch, widths):
    tile, d = x_ref.shape
    shift, scale = _mod_rows(mod_ref, per_row, rows_per_batch, tile, d)
    u = (_rms(x_ref[...]) * g_ref[...]) * (1.0 + scale) + shift
    u = u.astype(BF16)
    col = 0
    for ref, width in zip(out_refs, widths):
        for a in range(0, width, 512):
            bw = min(512, width - a)
            r = jnp.dot(u, w_ref[:, col + a:col + a + bw], preferred_element_type=F32)
            ref[:, a:a + bw] = r.astype(ref.dtype)
        col += width


def _in_proj(x, mod, g, w_bf16, widths, out_dtype, per_row, rows_per_batch, tile):
    t, d = x.shape
    n = w_bf16.shape[1]
    mod_spec = _mod_spec(mod, per_row, tile)
    dts = [out_dtype] * (len(widths) - 1) + [F32]
    return pl.pallas_call(
        functools.partial(_in_kernel, per_row=per_row, rows_per_batch=rows_per_batch, widths=widths),
        out_shape=[jax.ShapeDtypeStruct((t, wd), dt) for wd, dt in zip(widths, dts)],
        grid=(t // tile,),
        in_specs=[pl.BlockSpec((tile, d), lambda i: (i, 0)), mod_spec,
                  _const_spec((1, d)), _const_spec((d, n), single=True)],
        out_specs=[pl.BlockSpec((tile, wd), lambda i: (i, 0)) for wd in widths],
        compiler_params=_params(("arbitrary",)),
        name="in_proj",
    )(x, mod, g.reshape(1, d), w_bf16)


def _ssd_kernel(*refs, q, rows_in, q_valid, has_init, heads):
    it = iter(refs)
    z_ref, xbc_ref, scb_ref, scc_ref, sch_ref, dt_ref = (next(it) for _ in range(6))
    if has_init:
        ssm0_ref, cst0_ref, scst0_ref = (next(it) for _ in range(3))
    (cw_ref, cb_ref, dtb_ref, alog_ref, dsk_ref, gssd_ref, scw_ref, gsc_ref,
     tril_ref, e_ref, et_ref) = (next(it) for _ in range(11))
    ymix_ref, ssm_ref, cst_ref, scst_ref = (next(it) for _ in range(4))
    h_scr, ext_scr, extv_scr = (next(it) for _ in range(3))

    c = pl.program_id(1)
    nc = pl.num_programs(1)
    ssd_w = dsk_ref.shape[1]
    gw = ssd_w // SSD_GROUPS
    hpg = heads // SSD_GROUPS
    n_state = D_STATE
    head0 = SUBLANES - (SSD_CONV_W - 1)
    headv = SUBLANES - (SC_CONV_W - 1)

    @pl.when(c == 0)
    def _():
        if has_init:
            h_scr[...] = ssm0_ref[0]
            ext_scr[head0:SUBLANES, :] = cst0_ref[0]
            extv_scr[headv:SUBLANES, :] = scst0_ref[0]
        else:
            h_scr[...] = jnp.zeros(h_scr.shape, F32)
            ext_scr[0:SUBLANES, :] = jnp.zeros((SUBLANES, ext_scr.shape[1]), F32)
            extv_scr[0:SUBLANES, :] = jnp.zeros((SUBLANES, extv_scr.shape[1]), F32)

    def pad_rows(v, n):
        if v.shape[0] == n:
            return v
        return jnp.concatenate([v, jnp.zeros((n - v.shape[0], v.shape[1]), v.dtype)], axis=0)

    load = lambda ref: pad_rows(ref[...].astype(F32), q)
    pad_t = lambda v: pad_rows(v, LANES)
    z, scb, scc, sch, dt_raw = load(z_ref), load(scb_ref), load(scc_ref), load(sch_ref), load(dt_ref)

    ext_scr[SUBLANES:SUBLANES + q, :] = load(xbc_ref)
    conv = cb_ref[...]
    for k in range(SSD_CONV_W):
        conv = conv + cw_ref[k:k + 1, :] * ext_scr[head0 + k:head0 + k + q, :]
    xc = _silu(conv)
    xs = xc[:, :ssd_w]
    bm = xc[:, ssd_w:ssd_w + SSD_GROUPS * n_state]
    cm = xc[:, ssd_w + SSD_GROUPS * n_state:]

    lane = lax.broadcasted_iota(I32, (q, LANES), 1)
    row = lax.broadcasted_iota(I32, (q, LANES), 0)
    dt = _softplus(dt_raw + dtb_ref[...])
    dt = jnp.where(jnp.logical_and(lane < heads, row < q_valid), dt, 0.0)
    a = dt * (-jnp.exp(alog_ref[...]))
    acum = jnp.dot(tril_ref[...], pad_t(a), precision=HI, preferred_element_type=F32)
    acum_t = pad_t(acum).T
    a_last = acum[q - 1:q, :]
    e = e_ref[...]
    expand = lambda v: jnp.dot(v.astype(BF16), e, preferred_element_type=F32)
    dt_e = expand(dt)
    dtdte_e = expand(dt * jnp.exp(a_last - acum))
    exa_e = expand(jnp.exp(acum))
    xdt = xs * dt_e
    xw_b = (xs * dtdte_e).astype(BF16)
    dlast = jnp.broadcast_to(jnp.exp(acum_t[:, q - 1:q]), (LANES, n_state))
    dcol = jnp.dot(et_ref[...], dlast, precision=HI, preferred_element_type=F32)

    tri = row >= lane
    xdt_t = pad_t(xdt)
    xw_t = pad_t(xw_b)
    lane_t = lax.broadcasted_iota(I32, (LANES, LANES), 1)
    y_groups = []
    for g in range(SSD_GROUPS):
        bm_g = bm[:, g * n_state:(g + 1) * n_state].astype(BF16)
        cm_g = cm[:, g * n_state:(g + 1) * n_state].astype(BF16)
        bm_t = pad_t(bm_g)
        cb = lax.dot_general(cm_g, bm_t, NT, preferred_element_type=F32)
        h_g = h_scr[g * gw:(g + 1) * gw, :]
        y_off = lax.dot_general(cm_g, h_g.astype(BF16), NT, preferred_element_type=F32)
        parts = []
        for pair in range(hpg // 2):
            lo = (g * hpg + 2 * pair) * SSD_HEAD_DIM
            x_pair = xdt_t[:, lo:lo + LANES]
            acc = None
            for half in range(2):
                h = g * hpg + 2 * pair + half
                ci = jnp.broadcast_to(acum[:, h:h + 1], (q, LANES))
                rj = jnp.broadcast_to(acum_t[h:h + 1, :], (q, LANES))
                dec = jnp.where(tri, jnp.exp(ci - rj), 0.0)
                m = (cb * dec).astype(BF16)
                own = (lane_t >= SSD_HEAD_DIM) if half else (lane_t < SSD_HEAD_DIM)
                y_h = jnp.dot(m, jnp.where(own, x_pair, 0.0).astype(BF16), preferred_element_type=F32)
                acc = y_h if acc is None else acc + y_h
            parts.append(acc)
        y_diag = jnp.concatenate(parts, axis=1)
        y_groups.append(y_diag + y_off * exa_e[:, g * gw:(g + 1) * gw])
        upd = lax.dot_general(xw_t[:, g * gw:(g + 1) * gw], bm_t, TN, preferred_element_type=F32)
        h_scr[g * gw:(g + 1) * gw, :] = h_g * dcol[g * gw:(g + 1) * gw, :] + upd

    y = (jnp.concatenate(y_groups, axis=1) + xs * dsk_ref[...]) * _silu(z)
    y = jnp.concatenate([_rms(y[:, g * gw:(g + 1) * gw]) for g in range(SSD_GROUPS)], axis=1)
    y_ssd = y * gssd_ref[...]

    extv_scr[SUBLANES:SUBLANES + q, :] = scc * sch
    cv = scw_ref[0:1, :] * extv_scr[headv:headv + q, :]
    for k in range(1, SC_CONV_W):
        cv = cv + scw_ref[k:k + 1, :] * extv_scr[headv + k:headv + k + q, :]
    t = scb * cv
    sc_per_group = t.shape[1] // SC_GROUPS
    gsum = jnp.dot((t * t).astype(BF16), et_ref[...].astype(BF16), preferred_element_type=F32)
    rs = lax.rsqrt(gsum * (1.0 / sc_per_group) + EPS)
    y_sc = t * expand(rs) * gsc_ref[...]

    ymix_ref[:, :ssd_w] = y_ssd[0:rows_in].astype(ymix_ref.dtype)
    ymix_ref[:, ssd_w:] = y_sc[0:rows_in].astype(ymix_ref.dtype)

    @pl.when(c == nc - 1)
    def _():
        ssm_ref[0] = h_scr[...]
        cst_ref[0] = ext_scr[SUBLANES + q_valid - (SSD_CONV_W - 1):SUBLANES + q_valid, :]
        scst_ref[0] = extv_scr[SUBLANES + q_valid - (SC_CONV_W - 1):SUBLANES + q_valid, :]

    ext_scr[0:SUBLANES, :] = ext_scr[q:q + SUBLANES, :]
    extv_scr[0:SUBLANES, :] = extv_scr[q:q + SUBLANES, :]


def _ssd(proj, init, consts, nb, nc, q, rows_in, q_valid, ymix_dtype):
    z, xbc, scb, scc, sch, dt = proj
    heads = consts["heads"]
    tril = jnp.tril(jnp.ones((q, LANES), F32))
    ssd_w, conv_dim, sc_w = z.shape[1], xbc.shape[1], scb.shape[1]
    has_init = init is not None
    row_spec = lambda w: pl.BlockSpec((rows_in, w), lambda b, c: (b * nc + c, 0))
    in_specs = [row_spec(ssd_w), row_spec(conv_dim), row_spec(sc_w), row_spec(sc_w), row_spec(sc_w),
                row_spec(LANES)]
    args = [z, xbc, scb, scc, sch, dt]
    if has_init:
        ssm0, cst0, scst0 = init
        in_specs += [pl.BlockSpec((1,) + ssm0.shape[1:], lambda b, c: (b, 0, 0)),
                     pl.BlockSpec((1,) + cst0.shape[1:], lambda b, c: (b, 0, 0)),
                     pl.BlockSpec((1,) + scst0.shape[1:], lambda b, c: (b, 0, 0))]
        args += [ssm0, cst0, scst0]
    weights = [consts[k] for k in ("conv_w", "conv_b", "dt_bias", "a_log", "d_skip_e", "g_ssd",
                                   "sc_w", "g_sc")] + [tril, consts["e"], consts["et"]]
    in_specs += [_const_spec(w.shape) for w in weights]
    args += weights
    n_state = D_STATE
    out_shape = [jax.ShapeDtypeStruct((nb * nc * rows_in, ssd_w + sc_w), ymix_dtype),
                 jax.ShapeDtypeStruct((nb, ssd_w, n_state), F32),
                 jax.ShapeDtypeStruct((nb, SSD_CONV_W - 1, conv_dim), F32),
                 jax.ShapeDtypeStruct((nb, SC_CONV_W - 1, sc_w), F32)]
    out_specs = [pl.BlockSpec((rows_in, ssd_w + sc_w), lambda b, c: (b * nc + c, 0)),
                 pl.BlockSpec((1, ssd_w, n_state), lambda b, c: (b, 0, 0)),
                 pl.BlockSpec((1, SSD_CONV_W - 1, conv_dim), lambda b, c: (b, 0, 0)),
                 pl.BlockSpec((1, SC_CONV_W - 1, sc_w), lambda b, c: (b, 0, 0))]
    scratch = [pltpu.VMEM((ssd_w, n_state), F32),
               pltpu.VMEM((q + SUBLANES, conv_dim), F32),
               pltpu.VMEM((q + SUBLANES, sc_w), F32)]
    return pl.pallas_call(
        functools.partial(_ssd_kernel, q=q, rows_in=rows_in, q_valid=q_valid, has_init=has_init, heads=heads),
        out_shape=out_shape, grid=(nb, nc), in_specs=in_specs, out_specs=out_specs,
        scratch_shapes=scratch,
        compiler_params=_params(("arbitrary", "arbitrary")),
        name="ssd",
    )(*args)


def _out_kernel(ymix_ref, x_ref, mod_ref, wout_ref, gpost_ref, gpre_ref, wr_hi_ref, wr_lo_ref,
                *rest, per_row, rows_per_batch, has_alias):
    x1_ref, u2_ref, lg_ref = rest[3:6] if has_alias else rest[0:3]
    tile, d = x_ref.shape
    gate1, shift2, scale2 = _mod_rows(mod_ref, per_row, rows_per_batch, tile, d)
    m = jnp.dot(ymix_ref[...].astype(BF16), wout_ref[...], preferred_element_type=F32)
    x1 = x_ref[...] + gate1 * (_rms(m) * gpost_ref[...])
    u2 = (_rms(x1) * gpre_ref[...]) * (1.0 + scale2) + shift2
    x1_ref[...] = x1
    u2_ref[...] = u2
    u_hi = u2.astype(BF16)
    u_lo = (u2 - u_hi.astype(F32)).astype(BF16)
    lg = lax.dot_general(wr_hi_ref[...], u_hi, NT, preferred_element_type=F32)
    lg = lg + lax.dot_general(wr_hi_ref[...], u_lo, NT, preferred_element_type=F32)
    lg = lg + lax.dot_general(wr_lo_ref[...], u_hi, NT, preferred_element_type=F32)
    lg_ref[...] = lg


def _out_proj(ymix, x, mod, consts, per_row, rows_per_batch, tile, t_total, tile_off, prev):
    t, d = x.shape
    dm = ymix.shape[1]
    ne = consts["wr_hi_t"].shape[0]
    in_specs = [pl.BlockSpec((tile, dm), lambda i: (i, 0)),
                pl.BlockSpec((tile, d), lambda i: (i, 0)), _mod_spec(mod, per_row, tile),
                _const_spec((dm, d)), _const_spec((1, d)), _const_spec((1, d)),
                _const_spec((ne, d)), _const_spec((ne, d))]
    args = [ymix, x, mod, consts["w_out"], consts["g_post_mix"], consts["g_pre_ffn"],
            consts["wr_hi_t"], consts["wr_lo_t"]]
    aliases = {}
    if prev is not None:
        in_specs += [pl.BlockSpec(memory_space=pl.ANY)] * 3
        aliases = {len(args) + k: k for k in range(3)}
        args += list(prev)
    return pl.pallas_call(
        functools.partial(_out_kernel, per_row=per_row, rows_per_batch=rows_per_batch,
                          has_alias=prev is not None),
        out_shape=[jax.ShapeDtypeStruct((t_total, d), F32), jax.ShapeDtypeStruct((t_total, d), F32),
                   jax.ShapeDtypeStruct((ne, t_total), F32)],
        grid=(t // tile,), in_specs=in_specs,
        out_specs=[pl.BlockSpec((tile, d), lambda i: (i + tile_off, 0)),
                   pl.BlockSpec((tile, d), lambda i: (i + tile_off, 0)),
                   pl.BlockSpec((ne, tile), lambda i: (0, i + tile_off))],
        input_output_aliases=aliases,
        compiler_params=_params(("arbitrary",)),
        name="out_proj",
    )(*args)


def _route_kernel(lg_ref, bias_ref, upper_ref, idx_ref, rank_ref, wtok_ref, cnt_ref, carry_scr):
    i = pl.program_id(0)
    ne, tm = lg_ref.shape
    per_group = ne // N_EXPERT_GROUPS
    neg = -jnp.inf

    @pl.when(i == 0)
    def _():
        carry_scr[...] = jnp.zeros(carry_scr.shape, F32)

    s = _sigmoid(lg_ref[...])
    biased = s + bias_ref[...]
    gl = []
    io_g = lax.broadcasted_iota(I32, (per_group, tm), 0).astype(F32)
    for g in range(N_EXPERT_GROUPS):
        blk = biased[g * per_group:(g + 1) * per_group, :]
        m1 = jnp.max(blk, axis=0, keepdims=True)
        f1 = jnp.min(jnp.where(blk == m1, io_g, float(per_group)), axis=0, keepdims=True)
        m2 = jnp.max(jnp.where(io_g == f1, neg, blk), axis=0, keepdims=True)
        gl.append(m1 + m2)
    gscore = jnp.concatenate(gl, axis=0)
    io8 = lax.broadcasted_iota(I32, (N_EXPERT_GROUPS, tm), 0).astype(F32)
    gsel = jnp.zeros((N_EXPERT_GROUPS, tm), F32)
    for _ in range(TOPK_GROUPS):
        m = jnp.max(gscore, axis=0, keepdims=True)
        f = jnp.min(jnp.where(gscore == m, io8, float(N_EXPERT_GROUPS)), axis=0, keepdims=True)
        hit = io8 == f
        gsel = jnp.where(hit, 1.0, gsel)
        gscore = jnp.where(hit, neg, gscore)
    emask = jnp.concatenate(
        [jnp.broadcast_to(gsel[g:g + 1, :], (per_group, tm)) for g in range(N_EXPERT_GROUPS)], axis=0)
    cand = jnp.where(emask > 0.5, biased, neg)
    io_e = lax.broadcasted_iota(I32, (ne, tm), 0).astype(F32)
    msel = jnp.zeros((ne, tm), F32)
    idxs, wts = [], []
    for _ in range(TOP_K):
        m = jnp.max(cand, axis=0, keepdims=True)
        f = jnp.min(jnp.where(cand == m, io_e, float(ne)), axis=0, keepdims=True)
        hit = io_e == f
        wts.append(jnp.sum(jnp.where(hit, s, 0.0), axis=0, keepdims=True))
        idxs.append(f)
        msel = jnp.where(hit, 1.0, msel)
        cand = jnp.where(hit, neg, cand)
    pref = jnp.dot(msel.astype(BF16), upper_ref[...], preferred_element_type=F32) + carry_scr[:, 0:1]
    ranks = [jnp.sum(jnp.where(io_e == f, pref, 0.0), axis=0, keepdims=True) for f in idxs]
    carry_scr[...] = carry_scr[...] + jnp.sum(msel, axis=1, keepdims=True)
    cnt_ref[...] = carry_scr[...].astype(I32)
    idx_ref[...] = jnp.concatenate(idxs, axis=0).astype(I32)
    rank_ref[...] = jnp.concatenate(ranks, axis=0).astype(I32)
    wsum = wts[0]
    for w in wts[1:]:
        wsum = wsum + w
    wn = jnp.concatenate([w / wsum * ROUTED_SCALE for w in wts]
                         + [jnp.zeros((LANES - TOP_K, tm), F32)], axis=0)
    for j in range(tm // LANES):
        wtok_ref[j * LANES:(j + 1) * LANES, :] = wn[:, j * LANES:(j + 1) * LANES].T


def _route(logits_t, bias, tile):
    ne, t = logits_t.shape
    upper = jnp.triu(jnp.ones((tile, tile), F32), 1).astype(BF16)
    return pl.pallas_call(
        _route_kernel,
        out_shape=[jax.ShapeDtypeStruct((TOP_K, t), I32), jax.ShapeDtypeStruct((TOP_K, t), I32),
                   jax.ShapeDtypeStruct((t, LANES), F32), jax.ShapeDtypeStruct((ne, LANES), I32)],
        grid=(t // tile,),
        in_specs=[pl.BlockSpec((ne, tile), lambda i: (0, i)), _const_spec((ne, 1)),
                  _const_spec((tile, tile))],
        out_specs=[pl.BlockSpec((TOP_K, tile), lambda i: (0, i)),
                   pl.BlockSpec((TOP_K, tile), lambda i: (0, i)),
                   pl.BlockSpec((tile, LANES), lambda i: (i, 0)),
                   _const_spec((ne, LANES))],
        scratch_shapes=[pltpu.VMEM((ne, LANES), F32)],
        compiler_params=_params(("arbitrary",)),
        name="route",
    )(logits_t, bias.reshape(ne, 1), upper)


META_POS, META_W, META_TOK = 0, TOP_K, 2 * TOP_K


def _pos_kernel(idx_ref, rank_ref, start_ref, wtok_ref, pos_ref, meta_ref):
    ne = start_ref.shape[0]
    tm = idx_ref.shape[1]
    io_e = lax.broadcasted_iota(I32, (ne, tm), 0)
    start = start_ref[...].astype(F32)
    rows = []
    for k in range(TOP_K):
        hit = io_e == idx_ref[k:k + 1, :]
        rows.append(jnp.sum(jnp.where(hit, start, 0.0), axis=0, keepdims=True))
    pos_f = jnp.concatenate(rows, axis=0) + rank_ref[...].astype(F32)
    pos_ref[...] = pos_f.astype(I32)
    pos_pad = jnp.concatenate([pos_f, jnp.zeros((LANES - TOP_K, tm), F32)], axis=0)
    lane = lax.broadcasted_iota(I32, (LANES, LANES), 1)
    row = lax.broadcasted_iota(I32, (LANES, LANES), 0)
    for j in range(tm // LANES):
        pos_tok = pos_pad[:, j * LANES:(j + 1) * LANES].T
        w_tok = pltpu.roll(wtok_ref[j * LANES:(j + 1) * LANES, :], META_W, axis=1)
        tok = (row + (pl.program_id(0) * tm + j * LANES)).astype(F32)
        tag = jnp.where(lane < META_W, pos_tok, jnp.where(lane == META_TOK, tok, w_tok))
        meta_ref[j * LANES:(j + 1) * LANES, :] = tag


def _positions(idx_t, rank_t, pad_start, wtok, tile):
    k, t = idx_t.shape
    ne = pad_start.shape[0]
    return pl.pallas_call(
        _pos_kernel,
        out_shape=[jax.ShapeDtypeStruct((k, t), I32), jax.ShapeDtypeStruct((t, LANES), F32)],
        grid=(t // tile,),
        in_specs=[pl.BlockSpec((k, tile), lambda i: (0, i)), pl.BlockSpec((k, tile), lambda i: (0, i)),
                  _const_spec((ne, 1)), pl.BlockSpec((tile, LANES), lambda i: (i, 0))],
        out_specs=[pl.BlockSpec((k, tile), lambda i: (0, i)), pl.BlockSpec((tile, LANES), lambda i: (i, 0))],
        compiler_params=_params(("arbitrary",)),
        name="positions",
    )(idx_t, rank_t, pad_start.reshape(ne, 1), wtok)


ZERO_BITS = (64, 32, 16, 8, 4, 2, 1)
assert sum(ZERO_BITS) * SUBLANES >= ROW_TILE


def _zero_copy(zbuf, xs_ref, sem, off, bit):
    rows = bit * SUBLANES
    return pltpu.make_async_copy(zbuf.at[pl.ds(0, rows)], xs_ref.at[pl.ds(pl.multiple_of(off, SUBLANES), rows)], sem)


def _zero_kernel(first_ref, units_ref, xs_ref, zbuf, sem):
    zbuf[...] = jnp.zeros(zbuf.shape, zbuf.dtype)
    ne = first_ref.shape[0]

    def each(e, start):
        off = first_ref[e]
        units = units_ref[e]
        for bit in ZERO_BITS:
            on = (units & bit) != 0

            @pl.when(on)
            def _():
                cp = _zero_copy(zbuf, xs_ref, sem, off, bit)
                cp.start() if start else cp.wait()
            off = off + jnp.where(on, bit * SUBLANES, 0)
        return start

    lax.fori_loop(0, ne, lambda e, c: (each(e, True), c)[1], 0)
    lax.fori_loop(0, ne, lambda e, c: (each(e, False), c)[1], 0)


def _zero_rows(first, units, n_rows, width):
    return pl.pallas_call(
        _zero_kernel,
        out_shape=jax.ShapeDtypeStruct((n_rows, width), F32),
        in_specs=[pl.BlockSpec(memory_space=pltpu.SMEM), pl.BlockSpec(memory_space=pltpu.SMEM)],
        out_specs=pl.BlockSpec(memory_space=pl.ANY),
        scratch_shapes=[pltpu.VMEM((ZERO_BITS[0] * SUBLANES, width), F32), pltpu.SemaphoreType.DMA],
        compiler_params=_params(),
        name="zero_rows",
    )(first, units)


def _row_copy(src_ref, dst_ref, sem, s, d):
    return pltpu.make_async_copy(src_ref.at[pl.ds(s, 1)], dst_ref.at[pl.ds(d, 1)], sem)


def _load_positions(pos_hbm, pos_smem, psem, tile_idx):
    per = pos_smem.shape[0]
    cp = pltpu.make_async_copy(pos_hbm.at[pl.ds(pl.multiple_of(tile_idx * per, per), per)], pos_smem, psem)
    cp.start()
    cp.wait()


def _dispatch_kernel(pos_hbm, u_ref, meta_ref, xs_in, xs_ref, pos_smem, src, psem, sem):
    del xs_in
    tm, d = u_ref.shape
    _load_positions(pos_hbm, pos_smem, psem, pl.program_id(0))
    src[:, :d] = u_ref[...]
    src[:, d:] = meta_ref[...]

    def issue(t, carry):
        for k in range(TOP_K):
            _row_copy(src, xs_ref, sem, t, pos_smem[k * tm + t]).start(priority=k % DMA_THREADS)
        return carry

    lax.fori_loop(0, tm, issue, 0)
    for k in range(TOP_K):
        pltpu.make_async_copy(src, xs_ref.at[pl.ds(0, tm)], sem).wait()


def _dispatch(pos_flat, u2, meta, xs, tm):
    per = TOP_K * tm
    nt = pos_flat.shape[0] // per
    d = u2.shape[1]
    return pl.pallas_call(
        _dispatch_kernel,
        out_shape=jax.ShapeDtypeStruct(xs.shape, xs.dtype),
        grid=(nt,),
        in_specs=[pl.BlockSpec(memory_space=pl.ANY), pl.BlockSpec((tm, d), lambda i: (i, 0)),
                  pl.BlockSpec((tm, LANES), lambda i: (i, 0)), pl.BlockSpec(memory_space=pl.ANY)],
        out_specs=pl.BlockSpec(memory_space=pl.ANY),
        scratch_shapes=[pltpu.SMEM((per,), I32), pltpu.VMEM((tm, d + LANES), F32),
                        pltpu.SemaphoreType.DMA, pltpu.SemaphoreType.DMA],
        input_output_aliases={3: 0},
        compiler_params=_params(("arbitrary",)),
        name="dispatch",
    )(pos_flat, u2, meta, xs)


WAIT_BITS = tuple(1 << b for b in reversed(range(ROW_TILE.bit_length())))


def _expert_kernel(te_ref, nu_ref, nv_ref, xs_ref, wg_ref, wu_ref, wd_ref, g_hbm,
                   ybuf, dst_vmem, dst_smem, sem, dsem, *, n_tok):
    del te_ref
    i = pl.program_id(0)
    nu = nu_ref[0]
    d = ybuf.shape[3]
    rows = ybuf.shape[1] * SUBLANES

    def drain(tile):
        slot = tile % 2
        n = nv_ref[tile]
        for bit in WAIT_BITS:
            @pl.when((n & bit) != 0)
            def _():
                pltpu.make_async_copy(g_hbm.at[pl.ds(0, bit)], g_hbm.at[pl.ds(0, bit)], sem.at[slot]).wait()

    @pl.when(i < nu)
    def _():
        slot = i % 2

        @pl.when(i >= 2)
        def _():
            drain(i - 2)

        dst_vmem[...] = jnp.zeros(dst_vmem.shape, I32)
        xfull = xs_ref[...]
        tag = xfull[:, d:]
        x = xfull[:, :d].astype(BF16)
        lane = lax.broadcasted_iota(I32, (rows, LANES), 1)
        p = (lax.broadcasted_iota(I32, (rows, LANES), 0) + i * rows).astype(F32)
        hit = jnp.logical_and(tag == p, lane < META_W)
        w_at = pltpu.roll(tag, LANES - META_W, axis=1)
        w = jnp.sum(jnp.where(hit, w_at, 0.0), axis=1, keepdims=True)
        kf = jnp.sum(jnp.where(hit, lane.astype(F32), 0.0), axis=1, keepdims=True)
        dest = kf * float(n_tok) + tag[:, META_TOK:META_TOK + 1]

        g = jnp.dot(x, wg_ref[0].astype(BF16), preferred_element_type=F32)
        u = jnp.dot(x, wu_ref[0].astype(BF16), preferred_element_type=F32)
        h = (_silu(g) * u).astype(BF16)
        y = jnp.dot(h, wd_ref[0].astype(BF16), preferred_element_type=F32) * w
        ybuf[slot] = y.reshape(rows // SUBLANES, SUBLANES, d)

        dest_b = jnp.broadcast_to(dest, (rows, LANES))
        n_blk = -(-rows // LANES)
        dest_b = jnp.concatenate([dest_b, jnp.zeros((n_blk * LANES - rows, LANES), F32)], axis=0)
        for b in range(n_blk):
            dst_vmem[b:b + 1, :] = dest_b[b * LANES:(b + 1) * LANES, :].T[0:1, :].astype(I32)
        to_smem = [pltpu.make_async_copy(dst_vmem.at[b], dst_smem.at[pl.ds(b * LANES, LANES)], dsem)
                   for b in range(n_blk)]
        for cp in to_smem:
            cp.start()
        for cp in to_smem:
            cp.wait()

        def send(r8, j, prio):
            to = dst_smem[r8 * SUBLANES + j]
            pltpu.make_async_copy(ybuf.at[slot, r8, pl.ds(j, 1)], g_hbm.at[pl.ds(to, 1)],
                                  sem.at[slot]).start(priority=prio)

        def issue8(r8, carry):
            for j in range(SUBLANES):
                send(r8, j, j % DMA_THREADS)
            return carry

        def issue1(r, carry):
            send(lax.shift_right_logical(r, 3), r & (SUBLANES - 1), 0)
            return carry

        nv = nv_ref[i]
        full = lax.shift_right_logical(nv, 3)
        lax.fori_loop(0, full, issue8, 0)
        lax.fori_loop(full * SUBLANES, nv, issue1, 0)

    @pl.when(i == pl.num_programs(0) - 1)
    def _():
        @pl.when(nu >= 2)
        def _():
            drain(nu - 2)

        @pl.when(nu >= 1)
        def _():
            drain(nu - 1)


def _experts(tile_expert, n_used, tile_valid, xs, w_gate, w_up, w_down, n_tok):
    n_rows, width = xs.shape
    d, de = w_gate.shape[1:]
    n_tiles = n_rows // ROW_TILE
    row_map = lambda i, te, nu, nv: (jnp.minimum(i, nu[0] - 1), 0)
    w_map = lambda i, te, nu, nv: (te[i], 0, 0)
    n_blk = -(-ROW_TILE // LANES)
    return pl.pallas_call(
        functools.partial(_expert_kernel, n_tok=n_tok),
        out_shape=jax.ShapeDtypeStruct((TOP_K * n_tok, d), F32),
        grid_spec=pltpu.PrefetchScalarGridSpec(
            num_scalar_prefetch=3, grid=(n_tiles,),
            in_specs=[pl.BlockSpec((ROW_TILE, width), row_map),
                      pl.BlockSpec((1, d, de), w_map), pl.BlockSpec((1, d, de), w_map),
                      pl.BlockSpec((1, de, d), w_map)],
            out_specs=pl.BlockSpec(memory_space=pl.ANY),
            scratch_shapes=[pltpu.VMEM((2, ROW_TILE // SUBLANES, SUBLANES, d), F32),
                            pltpu.VMEM((SUBLANES, LANES), I32),
                            pltpu.SMEM((n_blk * LANES,), I32), pltpu.SemaphoreType.DMA((2,)),
                            pltpu.SemaphoreType.DMA]),
        compiler_params=_params(("arbitrary",)),
        name="experts",
    )(tile_expert, n_used, tile_valid, xs, w_gate, w_up, w_down)


def _combine_kernel(g_ref, u_ref, x1_ref, mod_ref, wsg_ref, wsu_ref, wsd_ref, gpost_ref, o_ref,
                    *, per_row, rows_per_batch):
    tm, d = u_ref.shape
    ub = u_ref[...].astype(BF16)
    hs = _silu(jnp.dot(ub, wsg_ref[...], preferred_element_type=F32)) * jnp.dot(
        ub, wsu_ref[...], preferred_element_type=F32)
    f = jnp.dot(hs.astype(BF16), wsd_ref[...], preferred_element_type=F32)
    routed = g_ref[0]
    for k in range(1, TOP_K):
        routed = routed + g_ref[k]
    f = routed + f
    (gate2,) = _mod_rows(mod_ref, per_row, rows_per_batch, tm, d)
    o_ref[...] = x1_ref[...] + gate2 * (_rms(f) * gpost_ref[...])


def _combine(g_rows, u2, x1, mod, consts, per_row, rows_per_batch, tile_off, n_tok, tm):
    t_all, d = u2.shape
    ds_ = consts["w_sh_gate"].shape[1]
    tok_spec = lambda w: pl.BlockSpec((tm, w), lambda i: (i + tile_off, 0))
    return pl.pallas_call(
        functools.partial(_combine_kernel, per_row=per_row, rows_per_batch=rows_per_batch),
        out_shape=jax.ShapeDtypeStruct((n_tok, d), F32),
        grid=(n_tok // tm,),
        in_specs=[pl.BlockSpec((TOP_K, tm, d), lambda i: (0, i + tile_off, 0)),
                  tok_spec(d), tok_spec(d), _mod_spec(mod, per_row, tm),
                  _const_spec((d, ds_)), _const_spec((d, ds_)), _const_spec((ds_, d)), _const_spec((1, d))],
        out_specs=pl.BlockSpec((tm, d), lambda i: (i, 0)),
        compiler_params=_params(("arbitrary",)),
        name="combine",
    )(g_rows.reshape(TOP_K, t_all, d), u2, x1, mod, consts["w_sh_gate"], consts["w_sh_up"],
      consts["w_sh_down"], consts["g_post_ffn"])


def _repeat_rows(x, n):
    r, c = x.shape
    return jnp.broadcast_to(x[:, None, :], (r, n, c)).reshape(r * n, c)


def _tile_major(pos_t, tile):
    k, t = pos_t.shape
    return pos_t.reshape(k, t // tile, tile).transpose(1, 0, 2).reshape(-1)


def kernel(x_prompt, x_sample, c_prompt, c_sample, state_ssm, state_ssd_conv, state_short_conv, w_ada, b_ada, g_pre_mix, g_post_mix, g_pre_ffn, g_post_ffn, w_in, ssd_conv_w, ssd_conv_b, dt_bias, a_log, d_skip, g_ssd_norm, sc_conv_w, g_sc_norm, w_out, w_router, router_bias, w_exp_gate, w_exp_up, w_exp_down, w_sh_gate, w_sh_up, w_sh_down):
    depth = w_ada.shape[0]
    bp, seq, d = x_prompt.shape
    bs, dseq, _ = x_sample.shape
    heads = dt_bias.shape[1]
    ssd_w = heads * SSD_HEAD_DIM
    conv_dim = ssd_conv_w.shape[2]
    sc_w = sc_conv_w.shape[2]
    ne = w_router.shape[2]
    tp, ts = bp * seq, bs * dseq
    t_all = tp + ts

    assert sc_w // SC_GROUPS == SSD_HEAD_DIM and heads == SC_GROUPS
    head_of = jnp.arange(ssd_w, dtype=I32) // SSD_HEAD_DIM
    e_ind = (jnp.arange(LANES, dtype=I32)[:, None] == head_of[None, :])

    xp = x_prompt.reshape(tp, d)
    xs_pad = jnp.pad(x_sample, ((0, 0), (0, SUBLANES - dseq), (0, 0))).reshape(bs * SUBLANES, d)
    xs_tok = x_sample.reshape(ts, d)
    outs = {k: [] for k in ("ssm_p", "cst_p", "scst_p", "ssm_s", "cst_s", "scst_s")}

    for layer in range(depth):
        cuts = np.cumsum([0, ssd_w, conv_dim, heads, sc_w, sc_w, sc_w]).tolist()
        wi = w_in[layer]
        seg = lambda k: wi[:, cuts[k]:cuts[k + 1]]
        w_in_r = jnp.concatenate([seg(0), seg(1), seg(3), seg(4), seg(5),
                                  jnp.pad(seg(2), ((0, 0), (0, LANES - heads)))], axis=1).astype(BF16)
        widths = (ssd_w, conv_dim, sc_w, sc_w, sc_w, LANES)
        pad_h = lambda v: jnp.pad(v.reshape(1, heads), ((0, 0), (0, LANES - heads)))
        wr = w_router[layer].T
        wr_hi = wr.astype(BF16)
        consts = dict(
            heads=heads,
            conv_w=ssd_conv_w[layer], conv_b=ssd_conv_b[layer].reshape(1, conv_dim),
            dt_bias=pad_h(dt_bias[layer]), a_log=pad_h(a_log[layer]),
            d_skip_e=jnp.broadcast_to(d_skip[layer][:, None], (heads, SSD_HEAD_DIM)).reshape(1, ssd_w),
            g_ssd=g_ssd_norm[layer].reshape(1, ssd_w), sc_w=sc_conv_w[layer],
            g_sc=g_sc_norm[layer].reshape(1, sc_w),
            e=e_ind.astype(BF16), et=e_ind.T.astype(F32),
            w_out=w_out[layer].astype(BF16), g_post_mix=g_post_mix[layer].reshape(1, d),
            g_pre_ffn=g_pre_ffn[layer].reshape(1, d),
            wr_hi_t=wr_hi, wr_lo_t=(wr - wr_hi.astype(F32)).astype(BF16),
            w_sh_gate=w_sh_gate[layer].astype(BF16), w_sh_up=w_sh_up[layer].astype(BF16),
            w_sh_down=w_sh_down[layer].astype(BF16), g_post_ffn=g_post_ffn[layer].reshape(1, d))

        c_all = jnp.concatenate([c_prompt, c_sample], axis=0)
        m_rows = -(-c_all.shape[0] // 16) * 16
        mod = _ada(jnp.pad(c_all, ((0, m_rows - c_all.shape[0]), (0, 0))), w_ada[layer], b_ada[layer])
        mod_p = mod[:bp]
        mod_s = mod[bp:bp + bs]
        mod_s_pad = _repeat_rows(mod_s[:, :2 * d], SUBLANES)
        mod_s_tok = _repeat_rows(mod_s, dseq)

        proj_p = _in_proj(xp, mod_p[:, :2 * d], g_pre_mix[layer], w_in_r, widths, BF16, False, seq, TOK_TILE)
        proj_s = _in_proj(xs_pad, mod_s_pad, g_pre_mix[layer], w_in_r, widths, F32, True, 1, TOK_TILE)
        ymix_p, ssm_p, cst_p, scst_p = _ssd(proj_p, None, consts, bp, seq // CHUNK, CHUNK, CHUNK, CHUNK, BF16)
        init = (state_ssm[layer].reshape(bs, ssd_w, D_STATE), state_ssd_conv[layer], state_short_conv[layer])
        ymix_s, ssm_s, cst_s, scst_s = _ssd(proj_s, init, consts, bs, 1, SAMPLE_CHUNK, SUBLANES, dseq, F32)
        ymix_s = ymix_s.reshape(bs, SUBLANES, ssd_w + sc_w)[:, :dseq].reshape(ts, ssd_w + sc_w)

        merged = _out_proj(ymix_p, xp, mod_p[:, 2 * d:5 * d], consts, False, seq, TOK_TILE, t_all, 0, None)
        x1, u2, logits_t = _out_proj(ymix_s, xs_tok, mod_s_tok[:, 2 * d:5 * d], consts, True, 1, TOK_TILE,
                                     t_all, tp // TOK_TILE, merged)

        idx_t, rank_t, wtok, counts = _route(logits_t, router_bias[layer], TOK_TILE)
        counts = counts[:, 0]
        padded = (counts + ROW_TILE - 1) // ROW_TILE * ROW_TILE
        pad_end = jnp.cumsum(padded)
        pad_start = pad_end - padded
        n_tiles = -(-(t_all * TOP_K) // ROW_TILE) + ne
        n_used = (pad_end[-1] // ROW_TILE).astype(I32)
        tile_ids = jnp.minimum(jnp.arange(n_tiles, dtype=I32), n_used - 1)
        tile_expert = jnp.minimum(jnp.sum(pad_end[None, :] <= (tile_ids * ROW_TILE)[:, None], axis=1), ne - 1).astype(I32)
        first_pad = (pad_start + counts) // SUBLANES * SUBLANES
        units = (pad_end - first_pad) // SUBLANES
        tile_valid = jnp.clip(counts[tile_expert] - (tile_ids * ROW_TILE - pad_start[tile_expert]), 0,
                              ROW_TILE).astype(I32)
        pos_t, meta = _positions(idx_t, rank_t, pad_start.astype(I32), wtok, TOK_TILE)
        pos_flat = _tile_major(pos_t, GATHER_TILE)

        xs_rows = _zero_rows(first_pad.astype(I32), units.astype(I32), n_tiles * ROW_TILE, d + LANES)
        xs_rows = _dispatch(pos_flat, u2, meta, xs_rows, GATHER_TILE)
        g_rows = _experts(tile_expert, n_used.reshape(1), tile_valid, xs_rows, w_exp_gate[layer],
                          w_exp_up[layer], w_exp_down[layer], t_all)
        xp = _combine(g_rows, u2, x1, mod_p[:, 5 * d:], consts, False, seq, 0, tp, GATHER_TILE)
        xs_tok = _combine(g_rows, u2, x1, mod_s_tok[:, 5 * d:], consts, True, 1, tp // GATHER_TILE, ts,
                          GATHER_TILE)
        xs_pad = jnp.pad(xs_tok.reshape(bs, dseq, d), ((0, 0), (0, SUBLANES - dseq), (0, 0))).reshape(
            bs * SUBLANES, d)

        outs["ssm_p"].append(ssm_p.reshape(bp, heads, SSD_HEAD_DIM, D_STATE))
        outs["cst_p"].append(cst_p)
        outs["scst_p"].append(scst_p)
        outs["ssm_s"].append(ssm_s.reshape(bs, heads, SSD_HEAD_DIM, D_STATE))
        outs["cst_s"].append(cst_s)
        outs["scst_s"].append(scst_s)

    return (xp.reshape(bp, seq, d), xs_tok.reshape(bs, dseq, d),
            jnp.stack(outs["ssm_p"]), jnp.stack(outs["cst_p"]), jnp.stack(outs["scst_p"]),
            jnp.stack(outs["ssm_s"]), jnp.stack(outs["cst_s"]), jnp.stack(outs["scst_s"]))
```

```python
import functools

import jax
import jax.numpy as jnp
import numpy as np
from jax import lax
from jax.experimental import pallas as pl
from jax.experimental.pallas import tpu as pltpu

F32 = jnp.float32
BF16 = jnp.bfloat16
I32 = jnp.int32
U32 = jnp.uint32
HI = lax.Precision.HIGHEST

SSD_HEAD_DIM = 64
SSD_GROUPS = 2
D_STATE = 128
SSD_CONV_W = 4
SC_GROUPS = 16
SC_CONV_W = 3
TOP_K = 8
N_EXPERT_GROUPS = 8
TOPK_GROUPS = 4
ROUTED_SCALE = 2.5
EPS = 1e-6

LANES = 128
SUBLANES = 8
CHUNK = 128
SAMPLE_CHUNK = 16
TOK_TILE = 512
ROW_TILE = 576
GATHER_TILE = 256
VMEM_LIMIT = 56 * 1024 * 1024
DMA_THREADS = 2

NT = (((1,), (1,)), ((), ()))
TN = (((0,), (0,)), ((), ()))


def _sigmoid(x):
    return 1.0 / (1.0 + jnp.exp(-x))


def _silu(x):
    return x * _sigmoid(x)


def _softplus(x):
    return jnp.maximum(x, 0.0) + jnp.log1p(jnp.exp(-jnp.abs(x)))


def _rms(x, eps=EPS):
    return x * lax.rsqrt(jnp.mean(x * x, axis=-1, keepdims=True) + eps)


def _params(sem=None):
    return pltpu.CompilerParams(dimension_semantics=sem, vmem_limit_bytes=VMEM_LIMIT)


def _const_spec(shape, single=False):
    nd = len(shape)
    mode = dict(pipeline_mode=pl.Buffered(1)) if single else {}
    return pl.BlockSpec(shape, lambda *_: (0,) * nd, **mode)


def _ada_kernel(c_ref, w_ref, b_ref, o_ref):
    c = c_ref[...]
    s = _silu(c).astype(BF16)
    o_ref[...] = jnp.dot(s, w_ref[...].astype(BF16), preferred_element_type=F32) + b_ref[...]


def _ada(c, w_ada, b_ada):
    m, d = c.shape
    n = w_ada.shape[1]
    tn = 512
    return pl.pallas_call(
        _ada_kernel,
        out_shape=jax.ShapeDtypeStruct((m, n), F32),
        grid=(n // tn,),
        in_specs=[_const_spec((m, d)),
                  pl.BlockSpec((d, tn), lambda j: (0, j)),
                  pl.BlockSpec((1, tn), lambda j: (0, j))],
        out_specs=pl.BlockSpec((m, tn), lambda j: (0, j)),
        compiler_params=_params(("arbitrary",)),
        name="ada",
    )(c, w_ada, b_ada.reshape(1, n))


def _mod_rows(mod_ref, per_row, rows_per_batch, tile, d):
    n = mod_ref.shape[1] // d
    if per_row:
        return [mod_ref[:, k * d:(k + 1) * d] for k in range(n)]
    b = (pl.program_id(0) * tile) // rows_per_batch
    return [mod_ref[pl.ds(b, 1), k * d:(k + 1) * d] for k in range(n)]


def _mod_spec(mod, per_row, tile):
    if per_row:
        return pl.BlockSpec((tile, mod.shape[1]), lambda i: (i, 0))
    return _const_spec(mod.shape)


def _in_kernel(x_ref, mod_ref, g_ref, w_ref, *out_refs, per_row, rows_per_batch, widths):
    tile, d = x_ref.shape
    shift, scale = _mod_rows(mod_ref, per_row, rows_per_batch, tile, d)
    u = (_rms(x_ref[...]) * g_ref[...]) * (1.0 + scale) + shift
    u = u.astype(BF16)
    col = 0
    for ref, width in zip(out_refs, widths):
        for a in range(0, width, 512):
            bw = min(512, width - a)
            r = jnp.dot(u, w_ref[:, col + a:col + a + bw], preferred_element_type=F32)
            ref[:, a:a + bw] = r.astype(ref.dtype)
        col += width


def _in_proj(x, mod, g, w_bf16, widths, out_dtype, per_row, rows_per_batch, tile):
    t, d = x.shape
    n = w_bf16.shape[1]
    mod_spec = _mod_spec(mod, per_row, tile)
    dts = [out_dtype] * (len(widths) - 1) + [F32]
    return pl.pallas_call(
        functools.partial(_in_kernel, per_row=per_row, rows_per_batch=rows_per_batch, widths=widths),
        out_shape=[jax.ShapeDtypeStruct((t, wd), dt) for wd, dt in zip(widths, dts)],
        grid=(t // tile,),
        in_specs=[pl.BlockSpec((tile, d), lambda i: (i, 0)), mod_spec,
                  _const_spec((1, d)), _const_spec((d, n), single=True)],
        out_specs=[pl.BlockSpec((tile, wd), lambda i: (i, 0)) for wd in widths],
        compiler_params=_params(("arbitrary",)),
        name="in_proj",
    )(x, mod, g.reshape(1, d), w_bf16)


def _ssd_kernel(*refs, q, rows_in, q_valid, has_init, heads):
    it = iter(refs)
    z_ref, xbc_ref, scb_ref, scc_ref, sch_ref, dt_ref = (next(it) for _ in range(6))
    if has_init:
        ssm0_ref, cst0_ref, scst0_ref = (next(it) for _ in range(3))
    (cw_ref, cb_ref, dtb_ref, alog_ref, dsk_ref, gssd_ref, scw_ref, gsc_ref,
     tril_ref, e_ref, et_ref) = (next(it) for _ in range(11))
    ymix_ref, ssm_ref, cst_ref, scst_ref = (next(it) for _ in range(4))
    h_scr, ext_scr, extv_scr = (next(it) for _ in range(3))

    c = pl.program_id(1)
    nc = pl.num_programs(1)
    ssd_w = dsk_ref.shape[1]
    gw = ssd_w // SSD_GROUPS
    hpg = heads // SSD_GROUPS
    n_state = D_STATE
    head0 = SUBLANES - (SSD_CONV_W - 1)
    headv = SUBLANES - (SC_CONV_W - 1)

    @pl.when(c == 0)
    def _():
        if has_init:
            h_scr[...] = ssm0_ref[0]
            ext_scr[head0:SUBLANES, :] = cst0_ref[0]
            extv_scr[headv:SUBLANES, :] = scst0_ref[0]
        else:
            h_scr[...] = jnp.zeros(h_scr.shape, F32)
            ext_scr[0:SUBLANES, :] = jnp.zeros((SUBLANES, ext_scr.shape[1]), F32)
            extv_scr[0:SUBLANES, :] = jnp.zeros((SUBLANES, extv_scr.shape[1]), F32)

    def pad_rows(v, n):
        if v.shape[0] == n:
            return v
        return jnp.concatenate([v, jnp.zeros((n - v.shape[0], v.shape[1]), v.dtype)], axis=0)

    load = lambda ref: pad_rows(ref[...].astype(F32), q)
    pad_t = lambda v: pad_rows(v, LANES)
    z, scb, scc, sch, dt_raw = load(z_ref), load(scb_ref), load(scc_ref), load(sch_ref), load(dt_ref)

    ext_scr[SUBLANES:SUBLANES + q, :] = load(xbc_ref)
    conv = cb_ref[...]
    for k in range(SSD_CONV_W):
        conv = conv + cw_ref[k:k + 1, :] * ext_scr[head0 + k:head0 + k + q, :]
    xc = _silu(conv)
    xs = xc[:, :ssd_w]
    bm = xc[:, ssd_w:ssd_w + SSD_GROUPS * n_state]
    cm = xc[:, ssd_w + SSD_GROUPS * n_state:]

    lane = lax.broadcasted_iota(I32, (q, LANES), 1)
    row = lax.broadcasted_iota(I32, (q, LANES), 0)
    dt = _softplus(dt_raw + dtb_ref[...])
    dt = jnp.where(jnp.logical_and(lane < heads, row < q_valid), dt, 0.0)
    a = dt * (-jnp.exp(alog_ref[...]))
    acum = jnp.dot(tril_ref[...], pad_t(a), precision=HI, preferred_element_type=F32)
    acum_t = pad_t(acum).T
    a_last = acum[q - 1:q, :]
    e = e_ref[...]
    expand = lambda v: jnp.dot(v.astype(BF16), e, preferred_element_type=F32)
    dt_e = expand(dt)
    dtdte_e = expand(dt * jnp.exp(a_last - acum))
    exa_e = expand(jnp.exp(acum))
    xdt = xs * dt_e
    xw_b = (xs * dtdte_e).astype(BF16)
    dlast = jnp.broadcast_to(jnp.exp(acum_t[:, q - 1:q]), (LANES, n_state))
    dcol = jnp.dot(et_ref[...], dlast, precision=HI, preferred_element_type=F32)

    tri = row >= lane
    xdt_t = pad_t(xdt)
    xw_t = pad_t(xw_b)
    lane_t = lax.broadcasted_iota(I32, (LANES, LANES), 1)
    y_groups = []
    for g in range(SSD_GROUPS):
        bm_g = bm[:, g * n_state:(g + 1) * n_state].astype(BF16)
        cm_g = cm[:, g * n_state:(g + 1) * n_state].astype(BF16)
        bm_t = pad_t(bm_g)
        cb = lax.dot_general(cm_g, bm_t, NT, preferred_element_type=F32)
        h_g = h_scr[g * gw:(g + 1) * gw, :]
        y_off = lax.dot_general(cm_g, h_g.astype(BF16), NT, preferred_element_type=F32)
        parts = []
        for pair in range(hpg // 2):
            lo = (g * hpg + 2 * pair) * SSD_HEAD_DIM
            x_pair = xdt_t[:, lo:lo + LANES]
            acc = None
            for half in range(2):
                h = g * hpg + 2 * pair + half
                ci = jnp.broadcast_to(acum[:, h:h + 1], (q, LANES))
                rj = jnp.broadcast_to(acum_t[h:h + 1, :], (q, LANES))
                dec = jnp.where(tri, jnp.exp(ci - rj), 0.0)
                m = (cb * dec).astype(BF16)
                own = (lane_t >= SSD_HEAD_DIM) if half else (lane_t < SSD_HEAD_DIM)
                y_h = jnp.dot(m, jnp.where(own, x_pair, 0.0).astype(BF16), preferred_element_type=F32)
                acc = y_h if acc is None else acc + y_h
            parts.append(acc)
        y_diag = jnp.concatenate(parts, axis=1)
        y_groups.append(y_diag + y_off * exa_e[:, g * gw:(g + 1) * gw])
        upd = lax.dot_general(xw_t[:, g * gw:(g + 1) * gw], bm_t, TN, preferred_element_type=F32)
        h_scr[g * gw:(g + 1) * gw, :] = h_g * dcol[g * gw:(g + 1) * gw, :] + upd

    y = (jnp.concatenate(y_groups, axis=1) + xs * dsk_ref[...]) * _silu(z)
    y = jnp.concatenate([_rms(y[:, g * gw:(g + 1) * gw]) for g in range(SSD_GROUPS)], axis=1)
    y_ssd = y * gssd_ref[...]

    extv_scr[SUBLANES:SUBLANES + q, :] = scc * sch
    cv = scw_ref[0:1, :] * extv_scr[headv:headv + q, :]
    for k in range(1, SC_CONV_W):
        cv = cv + scw_ref[k:k + 1, :] * extv_scr[headv + k:headv + k + q, :]
    t = scb * cv
    sc_per_group = t.shape[1] // SC_GROUPS
    gsum = jnp.dot((t * t).astype(BF16), et_ref[...].astype(BF16), preferred_element_type=F32)
    rs = lax.rsqrt(gsum * (1.0 / sc_per_group) + EPS)
    y_sc = t * expand(rs) * gsc_ref[...]

    ymix_ref[:, :ssd_w] = y_ssd[0:rows_in].astype(ymix_ref.dtype)
    ymix_ref[:, ssd_w:] = y_sc[0:rows_in].astype(ymix_ref.dtype)

    @pl.when(c == nc - 1)
    def _():
        ssm_ref[0] = h_scr[...]
        cst_ref[0] = ext_scr[SUBLANES + q_valid - (SSD_CONV_W - 1):SUBLANES + q_valid, :]
        scst_ref[0] = extv_scr[SUBLANES + q_valid - (SC_CONV_W - 1):SUBLANES + q_valid, :]

    ext_scr[0:SUBLANES, :] = ext_scr[q:q + SUBLANES, :]
    extv_scr[0:SUBLANES, :] = extv_scr[q:q + SUBLANES, :]


def _ssd(proj, init, consts, nb, nc, q, rows_in, q_valid, ymix_dtype):
    z, xbc, scb, scc, sch, dt = proj
    heads = consts["heads"]
    tril = jnp.tril(jnp.ones((q, LANES), F32))
    ssd_w, conv_dim, sc_w = z.shape[1], xbc.shape[1], scb.shape[1]
    has_init = init is not None
    row_spec = lambda w: pl.BlockSpec((rows_in, w), lambda b, c: (b * nc + c, 0))
    in_specs = [row_spec(ssd_w), row_spec(conv_dim), row_spec(sc_w), row_spec(sc_w), row_spec(sc_w),
                row_spec(LANES)]
    args = [z, xbc, scb, scc, sch, dt]
    if has_init:
        ssm0, cst0, scst0 = init
        in_specs += [pl.BlockSpec((1,) + ssm0.shape[1:], lambda b, c: (b, 0, 0)),
                     pl.BlockSpec((1,) + cst0.shape[1:], lambda b, c: (b, 0, 0)),
                     pl.BlockSpec((1,) + scst0.shape[1:], lambda b, c: (b, 0, 0))]
        args += [ssm0, cst0, scst0]
    weights = [consts[k] for k in ("conv_w", "conv_b", "dt_bias", "a_log", "d_skip_e", "g_ssd",
                                   "sc_w", "g_sc")] + [tril, consts["e"], consts["et"]]
    in_specs += [_const_spec(w.shape) for w in weights]
    args += weights
    n_state = D_STATE
    out_shape = [jax.ShapeDtypeStruct((nb * nc * rows_in, ssd_w + sc_w), ymix_dtype),
                 jax.ShapeDtypeStruct((nb, ssd_w, n_state), F32),
                 jax.ShapeDtypeStruct((nb, SSD_CONV_W - 1, conv_dim), F32),
                 jax.ShapeDtypeStruct((nb, SC_CONV_W - 1, sc_w), F32)]
    out_specs = [pl.BlockSpec((rows_in, ssd_w + sc_w), lambda b, c: (b * nc + c, 0)),
                 pl.BlockSpec((1, ssd_w, n_state), lambda b, c: (b, 0, 0)),
                 pl.BlockSpec((1, SSD_CONV_W - 1, conv_dim), lambda b, c: (b, 0, 0)),
                 pl.BlockSpec((1, SC_CONV_W - 1, sc_w), lambda b, c: (b, 0, 0))]
    scratch = [pltpu.VMEM((ssd_w, n_state), F32),
               pltpu.VMEM((q + SUBLANES, conv_dim), F32),
               pltpu.VMEM((q + SUBLANES, sc_w), F32)]
    return pl.pallas_call(
        functools.partial(_ssd_kernel, q=q, rows_in=rows_in, q_valid=q_valid, has_init=has_init, heads=heads),
        out_shape=out_shape, grid=(nb, nc), in_specs=in_specs, out_specs=out_specs,
        scratch_shapes=scratch,
        compiler_params=_params(("arbitrary", "arbitrary")),
        name="ssd",
    )(*args)


def _out_kernel(ymix_ref, x_ref, mod_ref, wout_ref, gpost_ref, gpre_ref, wr_hi_ref, wr_lo_ref,
                *rest, per_row, rows_per_batch, has_alias):
    x1_ref, u2_ref, lg_ref = rest[3:6] if has_alias else rest[0:3]
    tile, d = x_ref.shape
    gate1, shift2, scale2 = _mod_rows(mod_ref, per_row, rows_per_batch, tile, d)
    m = jnp.dot(ymix_ref[...].astype(BF16), wout_ref[...], preferred_element_type=F32)
    x1 = x_ref[...] + gate1 * (_rms(m) * gpost_ref[...])
    u2 = (_rms(x1) * gpre_ref[...]) * (1.0 + scale2) + shift2
    x1_ref[...] = x1
    u2_ref[...] = u2
    u_hi = u2.astype(BF16)
    u_lo = (u2 - u_hi.astype(F32)).astype(BF16)
    lg = lax.dot_general(wr_hi_ref[...], u_hi, NT, preferred_element_type=F32)
    lg = lg + lax.dot_general(wr_hi_ref[...], u_lo, NT, preferred_element_type=F32)
    lg = lg + lax.dot_general(wr_lo_ref[...], u_hi, NT, preferred_element_type=F32)
    lg_ref[...] = lg


def _out_proj(ymix, x, mod, consts, per_row, rows_per_batch, tile, t_total, tile_off, prev):
    t, d = x.shape
    dm = ymix.shape[1]
    ne = consts["wr_hi_t"].shape[0]
    in_specs = [pl.BlockSpec((tile, dm), lambda i: (i, 0)),
                pl.BlockSpec((tile, d), lambda i: (i, 0)), _mod_spec(mod, per_row, tile),
                _const_spec((dm, d)), _const_spec((1, d)), _const_spec((1, d)),
                _const_spec((ne, d)), _const_spec((ne, d))]
    args = [ymix, x, mod, consts["w_out"], consts["g_post_mix"], consts["g_pre_ffn"],
            consts["wr_hi_t"], consts["wr_lo_t"]]
    aliases = {}
    if prev is not None:
        in_specs += [pl.BlockSpec(memory_space=pl.ANY)] * 3
        aliases = {len(args) + k: k for k in range(3)}
        args += list(prev)
    return pl.pallas_call(
        functools.partial(_out_kernel, per_row=per_row, rows_per_batch=rows_per_batch,
                          has_alias=prev is not None),
        out_shape=[jax.ShapeDtypeStruct((t_total, d), F32), jax.ShapeDtypeStruct((t_total, d), F32),
                   jax.ShapeDtypeStruct((ne, t_total), F32)],
        grid=(t // tile,), in_specs=in_specs,
        out_specs=[pl.BlockSpec((tile, d), lambda i: (i + tile_off, 0)),
                   pl.BlockSpec((tile, d), lambda i: (i + tile_off, 0)),
                   pl.BlockSpec((ne, tile), lambda i: (0, i + tile_off))],
        input_output_aliases=aliases,
        compiler_params=_params(("arbitrary",)),
        name="out_proj",
    )(*args)


def _route_kernel(lg_ref, bias_ref, upper_ref, idx_ref, rank_ref, wtok_ref, cnt_ref, carry_scr):
    i = pl.program_id(0)
    ne, tm = lg_ref.shape
    per_group = ne // N_EXPERT_GROUPS
    neg = -jnp.inf

    @pl.when(i == 0)
    def _():
        carry_scr[...] = jnp.zeros(carry_scr.shape, F32)

    s = _sigmoid(lg_ref[...])
    biased = s + bias_ref[...]
    gl = []
    io_g = lax.broadcasted_iota(I32, (per_group, tm), 0).astype(F32)
    for g in range(N_EXPERT_GROUPS):
        blk = biased[g * per_group:(g + 1) * per_group, :]
        m1 = jnp.max(blk, axis=0, keepdims=True)
        f1 = jnp.min(jnp.where(blk == m1, io_g, float(per_group)), axis=0, keepdims=True)
        m2 = jnp.max(jnp.where(io_g == f1, neg, blk), axis=0, keepdims=True)
        gl.append(m1 + m2)
    gscore = jnp.concatenate(gl, axis=0)
    io8 = lax.broadcasted_iota(I32, (N_EXPERT_GROUPS, tm), 0).astype(F32)
    gsel = jnp.zeros((N_EXPERT_GROUPS, tm), F32)
    for _ in range(TOPK_GROUPS):
        m = jnp.max(gscore, axis=0, keepdims=True)
        f = jnp.min(jnp.where(gscore == m, io8, float(N_EXPERT_GROUPS)), axis=0, keepdims=True)
        hit = io8 == f
        gsel = jnp.where(hit, 1.0, gsel)
        gscore = jnp.where(hit, neg, gscore)
    emask = jnp.concatenate(
        [jnp.broadcast_to(gsel[g:g + 1, :], (per_group, tm)) for g in range(N_EXPERT_GROUPS)], axis=0)
    cand = jnp.where(emask > 0.5, biased, neg)
    io_e = lax.broadcasted_iota(I32, (ne, tm), 0).astype(F32)
    msel = jnp.zeros((ne, tm), F32)
    idxs, wts = [], []
    for _ in range(TOP_K):
        m = jnp.max(cand, axis=0, keepdims=True)
        f = jnp.min(jnp.where(cand == m, io_e, float(ne)), axis=0, keepdims=True)
        hit = io_e == f
        wts.append(jnp.sum(jnp.where(hit, s, 0.0), axis=0, keepdims=True))
        idxs.append(f)
        msel = jnp.where(hit, 1.0, msel)
        cand = jnp.where(hit, neg, cand)
    pref = jnp.dot(msel.astype(BF16), upper_ref[...], preferred_element_type=F32) + carry_scr[:, 0:1]
    ranks = [jnp.sum(jnp.where(io_e == f, pref, 0.0), axis=0, keepdims=True) for f in idxs]
    carry_scr[...] = carry_scr[...] + jnp.sum(msel, axis=1, keepdims=True)
    cnt_ref[...] = carry_scr[...].astype(I32)
    idx_ref[...] = jnp.concatenate(idxs, axis=0).astype(I32)
    rank_ref[...] = jnp.concatenate(ranks, axis=0).astype(I32)
    wsum = wts[0]
    for w in wts[1:]:
        wsum = wsum + w
    wn = jnp.concatenate([w / wsum * ROUTED_SCALE for w in wts]
                         + [jnp.zeros((LANES - TOP_K, tm), F32)], axis=0)
    for j in range(tm // LANES):
        wtok_ref[j * LANES:(j + 1) * LANES, :] = wn[:, j * LANES:(j + 1) * LANES].T


def _route(logits_t, bias, tile):
    ne, t = logits_t.shape
    upper = jnp.triu(jnp.ones((tile, tile), F32), 1).astype(BF16)
    return pl.pallas_call(
        _route_kernel,
        out_shape=[jax.ShapeDtypeStruct((TOP_K, t), I32), jax.ShapeDtypeStruct((TOP_K, t), I32),
                   jax.ShapeDtypeStruct((t, LANES), F32), jax.ShapeDtypeStruct((ne, LANES), I32)],
        grid=(t // tile,),
        in_specs=[pl.BlockSpec((ne, tile), lambda i: (0, i)), _const_spec((ne, 1)),
                  _const_spec((tile, tile))],
        out_specs=[pl.BlockSpec((TOP_K, tile), lambda i: (0, i)),
                   pl.BlockSpec((TOP_K, tile), lambda i: (0, i)),
                   pl.BlockSpec((tile, LANES), lambda i: (i, 0)),
                   _const_spec((ne, LANES))],
        scratch_shapes=[pltpu.VMEM((ne, LANES), F32)],
        compiler_params=_params(("arbitrary",)),
        name="route",
    )(logits_t, bias.reshape(ne, 1), upper)


META_POS, META_W, META_TOK = 0, TOP_K, 2 * TOP_K


def _pos_kernel(idx_ref, rank_ref, start_ref, wtok_ref, pos_ref, meta_ref):
    ne = start_ref.shape[0]
    tm = idx_ref.shape[1]
    io_e = lax.broadcasted_iota(I32, (ne, tm), 0)
    start = start_ref[...].astype(F32)
    rows = []
    for k in range(TOP_K):
        hit = io_e == idx_ref[k:k + 1, :]
        rows.append(jnp.sum(jnp.where(hit, start, 0.0), axis=0, keepdims=True))
    pos_f = jnp.concatenate(rows, axis=0) + rank_ref[...].astype(F32)
    pos_ref[...] = pos_f.astype(I32)
    pos_pad = jnp.concatenate([pos_f, jnp.zeros((LANES - TOP_K, tm), F32)], axis=0)
    lane = lax.broadcasted_iota(I32, (LANES, LANES), 1)
    row = lax.broadcasted_iota(I32, (LANES, LANES), 0)
    for j in range(tm // LANES):
        pos_tok = pos_pad[:, j * LANES:(j + 1) * LANES].T
        w_tok = pltpu.roll(wtok_ref[j * LANES:(j + 1) * LANES, :], META_W, axis=1)
        tok = (row + (pl.program_id(0) * tm + j * LANES)).astype(F32)
        tag = jnp.where(lane < META_W, pos_tok, jnp.where(lane == META_TOK, tok, w_tok))
        meta_ref[j * LANES:(j + 1) * LANES, :] = tag


def _positions(idx_t, rank_t, pad_start, wtok, tile):
    k, t = idx_t.shape
    ne = pad_start.shape[0]
    return pl.pallas_call(
        _pos_kernel,
        out_shape=[jax.ShapeDtypeStruct((k, t), I32), jax.ShapeDtypeStruct((t, LANES), F32)],
        grid=(t // tile,),
        in_specs=[pl.BlockSpec((k, tile), lambda i: (0, i)), pl.BlockSpec((k, tile), lambda i: (0, i)),
                  _const_spec((ne, 1)), pl.BlockSpec((tile, LANES), lambda i: (i, 0))],
        out_specs=[pl.BlockSpec((k, tile), lambda i: (0, i)), pl.BlockSpec((tile, LANES), lambda i: (i, 0))],
        compiler_params=_params(("arbitrary",)),
        name="positions",
    )(idx_t, rank_t, pad_start.reshape(ne, 1), wtok)


ZERO_BITS = (64, 32, 16, 8, 4, 2, 1)
assert sum(ZERO_BITS) * SUBLANES >= ROW_TILE


def _zero_copy(zbuf, xs_ref, sem, off, bit):
    rows = bit * SUBLANES
    return pltpu.make_async_copy(zbuf.at[pl.ds(0, rows)], xs_ref.at[pl.ds(pl.multiple_of(off, SUBLANES), rows)], sem)


def _zero_kernel(first_ref, units_ref, xs_ref, zbuf, sem):
    zbuf[...] = jnp.zeros(zbuf.shape, zbuf.dtype)
    ne = first_ref.shape[0]

    def each(e, start):
        off = first_ref[e]
        units = units_ref[e]
        for bit in ZERO_BITS:
            on = (units & bit) != 0

            @pl.when(on)
            def _():
                cp = _zero_copy(zbuf, xs_ref, sem, off, bit)
                cp.start() if start else cp.wait()
            off = off + jnp.where(on, bit * SUBLANES, 0)
        return start

    lax.fori_loop(0, ne, lambda e, c: (each(e, True), c)[1], 0)
    lax.fori_loop(0, ne, lambda e, c: (each(e, False), c)[1], 0)


def _zero_rows(first, units, n_rows, width):
    return pl.pallas_call(
        _zero_kernel,
        out_shape=jax.ShapeDtypeStruct((n_rows, width), F32),
        in_specs=[pl.BlockSpec(memory_space=pltpu.SMEM), pl.BlockSpec(memory_space=pltpu.SMEM)],
        out_specs=pl.BlockSpec(memory_space=pl.ANY),
        scratch_shapes=[pltpu.VMEM((ZERO_BITS[0] * SUBLANES, width), F32), pltpu.SemaphoreType.DMA],
        compiler_params=_params(),
        name="zero_rows",
    )(first, units)


def _row_copy(src_ref, dst_ref, sem, s, d):
    return pltpu.make_async_copy(src_ref.at[pl.ds(s, 1)], dst_ref.at[pl.ds(d, 1)], sem)


def _load_positions(pos_hbm, pos_smem, psem, tile_idx):
    per = pos_smem.shape[0]
    cp = pltpu.make_async_copy(pos_hbm.at[pl.ds(pl.multiple_of(tile_idx * per, per), per)], pos_smem, psem)
    cp.start()
    cp.wait()


def _dispatch_kernel(pos_hbm, u_ref, meta_ref, xs_in, xs_ref, pos_smem, src, psem, sem):
    del xs_in
    tm, d = u_ref.shape
    _load_positions(pos_hbm, pos_smem, psem, pl.program_id(0))
    src[:, :d] = u_ref[...]
    src[:, d:] = meta_ref[...]

    def issue(t, carry):
        for k in range(TOP_K):
            _row_copy(src, xs_ref, sem, t, pos_smem[k * tm + t]).start(priority=k % DMA_THREADS)
        return carry

    lax.fori_loop(0, tm, issue, 0)
    for k in range(TOP_K):
        pltpu.make_async_copy(src, xs_ref.at[pl.ds(0, tm)], sem).wait()


def _dispatch(pos_flat, u2, meta, xs, tm):
    per = TOP_K * tm
    nt = pos_flat.shape[0] // per
    d = u2.shape[1]
    return pl.pallas_call(
        _dispatch_kernel,
        out_shape=jax.ShapeDtypeStruct(xs.shape, xs.dtype),
        grid=(nt,),
        in_specs=[pl.BlockSpec(memory_space=pl.ANY), pl.BlockSpec((tm, d), lambda i: (i, 0)),
                  pl.BlockSpec((tm, LANES), lambda i: (i, 0)), pl.BlockSpec(memory_space=pl.ANY)],
        out_specs=pl.BlockSpec(memory_space=pl.ANY),
        scratch_shapes=[pltpu.SMEM((per,), I32), pltpu.VMEM((tm, d + LANES), F32),
                        pltpu.SemaphoreType.DMA, pltpu.SemaphoreType.DMA],
        input_output_aliases={3: 0},
        compiler_params=_params(("arbitrary",)),
        name="dispatch",
    )(pos_flat, u2, meta, xs)


WAIT_BITS = tuple(1 << b for b in reversed(range(ROW_TILE.bit_length())))


def _expert_kernel(te_ref, nu_ref, nv_ref, xs_ref, wg_ref, wu_ref, wd_ref, g_hbm,
                   ybuf, dst_vmem, dst_smem, sem, dsem, *, n_tok):
    del te_ref
    i = pl.program_id(0)
    nu = nu_ref[0]
    d = ybuf.shape[3]
    rows = ybuf.shape[1] * SUBLANES

    def drain(tile):
        slot = tile % 2
        n = nv_ref[tile]
        for bit in WAIT_BITS:
            @pl.when((n & bit) != 0)
            def _():
                pltpu.make_async_copy(g_hbm.at[pl.ds(0, bit)], g_hbm.at[pl.ds(0, bit)], sem.at[slot]).wait()

    @pl.when(i < nu)
    def _():
        slot = i % 2

        @pl.when(i >= 2)
        def _():
            drain(i - 2)

        dst_vmem[...] = jnp.zeros(dst_vmem.shape, I32)
        xfull = xs_ref[...]
        tag = xfull[:, d:]
        x = xfull[:, :d].astype(BF16)
        lane = lax.broadcasted_iota(I32, (rows, LANES), 1)
        p = (lax.broadcasted_iota(I32, (rows, LANES), 0) + i * rows).astype(F32)
        hit = jnp.logical_and(tag == p, lane < META_W)
        w_at = pltpu.roll(tag, LANES - META_W, axis=1)
        w = jnp.sum(jnp.where(hit, w_at, 0.0), axis=1, keepdims=True)
        kf = jnp.sum(jnp.where(hit, lane.astype(F32), 0.0), axis=1, keepdims=True)
        dest = kf * float(n_tok) + tag[:, META_TOK:META_TOK + 1]

        dest_b = jnp.broadcast_to(dest, (rows, LANES))
        n_blk = -(-rows // LANES)
        dest_b = jnp.concatenate([dest_b, jnp.zeros((n_blk * LANES - rows, LANES), F32)], axis=0)
        for b in range(n_blk):
            dst_vmem[b:b + 1, :] = dest_b[b * LANES:(b + 1) * LANES, :].T[0:1, :].astype(I32)
        to_smem = [pltpu.make_async_copy(dst_vmem.at[b], dst_smem.at[pl.ds(b * LANES, LANES)], dsem)
                   for b in range(n_blk)]
        for cp in to_smem:
            cp.start()

        g = jnp.dot(x, wg_ref[0].astype(BF16), preferred_element_type=F32)
        u = jnp.dot(x, wu_ref[0].astype(BF16), preferred_element_type=F32)
        h = (_silu(g) * u).astype(BF16)
        y = jnp.dot(h, wd_ref[0].astype(BF16), preferred_element_type=F32) * w
        ybuf[slot] = y.reshape(rows // SUBLANES, SUBLANES, d)
        for cp in to_smem:
            cp.wait()

        nv = nv_ref[i]
        full = lax.shift_right_logical(nv, 3)

        def scatter_rows(s):
            def send(r8, j, prio):
                to = dst_smem[r8 * SUBLANES + j]
                pltpu.make_async_copy(ybuf.at[s, r8, pl.ds(j, 1)], g_hbm.at[pl.ds(to, 1)],
                                      sem.at[s]).start(priority=prio)

            def issue8(r8, carry):
                for j in range(SUBLANES):
                    send(r8, j, j % DMA_THREADS)
                return carry

            def issue1(r, carry):
                send(lax.shift_right_logical(r, 3), r & (SUBLANES - 1), 0)
                return carry

            lax.fori_loop(0, full, issue8, 0)
            lax.fori_loop(full * SUBLANES, nv, issue1, 0)

        for s in range(2):
            pl.when(slot == s)(functools.partial(scatter_rows, s))

    @pl.when(i == pl.num_programs(0) - 1)
    def _():
        @pl.when(nu >= 2)
        def _():
            drain(nu - 2)

        @pl.when(nu >= 1)
        def _():
            drain(nu - 1)


def _experts(tile_expert, n_used, tile_valid, xs, w_gate, w_up, w_down, n_tok):
    n_rows, width = xs.shape
    d, de = w_gate.shape[1:]
    n_tiles = n_rows // ROW_TILE
    row_map = lambda i, te, nu, nv: (jnp.minimum(i, nu[0] - 1), 0)
    w_map = lambda i, te, nu, nv: (te[i], 0, 0)
    n_blk = -(-ROW_TILE // LANES)
    return pl.pallas_call(
        functools.partial(_expert_kernel, n_tok=n_tok),
        out_shape=jax.ShapeDtypeStruct((TOP_K * n_tok, d), F32),
        grid_spec=pltpu.PrefetchScalarGridSpec(
            num_scalar_prefetch=3, grid=(n_tiles,),
            in_specs=[pl.BlockSpec((ROW_TILE, width), row_map),
                      pl.BlockSpec((1, d, de), w_map), pl.BlockSpec((1, d, de), w_map),
                      pl.BlockSpec((1, de, d), w_map)],
            out_specs=pl.BlockSpec(memory_space=pl.ANY),
            scratch_shapes=[pltpu.VMEM((2, ROW_TILE // SUBLANES, SUBLANES, d), F32),
                            pltpu.VMEM((SUBLANES, LANES), I32),
                            pltpu.SMEM((n_blk * LANES,), I32), pltpu.SemaphoreType.DMA((2,)),
                            pltpu.SemaphoreType.DMA]),
        compiler_params=_params(("arbitrary",)),
        name="experts",
    )(tile_expert, n_used, tile_valid, xs, w_gate, w_up, w_down)


def _combine_kernel(g_ref, u_ref, x1_ref, mod_ref, wsg_ref, wsu_ref, wsd_ref, gpost_ref, o_ref,
                    *, per_row, rows_per_batch):
    tm, d = u_ref.shape
    ub = u_ref[...].astype(BF16)
    hs = _silu(jnp.dot(ub, wsg_ref[...], preferred_element_type=F32)) * jnp.dot(
        ub, wsu_ref[...], preferred_element_type=F32)
    f = jnp.dot(hs.astype(BF16), wsd_ref[...], preferred_element_type=F32)
    routed = g_ref[0]
    for k in range(1, TOP_K):
        routed = routed + g_ref[k]
    f = routed + f
    (gate2,) = _mod_rows(mod_ref, per_row, rows_per_batch, tm, d)
    o_ref[...] = x1_ref[...] + gate2 * (_rms(f) * gpost_ref[...])


def _combine(g_rows, u2, x1, mod, consts, per_row, rows_per_batch, tile_off, n_tok, tm):
    t_all, d = u2.shape
    ds_ = consts["w_sh_gate"].shape[1]
    tok_spec = lambda w: pl.BlockSpec((tm, w), lambda i: (i + tile_off, 0))
    return pl.pallas_call(
        functools.partial(_combine_kernel, per_row=per_row, rows_per_batch=rows_per_batch),
        out_shape=jax.ShapeDtypeStruct((n_tok, d), F32),
        grid=(n_tok // tm,),
        in_specs=[pl.BlockSpec((TOP_K, tm, d), lambda i: (0, i + tile_off, 0)),
                  tok_spec(d), tok_spec(d), _mod_spec(mod, per_row, tm),
                  _const_spec((d, ds_)), _const_spec((d, ds_)), _const_spec((ds_, d)), _const_spec((1, d))],
        out_specs=pl.BlockSpec((tm, d), lambda i: (i, 0)),
        compiler_params=_params(("arbitrary",)),
        name="combine",
    )(g_rows.reshape(TOP_K, t_all, d), u2, x1, mod, consts["w_sh_gate"], consts["w_sh_up"],
      consts["w_sh_down"], consts["g_post_ffn"])


def _repeat_rows(x, n):
    r, c = x.shape
    return jnp.broadcast_to(x[:, None, :], (r, n, c)).reshape(r * n, c)


def _tile_major(pos_t, tile):
    k, t = pos_t.shape
    return pos_t.reshape(k, t // tile, tile).transpose(1, 0, 2).reshape(-1)


def kernel(x_prompt, x_sample, c_prompt, c_sample, state_ssm, state_ssd_conv, state_short_conv, w_ada, b_ada, g_pre_mix, g_post_mix, g_pre_ffn, g_post_ffn, w_in, ssd_conv_w, ssd_conv_b, dt_bias, a_log, d_skip, g_ssd_norm, sc_conv_w, g_sc_norm, w_out, w_router, router_bias, w_exp_gate, w_exp_up, w_exp_down, w_sh_gate, w_sh_up, w_sh_down):
    depth = w_ada.shape[0]
    bp, seq, d = x_prompt.shape
    bs, dseq, _ = x_sample.shape
    heads = dt_bias.shape[1]
    ssd_w = heads * SSD_HEAD_DIM
    conv_dim = ssd_conv_w.shape[2]
    sc_w = sc_conv_w.shape[2]
    ne = w_router.shape[2]
    tp, ts = bp * seq, bs * dseq
    t_all = tp + ts

    assert sc_w // SC_GROUPS == SSD_HEAD_DIM and heads == SC_GROUPS
    head_of = jnp.arange(ssd_w, dtype=I32) // SSD_HEAD_DIM
    e_ind = (jnp.arange(LANES, dtype=I32)[:, None] == head_of[None, :])

    xp = x_prompt.reshape(tp, d)
    xs_pad = jnp.pad(x_sample, ((0, 0), (0, SUBLANES - dseq), (0, 0))).reshape(bs * SUBLANES, d)
    xs_tok = x_sample.reshape(ts, d)
    outs = {k: [] for k in ("ssm_p", "cst_p", "scst_p", "ssm_s", "cst_s", "scst_s")}

    for layer in range(depth):
        cuts = np.cumsum([0, ssd_w, conv_dim, heads, sc_w, sc_w, sc_w]).tolist()
        wi = w_in[layer]
        seg = lambda k: wi[:, cuts[k]:cuts[k + 1]]
        w_in_r = jnp.concatenate([seg(0), seg(1), seg(3), seg(4), seg(5),
                                  jnp.pad(seg(2), ((0, 0), (0, LANES - heads)))], axis=1).astype(BF16)
        widths = (ssd_w, conv_dim, sc_w, sc_w, sc_w, LANES)
        pad_h = lambda v: jnp.pad(v.reshape(1, heads), ((0, 0), (0, LANES - heads)))
        wr = w_router[layer].T
        wr_hi = wr.astype(BF16)
        consts = dict(
            heads=heads,
            conv_w=ssd_conv_w[layer], conv_b=ssd_conv_b[layer].reshape(1, conv_dim),
            dt_bias=pad_h(dt_bias[layer]), a_log=pad_h(a_log[layer]),
            d_skip_e=jnp.broadcast_to(d_skip[layer][:, None], (heads, SSD_HEAD_DIM)).reshape(1, ssd_w),
            g_ssd=g_ssd_norm[layer].reshape(1, ssd_w), sc_w=sc_conv_w[layer],
            g_sc=g_sc_norm[layer].reshape(1, sc_w),
            e=e_ind.astype(BF16), et=e_ind.T.astype(F32),
            w_out=w_out[layer].astype(BF16), g_post_mix=g_post_mix[layer].reshape(1, d),
            g_pre_ffn=g_pre_ffn[layer].reshape(1, d),
            wr_hi_t=wr_hi, wr_lo_t=(wr - wr_hi.astype(F32)).astype(BF16),
            w_sh_gate=w_sh_gate[layer].astype(BF16), w_sh_up=w_sh_up[layer].astype(BF16),
            w_sh_down=w_sh_down[layer].astype(BF16), g_post_ffn=g_post_ffn[layer].reshape(1, d))

        c_all = jnp.concatenate([c_prompt, c_sample], axis=0)
        m_rows = -(-c_all.shape[0] // 16) * 16
        mod = _ada(jnp.pad(c_all, ((0, m_rows - c_all.shape[0]), (0, 0))), w_ada[layer], b_ada[layer])
        mod_p = mod[:bp]
        mod_s = mod[bp:bp + bs]
        mod_s_pad = _repeat_rows(mod_s[:, :2 * d], SUBLANES)
        mod_s_tok = _repeat_rows(mod_s, dseq)

        proj_p = _in_proj(xp, mod_p[:, :2 * d], g_pre_mix[layer], w_in_r, widths, BF16, False, seq, TOK_TILE)
        proj_s = _in_proj(xs_pad, mod_s_pad, g_pre_mix[layer], w_in_r, widths, F32, True, 1, TOK_TILE)
        ymix_p, ssm_p, cst_p, scst_p = _ssd(proj_p, None, consts, bp, seq // CHUNK, CHUNK, CHUNK, CHUNK, BF16)
        init = (state_ssm[layer].reshape(bs, ssd_w, D_STATE), state_ssd_conv[layer], state_short_conv[layer])
        ymix_s, ssm_s, cst_s, scst_s = _ssd(proj_s, init, consts, bs, 1, SAMPLE_CHUNK, SUBLANES, dseq, F32)
        ymix_s = ymix_s.reshape(bs, SUBLANES, ssd_w + sc_w)[:, :dseq].reshape(ts, ssd_w + sc_w)

        merged = _out_proj(ymix_p, xp, mod_p[:, 2 * d:5 * d], consts, False, seq, TOK_TILE, t_all, 0, None)
        x1, u2, logits_t = _out_proj(ymix_s, xs_tok, mod_s_tok[:, 2 * d:5 * d], consts, True, 1, TOK_TILE,
                                     t_all, tp // TOK_TILE, merged)

        idx_t, rank_t, wtok, counts = _route(logits_t, router_bias[layer], TOK_TILE)
        counts = counts[:, 0]
        padded = (counts + ROW_TILE - 1) // ROW_TILE * ROW_TILE
        pad_end = jnp.cumsum(padded)
        pad_start = pad_end - padded
        n_tiles = -(-(t_all * TOP_K) // ROW_TILE) + ne
        n_used = (pad_end[-1] // ROW_TILE).astype(I32)
        tile_ids = jnp.minimum(jnp.arange(n_tiles, dtype=I32), n_used - 1)
        tile_expert = jnp.minimum(jnp.sum(pad_end[None, :] <= (tile_ids * ROW_TILE)[:, None], axis=1), ne - 1).astype(I32)
        first_pad = (pad_start + counts) // SUBLANES * SUBLANES
        units = (pad_end - first_pad) // SUBLANES
        tile_valid = jnp.clip(counts[tile_expert] - (tile_ids * ROW_TILE - pad_start[tile_expert]), 0,
                              ROW_TILE).astype(I32)
        pos_t, meta = _positions(idx_t, rank_t, pad_start.astype(I32), wtok, TOK_TILE)
        pos_flat = _tile_major(pos_t, GATHER_TILE)

        xs_rows = _zero_rows(first_pad.astype(I32), units.astype(I32), n_tiles * ROW_TILE, d + LANES)
        xs_rows = _dispatch(pos_flat, u2, meta, xs_rows, GATHER_TILE)
        g_rows = _experts(tile_expert, n_used.reshape(1), tile_valid, xs_rows, w_exp_gate[layer],
                          w_exp_up[layer], w_exp_down[layer], t_all)
        xp = _combine(g_rows, u2, x1, mod_p[:, 5 * d:], consts, False, seq, 0, tp, GATHER_TILE)
        xs_tok = _combine(g_rows, u2, x1, mod_s_tok[:, 5 * d:], consts, True, 1, tp // GATHER_TILE, ts,
                          GATHER_TILE)
        xs_pad = jnp.pad(xs_tok.reshape(bs, dseq, d), ((0, 0), (0, SUBLANES - dseq), (0, 0))).reshape(
            bs * SUBLANES, d)

        outs["ssm_p"].append(ssm_p.reshape(bp, heads, SSD_HEAD_DIM, D_STATE))
        outs["cst_p"].append(cst_p)
        outs["scst_p"].append(scst_p)
        outs["ssm_s"].append(ssm_s.reshape(bs, heads, SSD_HEAD_DIM, D_STATE))
        outs["cst_s"].append(cst_s)
        outs["scst_s"].append(scst_s)

    return (xp.reshape(bp, seq, d), xs_tok.reshape(bs, dseq, d),
            jnp.stack(outs["ssm_p"]), jnp.stack(outs["cst_p"]), jnp.stack(outs["scst_p"]),
            jnp.stack(outs["ssm_s"]), jnp.stack(outs["cst_s"]), jnp.stack(outs["scst_s"]))
```

```python
import functools

import jax
import jax.numpy as jnp
import numpy as np
from jax import lax
from jax.experimental import pallas as pl
from jax.experimental.pallas import tpu as pltpu

F32 = jnp.float32
BF16 = jnp.bfloat16
I32 = jnp.int32
U32 = jnp.uint32
HI = lax.Precision.HIGHEST

SSD_HEAD_DIM = 64
SSD_GROUPS = 2
D_STATE = 128
SSD_CONV_W = 4
SC_GROUPS = 16
SC_CONV_W = 3
TOP_K = 8
N_EXPERT_GROUPS = 8
TOPK_GROUPS = 4
ROUTED_SCALE = 2.5
EPS = 1e-6

LANES = 128
SUBLANES = 8
CHUNK = 128
SAMPLE_CHUNK = 16
TOK_TILE = 512
ROW_TILE = 576
GATHER_TILE = 256
VMEM_LIMIT = 56 * 1024 * 1024
DMA_THREADS = 2

NT = (((1,), (1,)), ((), ()))
TN = (((0,), (0,)), ((), ()))


def _sigmoid(x):
    return 1.0 / (1.0 + jnp.exp(-x))


def _silu(x):
    return x * _sigmoid(x)


def _softplus(x):
    return jnp.maximum(x, 0.0) + jnp.log1p(jnp.exp(-jnp.abs(x)))


def _rms(x, eps=EPS):
    return x * lax.rsqrt(jnp.mean(x * x, axis=-1, keepdims=True) + eps)


def _params(sem=None):
    return pltpu.CompilerParams(dimension_semantics=sem, vmem_limit_bytes=VMEM_LIMIT)


def _const_spec(shape, single=False):
    nd = len(shape)
    mode = dict(pipeline_mode=pl.Buffered(1)) if single else {}
    return pl.BlockSpec(shape, lambda *_: (0,) * nd, **mode)


def _ada_kernel(c_ref, w_ref, b_ref, o_ref):
    c = c_ref[...]
    s = _silu(c).astype(BF16)
    o_ref[...] = jnp.dot(s, w_ref[...].astype(BF16), preferred_element_type=F32) + b_ref[...]


def _ada(c, w_ada, b_ada):
    m, d = c.shape
    n = w_ada.shape[1]
    tn = 512
    return pl.pallas_call(
        _ada_kernel,
        out_shape=jax.ShapeDtypeStruct((m, n), F32),
        grid=(n // tn,),
        in_specs=[_const_spec((m, d)),
                  pl.BlockSpec((d, tn), lambda j: (0, j)),
                  pl.BlockSpec((1, tn), lambda j: (0, j))],
        out_specs=pl.BlockSpec((m, tn), lambda j: (0, j)),
        compiler_params=_params(("arbitrary",)),
        name="ada",
    )(c, w_ada, b_ada.reshape(1, n))


def _mod_rows(mod_ref, per_row, rows_per_batch, tile, d):
    n = mod_ref.shape[1] // d
    if per_row:
        return [mod_ref[:, k * d:(k + 1) * d] for k in range(n)]
    b = (pl.program_id(0) * tile) // rows_per_batch
    return [mod_ref[pl.ds(b, 1), k * d:(k + 1) * d] for k in range(n)]


def _mod_spec(mod, per_row, tile):
    if per_row:
        return pl.BlockSpec((tile, mod.shape[1]), lambda i: (i, 0))
    return _const_spec(mod.shape)


def _in_kernel(x_ref, mod_ref, g_ref, w_ref, *out_refs, per_row, rows_per_batch, widths):
    tile, d = x_ref.shape
    shift, scale = _mod_rows(mod_ref, per_row, rows_per_batch, tile, d)
    u = (_rms(x_ref[...]) * g_ref[...]) * (1.0 + scale) + shift
    u = u.astype(BF16)
    col = 0
    for ref, width in zip(out_refs, widths):
        for a in range(0, width, 512):
            bw = min(512, width - a)
            r = jnp.dot(u, w_ref[:, col + a:col + a + bw], preferred_element_type=F32)
            ref[:, a:a + bw] = r.astype(ref.dtype)
        col += width


def _in_proj(x, mod, g, w_bf16, widths, out_dtype, per_row, rows_per_batch, tile):
    t, d = x.shape
    n = w_bf16.shape[1]
    mod_spec = _mod_spec(mod, per_row, tile)
    dts = [out_dtype] * (len(widths) - 1) + [F32]
    return pl.pallas_call(
        functools.partial(_in_kernel, per_row=per_row, rows_per_batch=rows_per_batch, widths=widths),
        out_shape=[jax.ShapeDtypeStruct((t, wd), dt) for wd, dt in zip(widths, dts)],
        grid=(t // tile,),
        in_specs=[pl.BlockSpec((tile, d), lambda i: (i, 0)), mod_spec,
                  _const_spec((1, d)), _const_spec((d, n), single=True)],
        out_specs=[pl.BlockSpec((tile, wd), lambda i: (i, 0)) for wd in widths],
        compiler_params=_params(("arbitrary",)),
        name="in_proj",
    )(x, mod, g.reshape(1, d), w_bf16)


def _ssd_kernel(*refs, q, rows_in, q_valid, has_init, heads, widths):
    it = iter(refs)
    if widths is None:
        z_ref, xbc_ref, scb_ref, scc_ref, sch_ref, dt_ref = (next(it) for _ in range(6))
    else:
        x_ref, mod_ref, gpre_ref, win_ref = (next(it) for _ in range(4))
    if has_init:
        ssm0_ref, cst0_ref, scst0_ref = (next(it) for _ in range(3))
    (cw_ref, cb_ref, dtb_ref, alog_ref, dsk_ref, gssd_ref, scw_ref, gsc_ref,
     tril_ref, e_ref, et_ref) = (next(it) for _ in range(11))
    ymix_ref, ssm_ref, cst_ref, scst_ref = (next(it) for _ in range(4))
    h_scr, ext_scr, extv_scr = (next(it) for _ in range(3))

    c = pl.program_id(1)
    nc = pl.num_programs(1)
    ssd_w = dsk_ref.shape[1]
    gw = ssd_w // SSD_GROUPS
    hpg = heads // SSD_GROUPS
    n_state = D_STATE
    head0 = SUBLANES - (SSD_CONV_W - 1)
    headv = SUBLANES - (SC_CONV_W - 1)

    @pl.when(c == 0)
    def _():
        if has_init:
            h_scr[...] = ssm0_ref[0]
            ext_scr[head0:SUBLANES, :] = cst0_ref[0]
            extv_scr[headv:SUBLANES, :] = scst0_ref[0]
        else:
            h_scr[...] = jnp.zeros(h_scr.shape, F32)
            ext_scr[0:SUBLANES, :] = jnp.zeros((SUBLANES, ext_scr.shape[1]), F32)
            extv_scr[0:SUBLANES, :] = jnp.zeros((SUBLANES, extv_scr.shape[1]), F32)

    def pad_rows(v, n):
        if v.shape[0] == n:
            return v
        return jnp.concatenate([v, jnp.zeros((n - v.shape[0], v.shape[1]), v.dtype)], axis=0)

    load = lambda ref: pad_rows(ref[...].astype(F32), q)
    pad_t = lambda v: pad_rows(v, LANES)
    if widths is None:
        z, xbc, scb, scc, sch, dt_raw = (load(r) for r in (z_ref, xbc_ref, scb_ref, scc_ref, sch_ref, dt_ref))
    else:
        d_model = x_ref.shape[1]
        b = pl.program_id(0)
        shift = mod_ref[pl.ds(b, 1), 0:d_model]
        scale = mod_ref[pl.ds(b, 1), d_model:2 * d_model]
        u = ((_rms(x_ref[...]) * gpre_ref[...]) * (1.0 + scale) + shift).astype(BF16)
        pieces, col = [], 0
        for width in widths:
            parts = [jnp.dot(u, win_ref[:, col + a:col + a + min(512, width - a)], preferred_element_type=F32)
                     for a in range(0, width, 512)]
            pieces.append(parts[0] if len(parts) == 1 else jnp.concatenate(parts, axis=1))
            col += width
        z, xbc, scb, scc, sch = (pad_rows(p.astype(BF16).astype(F32), q) for p in pieces[:5])
        dt_raw = pad_rows(pieces[5], q)

    ext_scr[SUBLANES:SUBLANES + q, :] = xbc
    conv = cb_ref[...]
    for k in range(SSD_CONV_W):
        conv = conv + cw_ref[k:k + 1, :] * ext_scr[head0 + k:head0 + k + q, :]
    xc = _silu(conv)
    xs = xc[:, :ssd_w]
    bm = xc[:, ssd_w:ssd_w + SSD_GROUPS * n_state]
    cm = xc[:, ssd_w + SSD_GROUPS * n_state:]

    lane = lax.broadcasted_iota(I32, (q, LANES), 1)
    row = lax.broadcasted_iota(I32, (q, LANES), 0)
    dt = _softplus(dt_raw + dtb_ref[...])
    dt = jnp.where(jnp.logical_and(lane < heads, row < q_valid), dt, 0.0)
    a = dt * (-jnp.exp(alog_ref[...]))
    acum = jnp.dot(tril_ref[...], pad_t(a), precision=HI, preferred_element_type=F32)
    acum_t = pad_t(acum).T
    a_last = acum[q - 1:q, :]
    e = e_ref[...]
    expand = lambda v: jnp.dot(v.astype(BF16), e, preferred_element_type=F32)
    dt_e = expand(dt)
    dtdte_e = expand(dt * jnp.exp(a_last - acum))
    exa_e = expand(jnp.exp(acum))
    xdt = xs * dt_e
    xw_b = (xs * dtdte_e).astype(BF16)
    dlast = jnp.broadcast_to(jnp.exp(acum_t[:, q - 1:q]), (LANES, n_state))
    dcol = jnp.dot(et_ref[...], dlast, precision=HI, preferred_element_type=F32)

    tri = row >= lane
    xdt_t = pad_t(xdt)
    xw_t = pad_t(xw_b)
    lane_t = lax.broadcasted_iota(I32, (LANES, LANES), 1)
    y_groups = []
    for g in range(SSD_GROUPS):
        bm_g = bm[:, g * n_state:(g + 1) * n_state].astype(BF16)
        cm_g = cm[:, g * n_state:(g + 1) * n_state].astype(BF16)
        bm_t = pad_t(bm_g)
        cb = lax.dot_general(cm_g, bm_t, NT, preferred_element_type=F32)
        h_g = h_scr[g * gw:(g + 1) * gw, :]
        y_off = lax.dot_general(cm_g, h_g.astype(BF16), NT, preferred_element_type=F32)
        parts = []
        for pair in range(hpg // 2):
            lo = (g * hpg + 2 * pair) * SSD_HEAD_DIM
            x_pair = xdt_t[:, lo:lo + LANES]
            acc = None
            for half in range(2):
                h = g * hpg + 2 * pair + half
                ci = jnp.broadcast_to(acum[:, h:h + 1], (q, LANES))
                rj = jnp.broadcast_to(acum_t[h:h + 1, :], (q, LANES))
                dec = jnp.where(tri, jnp.exp(ci - rj), 0.0)
                m = (cb * dec).astype(BF16)
                own = (lane_t >= SSD_HEAD_DIM) if half else (lane_t < SSD_HEAD_DIM)
                y_h = jnp.dot(m, jnp.where(own, x_pair, 0.0).astype(BF16), preferred_element_type=F32)
                acc = y_h if acc is None else acc + y_h
            parts.append(acc)
        y_diag = jnp.concatenate(parts, axis=1)
        y_groups.append(y_diag + y_off * exa_e[:, g * gw:(g + 1) * gw])
        upd = lax.dot_general(xw_t[:, g * gw:(g + 1) * gw], bm_t, TN, preferred_element_type=F32)
        h_scr[g * gw:(g + 1) * gw, :] = h_g * dcol[g * gw:(g + 1) * gw, :] + upd

    y = (jnp.concatenate(y_groups, axis=1) + xs * dsk_ref[...]) * _silu(z)
    y = jnp.concatenate([_rms(y[:, g * gw:(g + 1) * gw]) for g in range(SSD_GROUPS)], axis=1)
    y_ssd = y * gssd_ref[...]

    extv_scr[SUBLANES:SUBLANES + q, :] = scc * sch
    cv = scw_ref[0:1, :] * extv_scr[headv:headv + q, :]
    for k in range(1, SC_CONV_W):
        cv = cv + scw_ref[k:k + 1, :] * extv_scr[headv + k:headv + k + q, :]
    t = scb * cv
    sc_per_group = t.shape[1] // SC_GROUPS
    gsum = jnp.dot((t * t).astype(BF16), et_ref[...].astype(BF16), preferred_element_type=F32)
    rs = lax.rsqrt(gsum * (1.0 / sc_per_group) + EPS)
    y_sc = t * expand(rs) * gsc_ref[...]

    ymix_ref[:, :ssd_w] = y_ssd[0:rows_in].astype(ymix_ref.dtype)
    ymix_ref[:, ssd_w:] = y_sc[0:rows_in].astype(ymix_ref.dtype)

    @pl.when(c == nc - 1)
    def _():
        ssm_ref[0] = h_scr[...]
        cst_ref[0] = ext_scr[SUBLANES + q_valid - (SSD_CONV_W - 1):SUBLANES + q_valid, :]
        scst_ref[0] = extv_scr[SUBLANES + q_valid - (SC_CONV_W - 1):SUBLANES + q_valid, :]

    ext_scr[0:SUBLANES, :] = ext_scr[q:q + SUBLANES, :]
    extv_scr[0:SUBLANES, :] = extv_scr[q:q + SUBLANES, :]


def _ssd(proj, init, consts, nb, nc, q, rows_in, q_valid, ymix_dtype, widths=None):
    heads = consts["heads"]
    tril = jnp.tril(jnp.ones((q, LANES), F32))
    ssd_w, conv_dim, sc_w = consts["g_ssd"].shape[1], consts["conv_w"].shape[1], consts["g_sc"].shape[1]
    has_init = init is not None
    row_spec = lambda w: pl.BlockSpec((rows_in, w), lambda b, c: (b * nc + c, 0))
    if widths is None:
        in_specs = [row_spec(ssd_w), row_spec(conv_dim), row_spec(sc_w), row_spec(sc_w), row_spec(sc_w),
                    row_spec(LANES)]
    else:
        x, mod, g, w = proj
        in_specs = [row_spec(x.shape[1]), _const_spec(mod.shape), _const_spec(g.shape),
                    _const_spec(w.shape, single=True)]
    args = list(proj)
    if has_init:
        ssm0, cst0, scst0 = init
        in_specs += [pl.BlockSpec((1,) + ssm0.shape[1:], lambda b, c: (b, 0, 0)),
                     pl.BlockSpec((1,) + cst0.shape[1:], lambda b, c: (b, 0, 0)),
                     pl.BlockSpec((1,) + scst0.shape[1:], lambda b, c: (b, 0, 0))]
        args += [ssm0, cst0, scst0]
    weights = [consts[k] for k in ("conv_w", "conv_b", "dt_bias", "a_log", "d_skip_e", "g_ssd",
                                   "sc_w", "g_sc")] + [tril, consts["e"], consts["et"]]
    in_specs += [_const_spec(w.shape) for w in weights]
    args += weights
    n_state = D_STATE
    out_shape = [jax.ShapeDtypeStruct((nb * nc * rows_in, ssd_w + sc_w), ymix_dtype),
                 jax.ShapeDtypeStruct((nb, ssd_w, n_state), F32),
                 jax.ShapeDtypeStruct((nb, SSD_CONV_W - 1, conv_dim), F32),
                 jax.ShapeDtypeStruct((nb, SC_CONV_W - 1, sc_w), F32)]
    out_specs = [pl.BlockSpec((rows_in, ssd_w + sc_w), lambda b, c: (b * nc + c, 0)),
                 pl.BlockSpec((1, ssd_w, n_state), lambda b, c: (b, 0, 0)),
                 pl.BlockSpec((1, SSD_CONV_W - 1, conv_dim), lambda b, c: (b, 0, 0)),
                 pl.BlockSpec((1, SC_CONV_W - 1, sc_w), lambda b, c: (b, 0, 0))]
    scratch = [pltpu.VMEM((ssd_w, n_state), F32),
               pltpu.VMEM((q + SUBLANES, conv_dim), F32),
               pltpu.VMEM((q + SUBLANES, sc_w), F32)]
    return pl.pallas_call(
        functools.partial(_ssd_kernel, q=q, rows_in=rows_in, q_valid=q_valid, has_init=has_init, heads=heads,
                          widths=widths),
        out_shape=out_shape, grid=(nb, nc), in_specs=in_specs, out_specs=out_specs,
        scratch_shapes=scratch,
        compiler_params=_params(("arbitrary", "arbitrary")),
        name="ssd",
    )(*args)


def _out_kernel(ymix_ref, x_ref, mod_ref, wout_ref, gpost_ref, gpre_ref, wr_hi_ref, wr_lo_ref,
                *rest, per_row, rows_per_batch, has_alias):
    x1_ref, u2_ref, lg_ref = rest[3:6] if has_alias else rest[0:3]
    tile, d = x_ref.shape
    gate1, shift2, scale2 = _mod_rows(mod_ref, per_row, rows_per_batch, tile, d)
    m = jnp.dot(ymix_ref[...].astype(BF16), wout_ref[...], preferred_element_type=F32)
    x1 = x_ref[...] + gate1 * (_rms(m) * gpost_ref[...])
    u2 = (_rms(x1) * gpre_ref[...]) * (1.0 + scale2) + shift2
    x1_ref[...] = x1
    u2_ref[...] = u2
    u_hi = u2.astype(BF16)
    u_lo = (u2 - u_hi.astype(F32)).astype(BF16)
    lg = lax.dot_general(wr_hi_ref[...], u_hi, NT, preferred_element_type=F32)
    lg = lg + lax.dot_general(wr_hi_ref[...], u_lo, NT, preferred_element_type=F32)
    lg = lg + lax.dot_general(wr_lo_ref[...], u_hi, NT, preferred_element_type=F32)
    lg_ref[...] = lg


def _out_proj(ymix, x, mod, consts, per_row, rows_per_batch, tile, t_total, tile_off, prev):
    t, d = x.shape
    dm = ymix.shape[1]
    ne = consts["wr_hi_t"].shape[0]
    in_specs = [pl.BlockSpec((tile, dm), lambda i: (i, 0)),
                pl.BlockSpec((tile, d), lambda i: (i, 0)), _mod_spec(mod, per_row, tile),
                _const_spec((dm, d)), _const_spec((1, d)), _const_spec((1, d)),
                _const_spec((ne, d)), _const_spec((ne, d))]
    args = [ymix, x, mod, consts["w_out"], consts["g_post_mix"], consts["g_pre_ffn"],
            consts["wr_hi_t"], consts["wr_lo_t"]]
    aliases = {}
    if prev is not None:
        in_specs += [pl.BlockSpec(memory_space=pl.ANY)] * 3
        aliases = {len(args) + k: k for k in range(3)}
        args += list(prev)
    return pl.pallas_call(
        functools.partial(_out_kernel, per_row=per_row, rows_per_batch=rows_per_batch,
                          has_alias=prev is not None),
        out_shape=[jax.ShapeDtypeStruct((t_total, d), F32), jax.ShapeDtypeStruct((t_total, d), F32),
                   jax.ShapeDtypeStruct((ne, t_total), F32)],
        grid=(t // tile,), in_specs=in_specs,
        out_specs=[pl.BlockSpec((tile, d), lambda i: (i + tile_off, 0)),
                   pl.BlockSpec((tile, d), lambda i: (i + tile_off, 0)),
                   pl.BlockSpec((ne, tile), lambda i: (0, i + tile_off))],
        input_output_aliases=aliases,
        compiler_params=_params(("arbitrary",)),
        name="out_proj",
    )(*args)


def _route_kernel(lg_ref, bias_ref, upper_ref, idx_ref, rank_ref, wtok_ref, cnt_ref, carry_scr):
    i = pl.program_id(0)
    ne, tm = lg_ref.shape
    per_group = ne // N_EXPERT_GROUPS
    neg = -jnp.inf

    @pl.when(i == 0)
    def _():
        carry_scr[...] = jnp.zeros(carry_scr.shape, F32)

    s = _sigmoid(lg_ref[...])
    biased = s + bias_ref[...]
    gl = []
    io_g = lax.broadcasted_iota(I32, (per_group, tm), 0).astype(F32)
    for g in range(N_EXPERT_GROUPS):
        blk = biased[g * per_group:(g + 1) * per_group, :]
        m1 = jnp.max(blk, axis=0, keepdims=True)
        f1 = jnp.min(jnp.where(blk == m1, io_g, float(per_group)), axis=0, keepdims=True)
        m2 = jnp.max(jnp.where(io_g == f1, neg, blk), axis=0, keepdims=True)
        gl.append(m1 + m2)
    gscore = jnp.concatenate(gl, axis=0)
    io8 = lax.broadcasted_iota(I32, (N_EXPERT_GROUPS, tm), 0).astype(F32)
    gsel = jnp.zeros((N_EXPERT_GROUPS, tm), F32)
    for _ in range(TOPK_GROUPS):
        m = jnp.max(gscore, axis=0, keepdims=True)
        f = jnp.min(jnp.where(gscore == m, io8, float(N_EXPERT_GROUPS)), axis=0, keepdims=True)
        hit = io8 == f
        gsel = jnp.where(hit, 1.0, gsel)
        gscore = jnp.where(hit, neg, gscore)
    emask = jnp.concatenate(
        [jnp.broadcast_to(gsel[g:g + 1, :], (per_group, tm)) for g in range(N_EXPERT_GROUPS)], axis=0)
    cand = jnp.where(emask > 0.5, biased, neg)
    io_e = lax.broadcasted_iota(I32, (ne, tm), 0).astype(F32)
    msel = jnp.zeros((ne, tm), F32)
    idxs, wts = [], []
    for _ in range(TOP_K):
        m = jnp.max(cand, axis=0, keepdims=True)
        f = jnp.min(jnp.where(cand == m, io_e, float(ne)), axis=0, keepdims=True)
        hit = io_e == f
        wts.append(jnp.sum(jnp.where(hit, s, 0.0), axis=0, keepdims=True))
        idxs.append(f)
        msel = jnp.where(hit, 1.0, msel)
        cand = jnp.where(hit, neg, cand)
    pref = jnp.dot(msel.astype(BF16), upper_ref[...], preferred_element_type=F32) + carry_scr[:, 0:1]
    ranks = [jnp.sum(jnp.where(io_e == f, pref, 0.0), axis=0, keepdims=True) for f in idxs]
    carry_scr[...] = carry_scr[...] + jnp.sum(msel, axis=1, keepdims=True)
    cnt_ref[...] = carry_scr[...].astype(I32)
    idx_ref[...] = jnp.concatenate(idxs, axis=0).astype(I32)
    rank_ref[...] = jnp.concatenate(ranks, axis=0).astype(I32)
    wsum = wts[0]
    for w in wts[1:]:
        wsum = wsum + w
    wn = jnp.concatenate([w / wsum * ROUTED_SCALE for w in wts]
                         + [jnp.zeros((LANES - TOP_K, tm), F32)], axis=0)
    for j in range(tm // LANES):
        wtok_ref[j * LANES:(j + 1) * LANES, :] = wn[:, j * LANES:(j + 1) * LANES].T


def _route(logits_t, bias, tile):
    ne, t = logits_t.shape
    upper = jnp.triu(jnp.ones((tile, tile), F32), 1).astype(BF16)
    return pl.pallas_call(
        _route_kernel,
        out_shape=[jax.ShapeDtypeStruct((TOP_K, t), I32), jax.ShapeDtypeStruct((TOP_K, t), I32),
                   jax.ShapeDtypeStruct((t, LANES), F32), jax.ShapeDtypeStruct((ne, LANES), I32)],
        grid=(t // tile,),
        in_specs=[pl.BlockSpec((ne, tile), lambda i: (0, i)), _const_spec((ne, 1)),
                  _const_spec((tile, tile))],
        out_specs=[pl.BlockSpec((TOP_K, tile), lambda i: (0, i)),
                   pl.BlockSpec((TOP_K, tile), lambda i: (0, i)),
                   pl.BlockSpec((tile, LANES), lambda i: (i, 0)),
                   _const_spec((ne, LANES))],
        scratch_shapes=[pltpu.VMEM((ne, LANES), F32)],
        compiler_params=_params(("arbitrary",)),
        name="route",
    )(logits_t, bias.reshape(ne, 1), upper)


META_POS, META_W, META_TOK = 0, TOP_K, 2 * TOP_K


def _pos_kernel(idx_ref, rank_ref, start_ref, wtok_ref, pos_ref, meta_ref):
    ne = start_ref.shape[0]
    tm = idx_ref.shape[1]
    io_e = lax.broadcasted_iota(I32, (ne, tm), 0)
    start = start_ref[...].astype(F32)
    rows = []
    for k in range(TOP_K):
        hit = io_e == idx_ref[k:k + 1, :]
        rows.append(jnp.sum(jnp.where(hit, start, 0.0), axis=0, keepdims=True))
    pos_f = jnp.concatenate(rows, axis=0) + rank_ref[...].astype(F32)
    pos_ref[...] = pos_f.astype(I32)
    pos_pad = jnp.concatenate([pos_f, jnp.zeros((LANES - TOP_K, tm), F32)], axis=0)
    lane = lax.broadcasted_iota(I32, (LANES, LANES), 1)
    row = lax.broadcasted_iota(I32, (LANES, LANES), 0)
    for j in range(tm // LANES):
        pos_tok = pos_pad[:, j * LANES:(j + 1) * LANES].T
        w_tok = pltpu.roll(wtok_ref[j * LANES:(j + 1) * LANES, :], META_W, axis=1)
        tok = (row + (pl.program_id(0) * tm + j * LANES)).astype(F32)
        tag = jnp.where(lane < META_W, pos_tok, jnp.where(lane == META_TOK, tok, w_tok))
        meta_ref[j * LANES:(j + 1) * LANES, :] = tag


def _positions(idx_t, rank_t, pad_start, wtok, tile):
    k, t = idx_t.shape
    ne = pad_start.shape[0]
    return pl.pallas_call(
        _pos_kernel,
        out_shape=[jax.ShapeDtypeStruct((k, t), I32), jax.ShapeDtypeStruct((t, LANES), F32)],
        grid=(t // tile,),
        in_specs=[pl.BlockSpec((k, tile), lambda i: (0, i)), pl.BlockSpec((k, tile), lambda i: (0, i)),
                  _const_spec((ne, 1)), pl.BlockSpec((tile, LANES), lambda i: (i, 0))],
        out_specs=[pl.BlockSpec((k, tile), lambda i: (0, i)), pl.BlockSpec((tile, LANES), lambda i: (i, 0))],
        compiler_params=_params(("arbitrary",)),
        name="positions",
    )(idx_t, rank_t, pad_start.reshape(ne, 1), wtok)


ZERO_BITS = (64, 32, 16, 8, 4, 2, 1)
assert sum(ZERO_BITS) * SUBLANES >= ROW_TILE


def _zero_copy(zbuf, xs_ref, sem, off, bit):
    rows = bit * SUBLANES
    return pltpu.make_async_copy(zbuf.at[pl.ds(0, rows)], xs_ref.at[pl.ds(pl.multiple_of(off, SUBLANES), rows)], sem)


def _zero_kernel(first_ref, units_ref, xs_ref, zbuf, sem):
    zbuf[...] = jnp.zeros(zbuf.shape, zbuf.dtype)
    ne = first_ref.shape[0]

    def each(e, start):
        off = first_ref[e]
        units = units_ref[e]
        for bit in ZERO_BITS:
            on = (units & bit) != 0

            @pl.when(on)
            def _():
                cp = _zero_copy(zbuf, xs_ref, sem, off, bit)
                cp.start() if start else cp.wait()
            off = off + jnp.where(on, bit * SUBLANES, 0)
        return start

    lax.fori_loop(0, ne, lambda e, c: (each(e, True), c)[1], 0)
    lax.fori_loop(0, ne, lambda e, c: (each(e, False), c)[1], 0)


def _zero_rows(first, units, n_rows, width):
    return pl.pallas_call(
        _zero_kernel,
        out_shape=jax.ShapeDtypeStruct((n_rows, width), F32),
        in_specs=[pl.BlockSpec(memory_space=pltpu.SMEM), pl.BlockSpec(memory_space=pltpu.SMEM)],
        out_specs=pl.BlockSpec(memory_space=pl.ANY),
        scratch_shapes=[pltpu.VMEM((ZERO_BITS[0] * SUBLANES, width), F32), pltpu.SemaphoreType.DMA],
        compiler_params=_params(),
        name="zero_rows",
    )(first, units)


def _row_copy(src_ref, dst_ref, sem, s, d):
    return pltpu.make_async_copy(src_ref.at[pl.ds(s, 1)], dst_ref.at[pl.ds(d, 1)], sem)


def _dispatch_kernel(pos_hbm, u_ref, meta_ref, xs_in, xs_ref, pos_a, pos_b, src, psem, sem, *, nt):
    del xs_in
    i = pl.program_id(0)
    tm, d = u_ref.shape
    per = pos_a.shape[0]
    pos_slots = (pos_a, pos_b)

    def pos_copy(tile, s):
        return pltpu.make_async_copy(pos_hbm.at[pl.ds(pl.multiple_of(tile * per, per), per)], pos_slots[s],
                                     psem.at[s])

    def drain(s):
        for _ in range(TOP_K):
            pltpu.make_async_copy(src.at[s], xs_ref.at[pl.ds(0, tm)], sem.at[s]).wait()

    def step(s):
        @pl.when(i == 0)
        def _():
            pos_copy(0, 0).start()

        @pl.when(i + 1 < nt)
        def _():
            pos_copy(i + 1, 1 - s).start()

        @pl.when(i >= 2)
        def _():
            drain(s)

        src[s, :, :d] = u_ref[...]
        src[s, :, d:] = meta_ref[...]
        pos_copy(i, s).wait()

        def issue(t, carry):
            for k in range(TOP_K):
                _row_copy(src.at[s], xs_ref, sem.at[s], t, pos_slots[s][k * tm + t]).start(
                    priority=k % DMA_THREADS)
            return carry

        lax.fori_loop(0, tm, issue, 0)

        @pl.when(i == nt - 1)
        def _():
            if nt >= 2:
                drain(1 - s)
            drain(s)

    for s in range(2):
        pl.when(i % 2 == s)(functools.partial(step, s))


def _dispatch(pos_flat, u2, meta, xs, tm):
    per = TOP_K * tm
    nt = pos_flat.shape[0] // per
    d = u2.shape[1]
    return pl.pallas_call(
        functools.partial(_dispatch_kernel, nt=nt),
        out_shape=jax.ShapeDtypeStruct(xs.shape, xs.dtype),
        grid=(nt,),
        in_specs=[pl.BlockSpec(memory_space=pl.ANY), pl.BlockSpec((tm, d), lambda i: (i, 0)),
                  pl.BlockSpec((tm, LANES), lambda i: (i, 0)), pl.BlockSpec(memory_space=pl.ANY)],
        out_specs=pl.BlockSpec(memory_space=pl.ANY),
        scratch_shapes=[pltpu.SMEM((per,), I32), pltpu.SMEM((per,), I32),
                        pltpu.VMEM((2, tm, d + LANES), F32),
                        pltpu.SemaphoreType.DMA((2,)), pltpu.SemaphoreType.DMA((2,))],
        input_output_aliases={3: 0},
        compiler_params=_params(("arbitrary",)),
        name="dispatch",
    )(pos_flat, u2, meta, xs)


WAIT_BITS = tuple(1 << b for b in reversed(range(ROW_TILE.bit_length())))


def _expert_kernel(te_ref, nu_ref, nv_ref, xs_ref, wg_ref, wu_ref, wd_ref, g_hbm,
                   ybuf, dst_vmem, dst_smem, sem, dsem, *, n_tok):
    del te_ref
    i = pl.program_id(0)
    nu = nu_ref[0]
    d = ybuf.shape[3]
    rows = ybuf.shape[1] * SUBLANES

    def drain(tile):
        slot = tile % 2
        n = nv_ref[tile]
        for bit in WAIT_BITS:
            @pl.when((n & bit) != 0)
            def _():
                pltpu.make_async_copy(g_hbm.at[pl.ds(0, bit)], g_hbm.at[pl.ds(0, bit)], sem.at[slot]).wait()

    @pl.when(i < nu)
    def _():
        slot = i % 2

        @pl.when(i >= 2)
        def _():
            drain(i - 2)

        dst_vmem[...] = jnp.zeros(dst_vmem.shape, I32)
        xfull = xs_ref[...]
        tag = xfull[:, d:]
        x = xfull[:, :d].astype(BF16)
        lane = lax.broadcasted_iota(I32, (rows, LANES), 1)
        p = (lax.broadcasted_iota(I32, (rows, LANES), 0) + i * rows).astype(F32)
        hit = jnp.logical_and(tag == p, lane < META_W)
        w_at = pltpu.roll(tag, LANES - META_W, axis=1)
        w = jnp.sum(jnp.where(hit, w_at, 0.0), axis=1, keepdims=True)
        kf = jnp.sum(jnp.where(hit, lane.astype(F32), 0.0), axis=1, keepdims=True)
        dest = kf * float(n_tok) + tag[:, META_TOK:META_TOK + 1]

        dest_b = jnp.broadcast_to(dest, (rows, LANES))
        n_blk = -(-rows // LANES)
        dest_b = jnp.concatenate([dest_b, jnp.zeros((n_blk * LANES - rows, LANES), F32)], axis=0)
        for b in range(n_blk):
            dst_vmem[b:b + 1, :] = dest_b[b * LANES:(b + 1) * LANES, :].T[0:1, :].astype(I32)
        to_smem = [pltpu.make_async_copy(dst_vmem.at[b], dst_smem.at[pl.ds(b * LANES, LANES)], dsem)
                   for b in range(n_blk)]
        for cp in to_smem:
            cp.start()

        g = jnp.dot(x, wg_ref[0].astype(BF16), preferred_element_type=F32)
        u = jnp.dot(x, wu_ref[0].astype(BF16), preferred_element_type=F32)
        h = (_silu(g) * u).astype(BF16)
        y = jnp.dot(h, wd_ref[0].astype(BF16), preferred_element_type=F32) * w
        ybuf[slot] = y.reshape(rows // SUBLANES, SUBLANES, d)
        for cp in to_smem:
            cp.wait()

        nv = nv_ref[i]
        full = lax.shift_right_logical(nv, 3)

        def scatter_rows(s):
            def send(r8, j, prio):
                to = dst_smem[r8 * SUBLANES + j]
                pltpu.make_async_copy(ybuf.at[s, r8, pl.ds(j, 1)], g_hbm.at[pl.ds(to, 1)],
                                      sem.at[s]).start(priority=prio)

            def issue8(r8, carry):
                for j in range(SUBLANES):
                    send(r8, j, j % DMA_THREADS)
                return carry

            def issue1(r, carry):
                send(lax.shift_right_logical(r, 3), r & (SUBLANES - 1), 0)
                return carry

            lax.fori_loop(0, full, issue8, 0)
            lax.fori_loop(full * SUBLANES, nv, issue1, 0)

        for s in range(2):
            pl.when(slot == s)(functools.partial(scatter_rows, s))

    @pl.when(i == pl.num_programs(0) - 1)
    def _():
        @pl.when(nu >= 2)
        def _():
            drain(nu - 2)

        @pl.when(nu >= 1)
        def _():
            drain(nu - 1)


def _experts(tile_expert, n_used, tile_valid, xs, w_gate, w_up, w_down, n_tok):
    n_rows, width = xs.shape
    d, de = w_gate.shape[1:]
    n_tiles = n_rows // ROW_TILE
    row_map = lambda i, te, nu, nv: (jnp.minimum(i, nu[0] - 1), 0)
    w_map = lambda i, te, nu, nv: (te[i], 0, 0)
    n_blk = -(-ROW_TILE // LANES)
    return pl.pallas_call(
        functools.partial(_expert_kernel, n_tok=n_tok),
        out_shape=jax.ShapeDtypeStruct((TOP_K * n_tok, d), F32),
        grid_spec=pltpu.PrefetchScalarGridSpec(
            num_scalar_prefetch=3, grid=(n_tiles,),
            in_specs=[pl.BlockSpec((ROW_TILE, width), row_map),
                      pl.BlockSpec((1, d, de), w_map), pl.BlockSpec((1, d, de), w_map),
                      pl.BlockSpec((1, de, d), w_map)],
            out_specs=pl.BlockSpec(memory_space=pl.ANY),
            scratch_shapes=[pltpu.VMEM((2, ROW_TILE // SUBLANES, SUBLANES, d), F32),
                            pltpu.VMEM((SUBLANES, LANES), I32),
                            pltpu.SMEM((n_blk * LANES,), I32), pltpu.SemaphoreType.DMA((2,)),
                            pltpu.SemaphoreType.DMA]),
        compiler_params=_params(("arbitrary",)),
        name="experts",
    )(tile_expert, n_used, tile_valid, xs, w_gate, w_up, w_down)


def _combine_kernel(g_ref, u_ref, x1_ref, mod_ref, wsg_ref, wsu_ref, wsd_ref, gpost_ref, o_ref,
                    *, per_row, rows_per_batch):
    tm, d = u_ref.shape
    ub = u_ref[...].astype(BF16)
    hs = _silu(jnp.dot(ub, wsg_ref[...], preferred_element_type=F32)) * jnp.dot(
        ub, wsu_ref[...], preferred_element_type=F32)
    f = jnp.dot(hs.astype(BF16), wsd_ref[...], preferred_element_type=F32)
    routed = g_ref[0]
    for k in range(1, TOP_K):
        routed = routed + g_ref[k]
    f = routed + f
    (gate2,) = _mod_rows(mod_ref, per_row, rows_per_batch, tm, d)
    o_ref[...] = x1_ref[...] + gate2 * (_rms(f) * gpost_ref[...])


def _combine(g_rows, u2, x1, mod, consts, per_row, rows_per_batch, tile_off, n_tok, tm):
    t_all, d = u2.shape
    ds_ = consts["w_sh_gate"].shape[1]
    tok_spec = lambda w: pl.BlockSpec((tm, w), lambda i: (i + tile_off, 0))
    return pl.pallas_call(
        functools.partial(_combine_kernel, per_row=per_row, rows_per_batch=rows_per_batch),
        out_shape=jax.ShapeDtypeStruct((n_tok, d), F32),
        grid=(n_tok // tm,),
        in_specs=[pl.BlockSpec((TOP_K, tm, d), lambda i: (0, i + tile_off, 0)),
                  tok_spec(d), tok_spec(d), _mod_spec(mod, per_row, tm),
                  _const_spec((d, ds_)), _const_spec((d, ds_)), _const_spec((ds_, d)), _const_spec((1, d))],
        out_specs=pl.BlockSpec((tm, d), lambda i: (i, 0)),
        compiler_params=_params(("arbitrary",)),
        name="combine",
    )(g_rows.reshape(TOP_K, t_all, d), u2, x1, mod, consts["w_sh_gate"], consts["w_sh_up"],
      consts["w_sh_down"], consts["g_post_ffn"])


def _repeat_rows(x, n):
    r, c = x.shape
    return jnp.broadcast_to(x[:, None, :], (r, n, c)).reshape(r * n, c)


def _tile_major(pos_t, tile):
    k, t = pos_t.shape
    return pos_t.reshape(k, t // tile, tile).transpose(1, 0, 2).reshape(-1)


def kernel(x_prompt, x_sample, c_prompt, c_sample, state_ssm, state_ssd_conv, state_short_conv, w_ada, b_ada, g_pre_mix, g_post_mix, g_pre_ffn, g_post_ffn, w_in, ssd_conv_w, ssd_conv_b, dt_bias, a_log, d_skip, g_ssd_norm, sc_conv_w, g_sc_norm, w_out, w_router, router_bias, w_exp_gate, w_exp_up, w_exp_down, w_sh_gate, w_sh_up, w_sh_down):
    depth = w_ada.shape[0]
    bp, seq, d = x_prompt.shape
    bs, dseq, _ = x_sample.shape
    heads = dt_bias.shape[1]
    ssd_w = heads * SSD_HEAD_DIM
    conv_dim = ssd_conv_w.shape[2]
    sc_w = sc_conv_w.shape[2]
    ne = w_router.shape[2]
    tp, ts = bp * seq, bs * dseq
    t_all = tp + ts

    assert sc_w // SC_GROUPS == SSD_HEAD_DIM and heads == SC_GROUPS
    head_of = jnp.arange(ssd_w, dtype=I32) // SSD_HEAD_DIM
    e_ind = (jnp.arange(LANES, dtype=I32)[:, None] == head_of[None, :])

    xp = x_prompt.reshape(tp, d)
    xs_pad = jnp.pad(x_sample, ((0, 0), (0, SUBLANES - dseq), (0, 0))).reshape(bs * SUBLANES, d)
    xs_tok = x_sample.reshape(ts, d)
    outs = {k: [] for k in ("ssm_p", "cst_p", "scst_p", "ssm_s", "cst_s", "scst_s")}

    for layer in range(depth):
        cuts = np.cumsum([0, ssd_w, conv_dim, heads, sc_w, sc_w, sc_w]).tolist()
        wi = w_in[layer]
        seg = lambda k: wi[:, cuts[k]:cuts[k + 1]]
        w_in_r = jnp.concatenate([seg(0), seg(1), seg(3), seg(4), seg(5),
                                  jnp.pad(seg(2), ((0, 0), (0, LANES - heads)))], axis=1).astype(BF16)
        widths = (ssd_w, conv_dim, sc_w, sc_w, sc_w, LANES)
        pad_h = lambda v: jnp.pad(v.reshape(1, heads), ((0, 0), (0, LANES - heads)))
        wr = w_router[layer].T
        wr_hi = wr.astype(BF16)
        consts = dict(
            heads=heads,
            conv_w=ssd_conv_w[layer], conv_b=ssd_conv_b[layer].reshape(1, conv_dim),
            dt_bias=pad_h(dt_bias[layer]), a_log=pad_h(a_log[layer]),
            d_skip_e=jnp.broadcast_to(d_skip[layer][:, None], (heads, SSD_HEAD_DIM)).reshape(1, ssd_w),
            g_ssd=g_ssd_norm[layer].reshape(1, ssd_w), sc_w=sc_conv_w[layer],
            g_sc=g_sc_norm[layer].reshape(1, sc_w),
            e=e_ind.astype(BF16), et=e_ind.T.astype(F32),
            w_out=w_out[layer].astype(BF16), g_post_mix=g_post_mix[layer].reshape(1, d),
            g_pre_ffn=g_pre_ffn[layer].reshape(1, d),
            wr_hi_t=wr_hi, wr_lo_t=(wr - wr_hi.astype(F32)).astype(BF16),
            w_sh_gate=w_sh_gate[layer].astype(BF16), w_sh_up=w_sh_up[layer].astype(BF16),
            w_sh_down=w_sh_down[layer].astype(BF16), g_post_ffn=g_post_ffn[layer].reshape(1, d))

        c_all = jnp.concatenate([c_prompt, c_sample], axis=0)
        m_rows = -(-c_all.shape[0] // 16) * 16
        mod = _ada(jnp.pad(c_all, ((0, m_rows - c_all.shape[0]), (0, 0))), w_ada[layer], b_ada[layer])
        mod_p = mod[:bp]
        mod_s = mod[bp:bp + bs]
        mod_s_pad = _repeat_rows(mod_s[:, :2 * d], SUBLANES)
        mod_s_tok = _repeat_rows(mod_s, dseq)

        proj_s = _in_proj(xs_pad, mod_s_pad, g_pre_mix[layer], w_in_r, widths, F32, True, 1, TOK_TILE)
        ymix_p, ssm_p, cst_p, scst_p = _ssd(
            (xp, mod_p[:, :2 * d], g_pre_mix[layer].reshape(1, d), w_in_r), None, consts, bp, seq // CHUNK,
            CHUNK, CHUNK, CHUNK, BF16, widths=widths)
        init = (state_ssm[layer].reshape(bs, ssd_w, D_STATE), state_ssd_conv[layer], state_short_conv[layer])
        ymix_s, ssm_s, cst_s, scst_s = _ssd(proj_s, init, consts, bs, 1, SAMPLE_CHUNK, SUBLANES, dseq, F32)
        ymix_s = ymix_s.reshape(bs, SUBLANES, ssd_w + sc_w)[:, :dseq].reshape(ts, ssd_w + sc_w)

        merged = _out_proj(ymix_p, xp, mod_p[:, 2 * d:5 * d], consts, False, seq, TOK_TILE, t_all, 0, None)
        x1, u2, logits_t = _out_proj(ymix_s, xs_tok, mod_s_tok[:, 2 * d:5 * d], consts, True, 1, TOK_TILE,
                                     t_all, tp // TOK_TILE, merged)

        idx_t, rank_t, wtok, counts = _route(logits_t, router_bias[layer], TOK_TILE)
        counts = counts[:, 0]
        padded = (counts + ROW_TILE - 1) // ROW_TILE * ROW_TILE
        pad_end = jnp.cumsum(padded)
        pad_start = pad_end - padded
        n_tiles = -(-(t_all * TOP_K) // ROW_TILE) + ne
        n_used = (pad_end[-1] // ROW_TILE).astype(I32)
        tile_ids = jnp.minimum(jnp.arange(n_tiles, dtype=I32), n_used - 1)
        tile_expert = jnp.minimum(jnp.sum(pad_end[None, :] <= (tile_ids * ROW_TILE)[:, None], axis=1), ne - 1).astype(I32)
        first_pad = (pad_start + counts) // SUBLANES * SUBLANES
        units = (pad_end - first_pad) // SUBLANES
        tile_valid = jnp.clip(counts[tile_expert] - (tile_ids * ROW_TILE - pad_start[tile_expert]), 0,
                              ROW_TILE).astype(I32)
        pos_t, meta = _positions(idx_t, rank_t, pad_start.astype(I32), wtok, TOK_TILE)
        pos_flat = _tile_major(pos_t, GATHER_TILE)

        xs_rows = _zero_rows(first_pad.astype(I32), units.astype(I32), n_tiles * ROW_TILE, d + LANES)
        xs_rows = _dispatch(pos_flat, u2, meta, xs_rows, GATHER_TILE)
        g_rows = _experts(tile_expert, n_used.reshape(1), tile_valid, xs_rows, w_exp_gate[layer],
                          w_exp_up[layer], w_exp_down[layer], t_all)
        xp = _combine(g_rows, u2, x1, mod_p[:, 5 * d:], consts, False, seq, 0, tp, GATHER_TILE)
        xs_tok = _combine(g_rows, u2, x1, mod_s_tok[:, 5 * d:], consts, True, 1, tp // GATHER_TILE, ts,
                          GATHER_TILE)
        xs_pad = jnp.pad(xs_tok.reshape(bs, dseq, d), ((0, 0), (0, SUBLANES - dseq), (0, 0))).reshape(
            bs * SUBLANES, d)

        outs["ssm_p"].append(ssm_p.reshape(bp, heads, SSD_HEAD_DIM, D_STATE))
        outs["cst_p"].append(cst_p)
        outs["scst_p"].append(scst_p)
        outs["ssm_s"].append(ssm_s.reshape(bs, heads, SSD_HEAD_DIM, D_STATE))
        outs["cst_s"].append(cst_s)
        outs["scst_s"].append(scst_s)

    return (xp.reshape(bp, seq, d), xs_tok.reshape(bs, dseq, d),
            jnp.stack(outs["ssm_p"]), jnp.stack(outs["cst_p"]), jnp.stack(outs["scst_p"]),
            jnp.stack(outs["ssm_s"]), jnp.stack(outs["cst_s"]), jnp.stack(outs["scst_s"]))
```

```python
import functools

import jax
import jax.numpy as jnp
import numpy as np
from jax import lax
from jax.experimental import pallas as pl
from jax.experimental.pallas import tpu as pltpu

F32 = jnp.float32
BF16 = jnp.bfloat16
I32 = jnp.int32
U32 = jnp.uint32
HI = lax.Precision.HIGHEST

SSD_HEAD_DIM = 64
SSD_GROUPS = 2
D_STATE = 128
SSD_CONV_W = 4
SC_GROUPS = 16
SC_CONV_W = 3
TOP_K = 8
N_EXPERT_GROUPS = 8
TOPK_GROUPS = 4
ROUTED_SCALE = 2.5
EPS = 1e-6

LANES = 128
SUBLANES = 8
CHUNK = 128
SAMPLE_CHUNK = 16
SCAN_SUB = 2
TOK_TILE = 512
ROW_TILE = 576
GATHER_TILE = 256
VMEM_LIMIT = 56 * 1024 * 1024
DMA_THREADS = 2

NT = (((1,), (1,)), ((), ()))
TN = (((0,), (0,)), ((), ()))


def _sigmoid(x):
    return 1.0 / (1.0 + jnp.exp(-x))


def _silu(x):
    return x * _sigmoid(x)


def _softplus(x):
    return jnp.maximum(x, 0.0) + jnp.log1p(jnp.exp(-jnp.abs(x)))


def _rms(x, eps=EPS):
    return x * lax.rsqrt(jnp.mean(x * x, axis=-1, keepdims=True) + eps)


def _params(sem=None):
    return pltpu.CompilerParams(dimension_semantics=sem, vmem_limit_bytes=VMEM_LIMIT)


def _const_spec(shape, single=False):
    nd = len(shape)
    mode = dict(pipeline_mode=pl.Buffered(1)) if single else {}
    return pl.BlockSpec(shape, lambda *_: (0,) * nd, **mode)


def _ada_kernel(c_ref, w_ref, b_ref, o_ref):
    c = c_ref[...]
    s = _silu(c).astype(BF16)
    o_ref[...] = jnp.dot(s, w_ref[...].astype(BF16), preferred_element_type=F32) + b_ref[...]


def _ada(c, w_ada, b_ada):
    m, d = c.shape
    n = w_ada.shape[1]
    tn = 512
    return pl.pallas_call(
        _ada_kernel,
        out_shape=jax.ShapeDtypeStruct((m, n), F32),
        grid=(n // tn,),
        in_specs=[_const_spec((m, d)),
                  pl.BlockSpec((d, tn), lambda j: (0, j)),
                  pl.BlockSpec((1, tn), lambda j: (0, j))],
        out_specs=pl.BlockSpec((m, tn), lambda j: (0, j)),
        compiler_params=_params(("arbitrary",)),
        name="ada",
    )(c, w_ada, b_ada.reshape(1, n))


def _mod_rows(mod_ref, per_row, rows_per_batch, tile, d):
    n = mod_ref.shape[1] // d
    if per_row:
        return [mod_ref[:, k * d:(k + 1) * d] for k in range(n)]
    b = (pl.program_id(0) * tile) // rows_per_batch
    return [mod_ref[pl.ds(b, 1), k * d:(k + 1) * d] for k in range(n)]


def _mod_spec(mod, per_row, tile):
    if per_row:
        return pl.BlockSpec((tile, mod.shape[1]), lambda i: (i, 0))
    return _const_spec(mod.shape)


def _in_kernel(x_ref, mod_ref, g_ref, w_ref, *out_refs, per_row, rows_per_batch, widths):
    tile, d = x_ref.shape
    shift, scale = _mod_rows(mod_ref, per_row, rows_per_batch, tile, d)
    u = (_rms(x_ref[...]) * g_ref[...]) * (1.0 + scale) + shift
    u = u.astype(BF16)
    col = 0
    for ref, width in zip(out_refs, widths):
        for a in range(0, width, 512):
            bw = min(512, width - a)
            r = jnp.dot(u, w_ref[:, col + a:col + a + bw], preferred_element_type=F32)
            ref[:, a:a + bw] = r.astype(ref.dtype)
        col += width


def _in_proj(x, mod, g, w_bf16, widths, out_dtype, per_row, rows_per_batch, tile):
    t, d = x.shape
    n = w_bf16.shape[1]
    mod_spec = _mod_spec(mod, per_row, tile)
    dts = [out_dtype] * (len(widths) - 1) + [F32]
    return pl.pallas_call(
        functools.partial(_in_kernel, per_row=per_row, rows_per_batch=rows_per_batch, widths=widths),
        out_shape=[jax.ShapeDtypeStruct((t, wd), dt) for wd, dt in zip(widths, dts)],
        grid=(t // tile,),
        in_specs=[pl.BlockSpec((tile, d), lambda i: (i, 0)), mod_spec,
                  _const_spec((1, d)), _const_spec((d, n), single=True)],
        out_specs=[pl.BlockSpec((tile, wd), lambda i: (i, 0)) for wd in widths],
        compiler_params=_params(("arbitrary",)),
        name="in_proj",
    )(x, mod, g.reshape(1, d), w_bf16)


def _ssd_kernel(*refs, q, rows_in, q_valid, has_init, heads, widths, n_sub):
    it = iter(refs)
    if widths is None:
        z_ref, xbc_ref, scb_ref, scc_ref, sch_ref, dt_ref = (next(it) for _ in range(6))
    else:
        x_ref, mod_ref, gpre_ref, win_ref = (next(it) for _ in range(4))
    if has_init:
        ssm0_ref, cst0_ref, scst0_ref = (next(it) for _ in range(3))
    (cw_ref, cb_ref, dtb_ref, alog_ref, dsk_ref, gssd_ref, scw_ref, gsc_ref,
     tril_ref, e_ref, et_ref) = (next(it) for _ in range(11))
    ymix_ref, ssm_ref, cst_ref, scst_ref = (next(it) for _ in range(4))
    h_scr, ext_scr, extv_scr = (next(it) for _ in range(3))

    c = pl.program_id(1)
    nc = pl.num_programs(1)
    ssd_w = dsk_ref.shape[1]
    gw = ssd_w // SSD_GROUPS
    hpg = heads // SSD_GROUPS
    n_state = D_STATE
    head0 = SUBLANES - (SSD_CONV_W - 1)
    headv = SUBLANES - (SC_CONV_W - 1)

    @pl.when(c == 0)
    def _():
        if has_init:
            h_scr[...] = ssm0_ref[0]
            ext_scr[head0:SUBLANES, :] = cst0_ref[0]
            extv_scr[headv:SUBLANES, :] = scst0_ref[0]
        else:
            h_scr[...] = jnp.zeros(h_scr.shape, F32)
            ext_scr[0:SUBLANES, :] = jnp.zeros((SUBLANES, ext_scr.shape[1]), F32)
            extv_scr[0:SUBLANES, :] = jnp.zeros((SUBLANES, extv_scr.shape[1]), F32)

    def pad_rows(v, n):
        if v.shape[0] == n:
            return v
        return jnp.concatenate([v, jnp.zeros((n - v.shape[0], v.shape[1]), v.dtype)], axis=0)

    load = lambda ref: pad_rows(ref[...].astype(F32), q)
    pad_t = lambda v: pad_rows(v, LANES)
    if widths is None:
        z, xbc, scb, scc, sch, dt_raw = (load(r) for r in (z_ref, xbc_ref, scb_ref, scc_ref, sch_ref, dt_ref))
    else:
        d_model = x_ref.shape[1]
        b = pl.program_id(0)
        shift = mod_ref[pl.ds(b, 1), 0:d_model]
        scale = mod_ref[pl.ds(b, 1), d_model:2 * d_model]
        u = ((_rms(x_ref[...]) * gpre_ref[...]) * (1.0 + scale) + shift).astype(BF16)
        pieces, col = [], 0
        for width in widths:
            parts = [jnp.dot(u, win_ref[:, col + a:col + a + min(512, width - a)], preferred_element_type=F32)
                     for a in range(0, width, 512)]
            pieces.append(parts[0] if len(parts) == 1 else jnp.concatenate(parts, axis=1))
            col += width
        z, xbc, scb, scc, sch = (p.astype(BF16).astype(F32) for p in pieces[:5])
        dt_raw = pieces[5]

    def scan_chunk(j, z, xbc, scb, scc, sch, dt_raw):
        ext_scr[SUBLANES:SUBLANES + q, :] = xbc
        conv = cb_ref[...]
        for k in range(SSD_CONV_W):
            conv = conv + cw_ref[k:k + 1, :] * ext_scr[head0 + k:head0 + k + q, :]
        xc = _silu(conv)
        xs = xc[:, :ssd_w]
        bm = xc[:, ssd_w:ssd_w + SSD_GROUPS * n_state]
        cm = xc[:, ssd_w + SSD_GROUPS * n_state:]

        lane = lax.broadcasted_iota(I32, (q, LANES), 1)
        row = lax.broadcasted_iota(I32, (q, LANES), 0)
        dt = _softplus(dt_raw + dtb_ref[...])
        dt = jnp.where(jnp.logical_and(lane < heads, row < q_valid), dt, 0.0)
        a = dt * (-jnp.exp(alog_ref[...]))
        acum = jnp.dot(tril_ref[...], pad_t(a), precision=HI, preferred_element_type=F32)
        acum_t = pad_t(acum).T
        a_last = acum[q - 1:q, :]
        e = e_ref[...]
        expand = lambda v: jnp.dot(v.astype(BF16), e, preferred_element_type=F32)
        dt_e = expand(dt)
        dtdte_e = expand(dt * jnp.exp(a_last - acum))
        exa_e = expand(jnp.exp(acum))
        xdt = xs * dt_e
        xw_b = (xs * dtdte_e).astype(BF16)
        dlast = jnp.broadcast_to(jnp.exp(acum_t[:, q - 1:q]), (LANES, n_state))
        dcol = jnp.dot(et_ref[...], dlast, precision=HI, preferred_element_type=F32)

        tri = row >= lane
        xdt_t = pad_t(xdt)
        xw_t = pad_t(xw_b)
        lane_t = lax.broadcasted_iota(I32, (LANES, LANES), 1)
        y_groups = []
        for g in range(SSD_GROUPS):
            bm_g = bm[:, g * n_state:(g + 1) * n_state].astype(BF16)
            cm_g = cm[:, g * n_state:(g + 1) * n_state].astype(BF16)
            bm_t = pad_t(bm_g)
            cb = lax.dot_general(cm_g, bm_t, NT, preferred_element_type=F32)
            h_g = h_scr[g * gw:(g + 1) * gw, :]
            y_off = lax.dot_general(cm_g, h_g.astype(BF16), NT, preferred_element_type=F32)
            parts = []
            for pair in range(hpg // 2):
                lo = (g * hpg + 2 * pair) * SSD_HEAD_DIM
                x_pair = xdt_t[:, lo:lo + LANES]
                acc = None
                for half in range(2):
                    h = g * hpg + 2 * pair + half
                    ci = jnp.broadcast_to(acum[:, h:h + 1], (q, LANES))
                    rj = jnp.broadcast_to(acum_t[h:h + 1, :], (q, LANES))
                    dec = jnp.where(tri, jnp.exp(ci - rj), 0.0)
                    m = (cb * dec).astype(BF16)
                    own = (lane_t >= SSD_HEAD_DIM) if half else (lane_t < SSD_HEAD_DIM)
                    y_h = jnp.dot(m, jnp.where(own, x_pair, 0.0).astype(BF16), preferred_element_type=F32)
                    acc = y_h if acc is None else acc + y_h
                parts.append(acc)
            y_diag = jnp.concatenate(parts, axis=1)
            y_groups.append(y_diag + y_off * exa_e[:, g * gw:(g + 1) * gw])
            upd = lax.dot_general(xw_t[:, g * gw:(g + 1) * gw], bm_t, TN, preferred_element_type=F32)
            h_scr[g * gw:(g + 1) * gw, :] = h_g * dcol[g * gw:(g + 1) * gw, :] + upd

        y = (jnp.concatenate(y_groups, axis=1) + xs * dsk_ref[...]) * _silu(z)
        y = jnp.concatenate([_rms(y[:, g * gw:(g + 1) * gw]) for g in range(SSD_GROUPS)], axis=1)
        y_ssd = y * gssd_ref[...]

        extv_scr[SUBLANES:SUBLANES + q, :] = scc * sch
        cv = scw_ref[0:1, :] * extv_scr[headv:headv + q, :]
        for k in range(1, SC_CONV_W):
            cv = cv + scw_ref[k:k + 1, :] * extv_scr[headv + k:headv + k + q, :]
        t = scb * cv
        sc_per_group = t.shape[1] // SC_GROUPS
        gsum = jnp.dot((t * t).astype(BF16), et_ref[...].astype(BF16), preferred_element_type=F32)
        rs = lax.rsqrt(gsum * (1.0 / sc_per_group) + EPS)
        y_sc = t * expand(rs) * gsc_ref[...]

        out_rows = slice(j * rows_in, (j + 1) * rows_in)
        ymix_ref[out_rows, :ssd_w] = y_ssd[0:rows_in].astype(ymix_ref.dtype)
        ymix_ref[out_rows, ssd_w:] = y_sc[0:rows_in].astype(ymix_ref.dtype)

        if j == n_sub - 1:
            @pl.when(c == nc - 1)
            def _():
                ssm_ref[0] = h_scr[...]
                cst_ref[0] = ext_scr[SUBLANES + q_valid - (SSD_CONV_W - 1):SUBLANES + q_valid, :]
                scst_ref[0] = extv_scr[SUBLANES + q_valid - (SC_CONV_W - 1):SUBLANES + q_valid, :]

        ext_scr[0:SUBLANES, :] = ext_scr[q:q + SUBLANES, :]
        extv_scr[0:SUBLANES, :] = extv_scr[q:q + SUBLANES, :]

    for j in range(n_sub):
        scan_chunk(j, *(p[j * q:(j + 1) * q] for p in (z, xbc, scb, scc, sch, dt_raw)))


def _ssd(proj, init, consts, nb, n_chunks, q, rows_in, q_valid, ymix_dtype, widths=None, n_sub=1):
    heads = consts["heads"]
    tril = jnp.tril(jnp.ones((q, LANES), F32))
    ssd_w, conv_dim, sc_w = consts["g_ssd"].shape[1], consts["conv_w"].shape[1], consts["g_sc"].shape[1]
    has_init = init is not None
    assert n_chunks % n_sub == 0 and (n_sub == 1 or rows_in == q)
    nc = n_chunks // n_sub
    row_spec = lambda w: pl.BlockSpec((n_sub * rows_in, w), lambda b, c: (b * nc + c, 0))
    if widths is None:
        in_specs = [row_spec(ssd_w), row_spec(conv_dim), row_spec(sc_w), row_spec(sc_w), row_spec(sc_w),
                    row_spec(LANES)]
    else:
        x, mod, g, w = proj
        in_specs = [row_spec(x.shape[1]), _const_spec(mod.shape), _const_spec(g.shape),
                    _const_spec(w.shape, single=True)]
    args = list(proj)
    if has_init:
        ssm0, cst0, scst0 = init
        in_specs += [pl.BlockSpec((1,) + ssm0.shape[1:], lambda b, c: (b, 0, 0)),
                     pl.BlockSpec((1,) + cst0.shape[1:], lambda b, c: (b, 0, 0)),
                     pl.BlockSpec((1,) + scst0.shape[1:], lambda b, c: (b, 0, 0))]
        args += [ssm0, cst0, scst0]
    weights = [consts[k] for k in ("conv_w", "conv_b", "dt_bias", "a_log", "d_skip_e", "g_ssd",
                                   "sc_w", "g_sc")] + [tril, consts["e"], consts["et"]]
    in_specs += [_const_spec(w.shape) for w in weights]
    args += weights
    n_state = D_STATE
    out_shape = [jax.ShapeDtypeStruct((nb * n_chunks * rows_in, ssd_w + sc_w), ymix_dtype),
                 jax.ShapeDtypeStruct((nb, ssd_w, n_state), F32),
                 jax.ShapeDtypeStruct((nb, SSD_CONV_W - 1, conv_dim), F32),
                 jax.ShapeDtypeStruct((nb, SC_CONV_W - 1, sc_w), F32)]
    out_specs = [pl.BlockSpec((n_sub * rows_in, ssd_w + sc_w), lambda b, c: (b * nc + c, 0)),
                 pl.BlockSpec((1, ssd_w, n_state), lambda b, c: (b, 0, 0)),
                 pl.BlockSpec((1, SSD_CONV_W - 1, conv_dim), lambda b, c: (b, 0, 0)),
                 pl.BlockSpec((1, SC_CONV_W - 1, sc_w), lambda b, c: (b, 0, 0))]
    scratch = [pltpu.VMEM((ssd_w, n_state), F32),
               pltpu.VMEM((q + SUBLANES, conv_dim), F32),
               pltpu.VMEM((q + SUBLANES, sc_w), F32)]
    return pl.pallas_call(
        functools.partial(_ssd_kernel, q=q, rows_in=rows_in, q_valid=q_valid, has_init=has_init, heads=heads,
                          widths=widths, n_sub=n_sub),
        out_shape=out_shape, grid=(nb, nc), in_specs=in_specs, out_specs=out_specs,
        scratch_shapes=scratch,
        compiler_params=_params(("arbitrary", "arbitrary")),
        name="ssd",
    )(*args)


def _out_kernel(ymix_ref, x_ref, mod_ref, wout_ref, gpost_ref, gpre_ref, wr_hi_ref, wr_lo_ref,
                *rest, per_row, rows_per_batch, has_alias):
    x1_ref, u2_ref, lg_ref = rest[3:6] if has_alias else rest[0:3]
    tile, d = x_ref.shape
    gate1, shift2, scale2 = _mod_rows(mod_ref, per_row, rows_per_batch, tile, d)
    m = jnp.dot(ymix_ref[...].astype(BF16), wout_ref[...], preferred_element_type=F32)
    x1 = x_ref[...] + gate1 * (_rms(m) * gpost_ref[...])
    u2 = (_rms(x1) * gpre_ref[...]) * (1.0 + scale2) + shift2
    x1_ref[...] = x1
    u2_ref[...] = u2
    u_hi = u2.astype(BF16)
    u_lo = (u2 - u_hi.astype(F32)).astype(BF16)
    lg = lax.dot_general(wr_hi_ref[...], u_hi, NT, preferred_element_type=F32)
    lg = lg + lax.dot_general(wr_hi_ref[...], u_lo, NT, preferred_element_type=F32)
    lg = lg + lax.dot_general(wr_lo_ref[...], u_hi, NT, preferred_element_type=F32)
    lg_ref[...] = lg


def _out_proj(ymix, x, mod, consts, per_row, rows_per_batch, tile, t_total, tile_off, prev):
    t, d = x.shape
    dm = ymix.shape[1]
    ne = consts["wr_hi_t"].shape[0]
    in_specs = [pl.BlockSpec((tile, dm), lambda i: (i, 0)),
                pl.BlockSpec((tile, d), lambda i: (i, 0)), _mod_spec(mod, per_row, tile),
                _const_spec((dm, d)), _const_spec((1, d)), _const_spec((1, d)),
                _const_spec((ne, d)), _const_spec((ne, d))]
    args = [ymix, x, mod, consts["w_out"], consts["g_post_mix"], consts["g_pre_ffn"],
            consts["wr_hi_t"], consts["wr_lo_t"]]
    aliases = {}
    if prev is not None:
        in_specs += [pl.BlockSpec(memory_space=pl.ANY)] * 3
        aliases = {len(args) + k: k for k in range(3)}
        args += list(prev)
    return pl.pallas_call(
        functools.partial(_out_kernel, per_row=per_row, rows_per_batch=rows_per_batch,
                          has_alias=prev is not None),
        out_shape=[jax.ShapeDtypeStruct((t_total, d), F32), jax.ShapeDtypeStruct((t_total, d), F32),
                   jax.ShapeDtypeStruct((ne, t_total), F32)],
        grid=(t // tile,), in_specs=in_specs,
        out_specs=[pl.BlockSpec((tile, d), lambda i: (i + tile_off, 0)),
                   pl.BlockSpec((tile, d), lambda i: (i + tile_off, 0)),
                   pl.BlockSpec((ne, tile), lambda i: (0, i + tile_off))],
        input_output_aliases=aliases,
        compiler_params=_params(("arbitrary",)),
        name="out_proj",
    )(*args)


def _route_kernel(lg_ref, bias_ref, upper_ref, idx_ref, rank_ref, wtok_ref, cnt_ref, carry_scr):
    i = pl.program_id(0)
    ne, tm = lg_ref.shape
    per_group = ne // N_EXPERT_GROUPS
    neg = -jnp.inf

    @pl.when(i == 0)
    def _():
        carry_scr[...] = jnp.zeros(carry_scr.shape, F32)

    s = _sigmoid(lg_ref[...])
    biased = s + bias_ref[...]
    gl = []
    io_g = lax.broadcasted_iota(I32, (per_group, tm), 0).astype(F32)
    for g in range(N_EXPERT_GROUPS):
        blk = biased[g * per_group:(g + 1) * per_group, :]
        m1 = jnp.max(blk, axis=0, keepdims=True)
        f1 = jnp.min(jnp.where(blk == m1, io_g, float(per_group)), axis=0, keepdims=True)
        m2 = jnp.max(jnp.where(io_g == f1, neg, blk), axis=0, keepdims=True)
        gl.append(m1 + m2)
    gscore = jnp.concatenate(gl, axis=0)
    io8 = lax.broadcasted_iota(I32, (N_EXPERT_GROUPS, tm), 0).astype(F32)
    gsel = jnp.zeros((N_EXPERT_GROUPS, tm), F32)
    for _ in range(TOPK_GROUPS):
        m = jnp.max(gscore, axis=0, keepdims=True)
        f = jnp.min(jnp.where(gscore == m, io8, float(N_EXPERT_GROUPS)), axis=0, keepdims=True)
        hit = io8 == f
        gsel = jnp.where(hit, 1.0, gsel)
        gscore = jnp.where(hit, neg, gscore)
    emask = jnp.concatenate(
        [jnp.broadcast_to(gsel[g:g + 1, :], (per_group, tm)) for g in range(N_EXPERT_GROUPS)], axis=0)
    cand = jnp.where(emask > 0.5, biased, neg)
    io_e = lax.broadcasted_iota(I32, (ne, tm), 0).astype(F32)
    msel = jnp.zeros((ne, tm), F32)
    idxs, wts = [], []
    for _ in range(TOP_K):
        m = jnp.max(cand, axis=0, keepdims=True)
        f = jnp.min(jnp.where(cand == m, io_e, float(ne)), axis=0, keepdims=True)
        hit = io_e == f
        wts.append(jnp.sum(jnp.where(hit, s, 0.0), axis=0, keepdims=True))
        idxs.append(f)
        msel = jnp.where(hit, 1.0, msel)
        cand = jnp.where(hit, neg, cand)
    pref = jnp.dot(msel.astype(BF16), upper_ref[...], preferred_element_type=F32) + carry_scr[:, 0:1]
    ranks = [jnp.sum(jnp.where(io_e == f, pref, 0.0), axis=0, keepdims=True) for f in idxs]
    carry_scr[...] = carry_scr[...] + jnp.sum(msel, axis=1, keepdims=True)
    cnt_ref[...] = carry_scr[...].astype(I32)
    idx_ref[...] = jnp.concatenate(idxs, axis=0).astype(I32)
    rank_ref[...] = jnp.concatenate(ranks, axis=0).astype(I32)
    wsum = wts[0]
    for w in wts[1:]:
        wsum = wsum + w
    wn = jnp.concatenate([w / wsum * ROUTED_SCALE for w in wts]
                         + [jnp.zeros((LANES - TOP_K, tm), F32)], axis=0)
    for j in range(tm // LANES):
        wtok_ref[j * LANES:(j + 1) * LANES, :] = wn[:, j * LANES:(j + 1) * LANES].T


def _route(logits_t, bias, tile):
    ne, t = logits_t.shape
    upper = jnp.triu(jnp.ones((tile, tile), F32), 1).astype(BF16)
    return pl.pallas_call(
        _route_kernel,
        out_shape=[jax.ShapeDtypeStruct((TOP_K, t), I32), jax.ShapeDtypeStruct((TOP_K, t), I32),
                   jax.ShapeDtypeStruct((t, LANES), F32), jax.ShapeDtypeStruct((ne, LANES), I32)],
        grid=(t // tile,),
        in_specs=[pl.BlockSpec((ne, tile), lambda i: (0, i)), _const_spec((ne, 1)),
                  _const_spec((tile, tile))],
        out_specs=[pl.BlockSpec((TOP_K, tile), lambda i: (0, i)),
                   pl.BlockSpec((TOP_K, tile), lambda i: (0, i)),
                   pl.BlockSpec((tile, LANES), lambda i: (i, 0)),
                   _const_spec((ne, LANES))],
        scratch_shapes=[pltpu.VMEM((ne, LANES), F32)],
        compiler_params=_params(("arbitrary",)),
        name="route",
    )(logits_t, bias.reshape(ne, 1), upper)


META_POS, META_W, META_TOK = 0, TOP_K, 2 * TOP_K


def _pos_kernel(idx_ref, rank_ref, start_ref, wtok_ref, pos_ref, meta_ref):
    ne = start_ref.shape[0]
    tm = idx_ref.shape[1]
    io_e = lax.broadcasted_iota(I32, (ne, tm), 0)
    start = start_ref[...].astype(F32)
    rows = []
    for k in range(TOP_K):
        hit = io_e == idx_ref[k:k + 1, :]
        rows.append(jnp.sum(jnp.where(hit, start, 0.0), axis=0, keepdims=True))
    pos_f = jnp.concatenate(rows, axis=0) + rank_ref[...].astype(F32)
    pos_ref[...] = pos_f.astype(I32)
    pos_pad = jnp.concatenate([pos_f, jnp.zeros((LANES - TOP_K, tm), F32)], axis=0)
    lane = lax.broadcasted_iota(I32, (LANES, LANES), 1)
    row = lax.broadcasted_iota(I32, (LANES, LANES), 0)
    for j in range(tm // LANES):
        pos_tok = pos_pad[:, j * LANES:(j + 1) * LANES].T
        w_tok = pltpu.roll(wtok_ref[j * LANES:(j + 1) * LANES, :], META_W, axis=1)
        tok = (row + (pl.program_id(0) * tm + j * LANES)).astype(F32)
        tag = jnp.where(lane < META_W, pos_tok, jnp.where(lane == META_TOK, tok, w_tok))
        meta_ref[j * LANES:(j + 1) * LANES, :] = tag


def _positions(idx_t, rank_t, pad_start, wtok, tile):
    k, t = idx_t.shape
    ne = pad_start.shape[0]
    return pl.pallas_call(
        _pos_kernel,
        out_shape=[jax.ShapeDtypeStruct((k, t), I32), jax.ShapeDtypeStruct((t, LANES), F32)],
        grid=(t // tile,),
        in_specs=[pl.BlockSpec((k, tile), lambda i: (0, i)), pl.BlockSpec((k, tile), lambda i: (0, i)),
                  _const_spec((ne, 1)), pl.BlockSpec((tile, LANES), lambda i: (i, 0))],
        out_specs=[pl.BlockSpec((k, tile), lambda i: (0, i)), pl.BlockSpec((tile, LANES), lambda i: (i, 0))],
        compiler_params=_params(("arbitrary",)),
        name="positions",
    )(idx_t, rank_t, pad_start.reshape(ne, 1), wtok)


ZERO_BITS = (64, 32, 16, 8, 4, 2, 1)
assert sum(ZERO_BITS) * SUBLANES >= ROW_TILE


def _zero_copy(zbuf, xs_ref, sem, off, bit):
    rows = bit * SUBLANES
    return pltpu.make_async_copy(zbuf.at[pl.ds(0, rows)], xs_ref.at[pl.ds(pl.multiple_of(off, SUBLANES), rows)], sem)


def _zero_kernel(first_ref, units_ref, xs_ref, zbuf, sem):
    zbuf[...] = jnp.zeros(zbuf.shape, zbuf.dtype)
    ne = first_ref.shape[0]

    def each(e, start):
        off = first_ref[e]
        units = units_ref[e]
        for bit in ZERO_BITS:
            on = (units & bit) != 0

            @pl.when(on)
            def _():
                cp = _zero_copy(zbuf, xs_ref, sem, off, bit)
                cp.start() if start else cp.wait()
            off = off + jnp.where(on, bit * SUBLANES, 0)
        return start

    lax.fori_loop(0, ne, lambda e, c: (each(e, True), c)[1], 0)
    lax.fori_loop(0, ne, lambda e, c: (each(e, False), c)[1], 0)


def _zero_rows(first, units, n_rows, width):
    return pl.pallas_call(
        _zero_kernel,
        out_shape=jax.ShapeDtypeStruct((n_rows, width), F32),
        in_specs=[pl.BlockSpec(memory_space=pltpu.SMEM), pl.BlockSpec(memory_space=pltpu.SMEM)],
        out_specs=pl.BlockSpec(memory_space=pl.ANY),
        scratch_shapes=[pltpu.VMEM((ZERO_BITS[0] * SUBLANES, width), F32), pltpu.SemaphoreType.DMA],
        compiler_params=_params(),
        name="zero_rows",
    )(first, units)


def _row_copy(src_ref, dst_ref, sem, s, d):
    return pltpu.make_async_copy(src_ref.at[pl.ds(s, 1)], dst_ref.at[pl.ds(d, 1)], sem)


def _dispatch_kernel(pos_hbm, u_ref, meta_ref, xs_ref, pos_a, pos_b, src, psem, sem, *, nt):
    i = pl.program_id(0)
    tm, d = u_ref.shape
    per = pos_a.shape[0]
    pos_slots = (pos_a, pos_b)

    def pos_copy(tile, s):
        return pltpu.make_async_copy(pos_hbm.at[pl.ds(pl.multiple_of(tile * per, per), per)], pos_slots[s],
                                     psem.at[s])

    def drain(s):
        for _ in range(TOP_K):
            pltpu.make_async_copy(src.at[s], xs_ref.at[pl.ds(0, tm)], sem.at[s]).wait()

    def step(s):
        @pl.when(i == 0)
        def _():
            pos_copy(0, 0).start()

        @pl.when(i + 1 < nt)
        def _():
            pos_copy(i + 1, 1 - s).start()

        @pl.when(i >= 2)
        def _():
            drain(s)

        src[s, :, :d] = u_ref[...]
        src[s, :, d:] = meta_ref[...]
        pos_copy(i, s).wait()

        def issue(t, carry):
            for k in range(TOP_K):
                _row_copy(src.at[s], xs_ref, sem.at[s], t, pos_slots[s][k * tm + t]).start(
                    priority=k % DMA_THREADS)
            return carry

        lax.fori_loop(0, tm, issue, 0)

        @pl.when(i == nt - 1)
        def _():
            if nt >= 2:
                drain(1 - s)
            drain(s)

    for s in range(2):
        pl.when(i % 2 == s)(functools.partial(step, s))


def _dispatch(pos_flat, u2, meta, n_rows, tm):
    per = TOP_K * tm
    nt = pos_flat.shape[0] // per
    d = u2.shape[1]
    return pl.pallas_call(
        functools.partial(_dispatch_kernel, nt=nt),
        out_shape=jax.ShapeDtypeStruct((n_rows, d + LANES), F32),
        grid=(nt,),
        in_specs=[pl.BlockSpec(memory_space=pl.ANY), pl.BlockSpec((tm, d), lambda i: (i, 0)),
                  pl.BlockSpec((tm, LANES), lambda i: (i, 0))],
        out_specs=pl.BlockSpec(memory_space=pl.ANY),
        scratch_shapes=[pltpu.SMEM((per,), I32), pltpu.SMEM((per,), I32),
                        pltpu.VMEM((2, tm, d + LANES), F32),
                        pltpu.SemaphoreType.DMA((2,)), pltpu.SemaphoreType.DMA((2,))],
        compiler_params=_params(("arbitrary",)),
        name="dispatch",
    )(pos_flat, u2, meta)


WAIT_BITS = tuple(1 << b for b in reversed(range(ROW_TILE.bit_length())))


def _expert_kernel(te_ref, nu_ref, nv_ref, xs_ref, wg_ref, wu_ref, wd_ref, g_hbm,
                   ybuf, dst_vmem, dst_smem, sem, dsem, *, n_tok):
    del te_ref
    i = pl.program_id(0)
    nu = nu_ref[0]
    d = ybuf.shape[3]
    rows = ybuf.shape[1] * SUBLANES

    def drain(tile):
        slot = tile % 2
        n = nv_ref[tile]
        for bit in WAIT_BITS:
            @pl.when((n & bit) != 0)
            def _():
                pltpu.make_async_copy(g_hbm.at[pl.ds(0, bit)], g_hbm.at[pl.ds(0, bit)], sem.at[slot]).wait()

    @pl.when(i < nu)
    def _():
        slot = i % 2

        @pl.when(i >= 2)
        def _():
            drain(i - 2)

        dst_vmem[...] = jnp.zeros(dst_vmem.shape, I32)
        xfull = xs_ref[...]
        tag = xfull[:, d:]
        x = xfull[:, :d].astype(BF16)
        lane = lax.broadcasted_iota(I32, (rows, LANES), 1)
        p = (lax.broadcasted_iota(I32, (rows, LANES), 0) + i * rows).astype(F32)
        hit = jnp.logical_and(tag == p, lane < META_W)
        w_at = pltpu.roll(tag, LANES - META_W, axis=1)
        w = jnp.sum(jnp.where(hit, w_at, 0.0), axis=1, keepdims=True)
        kf = jnp.sum(jnp.where(hit, lane.astype(F32), 0.0), axis=1, keepdims=True)
        dest = kf * float(n_tok) + tag[:, META_TOK:META_TOK + 1]

        dest_b = jnp.broadcast_to(dest, (rows, LANES))
        n_blk = -(-rows // LANES)
        dest_b = jnp.concatenate([dest_b, jnp.zeros((n_blk * LANES - rows, LANES), F32)], axis=0)
        for b in range(n_blk):
            dst_vmem[b:b + 1, :] = dest_b[b * LANES:(b + 1) * LANES, :].T[0:1, :].astype(I32)
        to_smem = [pltpu.make_async_copy(dst_vmem.at[b], dst_smem.at[pl.ds(b * LANES, LANES)], dsem)
                   for b in range(n_blk)]
        for cp in to_smem:
            cp.start()

        g = jnp.dot(x, wg_ref[0].astype(BF16), preferred_element_type=F32)
        u = jnp.dot(x, wu_ref[0].astype(BF16), preferred_element_type=F32)
        h = (_silu(g) * u).astype(BF16)
        y = jnp.dot(h, wd_ref[0].astype(BF16), preferred_element_type=F32) * w
        ybuf[slot] = y.reshape(rows // SUBLANES, SUBLANES, d)
        for cp in to_smem:
            cp.wait()

        nv = nv_ref[i]
        full = lax.shift_right_logical(nv, 3)

        def scatter_rows(s):
            def send(r8, j, prio):
                to = dst_smem[r8 * SUBLANES + j]
                pltpu.make_async_copy(ybuf.at[s, r8, pl.ds(j, 1)], g_hbm.at[pl.ds(to, 1)],
                                      sem.at[s]).start(priority=prio)

            def issue8(r8, carry):
                for j in range(SUBLANES):
                    send(r8, j, j % DMA_THREADS)
                return carry

            def issue1(r, carry):
                send(lax.shift_right_logical(r, 3), r & (SUBLANES - 1), 0)
                return carry

            lax.fori_loop(0, full, issue8, 0)
            lax.fori_loop(full * SUBLANES, nv, issue1, 0)

        for s in range(2):
            pl.when(slot == s)(functools.partial(scatter_rows, s))

    @pl.when(i == pl.num_programs(0) - 1)
    def _():
        @pl.when(nu >= 2)
        def _():
            drain(nu - 2)

        @pl.when(nu >= 1)
        def _():
            drain(nu - 1)


def _experts(tile_expert, n_used, tile_valid, xs, w_gate, w_up, w_down, n_tok):
    n_rows, width = xs.shape
    d, de = w_gate.shape[1:]
    n_tiles = n_rows // ROW_TILE
    row_map = lambda i, te, nu, nv: (jnp.minimum(i, nu[0] - 1), 0)
    w_map = lambda i, te, nu, nv: (te[i], 0, 0)
    n_blk = -(-ROW_TILE // LANES)
    return pl.pallas_call(
        functools.partial(_expert_kernel, n_tok=n_tok),
        out_shape=jax.ShapeDtypeStruct((TOP_K * n_tok, d), F32),
        grid_spec=pltpu.PrefetchScalarGridSpec(
            num_scalar_prefetch=3, grid=(n_tiles,),
            in_specs=[pl.BlockSpec((ROW_TILE, width), row_map),
                      pl.BlockSpec((1, d, de), w_map), pl.BlockSpec((1, d, de), w_map),
                      pl.BlockSpec((1, de, d), w_map)],
            out_specs=pl.BlockSpec(memory_space=pl.ANY),
            scratch_shapes=[pltpu.VMEM((2, ROW_TILE // SUBLANES, SUBLANES, d), F32),
                            pltpu.VMEM((SUBLANES, LANES), I32),
                            pltpu.SMEM((n_blk * LANES,), I32), pltpu.SemaphoreType.DMA((2,)),
                            pltpu.SemaphoreType.DMA]),
        compiler_params=_params(("arbitrary",)),
        name="experts",
    )(tile_expert, n_used, tile_valid, xs, w_gate, w_up, w_down)


def _combine_kernel(g_ref, u_ref, x1_ref, mod_ref, wsg_ref, wsu_ref, wsd_ref, gpost_ref, o_ref,
                    *, per_row, rows_per_batch):
    tm, d = u_ref.shape
    ub = u_ref[...].astype(BF16)
    hs = _silu(jnp.dot(ub, wsg_ref[...], preferred_element_type=F32)) * jnp.dot(
        ub, wsu_ref[...], preferred_element_type=F32)
    f = jnp.dot(hs.astype(BF16), wsd_ref[...], preferred_element_type=F32)
    routed = g_ref[0]
    for k in range(1, TOP_K):
        routed = routed + g_ref[k]
    f = routed + f
    (gate2,) = _mod_rows(mod_ref, per_row, rows_per_batch, tm, d)
    o_ref[...] = x1_ref[...] + gate2 * (_rms(f) * gpost_ref[...])


def _combine(g_rows, u2, x1, mod, consts, per_row, rows_per_batch, tile_off, n_tok, tm):
    t_all, d = u2.shape
    ds_ = consts["w_sh_gate"].shape[1]
    tok_spec = lambda w: pl.BlockSpec((tm, w), lambda i: (i + tile_off, 0))
    return pl.pallas_call(
        functools.partial(_combine_kernel, per_row=per_row, rows_per_batch=rows_per_batch),
        out_shape=jax.ShapeDtypeStruct((n_tok, d), F32),
        grid=(n_tok // tm,),
        in_specs=[pl.BlockSpec((TOP_K, tm, d), lambda i: (0, i + tile_off, 0)),
                  tok_spec(d), tok_spec(d), _mod_spec(mod, per_row, tm),
                  _const_spec((d, ds_)), _const_spec((d, ds_)), _const_spec((ds_, d)), _const_spec((1, d))],
        out_specs=pl.BlockSpec((tm, d), lambda i: (i, 0)),
        compiler_params=_params(("arbitrary",)),
        name="combine",
    )(g_rows.reshape(TOP_K, t_all, d), u2, x1, mod, consts["w_sh_gate"], consts["w_sh_up"],
      consts["w_sh_down"], consts["g_post_ffn"])


def _repeat_rows(x, n):
    r, c = x.shape
    return jnp.broadcast_to(x[:, None, :], (r, n, c)).reshape(r * n, c)


def _tile_major(pos_t, tile):
    k, t = pos_t.shape
    return pos_t.reshape(k, t // tile, tile).transpose(1, 0, 2).reshape(-1)


def kernel(x_prompt, x_sample, c_prompt, c_sample, state_ssm, state_ssd_conv, state_short_conv, w_ada, b_ada, g_pre_mix, g_post_mix, g_pre_ffn, g_post_ffn, w_in, ssd_conv_w, ssd_conv_b, dt_bias, a_log, d_skip, g_ssd_norm, sc_conv_w, g_sc_norm, w_out, w_router, router_bias, w_exp_gate, w_exp_up, w_exp_down, w_sh_gate, w_sh_up, w_sh_down):
    depth = w_ada.shape[0]
    bp, seq, d = x_prompt.shape
    bs, dseq, _ = x_sample.shape
    heads = dt_bias.shape[1]
    ssd_w = heads * SSD_HEAD_DIM
    conv_dim = ssd_conv_w.shape[2]
    sc_w = sc_conv_w.shape[2]
    ne = w_router.shape[2]
    tp, ts = bp * seq, bs * dseq
    t_all = tp + ts

    assert sc_w // SC_GROUPS == SSD_HEAD_DIM and heads == SC_GROUPS
    head_of = jnp.arange(ssd_w, dtype=I32) // SSD_HEAD_DIM
    e_ind = (jnp.arange(LANES, dtype=I32)[:, None] == head_of[None, :])

    xp = x_prompt.reshape(tp, d)
    xs_pad = jnp.pad(x_sample, ((0, 0), (0, SUBLANES - dseq), (0, 0))).reshape(bs * SUBLANES, d)
    xs_tok = x_sample.reshape(ts, d)
    outs = {k: [] for k in ("ssm_p", "cst_p", "scst_p", "ssm_s", "cst_s", "scst_s")}

    for layer in range(depth):
        cuts = np.cumsum([0, ssd_w, conv_dim, heads, sc_w, sc_w, sc_w]).tolist()
        wi = w_in[layer]
        seg = lambda k: wi[:, cuts[k]:cuts[k + 1]]
        w_in_r = jnp.concatenate([seg(0), seg(1), seg(3), seg(4), seg(5),
                                  jnp.pad(seg(2), ((0, 0), (0, LANES - heads)))], axis=1).astype(BF16)
        widths = (ssd_w, conv_dim, sc_w, sc_w, sc_w, LANES)
        pad_h = lambda v: jnp.pad(v.reshape(1, heads), ((0, 0), (0, LANES - heads)))
        wr = w_router[layer].T
        wr_hi = wr.astype(BF16)
        consts = dict(
            heads=heads,
            conv_w=ssd_conv_w[layer], conv_b=ssd_conv_b[layer].reshape(1, conv_dim),
            dt_bias=pad_h(dt_bias[layer]), a_log=pad_h(a_log[layer]),
            d_skip_e=jnp.broadcast_to(d_skip[layer][:, None], (heads, SSD_HEAD_DIM)).reshape(1, ssd_w),
            g_ssd=g_ssd_norm[layer].reshape(1, ssd_w), sc_w=sc_conv_w[layer],
            g_sc=g_sc_norm[layer].reshape(1, sc_w),
            e=e_ind.astype(BF16), et=e_ind.T.astype(F32),
            w_out=w_out[layer].astype(BF16), g_post_mix=g_post_mix[layer].reshape(1, d),
            g_pre_ffn=g_pre_ffn[layer].reshape(1, d),
            wr_hi_t=wr_hi, wr_lo_t=(wr - wr_hi.astype(F32)).astype(BF16),
            w_sh_gate=w_sh_gate[layer].astype(BF16), w_sh_up=w_sh_up[layer].astype(BF16),
            w_sh_down=w_sh_down[layer].astype(BF16), g_post_ffn=g_post_ffn[layer].reshape(1, d))

        c_all = jnp.concatenate([c_prompt, c_sample], axis=0)
        m_rows = -(-c_all.shape[0] // 16) * 16
        mod = _ada(jnp.pad(c_all, ((0, m_rows - c_all.shape[0]), (0, 0))), w_ada[layer], b_ada[layer])
        mod_p = mod[:bp]
        mod_s = mod[bp:bp + bs]
        mod_s_pad = _repeat_rows(mod_s[:, :2 * d], SUBLANES)
        mod_s_tok = _repeat_rows(mod_s, dseq)

        proj_s = _in_proj(xs_pad, mod_s_pad, g_pre_mix[layer], w_in_r, widths, F32, True, 1, TOK_TILE)
        ymix_p, ssm_p, cst_p, scst_p = _ssd(
            (xp, mod_p[:, :2 * d], g_pre_mix[layer].reshape(1, d), w_in_r), None, consts, bp, seq // CHUNK,
            CHUNK, CHUNK, CHUNK, BF16, widths=widths, n_sub=SCAN_SUB)
        init = (state_ssm[layer].reshape(bs, ssd_w, D_STATE), state_ssd_conv[layer], state_short_conv[layer])
        ymix_s, ssm_s, cst_s, scst_s = _ssd(proj_s, init, consts, bs, 1, SAMPLE_CHUNK, SUBLANES, dseq, F32)
        ymix_s = ymix_s.reshape(bs, SUBLANES, ssd_w + sc_w)[:, :dseq].reshape(ts, ssd_w + sc_w)

        merged = _out_proj(ymix_p, xp, mod_p[:, 2 * d:5 * d], consts, False, seq, TOK_TILE, t_all, 0, None)
        x1, u2, logits_t = _out_proj(ymix_s, xs_tok, mod_s_tok[:, 2 * d:5 * d], consts, True, 1, TOK_TILE,
                                     t_all, tp // TOK_TILE, merged)

        idx_t, rank_t, wtok, counts = _route(logits_t, router_bias[layer], TOK_TILE)
        counts = counts[:, 0]
        padded = (counts + ROW_TILE - 1) // ROW_TILE * ROW_TILE
        pad_end = jnp.cumsum(padded)
        pad_start = pad_end - padded
        n_tiles = -(-(t_all * TOP_K) // ROW_TILE) + ne
        n_used = (pad_end[-1] // ROW_TILE).astype(I32)
        tile_ids = jnp.minimum(jnp.arange(n_tiles, dtype=I32), n_used - 1)
        tile_expert = jnp.minimum(jnp.sum(pad_end[None, :] <= (tile_ids * ROW_TILE)[:, None], axis=1), ne - 1).astype(I32)
        first_pad = (pad_start + counts) // SUBLANES * SUBLANES
        units = (pad_end - first_pad) // SUBLANES
        tile_valid = jnp.clip(counts[tile_expert] - (tile_ids * ROW_TILE - pad_start[tile_expert]), 0,
                              ROW_TILE).astype(I32)
        pos_t, meta = _positions(idx_t, rank_t, pad_start.astype(I32), wtok, TOK_TILE)
        pos_flat = _tile_major(pos_t, GATHER_TILE)

        xs_rows = _dispatch(pos_flat, u2, meta, n_tiles * ROW_TILE, GATHER_TILE)
        g_rows = _experts(tile_expert, n_used.reshape(1), tile_valid, xs_rows, w_exp_gate[layer],
                          w_exp_up[layer], w_exp_down[layer], t_all)
        xp = _combine(g_rows, u2, x1, mod_p[:, 5 * d:], consts, False, seq, 0, tp, GATHER_TILE)
        xs_tok = _combine(g_rows, u2, x1, mod_s_tok[:, 5 * d:], consts, True, 1, tp // GATHER_TILE, ts,
                          GATHER_TILE)
        xs_pad = jnp.pad(xs_tok.reshape(bs, dseq, d), ((0, 0), (0, SUBLANES - dseq), (0, 0))).reshape(
            bs * SUBLANES, d)

        outs["ssm_p"].append(ssm_p.reshape(bp, heads, SSD_HEAD_DIM, D_STATE))
        outs["cst_p"].append(cst_p)
        outs["scst_p"].append(scst_p)
        outs["ssm_s"].append(ssm_s.reshape(bs, heads, SSD_HEAD_DIM, D_STATE))
        outs["cst_s"].append(cst_s)
        outs["scst_s"].append(scst_s)

    return (xp.reshape(bp, seq, d), xs_tok.reshape(bs, dseq, d),
            jnp.stack(outs["ssm_p"]), jnp.stack(outs["cst_p"]), jnp.stack(outs["scst_p"]),
            jnp.stack(outs["ssm_s"]), jnp.stack(outs["cst_s"]), jnp.stack(outs["scst_s"]))
```

```python
import functools

import jax
import jax.numpy as jnp
import numpy as np
from jax import lax
from jax.experimental import pallas as pl
from jax.experimental.pallas import tpu as pltpu

F32 = jnp.float32
BF16 = jnp.bfloat16
I32 = jnp.int32
U32 = jnp.uint32
HI = lax.Precision.HIGHEST

SSD_HEAD_DIM = 64
SSD_GROUPS = 2
D_STATE = 128
SSD_CONV_W = 4
SC_GROUPS = 16
SC_CONV_W = 3
TOP_K = 8
N_EXPERT_GROUPS = 8
TOPK_GROUPS = 4
ROUTED_SCALE = 2.5
EPS = 1e-6

LANES = 128
SUBLANES = 8
CHUNK = 128
SAMPLE_CHUNK = 16
SCAN_SUB = 4
TOK_TILE = 512
ROW_TILE = 576
GATHER_TILE = 512
COMBINE_TILE = 256
VMEM_LIMIT = 56 * 1024 * 1024
DMA_THREADS = 2

NT = (((1,), (1,)), ((), ()))
TN = (((0,), (0,)), ((), ()))


def _sigmoid(x):
    return 1.0 / (1.0 + jnp.exp(-x))


def _silu(x):
    return x * _sigmoid(x)


def _softplus(x):
    return jnp.maximum(x, 0.0) + jnp.log1p(jnp.exp(-jnp.abs(x)))


def _rms(x, eps=EPS):
    return x * lax.rsqrt(jnp.mean(x * x, axis=-1, keepdims=True) + eps)


def _params(sem=None):
    return pltpu.CompilerParams(dimension_semantics=sem, vmem_limit_bytes=VMEM_LIMIT)


def _const_spec(shape, single=False):
    nd = len(shape)
    mode = dict(pipeline_mode=pl.Buffered(1)) if single else {}
    return pl.BlockSpec(shape, lambda *_: (0,) * nd, **mode)


def _ada_kernel(c_ref, w_ref, b_ref, o_ref):
    c = c_ref[...]
    s = _silu(c).astype(BF16)
    o_ref[...] = jnp.dot(s, w_ref[...].astype(BF16), preferred_element_type=F32) + b_ref[...]


def _ada(c, w_ada, b_ada):
    m, d = c.shape
    n = w_ada.shape[1]
    tn = 512
    return pl.pallas_call(
        _ada_kernel,
        out_shape=jax.ShapeDtypeStruct((m, n), F32),
        grid=(n // tn,),
        in_specs=[_const_spec((m, d)),
                  pl.BlockSpec((d, tn), lambda j: (0, j)),
                  pl.BlockSpec((1, tn), lambda j: (0, j))],
        out_specs=pl.BlockSpec((m, tn), lambda j: (0, j)),
        compiler_params=_params(("arbitrary",)),
        name="ada",
    )(c, w_ada, b_ada.reshape(1, n))


def _mod_rows(mod_ref, per_row, rows_per_batch, tile, d):
    n = mod_ref.shape[1] // d
    if per_row:
        return [mod_ref[:, k * d:(k + 1) * d] for k in range(n)]
    b = (pl.program_id(0) * tile) // rows_per_batch
    return [mod_ref[pl.ds(b, 1), k * d:(k + 1) * d] for k in range(n)]


def _mod_spec(mod, per_row, tile):
    if per_row:
        return pl.BlockSpec((tile, mod.shape[1]), lambda i: (i, 0))
    return _const_spec(mod.shape)


def _in_kernel(x_ref, mod_ref, g_ref, w_ref, *out_refs, per_row, rows_per_batch, widths):
    tile, d = x_ref.shape
    shift, scale = _mod_rows(mod_ref, per_row, rows_per_batch, tile, d)
    u = (_rms(x_ref[...]) * g_ref[...]) * (1.0 + scale) + shift
    u = u.astype(BF16)
    col = 0
    for ref, width in zip(out_refs, widths):
        for a in range(0, width, 512):
            bw = min(512, width - a)
            r = jnp.dot(u, w_ref[:, col + a:col + a + bw], preferred_element_type=F32)
            ref[:, a:a + bw] = r.astype(ref.dtype)
        col += width


def _in_proj(x, mod, g, w_bf16, widths, out_dtype, per_row, rows_per_batch, tile):
    t, d = x.shape
    n = w_bf16.shape[1]
    mod_spec = _mod_spec(mod, per_row, tile)
    dts = [out_dtype] * (len(widths) - 1) + [F32]
    return pl.pallas_call(
        functools.partial(_in_kernel, per_row=per_row, rows_per_batch=rows_per_batch, widths=widths),
        out_shape=[jax.ShapeDtypeStruct((t, wd), dt) for wd, dt in zip(widths, dts)],
        grid=(t // tile,),
        in_specs=[pl.BlockSpec((tile, d), lambda i: (i, 0)), mod_spec,
                  _const_spec((1, d)), _const_spec((d, n), single=True)],
        out_specs=[pl.BlockSpec((tile, wd), lambda i: (i, 0)) for wd in widths],
        compiler_params=_params(("arbitrary",)),
        name="in_proj",
    )(x, mod, g.reshape(1, d), w_bf16)


def _ssd_kernel(*refs, q, rows_in, q_valid, has_init, heads, widths, n_sub):
    it = iter(refs)
    if widths is None:
        z_ref, xbc_ref, scb_ref, scc_ref, sch_ref, dt_ref = (next(it) for _ in range(6))
    else:
        x_ref, mod_ref, gpre_ref, win_ref = (next(it) for _ in range(4))
    if has_init:
        ssm0_ref, cst0_ref, scst0_ref = (next(it) for _ in range(3))
    (cw_ref, cb_ref, dtb_ref, alog_ref, dsk_ref, gssd_ref, scw_ref, gsc_ref,
     tril_ref, e_ref, et_ref) = (next(it) for _ in range(11))
    ymix_ref, ssm_ref, cst_ref, scst_ref = (next(it) for _ in range(4))
    h_scr, ext_scr, extv_scr = (next(it) for _ in range(3))

    c = pl.program_id(1)
    nc = pl.num_programs(1)
    ssd_w = dsk_ref.shape[1]
    gw = ssd_w // SSD_GROUPS
    hpg = heads // SSD_GROUPS
    n_state = D_STATE
    head0 = SUBLANES - (SSD_CONV_W - 1)
    headv = SUBLANES - (SC_CONV_W - 1)

    @pl.when(c == 0)
    def _():
        if has_init:
            h_scr[...] = ssm0_ref[0]
            ext_scr[head0:SUBLANES, :] = cst0_ref[0]
            extv_scr[headv:SUBLANES, :] = scst0_ref[0]
        else:
            h_scr[...] = jnp.zeros(h_scr.shape, F32)
            ext_scr[0:SUBLANES, :] = jnp.zeros((SUBLANES, ext_scr.shape[1]), F32)
            extv_scr[0:SUBLANES, :] = jnp.zeros((SUBLANES, extv_scr.shape[1]), F32)

    def pad_rows(v, n):
        if v.shape[0] == n:
            return v
        return jnp.concatenate([v, jnp.zeros((n - v.shape[0], v.shape[1]), v.dtype)], axis=0)

    load = lambda ref: pad_rows(ref[...].astype(F32), q)
    pad_t = lambda v: pad_rows(v, LANES)
    if widths is None:
        z, xbc, scb, scc, sch, dt_raw = (load(r) for r in (z_ref, xbc_ref, scb_ref, scc_ref, sch_ref, dt_ref))
    else:
        d_model = x_ref.shape[1]
        b = pl.program_id(0)
        shift = mod_ref[pl.ds(b, 1), 0:d_model]
        scale = mod_ref[pl.ds(b, 1), d_model:2 * d_model]
        u = ((_rms(x_ref[...]) * gpre_ref[...]) * (1.0 + scale) + shift).astype(BF16)
        pieces, col = [], 0
        for width in widths:
            parts = [jnp.dot(u, win_ref[:, col + a:col + a + min(512, width - a)], preferred_element_type=F32)
                     for a in range(0, width, 512)]
            pieces.append(parts[0] if len(parts) == 1 else jnp.concatenate(parts, axis=1))
            col += width
        z, xbc, scb, scc, sch = (p.astype(BF16).astype(F32) for p in pieces[:5])
        dt_raw = pieces[5]

    def scan_chunk(j, z, xbc, scb, scc, sch, dt_raw):
        ext_scr[SUBLANES:SUBLANES + q, :] = xbc
        conv = cb_ref[...]
        for k in range(SSD_CONV_W):
            conv = conv + cw_ref[k:k + 1, :] * ext_scr[head0 + k:head0 + k + q, :]
        xc = _silu(conv)
        xs = xc[:, :ssd_w]
        bm = xc[:, ssd_w:ssd_w + SSD_GROUPS * n_state]
        cm = xc[:, ssd_w + SSD_GROUPS * n_state:]

        lane = lax.broadcasted_iota(I32, (q, LANES), 1)
        row = lax.broadcasted_iota(I32, (q, LANES), 0)
        dt = _softplus(dt_raw + dtb_ref[...])
        dt = jnp.where(jnp.logical_and(lane < heads, row < q_valid), dt, 0.0)
        a = dt * (-jnp.exp(alog_ref[...]))
        acum = jnp.dot(tril_ref[...], pad_t(a), precision=HI, preferred_element_type=F32)
        acum_t = pad_t(acum).T
        a_last = acum[q - 1:q, :]
        e = e_ref[...]
        expand = lambda v: jnp.dot(v.astype(BF16), e, preferred_element_type=F32)
        dt_e = expand(dt)
        dtdte_e = expand(dt * jnp.exp(a_last - acum))
        exa_e = expand(jnp.exp(acum))
        xdt = xs * dt_e
        xw_b = (xs * dtdte_e).astype(BF16)
        dlast = jnp.broadcast_to(jnp.exp(acum_t[:, q - 1:q]), (LANES, n_state))
        dcol = jnp.dot(et_ref[...], dlast, precision=HI, preferred_element_type=F32)

        tri = row >= lane
        xdt_t = pad_t(xdt)
        xw_t = pad_t(xw_b)
        lane_t = lax.broadcasted_iota(I32, (LANES, LANES), 1)
        y_groups = []
        for g in range(SSD_GROUPS):
            bm_g = bm[:, g * n_state:(g + 1) * n_state].astype(BF16)
            cm_g = cm[:, g * n_state:(g + 1) * n_state].astype(BF16)
            bm_t = pad_t(bm_g)
            cb = lax.dot_general(cm_g, bm_t, NT, preferred_element_type=F32)
            h_g = h_scr[g * gw:(g + 1) * gw, :]
            y_off = lax.dot_general(cm_g, h_g.astype(BF16), NT, preferred_element_type=F32)
            parts = []
            for pair in range(hpg // 2):
                lo = (g * hpg + 2 * pair) * SSD_HEAD_DIM
                x_pair = xdt_t[:, lo:lo + LANES]
                acc = None
                for half in range(2):
                    h = g * hpg + 2 * pair + half
                    ci = jnp.broadcast_to(acum[:, h:h + 1], (q, LANES))
                    rj = jnp.broadcast_to(acum_t[h:h + 1, :], (q, LANES))
                    dec = jnp.where(tri, jnp.exp(ci - rj), 0.0)
                    m = (cb * dec).astype(BF16)
                    own = (lane_t >= SSD_HEAD_DIM) if half else (lane_t < SSD_HEAD_DIM)
                    y_h = jnp.dot(m, jnp.where(own, x_pair, 0.0).astype(BF16), preferred_element_type=F32)
                    acc = y_h if acc is None else acc + y_h
                parts.append(acc)
            y_diag = jnp.concatenate(parts, axis=1)
            y_groups.append(y_diag + y_off * exa_e[:, g * gw:(g + 1) * gw])
            upd = lax.dot_general(xw_t[:, g * gw:(g + 1) * gw], bm_t, TN, preferred_element_type=F32)
            h_scr[g * gw:(g + 1) * gw, :] = h_g * dcol[g * gw:(g + 1) * gw, :] + upd

        y = (jnp.concatenate(y_groups, axis=1) + xs * dsk_ref[...]) * _silu(z)
        y = jnp.concatenate([_rms(y[:, g * gw:(g + 1) * gw]) for g in range(SSD_GROUPS)], axis=1)
        y_ssd = y * gssd_ref[...]

        extv_scr[SUBLANES:SUBLANES + q, :] = scc * sch
        cv = scw_ref[0:1, :] * extv_scr[headv:headv + q, :]
        for k in range(1, SC_CONV_W):
            cv = cv + scw_ref[k:k + 1, :] * extv_scr[headv + k:headv + k + q, :]
        t = scb * cv
        sc_per_group = t.shape[1] // SC_GROUPS
        gsum = jnp.dot((t * t).astype(BF16), et_ref[...].astype(BF16), preferred_element_type=F32)
        rs = lax.rsqrt(gsum * (1.0 / sc_per_group) + EPS)
        y_sc = t * expand(rs) * gsc_ref[...]

        out_rows = slice(j * rows_in, (j + 1) * rows_in)
        ymix_ref[out_rows, :ssd_w] = y_ssd[0:rows_in].astype(ymix_ref.dtype)
        ymix_ref[out_rows, ssd_w:] = y_sc[0:rows_in].astype(ymix_ref.dtype)

        if j == n_sub - 1:
            @pl.when(c == nc - 1)
            def _():
                ssm_ref[0] = h_scr[...]
                cst_ref[0] = ext_scr[SUBLANES + q_valid - (SSD_CONV_W - 1):SUBLANES + q_valid, :]
                scst_ref[0] = extv_scr[SUBLANES + q_valid - (SC_CONV_W - 1):SUBLANES + q_valid, :]

        ext_scr[0:SUBLANES, :] = ext_scr[q:q + SUBLANES, :]
        extv_scr[0:SUBLANES, :] = extv_scr[q:q + SUBLANES, :]

    for j in range(n_sub):
        scan_chunk(j, *(p[j * q:(j + 1) * q] for p in (z, xbc, scb, scc, sch, dt_raw)))


def _ssd(proj, init, consts, nb, n_chunks, q, rows_in, q_valid, ymix_dtype, widths=None, n_sub=1):
    heads = consts["heads"]
    tril = jnp.tril(jnp.ones((q, LANES), F32))
    ssd_w, conv_dim, sc_w = consts["g_ssd"].shape[1], consts["conv_w"].shape[1], consts["g_sc"].shape[1]
    has_init = init is not None
    assert n_chunks % n_sub == 0 and (n_sub == 1 or rows_in == q)
    nc = n_chunks // n_sub
    row_spec = lambda w: pl.BlockSpec((n_sub * rows_in, w), lambda b, c: (b * nc + c, 0))
    if widths is None:
        in_specs = [row_spec(ssd_w), row_spec(conv_dim), row_spec(sc_w), row_spec(sc_w), row_spec(sc_w),
                    row_spec(LANES)]
    else:
        x, mod, g, w = proj
        in_specs = [row_spec(x.shape[1]), _const_spec(mod.shape), _const_spec(g.shape),
                    _const_spec(w.shape, single=True)]
    args = list(proj)
    if has_init:
        ssm0, cst0, scst0 = init
        in_specs += [pl.BlockSpec((1,) + ssm0.shape[1:], lambda b, c: (b, 0, 0)),
                     pl.BlockSpec((1,) + cst0.shape[1:], lambda b, c: (b, 0, 0)),
                     pl.BlockSpec((1,) + scst0.shape[1:], lambda b, c: (b, 0, 0))]
        args += [ssm0, cst0, scst0]
    weights = [consts[k] for k in ("conv_w", "conv_b", "dt_bias", "a_log", "d_skip_e", "g_ssd",
                                   "sc_w", "g_sc")] + [tril, consts["e"], consts["et"]]
    in_specs += [_const_spec(w.shape) for w in weights]
    args += weights
    n_state = D_STATE
    out_shape = [jax.ShapeDtypeStruct((nb * n_chunks * rows_in, ssd_w + sc_w), ymix_dtype),
                 jax.ShapeDtypeStruct((nb, ssd_w, n_state), F32),
                 jax.ShapeDtypeStruct((nb, SSD_CONV_W - 1, conv_dim), F32),
                 jax.ShapeDtypeStruct((nb, SC_CONV_W - 1, sc_w), F32)]
    out_specs = [pl.BlockSpec((n_sub * rows_in, ssd_w + sc_w), lambda b, c: (b * nc + c, 0)),
                 pl.BlockSpec((1, ssd_w, n_state), lambda b, c: (b, 0, 0)),
                 pl.BlockSpec((1, SSD_CONV_W - 1, conv_dim), lambda b, c: (b, 0, 0)),
                 pl.BlockSpec((1, SC_CONV_W - 1, sc_w), lambda b, c: (b, 0, 0))]
    scratch = [pltpu.VMEM((ssd_w, n_state), F32),
               pltpu.VMEM((q + SUBLANES, conv_dim), F32),
               pltpu.VMEM((q + SUBLANES, sc_w), F32)]
    return pl.pallas_call(
        functools.partial(_ssd_kernel, q=q, rows_in=rows_in, q_valid=q_valid, has_init=has_init, heads=heads,
                          widths=widths, n_sub=n_sub),
        out_shape=out_shape, grid=(nb, nc), in_specs=in_specs, out_specs=out_specs,
        scratch_shapes=scratch,
        compiler_params=_params(("arbitrary", "arbitrary")),
        name="ssd",
    )(*args)


def _out_kernel(ymix_ref, x_ref, mod_ref, wout_ref, gpost_ref, gpre_ref, wr_hi_ref, wr_lo_ref,
                *rest, per_row, rows_per_batch, has_alias):
    x1_ref, u2_ref, lg_ref = rest[3:6] if has_alias else rest[0:3]
    tile, d = x_ref.shape
    gate1, shift2, scale2 = _mod_rows(mod_ref, per_row, rows_per_batch, tile, d)
    m = jnp.dot(ymix_ref[...].astype(BF16), wout_ref[...], preferred_element_type=F32)
    x1 = x_ref[...] + gate1 * (_rms(m) * gpost_ref[...])
    u2 = (_rms(x1) * gpre_ref[...]) * (1.0 + scale2) + shift2
    x1_ref[...] = x1
    u2_ref[...] = u2
    u_hi = u2.astype(BF16)
    u_lo = (u2 - u_hi.astype(F32)).astype(BF16)
    lg = lax.dot_general(wr_hi_ref[...], u_hi, NT, preferred_element_type=F32)
    lg = lg + lax.dot_general(wr_hi_ref[...], u_lo, NT, preferred_element_type=F32)
    lg = lg + lax.dot_general(wr_lo_ref[...], u_hi, NT, preferred_element_type=F32)
    lg_ref[...] = lg


def _out_proj(ymix, x, mod, consts, per_row, rows_per_batch, tile, t_total, tile_off, prev):
    t, d = x.shape
    dm = ymix.shape[1]
    ne = consts["wr_hi_t"].shape[0]
    in_specs = [pl.BlockSpec((tile, dm), lambda i: (i, 0)),
                pl.BlockSpec((tile, d), lambda i: (i, 0)), _mod_spec(mod, per_row, tile),
                _const_spec((dm, d)), _const_spec((1, d)), _const_spec((1, d)),
                _const_spec((ne, d)), _const_spec((ne, d))]
    args = [ymix, x, mod, consts["w_out"], consts["g_post_mix"], consts["g_pre_ffn"],
            consts["wr_hi_t"], consts["wr_lo_t"]]
    aliases = {}
    if prev is not None:
        in_specs += [pl.BlockSpec(memory_space=pl.ANY)] * 3
        aliases = {len(args) + k: k for k in range(3)}
        args += list(prev)
    return pl.pallas_call(
        functools.partial(_out_kernel, per_row=per_row, rows_per_batch=rows_per_batch,
                          has_alias=prev is not None),
        out_shape=[jax.ShapeDtypeStruct((t_total, d), F32), jax.ShapeDtypeStruct((t_total, d), F32),
                   jax.ShapeDtypeStruct((ne, t_total), F32)],
        grid=(t // tile,), in_specs=in_specs,
        out_specs=[pl.BlockSpec((tile, d), lambda i: (i + tile_off, 0)),
                   pl.BlockSpec((tile, d), lambda i: (i + tile_off, 0)),
                   pl.BlockSpec((ne, tile), lambda i: (0, i + tile_off))],
        input_output_aliases=aliases,
        compiler_params=_params(("arbitrary",)),
        name="out_proj",
    )(*args)


def _route_kernel(lg_ref, bias_ref, upper_ref, idx_ref, rank_ref, wtok_ref, cnt_ref, carry_scr):
    i = pl.program_id(0)
    ne, tm = lg_ref.shape
    per_group = ne // N_EXPERT_GROUPS
    neg = -jnp.inf

    @pl.when(i == 0)
    def _():
        carry_scr[...] = jnp.zeros(carry_scr.shape, F32)

    s = _sigmoid(lg_ref[...])
    biased = s + bias_ref[...]
    gl = []
    io_g = lax.broadcasted_iota(I32, (per_group, tm), 0).astype(F32)
    for g in range(N_EXPERT_GROUPS):
        blk = biased[g * per_group:(g + 1) * per_group, :]
        m1 = jnp.max(blk, axis=0, keepdims=True)
        f1 = jnp.min(jnp.where(blk == m1, io_g, float(per_group)), axis=0, keepdims=True)
        m2 = jnp.max(jnp.where(io_g == f1, neg, blk), axis=0, keepdims=True)
        gl.append(m1 + m2)
    gscore = jnp.concatenate(gl, axis=0)
    io8 = lax.broadcasted_iota(I32, (N_EXPERT_GROUPS, tm), 0).astype(F32)
    gsel = jnp.zeros((N_EXPERT_GROUPS, tm), F32)
    for _ in range(TOPK_GROUPS):
        m = jnp.max(gscore, axis=0, keepdims=True)
        f = jnp.min(jnp.where(gscore == m, io8, float(N_EXPERT_GROUPS)), axis=0, keepdims=True)
        hit = io8 == f
        gsel = jnp.where(hit, 1.0, gsel)
        gscore = jnp.where(hit, neg, gscore)
    emask = jnp.concatenate(
        [jnp.broadcast_to(gsel[g:g + 1, :], (per_group, tm)) for g in range(N_EXPERT_GROUPS)], axis=0)
    cand = jnp.where(emask > 0.5, biased, neg)
    io_e = lax.broadcasted_iota(I32, (ne, tm), 0).astype(F32)
    msel = jnp.zeros((ne, tm), F32)
    idxs, wts = [], []
    for _ in range(TOP_K):
        m = jnp.max(cand, axis=0, keepdims=True)
        f = jnp.min(jnp.where(cand == m, io_e, float(ne)), axis=0, keepdims=True)
        hit = io_e == f
        wts.append(jnp.sum(jnp.where(hit, s, 0.0), axis=0, keepdims=True))
        idxs.append(f)
        msel = jnp.where(hit, 1.0, msel)
        cand = jnp.where(hit, neg, cand)
    pref = jnp.dot(msel.astype(BF16), upper_ref[...], preferred_element_type=F32) + carry_scr[:, 0:1]
    ranks = [jnp.sum(jnp.where(io_e == f, pref, 0.0), axis=0, keepdims=True) for f in idxs]
    carry_scr[...] = carry_scr[...] + jnp.sum(msel, axis=1, keepdims=True)
    cnt_ref[...] = carry_scr[...].astype(I32)
    idx_ref[...] = jnp.concatenate(idxs, axis=0).astype(I32)
    rank_ref[...] = jnp.concatenate(ranks, axis=0).astype(I32)
    wsum = wts[0]
    for w in wts[1:]:
        wsum = wsum + w
    wn = jnp.concatenate([w / wsum * ROUTED_SCALE for w in wts]
                         + [jnp.zeros((LANES - TOP_K, tm), F32)], axis=0)
    for j in range(tm // LANES):
        wtok_ref[j * LANES:(j + 1) * LANES, :] = wn[:, j * LANES:(j + 1) * LANES].T


def _route(logits_t, bias, tile):
    ne, t = logits_t.shape
    upper = jnp.triu(jnp.ones((tile, tile), F32), 1).astype(BF16)
    return pl.pallas_call(
        _route_kernel,
        out_shape=[jax.ShapeDtypeStruct((TOP_K, t), I32), jax.ShapeDtypeStruct((TOP_K, t), I32),
                   jax.ShapeDtypeStruct((t, LANES), F32), jax.ShapeDtypeStruct((ne, LANES), I32)],
        grid=(t // tile,),
        in_specs=[pl.BlockSpec((ne, tile), lambda i: (0, i)), _const_spec((ne, 1)),
                  _const_spec((tile, tile))],
        out_specs=[pl.BlockSpec((TOP_K, tile), lambda i: (0, i)),
                   pl.BlockSpec((TOP_K, tile), lambda i: (0, i)),
                   pl.BlockSpec((tile, LANES), lambda i: (i, 0)),
                   _const_spec((ne, LANES))],
        scratch_shapes=[pltpu.VMEM((ne, LANES), F32)],
        compiler_params=_params(("arbitrary",)),
        name="route",
    )(logits_t, bias.reshape(ne, 1), upper)


META_POS, META_W, META_TOK = 0, TOP_K, 2 * TOP_K


def _pos_kernel(idx_ref, rank_ref, start_ref, wtok_ref, pos_ref, meta_ref):
    ne = start_ref.shape[0]
    tm = idx_ref.shape[1]
    io_e = lax.broadcasted_iota(I32, (ne, tm), 0)
    start = start_ref[...].astype(F32)
    rows = []
    for k in range(TOP_K):
        hit = io_e == idx_ref[k:k + 1, :]
        rows.append(jnp.sum(jnp.where(hit, start, 0.0), axis=0, keepdims=True))
    pos_f = jnp.concatenate(rows, axis=0) + rank_ref[...].astype(F32)
    pos_ref[...] = pos_f.astype(I32)
    pos_pad = jnp.concatenate([pos_f, jnp.zeros((LANES - TOP_K, tm), F32)], axis=0)
    lane = lax.broadcasted_iota(I32, (LANES, LANES), 1)
    row = lax.broadcasted_iota(I32, (LANES, LANES), 0)
    for j in range(tm // LANES):
        pos_tok = pos_pad[:, j * LANES:(j + 1) * LANES].T
        w_tok = pltpu.roll(wtok_ref[j * LANES:(j + 1) * LANES, :], META_W, axis=1)
        tok = (row + (pl.program_id(0) * tm + j * LANES)).astype(F32)
        tag = jnp.where(lane < META_W, pos_tok, jnp.where(lane == META_TOK, tok, w_tok))
        meta_ref[j * LANES:(j + 1) * LANES, :] = tag


def _positions(idx_t, rank_t, pad_start, wtok, tile):
    k, t = idx_t.shape
    ne = pad_start.shape[0]
    return pl.pallas_call(
        _pos_kernel,
        out_shape=[jax.ShapeDtypeStruct((k, t), I32), jax.ShapeDtypeStruct((t, LANES), F32)],
        grid=(t // tile,),
        in_specs=[pl.BlockSpec((k, tile), lambda i: (0, i)), pl.BlockSpec((k, tile), lambda i: (0, i)),
                  _const_spec((ne, 1)), pl.BlockSpec((tile, LANES), lambda i: (i, 0))],
        out_specs=[pl.BlockSpec((k, tile), lambda i: (0, i)), pl.BlockSpec((tile, LANES), lambda i: (i, 0))],
        compiler_params=_params(("arbitrary",)),
        name="positions",
    )(idx_t, rank_t, pad_start.reshape(ne, 1), wtok)


ZERO_BITS = (64, 32, 16, 8, 4, 2, 1)
assert sum(ZERO_BITS) * SUBLANES >= ROW_TILE


def _zero_copy(zbuf, xs_ref, sem, off, bit):
    rows = bit * SUBLANES
    return pltpu.make_async_copy(zbuf.at[pl.ds(0, rows)], xs_ref.at[pl.ds(pl.multiple_of(off, SUBLANES), rows)], sem)


def _zero_kernel(first_ref, units_ref, xs_ref, zbuf, sem):
    zbuf[...] = jnp.zeros(zbuf.shape, zbuf.dtype)
    ne = first_ref.shape[0]

    def each(e, start):
        off = first_ref[e]
        units = units_ref[e]
        for bit in ZERO_BITS:
            on = (units & bit) != 0

            @pl.when(on)
            def _():
                cp = _zero_copy(zbuf, xs_ref, sem, off, bit)
                cp.start() if start else cp.wait()
            off = off + jnp.where(on, bit * SUBLANES, 0)
        return start

    lax.fori_loop(0, ne, lambda e, c: (each(e, True), c)[1], 0)
    lax.fori_loop(0, ne, lambda e, c: (each(e, False), c)[1], 0)


def _zero_rows(first, units, n_rows, width):
    return pl.pallas_call(
        _zero_kernel,
        out_shape=jax.ShapeDtypeStruct((n_rows, width), F32),
        in_specs=[pl.BlockSpec(memory_space=pltpu.SMEM), pl.BlockSpec(memory_space=pltpu.SMEM)],
        out_specs=pl.BlockSpec(memory_space=pl.ANY),
        scratch_shapes=[pltpu.VMEM((ZERO_BITS[0] * SUBLANES, width), F32), pltpu.SemaphoreType.DMA],
        compiler_params=_params(),
        name="zero_rows",
    )(first, units)


def _row_copy(src_ref, dst_ref, sem, s, d):
    return pltpu.make_async_copy(src_ref.at[pl.ds(s, 1)], dst_ref.at[pl.ds(d, 1)], sem)


def _dispatch_kernel(pos_hbm, u_ref, meta_ref, xs_ref, pos_a, pos_b, src, psem, sem, *, nt):
    i = pl.program_id(0)
    tm, d = u_ref.shape
    per = pos_a.shape[0]
    pos_slots = (pos_a, pos_b)

    def pos_copy(tile, s):
        return pltpu.make_async_copy(pos_hbm.at[pl.ds(pl.multiple_of(tile * per, per), per)], pos_slots[s],
                                     psem.at[s])

    def drain(s):
        for _ in range(TOP_K):
            pltpu.make_async_copy(src.at[s], xs_ref.at[pl.ds(0, tm)], sem.at[s]).wait()

    def step(s):
        @pl.when(i == 0)
        def _():
            pos_copy(0, 0).start()

        @pl.when(i + 1 < nt)
        def _():
            pos_copy(i + 1, 1 - s).start()

        @pl.when(i >= 2)
        def _():
            drain(s)

        src[s, :, :d] = u_ref[...]
        src[s, :, d:] = meta_ref[...]
        pos_copy(i, s).wait()

        def issue(t, carry):
            for k in range(TOP_K):
                _row_copy(src.at[s], xs_ref, sem.at[s], t, pos_slots[s][k * tm + t]).start(
                    priority=k % DMA_THREADS)
            return carry

        lax.fori_loop(0, tm, issue, 0)

        @pl.when(i == nt - 1)
        def _():
            if nt >= 2:
                drain(1 - s)
            drain(s)

    for s in range(2):
        pl.when(i % 2 == s)(functools.partial(step, s))


def _dispatch(pos_flat, u2, meta, n_rows, tm):
    per = TOP_K * tm
    nt = pos_flat.shape[0] // per
    d = u2.shape[1]
    return pl.pallas_call(
        functools.partial(_dispatch_kernel, nt=nt),
        out_shape=jax.ShapeDtypeStruct((n_rows, d + LANES), F32),
        grid=(nt,),
        in_specs=[pl.BlockSpec(memory_space=pl.ANY), pl.BlockSpec((tm, d), lambda i: (i, 0)),
                  pl.BlockSpec((tm, LANES), lambda i: (i, 0))],
        out_specs=pl.BlockSpec(memory_space=pl.ANY),
        scratch_shapes=[pltpu.SMEM((per,), I32), pltpu.SMEM((per,), I32),
                        pltpu.VMEM((2, tm, d + LANES), F32),
                        pltpu.SemaphoreType.DMA((2,)), pltpu.SemaphoreType.DMA((2,))],
        compiler_params=_params(("arbitrary",)),
        name="dispatch",
    )(pos_flat, u2, meta)


WAIT_BITS = tuple(1 << b for b in reversed(range(ROW_TILE.bit_length())))


def _expert_kernel(te_ref, nu_ref, nv_ref, xs_ref, wg_ref, wu_ref, wd_ref, g_hbm,
                   ybuf, dst_vmem, dst_smem, sem, dsem, *, n_tok):
    del te_ref
    i = pl.program_id(0)
    nu = nu_ref[0]
    d = ybuf.shape[3]
    rows = ybuf.shape[1] * SUBLANES

    def drain(tile):
        slot = tile % 2
        n = nv_ref[tile]
        for bit in WAIT_BITS:
            @pl.when((n & bit) != 0)
            def _():
                pltpu.make_async_copy(g_hbm.at[pl.ds(0, bit)], g_hbm.at[pl.ds(0, bit)], sem.at[slot]).wait()

    @pl.when(i < nu)
    def _():
        slot = i % 2

        @pl.when(i >= 2)
        def _():
            drain(i - 2)

        dst_vmem[...] = jnp.zeros(dst_vmem.shape, I32)
        xfull = xs_ref[...]
        tag = xfull[:, d:]
        x = xfull[:, :d].astype(BF16)
        lane = lax.broadcasted_iota(I32, (rows, LANES), 1)
        p = (lax.broadcasted_iota(I32, (rows, LANES), 0) + i * rows).astype(F32)
        hit = jnp.logical_and(tag == p, lane < META_W)
        w_at = pltpu.roll(tag, LANES - META_W, axis=1)
        w = jnp.sum(jnp.where(hit, w_at, 0.0), axis=1, keepdims=True)
        kf = jnp.sum(jnp.where(hit, lane.astype(F32), 0.0), axis=1, keepdims=True)
        dest = kf * float(n_tok) + tag[:, META_TOK:META_TOK + 1]

        dest_b = jnp.broadcast_to(dest, (rows, LANES))
        n_blk = -(-rows // LANES)
        dest_b = jnp.concatenate([dest_b, jnp.zeros((n_blk * LANES - rows, LANES), F32)], axis=0)
        for b in range(n_blk):
            dst_vmem[b:b + 1, :] = dest_b[b * LANES:(b + 1) * LANES, :].T[0:1, :].astype(I32)
        to_smem = [pltpu.make_async_copy(dst_vmem.at[b], dst_smem.at[pl.ds(b * LANES, LANES)], dsem)
                   for b in range(n_blk)]
        for cp in to_smem:
            cp.start()

        g = jnp.dot(x, wg_ref[0].astype(BF16), preferred_element_type=F32)
        u = jnp.dot(x, wu_ref[0].astype(BF16), preferred_element_type=F32)
        h = (_silu(g) * u).astype(BF16)
        y = jnp.dot(h, wd_ref[0].astype(BF16), preferred_element_type=F32) * w
        ybuf[slot] = y.reshape(rows // SUBLANES, SUBLANES, d)
        for cp in to_smem:
            cp.wait()

        nv = nv_ref[i]
        full = lax.shift_right_logical(nv, 3)

        def scatter_rows(s):
            def send(r8, j, prio):
                to = dst_smem[r8 * SUBLANES + j]
                pltpu.make_async_copy(ybuf.at[s, r8, pl.ds(j, 1)], g_hbm.at[pl.ds(to, 1)],
                                      sem.at[s]).start(priority=prio)

            def issue8(r8, carry):
                for j in range(SUBLANES):
                    send(r8, j, j % DMA_THREADS)
                return carry

            def issue1(r, carry):
                send(lax.shift_right_logical(r, 3), r & (SUBLANES - 1), 0)
                return carry

            lax.fori_loop(0, full, issue8, 0)
            lax.fori_loop(full * SUBLANES, nv, issue1, 0)

        for s in range(2):
            pl.when(slot == s)(functools.partial(scatter_rows, s))

    @pl.when(i == pl.num_programs(0) - 1)
    def _():
        @pl.when(nu >= 2)
        def _():
            drain(nu - 2)

        @pl.when(nu >= 1)
        def _():
            drain(nu - 1)


def _experts(tile_expert, n_used, tile_valid, xs, w_gate, w_up, w_down, n_tok):
    n_rows, width = xs.shape
    d, de = w_gate.shape[1:]
    n_tiles = n_rows // ROW_TILE
    row_map = lambda i, te, nu, nv: (jnp.minimum(i, nu[0] - 1), 0)
    w_map = lambda i, te, nu, nv: (te[i], 0, 0)
    n_blk = -(-ROW_TILE // LANES)
    return pl.pallas_call(
        functools.partial(_expert_kernel, n_tok=n_tok),
        out_shape=jax.ShapeDtypeStruct((TOP_K * n_tok, d), F32),
        grid_spec=pltpu.PrefetchScalarGridSpec(
            num_scalar_prefetch=3, grid=(n_tiles,),
            in_specs=[pl.BlockSpec((ROW_TILE, width), row_map),
                      pl.BlockSpec((1, d, de), w_map), pl.BlockSpec((1, d, de), w_map),
                      pl.BlockSpec((1, de, d), w_map)],
            out_specs=pl.BlockSpec(memory_space=pl.ANY),
            scratch_shapes=[pltpu.VMEM((2, ROW_TILE // SUBLANES, SUBLANES, d), F32),
                            pltpu.VMEM((SUBLANES, LANES), I32),
                            pltpu.SMEM((n_blk * LANES,), I32), pltpu.SemaphoreType.DMA((2,)),
                            pltpu.SemaphoreType.DMA]),
        compiler_params=_params(("arbitrary",)),
        name="experts",
    )(tile_expert, n_used, tile_valid, xs, w_gate, w_up, w_down)


def _combine_kernel(g_ref, u_ref, x1_ref, mod_ref, wsg_ref, wsu_ref, wsd_ref, gpost_ref, o_ref,
                    *, per_row, rows_per_batch):
    tm, d = u_ref.shape
    ub = u_ref[...].astype(BF16)
    hs = _silu(jnp.dot(ub, wsg_ref[...], preferred_element_type=F32)) * jnp.dot(
        ub, wsu_ref[...], preferred_element_type=F32)
    f = jnp.dot(hs.astype(BF16), wsd_ref[...], preferred_element_type=F32)
    routed = g_ref[0]
    for k in range(1, TOP_K):
        routed = routed + g_ref[k]
    f = routed + f
    (gate2,) = _mod_rows(mod_ref, per_row, rows_per_batch, tm, d)
    o_ref[...] = x1_ref[...] + gate2 * (_rms(f) * gpost_ref[...])


def _combine(g_rows, u2, x1, mod, consts, per_row, rows_per_batch, tile_off, n_tok, tm):
    t_all, d = u2.shape
    ds_ = consts["w_sh_gate"].shape[1]
    tok_spec = lambda w: pl.BlockSpec((tm, w), lambda i: (i + tile_off, 0))
    return pl.pallas_call(
        functools.partial(_combine_kernel, per_row=per_row, rows_per_batch=rows_per_batch),
        out_shape=jax.ShapeDtypeStruct((n_tok, d), F32),
        grid=(n_tok // tm,),
        in_specs=[pl.BlockSpec((TOP_K, tm, d), lambda i: (0, i + tile_off, 0)),
                  tok_spec(d), tok_spec(d), _mod_spec(mod, per_row, tm),
                  _const_spec((d, ds_)), _const_spec((d, ds_)), _const_spec((ds_, d)), _const_spec((1, d))],
        out_specs=pl.BlockSpec((tm, d), lambda i: (i, 0)),
        compiler_params=_params(("arbitrary",)),
        name="combine",
    )(g_rows.reshape(TOP_K, t_all, d), u2, x1, mod, consts["w_sh_gate"], consts["w_sh_up"],
      consts["w_sh_down"], consts["g_post_ffn"])


def _repeat_rows(x, n):
    r, c = x.shape
    return jnp.broadcast_to(x[:, None, :], (r, n, c)).reshape(r * n, c)


def _tile_major(pos_t, tile):
    k, t = pos_t.shape
    return pos_t.reshape(k, t // tile, tile).transpose(1, 0, 2).reshape(-1)


def kernel(x_prompt, x_sample, c_prompt, c_sample, state_ssm, state_ssd_conv, state_short_conv, w_ada, b_ada, g_pre_mix, g_post_mix, g_pre_ffn, g_post_ffn, w_in, ssd_conv_w, ssd_conv_b, dt_bias, a_log, d_skip, g_ssd_norm, sc_conv_w, g_sc_norm, w_out, w_router, router_bias, w_exp_gate, w_exp_up, w_exp_down, w_sh_gate, w_sh_up, w_sh_down):
    depth = w_ada.shape[0]
    bp, seq, d = x_prompt.shape
    bs, dseq, _ = x_sample.shape
    heads = dt_bias.shape[1]
    ssd_w = heads * SSD_HEAD_DIM
    conv_dim = ssd_conv_w.shape[2]
    sc_w = sc_conv_w.shape[2]
    ne = w_router.shape[2]
    tp, ts = bp * seq, bs * dseq
    t_all = tp + ts

    assert sc_w // SC_GROUPS == SSD_HEAD_DIM and heads == SC_GROUPS
    head_of = jnp.arange(ssd_w, dtype=I32) // SSD_HEAD_DIM
    e_ind = (jnp.arange(LANES, dtype=I32)[:, None] == head_of[None, :])

    xp = x_prompt.reshape(tp, d)
    xs_pad = jnp.pad(x_sample, ((0, 0), (0, SUBLANES - dseq), (0, 0))).reshape(bs * SUBLANES, d)
    xs_tok = x_sample.reshape(ts, d)
    outs = {k: [] for k in ("ssm_p", "cst_p", "scst_p", "ssm_s", "cst_s", "scst_s")}

    for layer in range(depth):
        cuts = np.cumsum([0, ssd_w, conv_dim, heads, sc_w, sc_w, sc_w]).tolist()
        wi = w_in[layer]
        seg = lambda k: wi[:, cuts[k]:cuts[k + 1]]
        w_in_r = jnp.concatenate([seg(0), seg(1), seg(3), seg(4), seg(5),
                                  jnp.pad(seg(2), ((0, 0), (0, LANES - heads)))], axis=1).astype(BF16)
        widths = (ssd_w, conv_dim, sc_w, sc_w, sc_w, LANES)
        pad_h = lambda v: jnp.pad(v.reshape(1, heads), ((0, 0), (0, LANES - heads)))
        wr = w_router[layer].T
        wr_hi = wr.astype(BF16)
        consts = dict(
            heads=heads,
            conv_w=ssd_conv_w[layer], conv_b=ssd_conv_b[layer].reshape(1, conv_dim),
            dt_bias=pad_h(dt_bias[layer]), a_log=pad_h(a_log[layer]),
            d_skip_e=jnp.broadcast_to(d_skip[layer][:, None], (heads, SSD_HEAD_DIM)).reshape(1, ssd_w),
            g_ssd=g_ssd_norm[layer].reshape(1, ssd_w), sc_w=sc_conv_w[layer],
            g_sc=g_sc_norm[layer].reshape(1, sc_w),
            e=e_ind.astype(BF16), et=e_ind.T.astype(F32),
            w_out=w_out[layer].astype(BF16), g_post_mix=g_post_mix[layer].reshape(1, d),
            g_pre_ffn=g_pre_ffn[layer].reshape(1, d),
            wr_hi_t=wr_hi, wr_lo_t=(wr - wr_hi.astype(F32)).astype(BF16),
            w_sh_gate=w_sh_gate[layer].astype(BF16), w_sh_up=w_sh_up[layer].astype(BF16),
            w_sh_down=w_sh_down[layer].astype(BF16), g_post_ffn=g_post_ffn[layer].reshape(1, d))

        c_all = jnp.concatenate([c_prompt, c_sample], axis=0)
        m_rows = -(-c_all.shape[0] // 16) * 16
        mod = _ada(jnp.pad(c_all, ((0, m_rows - c_all.shape[0]), (0, 0))), w_ada[layer], b_ada[layer])
        mod_p = mod[:bp]
        mod_s = mod[bp:bp + bs]
        mod_s_pad = _repeat_rows(mod_s[:, :2 * d], SUBLANES)
        mod_s_tok = _repeat_rows(mod_s, dseq)

        proj_s = _in_proj(xs_pad, mod_s_pad, g_pre_mix[layer], w_in_r, widths, F32, True, 1, TOK_TILE)
        ymix_p, ssm_p, cst_p, scst_p = _ssd(
            (xp, mod_p[:, :2 * d], g_pre_mix[layer].reshape(1, d), w_in_r), None, consts, bp, seq // CHUNK,
            CHUNK, CHUNK, CHUNK, BF16, widths=widths, n_sub=SCAN_SUB)
        init = (state_ssm[layer].reshape(bs, ssd_w, D_STATE), state_ssd_conv[layer], state_short_conv[layer])
        ymix_s, ssm_s, cst_s, scst_s = _ssd(proj_s, init, consts, bs, 1, SAMPLE_CHUNK, SUBLANES, dseq, F32)
        ymix_s = ymix_s.reshape(bs, SUBLANES, ssd_w + sc_w)[:, :dseq].reshape(ts, ssd_w + sc_w)

        merged = _out_proj(ymix_p, xp, mod_p[:, 2 * d:5 * d], consts, False, seq, TOK_TILE, t_all, 0, None)
        x1, u2, logits_t = _out_proj(ymix_s, xs_tok, mod_s_tok[:, 2 * d:5 * d], consts, True, 1, TOK_TILE,
                                     t_all, tp // TOK_TILE, merged)

        idx_t, rank_t, wtok, counts = _route(logits_t, router_bias[layer], TOK_TILE)
        counts = counts[:, 0]
        padded = (counts + ROW_TILE - 1) // ROW_TILE * ROW_TILE
        pad_end = jnp.cumsum(padded)
        pad_start = pad_end - padded
        n_tiles = -(-(t_all * TOP_K) // ROW_TILE) + ne
        n_used = (pad_end[-1] // ROW_TILE).astype(I32)
        tile_ids = jnp.minimum(jnp.arange(n_tiles, dtype=I32), n_used - 1)
        tile_expert = jnp.minimum(jnp.sum(pad_end[None, :] <= (tile_ids * ROW_TILE)[:, None], axis=1), ne - 1).astype(I32)
        first_pad = (pad_start + counts) // SUBLANES * SUBLANES
        units = (pad_end - first_pad) // SUBLANES
        tile_valid = jnp.clip(counts[tile_expert] - (tile_ids * ROW_TILE - pad_start[tile_expert]), 0,
                              ROW_TILE).astype(I32)
        pos_t, meta = _positions(idx_t, rank_t, pad_start.astype(I32), wtok, TOK_TILE)
        pos_flat = _tile_major(pos_t, GATHER_TILE)

        xs_rows = _dispatch(pos_flat, u2, meta, n_tiles * ROW_TILE, GATHER_TILE)
        g_rows = _experts(tile_expert, n_used.reshape(1), tile_valid, xs_rows, w_exp_gate[layer],
                          w_exp_up[layer], w_exp_down[layer], t_all)
        xp = _combine(g_rows, u2, x1, mod_p[:, 5 * d:], consts, False, seq, 0, tp, COMBINE_TILE)
        xs_tok = _combine(g_rows, u2, x1, mod_s_tok[:, 5 * d:], consts, True, 1, tp // COMBINE_TILE, ts,
                          COMBINE_TILE)
        xs_pad = jnp.pad(xs_tok.reshape(bs, dseq, d), ((0, 0), (0, SUBLANES - dseq), (0, 0))).reshape(
            bs * SUBLANES, d)

        outs["ssm_p"].append(ssm_p.reshape(bp, heads, SSD_HEAD_DIM, D_STATE))
        outs["cst_p"].append(cst_p)
        outs["scst_p"].append(scst_p)
        outs["ssm_s"].append(ssm_s.reshape(bs, heads, SSD_HEAD_DIM, D_STATE))
        outs["cst_s"].append(cst_s)
        outs["scst_s"].append(scst_s)

    return (xp.reshape(bp, seq, d), xs_tok.reshape(bs, dseq, d),
            jnp.stack(outs["ssm_p"]), jnp.stack(outs["cst_p"]), jnp.stack(outs["scst_p"]),
            jnp.stack(outs["ssm_s"]), jnp.stack(outs["cst_s"]), jnp.stack(outs["scst_s"]))
```

```python
import functools

import jax
import jax.numpy as jnp
import numpy as np
from jax import lax
from jax.experimental import pallas as pl
from jax.experimental.pallas import tpu as pltpu

F32 = jnp.float32
BF16 = jnp.bfloat16
I32 = jnp.int32
U32 = jnp.uint32
HI = lax.Precision.HIGHEST

SSD_HEAD_DIM = 64
SSD_GROUPS = 2
D_STATE = 128
SSD_CONV_W = 4
SC_GROUPS = 16
SC_CONV_W = 3
TOP_K = 8
N_EXPERT_GROUPS = 8
TOPK_GROUPS = 4
ROUTED_SCALE = 2.5
EPS = 1e-6

LANES = 128
SUBLANES = 8
CHUNK = 128
SAMPLE_CHUNK = 16
SCAN_SUB = 4
TOK_TILE = 512
ROW_TILE = 576
GATHER_TILE = 512
COMBINE_TILE = 256
VMEM_LIMIT = 56 * 1024 * 1024
DMA_THREADS = 2

NT = (((1,), (1,)), ((), ()))
TN = (((0,), (0,)), ((), ()))


def _sigmoid(x):
    return 1.0 / (1.0 + jnp.exp(-x))


def _silu(x):
    return x * _sigmoid(x)


def _softplus(x):
    return jnp.maximum(x, 0.0) + jnp.log1p(jnp.exp(-jnp.abs(x)))


def _rms(x, eps=EPS):
    return x * lax.rsqrt(jnp.mean(x * x, axis=-1, keepdims=True) + eps)


def _params(sem=None):
    return pltpu.CompilerParams(dimension_semantics=sem, vmem_limit_bytes=VMEM_LIMIT)


def _const_spec(shape, single=False):
    nd = len(shape)
    mode = dict(pipeline_mode=pl.Buffered(1)) if single else {}
    return pl.BlockSpec(shape, lambda *_: (0,) * nd, **mode)


def _ada_kernel(c_ref, w_ref, b_ref, o_ref):
    c = c_ref[...]
    s = _silu(c).astype(BF16)
    o_ref[...] = jnp.dot(s, w_ref[...].astype(BF16), preferred_element_type=F32) + b_ref[...]


def _ada(c, w_ada, b_ada):
    m, d = c.shape
    n = w_ada.shape[1]
    tn = 512
    return pl.pallas_call(
        _ada_kernel,
        out_shape=jax.ShapeDtypeStruct((m, n), F32),
        grid=(n // tn,),
        in_specs=[_const_spec((m, d)),
                  pl.BlockSpec((d, tn), lambda j: (0, j)),
                  pl.BlockSpec((1, tn), lambda j: (0, j))],
        out_specs=pl.BlockSpec((m, tn), lambda j: (0, j)),
        compiler_params=_params(("arbitrary",)),
        name="ada",
    )(c, w_ada, b_ada.reshape(1, n))


def _mod_rows(mod_ref, per_row, rows_per_batch, tile, d):
    n = mod_ref.shape[1] // d
    if per_row:
        return [mod_ref[:, k * d:(k + 1) * d] for k in range(n)]
    b = (pl.program_id(0) * tile) // rows_per_batch
    return [mod_ref[pl.ds(b, 1), k * d:(k + 1) * d] for k in range(n)]


def _mod_spec(mod, per_row, tile):
    if per_row:
        return pl.BlockSpec((tile, mod.shape[1]), lambda i: (i, 0))
    return _const_spec(mod.shape)


def _in_kernel(x_ref, mod_ref, g_ref, w_ref, *out_refs, per_row, rows_per_batch, widths):
    tile, d = x_ref.shape
    shift, scale = _mod_rows(mod_ref, per_row, rows_per_batch, tile, d)
    u = (_rms(x_ref[...]) * g_ref[...]) * (1.0 + scale) + shift
    u = u.astype(BF16)
    col = 0
    for ref, width in zip(out_refs, widths):
        for a in range(0, width, 512):
            bw = min(512, width - a)
            r = jnp.dot(u, w_ref[:, col + a:col + a + bw], preferred_element_type=F32)
            ref[:, a:a + bw] = r.astype(ref.dtype)
        col += width


def _in_proj(x, mod, g, w_bf16, widths, out_dtype, per_row, rows_per_batch, tile):
    t, d = x.shape
    n = w_bf16.shape[1]
    mod_spec = _mod_spec(mod, per_row, tile)
    dts = [out_dtype] * (len(widths) - 1) + [F32]
    return pl.pallas_call(
        functools.partial(_in_kernel, per_row=per_row, rows_per_batch=rows_per_batch, widths=widths),
        out_shape=[jax.ShapeDtypeStruct((t, wd), dt) for wd, dt in zip(widths, dts)],
        grid=(t // tile,),
        in_specs=[pl.BlockSpec((tile, d), lambda i: (i, 0)), mod_spec,
                  _const_spec((1, d)), _const_spec((d, n), single=True)],
        out_specs=[pl.BlockSpec((tile, wd), lambda i: (i, 0)) for wd in widths],
        compiler_params=_params(("arbitrary",)),
        name="in_proj",
    )(x, mod, g.reshape(1, d), w_bf16)


def _ssd_kernel(*refs, q, rows_in, q_valid, has_init, heads, widths, n_sub):
    it = iter(refs)
    if widths is None:
        z_ref, xbc_ref, scb_ref, scc_ref, sch_ref, dt_ref = (next(it) for _ in range(6))
    else:
        x_ref, mod_ref, gpre_ref, win_ref = (next(it) for _ in range(4))
    if has_init:
        ssm0_ref, cst0_ref, scst0_ref = (next(it) for _ in range(3))
    (cw_ref, cb_ref, dtb_ref, alog_ref, dsk_ref, gssd_ref, scw_ref, gsc_ref,
     tril_ref, e_ref, et_ref) = (next(it) for _ in range(11))
    ymix_ref, ssm_ref, cst_ref, scst_ref = (next(it) for _ in range(4))
    h_scr, ext_scr, extv_scr = (next(it) for _ in range(3))

    c = pl.program_id(1)
    nc = pl.num_programs(1)
    ssd_w = dsk_ref.shape[1]
    gw = ssd_w // SSD_GROUPS
    hpg = heads // SSD_GROUPS
    n_state = D_STATE
    head0 = SUBLANES - (SSD_CONV_W - 1)
    headv = SUBLANES - (SC_CONV_W - 1)

    @pl.when(c == 0)
    def _():
        if has_init:
            h_scr[...] = ssm0_ref[0]
            ext_scr[head0:SUBLANES, :] = cst0_ref[0]
            extv_scr[headv:SUBLANES, :] = scst0_ref[0]
        else:
            h_scr[...] = jnp.zeros(h_scr.shape, F32)
            ext_scr[0:SUBLANES, :] = jnp.zeros((SUBLANES, ext_scr.shape[1]), F32)
            extv_scr[0:SUBLANES, :] = jnp.zeros((SUBLANES, extv_scr.shape[1]), F32)

    def pad_rows(v, n):
        if v.shape[0] == n:
            return v
        return jnp.concatenate([v, jnp.zeros((n - v.shape[0], v.shape[1]), v.dtype)], axis=0)

    load = lambda ref: pad_rows(ref[...].astype(F32), q)
    pad_t = lambda v: pad_rows(v, LANES)
    if widths is None:
        z, xbc, scb, scc, sch, dt_raw = (load(r) for r in (z_ref, xbc_ref, scb_ref, scc_ref, sch_ref, dt_ref))
    else:
        d_model = x_ref.shape[1]
        b = pl.program_id(0)
        shift = mod_ref[pl.ds(b, 1), 0:d_model]
        scale = mod_ref[pl.ds(b, 1), d_model:2 * d_model]
        u = ((_rms(x_ref[...]) * gpre_ref[...]) * (1.0 + scale) + shift).astype(BF16)
        pieces, col = [], 0
        for width in widths:
            parts = [jnp.dot(u, win_ref[:, col + a:col + a + min(512, width - a)], preferred_element_type=F32)
                     for a in range(0, width, 512)]
            pieces.append(parts[0] if len(parts) == 1 else jnp.concatenate(parts, axis=1))
            col += width
        z, xbc, scb, scc, sch = (p.astype(BF16).astype(F32) for p in pieces[:5])
        dt_raw = pieces[5]

    def scan_chunk(j, z, xbc, scb, scc, sch, dt_raw):
        ext_scr[SUBLANES:SUBLANES + q, :] = xbc
        conv = cb_ref[...]
        for k in range(SSD_CONV_W):
            conv = conv + cw_ref[k:k + 1, :] * ext_scr[head0 + k:head0 + k + q, :]
        xc = _silu(conv)
        xs = xc[:, :ssd_w]
        bm = xc[:, ssd_w:ssd_w + SSD_GROUPS * n_state]
        cm = xc[:, ssd_w + SSD_GROUPS * n_state:]

        lane = lax.broadcasted_iota(I32, (q, LANES), 1)
        row = lax.broadcasted_iota(I32, (q, LANES), 0)
        dt = _softplus(dt_raw + dtb_ref[...])
        dt = jnp.where(jnp.logical_and(lane < heads, row < q_valid), dt, 0.0)
        a = dt * (-jnp.exp(alog_ref[...]))
        acum = jnp.dot(tril_ref[...], pad_t(a), precision=HI, preferred_element_type=F32)
        acum_t = pad_t(acum).T
        a_last = acum[q - 1:q, :]
        e = e_ref[...]
        expand = lambda v: jnp.dot(v.astype(BF16), e, preferred_element_type=F32)
        dt_e = expand(dt)
        dtdte_e = expand(dt * jnp.exp(a_last - acum))
        exa_e = expand(jnp.exp(acum))
        xdt = xs * dt_e
        xw_b = (xs * dtdte_e).astype(BF16)
        dlast = jnp.broadcast_to(jnp.exp(acum_t[:, q - 1:q]), (LANES, n_state))
        et_b = et_ref[...].astype(BF16)
        dcol, rest = None, dlast
        for _ in range(3):
            term = rest.astype(BF16)
            part = jnp.dot(et_b, term, preferred_element_type=F32)
            dcol = part if dcol is None else dcol + part
            rest = rest - term.astype(F32)

        tri = row >= lane
        xdt_t = pad_t(xdt)
        xw_t = pad_t(xw_b)
        lane_t = lax.broadcasted_iota(I32, (LANES, LANES), 1)
        y_groups = []
        for g in range(SSD_GROUPS):
            bm_g = bm[:, g * n_state:(g + 1) * n_state].astype(BF16)
            cm_g = cm[:, g * n_state:(g + 1) * n_state].astype(BF16)
            bm_t = pad_t(bm_g)
            cb = lax.dot_general(cm_g, bm_t, NT, preferred_element_type=F32)
            h_g = h_scr[g * gw:(g + 1) * gw, :]
            y_off = lax.dot_general(cm_g, h_g.astype(BF16), NT, preferred_element_type=F32)
            parts = []
            for pair in range(hpg // 2):
                lo = (g * hpg + 2 * pair) * SSD_HEAD_DIM
                x_pair = xdt_t[:, lo:lo + LANES]
                acc = None
                for half in range(2):
                    h = g * hpg + 2 * pair + half
                    ci = jnp.broadcast_to(acum[:, h:h + 1], (q, LANES))
                    rj = jnp.broadcast_to(acum_t[h:h + 1, :], (q, LANES))
                    dec = jnp.where(tri, jnp.exp(ci - rj), 0.0)
                    m = (cb * dec).astype(BF16)
                    own = (lane_t >= SSD_HEAD_DIM) if half else (lane_t < SSD_HEAD_DIM)
                    y_h = jnp.dot(m, jnp.where(own, x_pair, 0.0).astype(BF16), preferred_element_type=F32)
                    acc = y_h if acc is None else acc + y_h
                parts.append(acc)
            y_diag = jnp.concatenate(parts, axis=1)
            y_groups.append(y_diag + y_off * exa_e[:, g * gw:(g + 1) * gw])
            upd = lax.dot_general(xw_t[:, g * gw:(g + 1) * gw], bm_t, TN, preferred_element_type=F32)
            h_scr[g * gw:(g + 1) * gw, :] = h_g * dcol[g * gw:(g + 1) * gw, :] + upd

        y = (jnp.concatenate(y_groups, axis=1) + xs * dsk_ref[...]) * _silu(z)
        y = jnp.concatenate([_rms(y[:, g * gw:(g + 1) * gw]) for g in range(SSD_GROUPS)], axis=1)
        y_ssd = y * gssd_ref[...]

        extv_scr[SUBLANES:SUBLANES + q, :] = scc * sch
        cv = scw_ref[0:1, :] * extv_scr[headv:headv + q, :]
        for k in range(1, SC_CONV_W):
            cv = cv + scw_ref[k:k + 1, :] * extv_scr[headv + k:headv + k + q, :]
        t = scb * cv
        sc_per_group = t.shape[1] // SC_GROUPS
        gsum = jnp.dot((t * t).astype(BF16), et_ref[...].astype(BF16), preferred_element_type=F32)
        rs = lax.rsqrt(gsum * (1.0 / sc_per_group) + EPS)
        y_sc = t * expand(rs) * gsc_ref[...]

        out_rows = slice(j * rows_in, (j + 1) * rows_in)
        ymix_ref[out_rows, :ssd_w] = y_ssd[0:rows_in].astype(ymix_ref.dtype)
        ymix_ref[out_rows, ssd_w:] = y_sc[0:rows_in].astype(ymix_ref.dtype)

        if j == n_sub - 1:
            @pl.when(c == nc - 1)
            def _():
                ssm_ref[0] = h_scr[...]
                cst_ref[0] = ext_scr[SUBLANES + q_valid - (SSD_CONV_W - 1):SUBLANES + q_valid, :]
                scst_ref[0] = extv_scr[SUBLANES + q_valid - (SC_CONV_W - 1):SUBLANES + q_valid, :]

        ext_scr[0:SUBLANES, :] = ext_scr[q:q + SUBLANES, :]
        extv_scr[0:SUBLANES, :] = extv_scr[q:q + SUBLANES, :]

    for j in range(n_sub):
        scan_chunk(j, *(p[j * q:(j + 1) * q] for p in (z, xbc, scb, scc, sch, dt_raw)))


def _ssd(proj, init, consts, nb, n_chunks, q, rows_in, q_valid, ymix_dtype, widths=None, n_sub=1):
    heads = consts["heads"]
    tril = jnp.tril(jnp.ones((q, LANES), F32))
    ssd_w, conv_dim, sc_w = consts["g_ssd"].shape[1], consts["conv_w"].shape[1], consts["g_sc"].shape[1]
    has_init = init is not None
    assert n_chunks % n_sub == 0 and (n_sub == 1 or rows_in == q)
    nc = n_chunks // n_sub
    row_spec = lambda w: pl.BlockSpec((n_sub * rows_in, w), lambda b, c: (b * nc + c, 0))
    if widths is None:
        in_specs = [row_spec(ssd_w), row_spec(conv_dim), row_spec(sc_w), row_spec(sc_w), row_spec(sc_w),
                    row_spec(LANES)]
    else:
        x, mod, g, w = proj
        in_specs = [row_spec(x.shape[1]), _const_spec(mod.shape), _const_spec(g.shape),
                    _const_spec(w.shape, single=True)]
    args = list(proj)
    if has_init:
        ssm0, cst0, scst0 = init
        in_specs += [pl.BlockSpec((1,) + ssm0.shape[1:], lambda b, c: (b, 0, 0)),
                     pl.BlockSpec((1,) + cst0.shape[1:], lambda b, c: (b, 0, 0)),
                     pl.BlockSpec((1,) + scst0.shape[1:], lambda b, c: (b, 0, 0))]
        args += [ssm0, cst0, scst0]
    weights = [consts[k] for k in ("conv_w", "conv_b", "dt_bias", "a_log", "d_skip_e", "g_ssd",
                                   "sc_w", "g_sc")] + [tril, consts["e"], consts["et"]]
    in_specs += [_const_spec(w.shape) for w in weights]
    args += weights
    n_state = D_STATE
    out_shape = [jax.ShapeDtypeStruct((nb * n_chunks * rows_in, ssd_w + sc_w), ymix_dtype),
                 jax.ShapeDtypeStruct((nb, ssd_w, n_state), F32),
                 jax.ShapeDtypeStruct((nb, SSD_CONV_W - 1, conv_dim), F32),
                 jax.ShapeDtypeStruct((nb, SC_CONV_W - 1, sc_w), F32)]
    out_specs = [pl.BlockSpec((n_sub * rows_in, ssd_w + sc_w), lambda b, c: (b * nc + c, 0)),
                 pl.BlockSpec((1, ssd_w, n_state), lambda b, c: (b, 0, 0)),
                 pl.BlockSpec((1, SSD_CONV_W - 1, conv_dim), lambda b, c: (b, 0, 0)),
                 pl.BlockSpec((1, SC_CONV_W - 1, sc_w), lambda b, c: (b, 0, 0))]
    scratch = [pltpu.VMEM((ssd_w, n_state), F32),
               pltpu.VMEM((q + SUBLANES, conv_dim), F32),
               pltpu.VMEM((q + SUBLANES, sc_w), F32)]
    return pl.pallas_call(
        functools.partial(_ssd_kernel, q=q, rows_in=rows_in, q_valid=q_valid, has_init=has_init, heads=heads,
                          widths=widths, n_sub=n_sub),
        out_shape=out_shape, grid=(nb, nc), in_specs=in_specs, out_specs=out_specs,
        scratch_shapes=scratch,
        compiler_params=_params(("arbitrary", "arbitrary")),
        name="ssd",
    )(*args)


def _out_kernel(ymix_ref, x_ref, mod_ref, wout_ref, gpost_ref, gpre_ref, wr_hi_ref, wr_lo_ref,
                *rest, per_row, rows_per_batch, has_alias):
    x1_ref, u2_ref, lg_ref = rest[3:6] if has_alias else rest[0:3]
    tile, d = x_ref.shape
    gate1, shift2, scale2 = _mod_rows(mod_ref, per_row, rows_per_batch, tile, d)
    m = jnp.dot(ymix_ref[...].astype(BF16), wout_ref[...], preferred_element_type=F32)
    x1 = x_ref[...] + gate1 * (_rms(m) * gpost_ref[...])
    u2 = (_rms(x1) * gpre_ref[...]) * (1.0 + scale2) + shift2
    x1_ref[...] = x1
    u2_ref[...] = u2
    u_hi = u2.astype(BF16)
    u_lo = (u2 - u_hi.astype(F32)).astype(BF16)
    lg = lax.dot_general(wr_hi_ref[...], u_hi, NT, preferred_element_type=F32)
    lg = lg + lax.dot_general(wr_hi_ref[...], u_lo, NT, preferred_element_type=F32)
    lg = lg + lax.dot_general(wr_lo_ref[...], u_hi, NT, preferred_element_type=F32)
    lg_ref[...] = lg


def _out_proj(ymix, x, mod, consts, per_row, rows_per_batch, tile, t_total, tile_off, prev):
    t, d = x.shape
    dm = ymix.shape[1]
    ne = consts["wr_hi_t"].shape[0]
    in_specs = [pl.BlockSpec((tile, dm), lambda i: (i, 0)),
                pl.BlockSpec((tile, d), lambda i: (i, 0)), _mod_spec(mod, per_row, tile),
                _const_spec((dm, d)), _const_spec((1, d)), _const_spec((1, d)),
                _const_spec((ne, d)), _const_spec((ne, d))]
    args = [ymix, x, mod, consts["w_out"], consts["g_post_mix"], consts["g_pre_ffn"],
            consts["wr_hi_t"], consts["wr_lo_t"]]
    aliases = {}
    if prev is not None:
        in_specs += [pl.BlockSpec(memory_space=pl.ANY)] * 3
        aliases = {len(args) + k: k for k in range(3)}
        args += list(prev)
    return pl.pallas_call(
        functools.partial(_out_kernel, per_row=per_row, rows_per_batch=rows_per_batch,
                          has_alias=prev is not None),
        out_shape=[jax.ShapeDtypeStruct((t_total, d), F32), jax.ShapeDtypeStruct((t_total, d), F32),
                   jax.ShapeDtypeStruct((ne, t_total), F32)],
        grid=(t // tile,), in_specs=in_specs,
        out_specs=[pl.BlockSpec((tile, d), lambda i: (i + tile_off, 0)),
                   pl.BlockSpec((tile, d), lambda i: (i + tile_off, 0)),
                   pl.BlockSpec((ne, tile), lambda i: (0, i + tile_off))],
        input_output_aliases=aliases,
        compiler_params=_params(("arbitrary",)),
        name="out_proj",
    )(*args)


def _route_kernel(lg_ref, bias_ref, upper_ref, idx_ref, rank_ref, wtok_ref, cnt_ref, carry_scr):
    i = pl.program_id(0)
    ne, tm = lg_ref.shape
    per_group = ne // N_EXPERT_GROUPS
    neg = -jnp.inf

    @pl.when(i == 0)
    def _():
        carry_scr[...] = jnp.zeros(carry_scr.shape, F32)

    s = _sigmoid(lg_ref[...])
    biased = s + bias_ref[...]
    gl = []
    io_g = lax.broadcasted_iota(I32, (per_group, tm), 0).astype(F32)
    for g in range(N_EXPERT_GROUPS):
        blk = biased[g * per_group:(g + 1) * per_group, :]
        m1 = jnp.max(blk, axis=0, keepdims=True)
        f1 = jnp.min(jnp.where(blk == m1, io_g, float(per_group)), axis=0, keepdims=True)
        m2 = jnp.max(jnp.where(io_g == f1, neg, blk), axis=0, keepdims=True)
        gl.append(m1 + m2)
    gscore = jnp.concatenate(gl, axis=0)
    io8 = lax.broadcasted_iota(I32, (N_EXPERT_GROUPS, tm), 0).astype(F32)
    gsel = jnp.zeros((N_EXPERT_GROUPS, tm), F32)
    for _ in range(TOPK_GROUPS):
        m = jnp.max(gscore, axis=0, keepdims=True)
        f = jnp.min(jnp.where(gscore == m, io8, float(N_EXPERT_GROUPS)), axis=0, keepdims=True)
        hit = io8 == f
        gsel = jnp.where(hit, 1.0, gsel)
        gscore = jnp.where(hit, neg, gscore)
    emask = jnp.concatenate(
        [jnp.broadcast_to(gsel[g:g + 1, :], (per_group, tm)) for g in range(N_EXPERT_GROUPS)], axis=0)
    cand = jnp.where(emask > 0.5, biased, neg)
    io_e = lax.broadcasted_iota(I32, (ne, tm), 0).astype(F32)
    msel = jnp.zeros((ne, tm), F32)
    idxs, wts = [], []
    for _ in range(TOP_K):
        m = jnp.max(cand, axis=0, keepdims=True)
        f = jnp.min(jnp.where(cand == m, io_e, float(ne)), axis=0, keepdims=True)
        hit = io_e == f
        wts.append(jnp.sum(jnp.where(hit, s, 0.0), axis=0, keepdims=True))
        idxs.append(f)
        msel = jnp.where(hit, 1.0, msel)
        cand = jnp.where(hit, neg, cand)
    pref = jnp.dot(msel.astype(BF16), upper_ref[...], preferred_element_type=F32) + carry_scr[:, 0:1]
    ranks = [jnp.sum(jnp.where(io_e == f, pref, 0.0), axis=0, keepdims=True) for f in idxs]
    carry_scr[...] = carry_scr[...] + jnp.sum(msel, axis=1, keepdims=True)
    cnt_ref[...] = carry_scr[...].astype(I32)
    idx_ref[...] = jnp.concatenate(idxs, axis=0).astype(I32)
    rank_ref[...] = jnp.concatenate(ranks, axis=0).astype(I32)
    wsum = wts[0]
    for w in wts[1:]:
        wsum = wsum + w
    wn = jnp.concatenate([w / wsum * ROUTED_SCALE for w in wts]
                         + [jnp.zeros((LANES - TOP_K, tm), F32)], axis=0)
    for j in range(tm // LANES):
        wtok_ref[j * LANES:(j + 1) * LANES, :] = wn[:, j * LANES:(j + 1) * LANES].T


def _route(logits_t, bias, tile):
    ne, t = logits_t.shape
    upper = jnp.triu(jnp.ones((tile, tile), F32), 1).astype(BF16)
    return pl.pallas_call(
        _route_kernel,
        out_shape=[jax.ShapeDtypeStruct((TOP_K, t), I32), jax.ShapeDtypeStruct((TOP_K, t), I32),
                   jax.ShapeDtypeStruct((t, LANES), F32), jax.ShapeDtypeStruct((ne, LANES), I32)],
        grid=(t // tile,),
        in_specs=[pl.BlockSpec((ne, tile), lambda i: (0, i)), _const_spec((ne, 1)),
                  _const_spec((tile, tile))],
        out_specs=[pl.BlockSpec((TOP_K, tile), lambda i: (0, i)),
                   pl.BlockSpec((TOP_K, tile), lambda i: (0, i)),
                   pl.BlockSpec((tile, LANES), lambda i: (i, 0)),
                   _const_spec((ne, LANES))],
        scratch_shapes=[pltpu.VMEM((ne, LANES), F32)],
        compiler_params=_params(("arbitrary",)),
        name="route",
    )(logits_t, bias.reshape(ne, 1), upper)


META_POS, META_W, META_TOK = 0, TOP_K, 2 * TOP_K


def _pos_kernel(idx_ref, rank_ref, start_ref, wtok_ref, pos_ref, meta_ref):
    ne = start_ref.shape[0]
    tm = idx_ref.shape[1]
    io_e = lax.broadcasted_iota(I32, (ne, tm), 0)
    start = start_ref[...].astype(F32)
    rows = []
    for k in range(TOP_K):
        hit = io_e == idx_ref[k:k + 1, :]
        rows.append(jnp.sum(jnp.where(hit, start, 0.0), axis=0, keepdims=True))
    pos_f = jnp.concatenate(rows, axis=0) + rank_ref[...].astype(F32)
    pos_ref[...] = pos_f.astype(I32)
    pos_pad = jnp.concatenate([pos_f, jnp.zeros((LANES - TOP_K, tm), F32)], axis=0)
    lane = lax.broadcasted_iota(I32, (LANES, LANES), 1)
    row = lax.broadcasted_iota(I32, (LANES, LANES), 0)
    for j in range(tm // LANES):
        pos_tok = pos_pad[:, j * LANES:(j + 1) * LANES].T
        w_tok = pltpu.roll(wtok_ref[j * LANES:(j + 1) * LANES, :], META_W, axis=1)
        tok = (row + (pl.program_id(0) * tm + j * LANES)).astype(F32)
        tag = jnp.where(lane < META_W, pos_tok, jnp.where(lane == META_TOK, tok, w_tok))
        meta_ref[j * LANES:(j + 1) * LANES, :] = tag


def _positions(idx_t, rank_t, pad_start, wtok, tile):
    k, t = idx_t.shape
    ne = pad_start.shape[0]
    return pl.pallas_call(
        _pos_kernel,
        out_shape=[jax.ShapeDtypeStruct((k, t), I32), jax.ShapeDtypeStruct((t, LANES), F32)],
        grid=(t // tile,),
        in_specs=[pl.BlockSpec((k, tile), lambda i: (0, i)), pl.BlockSpec((k, tile), lambda i: (0, i)),
                  _const_spec((ne, 1)), pl.BlockSpec((tile, LANES), lambda i: (i, 0))],
        out_specs=[pl.BlockSpec((k, tile), lambda i: (0, i)), pl.BlockSpec((tile, LANES), lambda i: (i, 0))],
        compiler_params=_params(("arbitrary",)),
        name="positions",
    )(idx_t, rank_t, pad_start.reshape(ne, 1), wtok)


ZERO_BITS = (64, 32, 16, 8, 4, 2, 1)
assert sum(ZERO_BITS) * SUBLANES >= ROW_TILE


def _zero_copy(zbuf, xs_ref, sem, off, bit):
    rows = bit * SUBLANES
    return pltpu.make_async_copy(zbuf.at[pl.ds(0, rows)], xs_ref.at[pl.ds(pl.multiple_of(off, SUBLANES), rows)], sem)


def _zero_kernel(first_ref, units_ref, xs_ref, zbuf, sem):
    zbuf[...] = jnp.zeros(zbuf.shape, zbuf.dtype)
    ne = first_ref.shape[0]

    def each(e, start):
        off = first_ref[e]
        units = units_ref[e]
        for bit in ZERO_BITS:
            on = (units & bit) != 0

            @pl.when(on)
            def _():
                cp = _zero_copy(zbuf, xs_ref, sem, off, bit)
                cp.start() if start else cp.wait()
            off = off + jnp.where(on, bit * SUBLANES, 0)
        return start

    lax.fori_loop(0, ne, lambda e, c: (each(e, True), c)[1], 0)
    lax.fori_loop(0, ne, lambda e, c: (each(e, False), c)[1], 0)


def _zero_rows(first, units, n_rows, width):
    return pl.pallas_call(
        _zero_kernel,
        out_shape=jax.ShapeDtypeStruct((n_rows, width), F32),
        in_specs=[pl.BlockSpec(memory_space=pltpu.SMEM), pl.BlockSpec(memory_space=pltpu.SMEM)],
        out_specs=pl.BlockSpec(memory_space=pl.ANY),
        scratch_shapes=[pltpu.VMEM((ZERO_BITS[0] * SUBLANES, width), F32), pltpu.SemaphoreType.DMA],
        compiler_params=_params(),
        name="zero_rows",
    )(first, units)


def _row_copy(src_ref, dst_ref, sem, s, d):
    return pltpu.make_async_copy(src_ref.at[pl.ds(s, 1)], dst_ref.at[pl.ds(d, 1)], sem)


def _dispatch_kernel(pos_hbm, u_ref, meta_ref, xs_ref, pos_a, pos_b, src, psem, sem, *, nt):
    i = pl.program_id(0)
    tm, d = u_ref.shape
    per = pos_a.shape[0]
    pos_slots = (pos_a, pos_b)

    def pos_copy(tile, s):
        return pltpu.make_async_copy(pos_hbm.at[pl.ds(pl.multiple_of(tile * per, per), per)], pos_slots[s],
                                     psem.at[s])

    def drain(s):
        for _ in range(TOP_K):
            pltpu.make_async_copy(src.at[s], xs_ref.at[pl.ds(0, tm)], sem.at[s]).wait()

    def step(s):
        @pl.when(i == 0)
        def _():
            pos_copy(0, 0).start()

        @pl.when(i + 1 < nt)
        def _():
            pos_copy(i + 1, 1 - s).start()

        @pl.when(i >= 2)
        def _():
            drain(s)

        src[s, :, :d] = u_ref[...]
        src[s, :, d:] = meta_ref[...]
        pos_copy(i, s).wait()

        def issue(t, carry):
            for k in range(TOP_K):
                _row_copy(src.at[s], xs_ref, sem.at[s], t, pos_slots[s][k * tm + t]).start(
                    priority=k % DMA_THREADS)
            return carry

        lax.fori_loop(0, tm, issue, 0)

        @pl.when(i == nt - 1)
        def _():
            if nt >= 2:
                drain(1 - s)
            drain(s)

    for s in range(2):
        pl.when(i % 2 == s)(functools.partial(step, s))


def _dispatch(pos_flat, u2, meta, n_rows, tm):
    per = TOP_K * tm
    nt = pos_flat.shape[0] // per
    d = u2.shape[1]
    return pl.pallas_call(
        functools.partial(_dispatch_kernel, nt=nt),
        out_shape=jax.ShapeDtypeStruct((n_rows, d + LANES), F32),
        grid=(nt,),
        in_specs=[pl.BlockSpec(memory_space=pl.ANY), pl.BlockSpec((tm, d), lambda i: (i, 0)),
                  pl.BlockSpec((tm, LANES), lambda i: (i, 0))],
        out_specs=pl.BlockSpec(memory_space=pl.ANY),
        scratch_shapes=[pltpu.SMEM((per,), I32), pltpu.SMEM((per,), I32),
                        pltpu.VMEM((2, tm, d + LANES), F32),
                        pltpu.SemaphoreType.DMA((2,)), pltpu.SemaphoreType.DMA((2,))],
        compiler_params=_params(("arbitrary",)),
        name="dispatch",
    )(pos_flat, u2, meta)


WAIT_BITS = tuple(1 << b for b in reversed(range(ROW_TILE.bit_length())))


def _expert_kernel(first_ref, ntile_ref, nu_ref, nv_ref, xs_hbm, wg_ref, wu_ref, wd_ref, g_hbm,
                   xbuf, ybuf, dst_vmem, dst_smem, sem_in, sem, dsem, *, n_tok):
    e = pl.program_id(0)
    ne = pl.num_programs(0)
    nu = nu_ref[0]
    n_own = ntile_ref[e]
    t0 = first_ref[e]
    d = ybuf.shape[3]
    rows = ybuf.shape[1] * SUBLANES
    xslot = e % 2

    def x_copy(tile, slot):
        return pltpu.make_async_copy(xs_hbm.at[pl.ds(pl.multiple_of(tile * rows, SUBLANES), rows)],
                                     xbuf.at[slot], sem_in.at[slot])

    def drain(tile):
        slot = tile % 2
        n = nv_ref[tile]
        for bit in WAIT_BITS:
            @pl.when((n & bit) != 0)
            def _():
                pltpu.make_async_copy(g_hbm.at[pl.ds(0, bit)], g_hbm.at[pl.ds(0, bit)], sem.at[slot]).wait()

    @pl.when(jnp.logical_and(e == 0, n_own > 0))
    def _():
        x_copy(t0, 0).start()

    nxt = jnp.minimum(e + 1, ne - 1)

    @pl.when(jnp.logical_and(e + 1 < ne, ntile_ref[nxt] > 0))
    def _():
        x_copy(first_ref[nxt], 1 - xslot).start()

    def tile_step(j, carry):
        i = t0 + j
        slot = i % 2

        @pl.when(j > 0)
        def _():
            x_copy(i, xslot).start()

        x_copy(i, xslot).wait()

        @pl.when(i >= 2)
        def _():
            drain(i - 2)

        dst_vmem[...] = jnp.zeros(dst_vmem.shape, I32)
        xfull = xbuf[xslot]
        tag = xfull[:, d:]
        x = xfull[:, :d].astype(BF16)
        lane = lax.broadcasted_iota(I32, (rows, LANES), 1)
        p = (lax.broadcasted_iota(I32, (rows, LANES), 0) + i * rows).astype(F32)
        hit = jnp.logical_and(tag == p, lane < META_W)
        w_at = pltpu.roll(tag, LANES - META_W, axis=1)
        w = jnp.sum(jnp.where(hit, w_at, 0.0), axis=1, keepdims=True)
        kf = jnp.sum(jnp.where(hit, lane.astype(F32), 0.0), axis=1, keepdims=True)
        dest = kf * float(n_tok) + tag[:, META_TOK:META_TOK + 1]

        dest_b = jnp.broadcast_to(dest, (rows, LANES))
        n_blk = -(-rows // LANES)
        dest_b = jnp.concatenate([dest_b, jnp.zeros((n_blk * LANES - rows, LANES), F32)], axis=0)
        for b in range(n_blk):
            dst_vmem[b:b + 1, :] = dest_b[b * LANES:(b + 1) * LANES, :].T[0:1, :].astype(I32)
        to_smem = [pltpu.make_async_copy(dst_vmem.at[b], dst_smem.at[pl.ds(b * LANES, LANES)], dsem)
                   for b in range(n_blk)]
        for cp in to_smem:
            cp.start()

        g = jnp.dot(x, wg_ref[0].astype(BF16), preferred_element_type=F32)
        u = jnp.dot(x, wu_ref[0].astype(BF16), preferred_element_type=F32)
        h = (_silu(g) * u).astype(BF16)
        y = jnp.dot(h, wd_ref[0].astype(BF16), preferred_element_type=F32) * w
        ybuf[slot] = y.reshape(rows // SUBLANES, SUBLANES, d)
        for cp in to_smem:
            cp.wait()

        nv = nv_ref[i]
        full = lax.shift_right_logical(nv, 3)

        def scatter_rows(s):
            def send(r8, j, prio):
                to = dst_smem[r8 * SUBLANES + j]
                pltpu.make_async_copy(ybuf.at[s, r8, pl.ds(j, 1)], g_hbm.at[pl.ds(to, 1)],
                                      sem.at[s]).start(priority=prio)

            def issue8(r8, carry):
                for j in range(SUBLANES):
                    send(r8, j, j % DMA_THREADS)
                return carry

            def issue1(r, carry):
                send(lax.shift_right_logical(r, 3), r & (SUBLANES - 1), 0)
                return carry

            lax.fori_loop(0, full, issue8, 0)
            lax.fori_loop(full * SUBLANES, nv, issue1, 0)

        for s in range(2):
            pl.when(slot == s)(functools.partial(scatter_rows, s))
        return carry

    lax.fori_loop(0, n_own, tile_step, 0)

    @pl.when(e == ne - 1)
    def _():
        @pl.when(nu >= 2)
        def _():
            drain(nu - 2)

        @pl.when(nu >= 1)
        def _():
            drain(nu - 1)


def _experts(first_tile, tiles_per_expert, n_used, tile_valid, xs, w_gate, w_up, w_down, n_tok):
    n_rows, width = xs.shape
    ne, d, de = w_gate.shape
    w_map = lambda e, *_: (e, 0, 0)
    n_blk = -(-ROW_TILE // LANES)
    return pl.pallas_call(
        functools.partial(_expert_kernel, n_tok=n_tok),
        out_shape=jax.ShapeDtypeStruct((TOP_K * n_tok, d), F32),
        grid_spec=pltpu.PrefetchScalarGridSpec(
            num_scalar_prefetch=4, grid=(ne,),
            in_specs=[pl.BlockSpec(memory_space=pl.ANY),
                      pl.BlockSpec((1, d, de), w_map), pl.BlockSpec((1, d, de), w_map),
                      pl.BlockSpec((1, de, d), w_map)],
            out_specs=pl.BlockSpec(memory_space=pl.ANY),
            scratch_shapes=[pltpu.VMEM((2, ROW_TILE, width), F32),
                            pltpu.VMEM((2, ROW_TILE // SUBLANES, SUBLANES, d), F32),
                            pltpu.VMEM((SUBLANES, LANES), I32),
                            pltpu.SMEM((n_blk * LANES,), I32), pltpu.SemaphoreType.DMA((2,)),
                            pltpu.SemaphoreType.DMA((2,)), pltpu.SemaphoreType.DMA]),
        compiler_params=_params(("arbitrary",)),
        name="experts",
    )(first_tile, tiles_per_expert, n_used, tile_valid, xs, w_gate, w_up, w_down)


def _combine_kernel(g_ref, u_ref, x1_ref, mod_ref, wsg_ref, wsu_ref, wsd_ref, gpost_ref, o_ref,
                    *, per_row, rows_per_batch):
    tm, d = u_ref.shape
    ub = u_ref[...].astype(BF16)
    hs = _silu(jnp.dot(ub, wsg_ref[...], preferred_element_type=F32)) * jnp.dot(
        ub, wsu_ref[...], preferred_element_type=F32)
    f = jnp.dot(hs.astype(BF16), wsd_ref[...], preferred_element_type=F32)
    routed = g_ref[0]
    for k in range(1, TOP_K):
        routed = routed + g_ref[k]
    f = routed + f
    (gate2,) = _mod_rows(mod_ref, per_row, rows_per_batch, tm, d)
    o_ref[...] = x1_ref[...] + gate2 * (_rms(f) * gpost_ref[...])


def _combine(g_rows, u2, x1, mod, consts, per_row, rows_per_batch, tile_off, n_tok, tm):
    t_all, d = u2.shape
    ds_ = consts["w_sh_gate"].shape[1]
    tok_spec = lambda w: pl.BlockSpec((tm, w), lambda i: (i + tile_off, 0))
    return pl.pallas_call(
        functools.partial(_combine_kernel, per_row=per_row, rows_per_batch=rows_per_batch),
        out_shape=jax.ShapeDtypeStruct((n_tok, d), F32),
        grid=(n_tok // tm,),
        in_specs=[pl.BlockSpec((TOP_K, tm, d), lambda i: (0, i + tile_off, 0)),
                  tok_spec(d), tok_spec(d), _mod_spec(mod, per_row, tm),
                  _const_spec((d, ds_)), _const_spec((d, ds_)), _const_spec((ds_, d)), _const_spec((1, d))],
        out_specs=pl.BlockSpec((tm, d), lambda i: (i, 0)),
        compiler_params=_params(("arbitrary",)),
        name="combine",
    )(g_rows.reshape(TOP_K, t_all, d), u2, x1, mod, consts["w_sh_gate"], consts["w_sh_up"],
      consts["w_sh_down"], consts["g_post_ffn"])


def _repeat_rows(x, n):
    r, c = x.shape
    return jnp.broadcast_to(x[:, None, :], (r, n, c)).reshape(r * n, c)


def _tile_major(pos_t, tile):
    k, t = pos_t.shape
    return pos_t.reshape(k, t // tile, tile).transpose(1, 0, 2).reshape(-1)


def kernel(x_prompt, x_sample, c_prompt, c_sample, state_ssm, state_ssd_conv, state_short_conv, w_ada, b_ada, g_pre_mix, g_post_mix, g_pre_ffn, g_post_ffn, w_in, ssd_conv_w, ssd_conv_b, dt_bias, a_log, d_skip, g_ssd_norm, sc_conv_w, g_sc_norm, w_out, w_router, router_bias, w_exp_gate, w_exp_up, w_exp_down, w_sh_gate, w_sh_up, w_sh_down):
    depth = w_ada.shape[0]
    bp, seq, d = x_prompt.shape
    bs, dseq, _ = x_sample.shape
    heads = dt_bias.shape[1]
    ssd_w = heads * SSD_HEAD_DIM
    conv_dim = ssd_conv_w.shape[2]
    sc_w = sc_conv_w.shape[2]
    ne = w_router.shape[2]
    tp, ts = bp * seq, bs * dseq
    t_all = tp + ts

    assert sc_w // SC_GROUPS == SSD_HEAD_DIM and heads == SC_GROUPS
    head_of = jnp.arange(ssd_w, dtype=I32) // SSD_HEAD_DIM
    e_ind = (jnp.arange(LANES, dtype=I32)[:, None] == head_of[None, :])

    xp = x_prompt.reshape(tp, d)
    xs_pad = jnp.pad(x_sample, ((0, 0), (0, SUBLANES - dseq), (0, 0))).reshape(bs * SUBLANES, d)
    xs_tok = x_sample.reshape(ts, d)
    outs = {k: [] for k in ("ssm_p", "cst_p", "scst_p", "ssm_s", "cst_s", "scst_s")}

    for layer in range(depth):
        cuts = np.cumsum([0, ssd_w, conv_dim, heads, sc_w, sc_w, sc_w]).tolist()
        wi = w_in[layer]
        seg = lambda k: wi[:, cuts[k]:cuts[k + 1]]
        w_in_r = jnp.concatenate([seg(0), seg(1), seg(3), seg(4), seg(5),
                                  jnp.pad(seg(2), ((0, 0), (0, LANES - heads)))], axis=1).astype(BF16)
        widths = (ssd_w, conv_dim, sc_w, sc_w, sc_w, LANES)
        pad_h = lambda v: jnp.pad(v.reshape(1, heads), ((0, 0), (0, LANES - heads)))
        wr = w_router[layer].T
        wr_hi = wr.astype(BF16)
        consts = dict(
            heads=heads,
            conv_w=ssd_conv_w[layer], conv_b=ssd_conv_b[layer].reshape(1, conv_dim),
            dt_bias=pad_h(dt_bias[layer]), a_log=pad_h(a_log[layer]),
            d_skip_e=jnp.broadcast_to(d_skip[layer][:, None], (heads, SSD_HEAD_DIM)).reshape(1, ssd_w),
            g_ssd=g_ssd_norm[layer].reshape(1, ssd_w), sc_w=sc_conv_w[layer],
            g_sc=g_sc_norm[layer].reshape(1, sc_w),
            e=e_ind.astype(BF16), et=e_ind.T.astype(F32),
            w_out=w_out[layer].astype(BF16), g_post_mix=g_post_mix[layer].reshape(1, d),
            g_pre_ffn=g_pre_ffn[layer].reshape(1, d),
            wr_hi_t=wr_hi, wr_lo_t=(wr - wr_hi.astype(F32)).astype(BF16),
            w_sh_gate=w_sh_gate[layer].astype(BF16), w_sh_up=w_sh_up[layer].astype(BF16),
            w_sh_down=w_sh_down[layer].astype(BF16), g_post_ffn=g_post_ffn[layer].reshape(1, d))

        c_all = jnp.concatenate([c_prompt, c_sample], axis=0)
        m_rows = -(-c_all.shape[0] // 16) * 16
        mod = _ada(jnp.pad(c_all, ((0, m_rows - c_all.shape[0]), (0, 0))), w_ada[layer], b_ada[layer])
        mod_p = mod[:bp]
        mod_s = mod[bp:bp + bs]
        mod_s_pad = _repeat_rows(mod_s[:, :2 * d], SUBLANES)
        mod_s_tok = _repeat_rows(mod_s, dseq)

        proj_s = _in_proj(xs_pad, mod_s_pad, g_pre_mix[layer], w_in_r, widths, F32, True, 1, TOK_TILE)
        ymix_p, ssm_p, cst_p, scst_p = _ssd(
            (xp, mod_p[:, :2 * d], g_pre_mix[layer].reshape(1, d), w_in_r), None, consts, bp, seq // CHUNK,
            CHUNK, CHUNK, CHUNK, BF16, widths=widths, n_sub=SCAN_SUB)
        init = (state_ssm[layer].reshape(bs, ssd_w, D_STATE), state_ssd_conv[layer], state_short_conv[layer])
        ymix_s, ssm_s, cst_s, scst_s = _ssd(proj_s, init, consts, bs, 1, SAMPLE_CHUNK, SUBLANES, dseq, F32)
        ymix_s = ymix_s.reshape(bs, SUBLANES, ssd_w + sc_w)[:, :dseq].reshape(ts, ssd_w + sc_w)

        merged = _out_proj(ymix_p, xp, mod_p[:, 2 * d:5 * d], consts, False, seq, TOK_TILE, t_all, 0, None)
        x1, u2, logits_t = _out_proj(ymix_s, xs_tok, mod_s_tok[:, 2 * d:5 * d], consts, True, 1, TOK_TILE,
                                     t_all, tp // TOK_TILE, merged)

        idx_t, rank_t, wtok, counts = _route(logits_t, router_bias[layer], TOK_TILE)
        counts = counts[:, 0]
        padded = (counts + ROW_TILE - 1) // ROW_TILE * ROW_TILE
        pad_end = jnp.cumsum(padded)
        pad_start = pad_end - padded
        n_tiles = -(-(t_all * TOP_K) // ROW_TILE) + ne
        n_used = (pad_end[-1] // ROW_TILE).astype(I32)
        tile_ids = jnp.minimum(jnp.arange(n_tiles, dtype=I32), n_used - 1)
        tile_expert = jnp.minimum(jnp.sum(pad_end[None, :] <= (tile_ids * ROW_TILE)[:, None], axis=1), ne - 1).astype(I32)
        first_pad = (pad_start + counts) // SUBLANES * SUBLANES
        units = (pad_end - first_pad) // SUBLANES
        tile_valid = jnp.clip(counts[tile_expert] - (tile_ids * ROW_TILE - pad_start[tile_expert]), 0,
                              ROW_TILE).astype(I32)
        pos_t, meta = _positions(idx_t, rank_t, pad_start.astype(I32), wtok, TOK_TILE)
        pos_flat = _tile_major(pos_t, GATHER_TILE)

        xs_rows = _dispatch(pos_flat, u2, meta, n_tiles * ROW_TILE, GATHER_TILE)
        g_rows = _experts((pad_start // ROW_TILE).astype(I32), (padded // ROW_TILE).astype(I32),
                          n_used.reshape(1), tile_valid, xs_rows, w_exp_gate[layer], w_exp_up[layer],
                          w_exp_down[layer], t_all)
        xp = _combine(g_rows, u2, x1, mod_p[:, 5 * d:], consts, False, seq, 0, tp, COMBINE_TILE)
        xs_tok = _combine(g_rows, u2, x1, mod_s_tok[:, 5 * d:], consts, True, 1, tp // COMBINE_TILE, ts,
                          COMBINE_TILE)
        xs_pad = jnp.pad(xs_tok.reshape(bs, dseq, d), ((0, 0), (0, SUBLANES - dseq), (0, 0))).reshape(
            bs * SUBLANES, d)

        outs["ssm_p"].append(ssm_p.reshape(bp, heads, SSD_HEAD_DIM, D_STATE))
        outs["cst_p"].append(cst_p)
        outs["scst_p"].append(scst_p)
        outs["ssm_s"].append(ssm_s.reshape(bs, heads, SSD_HEAD_DIM, D_STATE))
        outs["cst_s"].append(cst_s)
        outs["scst_s"].append(scst_s)

    return (xp.reshape(bp, seq, d), xs_tok.reshape(bs, dseq, d),
            jnp.stack(outs["ssm_p"]), jnp.stack(outs["cst_p"]), jnp.stack(outs["scst_p"]),
            jnp.stack(outs["ssm_s"]), jnp.stack(outs["cst_s"]), jnp.stack(outs["scst_s"]))
```

```python
import functools

import jax
import jax.numpy as jnp
import numpy as np
from jax import lax
from jax.experimental import pallas as pl
from jax.experimental.pallas import tpu as pltpu

F32 = jnp.float32
BF16 = jnp.bfloat16
I32 = jnp.int32
U32 = jnp.uint32
HI = lax.Precision.HIGHEST

SSD_HEAD_DIM = 64
SSD_GROUPS = 2
D_STATE = 128
SSD_CONV_W = 4
SC_GROUPS = 16
SC_CONV_W = 3
TOP_K = 8
N_EXPERT_GROUPS = 8
TOPK_GROUPS = 4
ROUTED_SCALE = 2.5
EPS = 1e-6

LANES = 128
SUBLANES = 8
CHUNK = 128
SAMPLE_CHUNK = 16
SCAN_SUB = 4
TOK_TILE = 512
ROW_TILE = 576
GATHER_TILE = 512
COMBINE_TILE = 256
VMEM_LIMIT = 56 * 1024 * 1024
DMA_THREADS = 2

NT = (((1,), (1,)), ((), ()))
TN = (((0,), (0,)), ((), ()))


def _sigmoid(x):
    return 1.0 / (1.0 + jnp.exp(-x))


def _silu(x):
    return x * _sigmoid(x)


def _softplus(x):
    return jnp.maximum(x, 0.0) + jnp.log1p(jnp.exp(-jnp.abs(x)))


def _rms(x, eps=EPS):
    return x * lax.rsqrt(jnp.mean(x * x, axis=-1, keepdims=True) + eps)


def _params(sem=None):
    return pltpu.CompilerParams(dimension_semantics=sem, vmem_limit_bytes=VMEM_LIMIT)


def _const_spec(shape, single=False):
    nd = len(shape)
    mode = dict(pipeline_mode=pl.Buffered(1)) if single else {}
    return pl.BlockSpec(shape, lambda *_: (0,) * nd, **mode)


def _ada_kernel(c_ref, w_ref, b_ref, o_ref):
    c = c_ref[...]
    s = _silu(c).astype(BF16)
    o_ref[...] = jnp.dot(s, w_ref[...].astype(BF16), preferred_element_type=F32) + b_ref[...]


def _ada(c, w_ada, b_ada):
    m, d = c.shape
    n = w_ada.shape[1]
    tn = 512
    return pl.pallas_call(
        _ada_kernel,
        out_shape=jax.ShapeDtypeStruct((m, n), F32),
        grid=(n // tn,),
        in_specs=[_const_spec((m, d)),
                  pl.BlockSpec((d, tn), lambda j: (0, j)),
                  pl.BlockSpec((1, tn), lambda j: (0, j))],
        out_specs=pl.BlockSpec((m, tn), lambda j: (0, j)),
        compiler_params=_params(("arbitrary",)),
        name="ada",
    )(c, w_ada, b_ada.reshape(1, n))


def _mod_rows(mod_ref, per_row, rows_per_batch, tile, d):
    n = mod_ref.shape[1] // d
    if per_row:
        return [mod_ref[:, k * d:(k + 1) * d] for k in range(n)]
    b = (pl.program_id(0) * tile) // rows_per_batch
    return [mod_ref[pl.ds(b, 1), k * d:(k + 1) * d] for k in range(n)]


def _mod_spec(mod, per_row, tile):
    if per_row:
        return pl.BlockSpec((tile, mod.shape[1]), lambda i: (i, 0))
    return _const_spec(mod.shape)


def _in_kernel(x_ref, mod_ref, g_ref, w_ref, *out_refs, per_row, rows_per_batch, widths):
    tile, d = x_ref.shape
    shift, scale = _mod_rows(mod_ref, per_row, rows_per_batch, tile, d)
    u = (_rms(x_ref[...]) * g_ref[...]) * (1.0 + scale) + shift
    u = u.astype(BF16)
    col = 0
    for ref, width in zip(out_refs, widths):
        for a in range(0, width, 512):
            bw = min(512, width - a)
            r = jnp.dot(u, w_ref[:, col + a:col + a + bw], preferred_element_type=F32)
            ref[:, a:a + bw] = r.astype(ref.dtype)
        col += width


def _in_proj(x, mod, g, w_bf16, widths, out_dtype, per_row, rows_per_batch, tile):
    t, d = x.shape
    n = w_bf16.shape[1]
    mod_spec = _mod_spec(mod, per_row, tile)
    dts = [out_dtype] * (len(widths) - 1) + [F32]
    return pl.pallas_call(
        functools.partial(_in_kernel, per_row=per_row, rows_per_batch=rows_per_batch, widths=widths),
        out_shape=[jax.ShapeDtypeStruct((t, wd), dt) for wd, dt in zip(widths, dts)],
        grid=(t // tile,),
        in_specs=[pl.BlockSpec((tile, d), lambda i: (i, 0)), mod_spec,
                  _const_spec((1, d)), _const_spec((d, n), single=True)],
        out_specs=[pl.BlockSpec((tile, wd), lambda i: (i, 0)) for wd in widths],
        compiler_params=_params(("arbitrary",)),
        name="in_proj",
    )(x, mod, g.reshape(1, d), w_bf16)


def _ssd_kernel(*refs, q, rows_in, q_valid, has_init, heads, widths, n_sub):
    it = iter(refs)
    if widths is None:
        z_ref, xbc_ref, scb_ref, scc_ref, sch_ref, dt_ref = (next(it) for _ in range(6))
    else:
        x_ref, mod_ref, gpre_ref, win_ref = (next(it) for _ in range(4))
    if has_init:
        ssm0_ref, cst0_ref, scst0_ref = (next(it) for _ in range(3))
    (cw_ref, cb_ref, dtb_ref, alog_ref, dsk_ref, gssd_ref, scw_ref, gsc_ref,
     tril_ref, e_ref, et_ref) = (next(it) for _ in range(11))
    ymix_ref, ssm_ref, cst_ref, scst_ref = (next(it) for _ in range(4))
    h_scr, ext_scr, extv_scr = (next(it) for _ in range(3))

    c = pl.program_id(1)
    nc = pl.num_programs(1)
    ssd_w = dsk_ref.shape[1]
    gw = ssd_w // SSD_GROUPS
    hpg = heads // SSD_GROUPS
    n_state = D_STATE
    head0 = SUBLANES - (SSD_CONV_W - 1)
    headv = SUBLANES - (SC_CONV_W - 1)

    @pl.when(c == 0)
    def _():
        if has_init:
            h_scr[...] = ssm0_ref[0]
            ext_scr[head0:SUBLANES, :] = cst0_ref[0]
            extv_scr[headv:SUBLANES, :] = scst0_ref[0]
        else:
            h_scr[...] = jnp.zeros(h_scr.shape, F32)
            ext_scr[0:SUBLANES, :] = jnp.zeros((SUBLANES, ext_scr.shape[1]), F32)
            extv_scr[0:SUBLANES, :] = jnp.zeros((SUBLANES, extv_scr.shape[1]), F32)

    def pad_rows(v, n):
        if v.shape[0] == n:
            return v
        return jnp.concatenate([v, jnp.zeros((n - v.shape[0], v.shape[1]), v.dtype)], axis=0)

    load = lambda ref: pad_rows(ref[...].astype(F32), q)
    pad_t = lambda v: pad_rows(v, LANES)
    if widths is None:
        z, xbc, scb, scc, sch, dt_raw = (load(r) for r in (z_ref, xbc_ref, scb_ref, scc_ref, sch_ref, dt_ref))
    else:
        d_model = x_ref.shape[1]
        b = pl.program_id(0)
        shift = mod_ref[pl.ds(b, 1), 0:d_model]
        scale = mod_ref[pl.ds(b, 1), d_model:2 * d_model]
        u = ((_rms(x_ref[...]) * gpre_ref[...]) * (1.0 + scale) + shift).astype(BF16)
        pieces, col = [], 0
        for width in widths:
            parts = [jnp.dot(u, win_ref[:, col + a:col + a + min(512, width - a)], preferred_element_type=F32)
                     for a in range(0, width, 512)]
            pieces.append(parts[0] if len(parts) == 1 else jnp.concatenate(parts, axis=1))
            col += width
        z, xbc, scb, scc, sch = (p.astype(BF16).astype(F32) for p in pieces[:5])
        dt_raw = pieces[5]

    def scan_chunk(j, z, xbc, scb, scc, sch, dt_raw):
        ext_scr[SUBLANES:SUBLANES + q, :] = xbc
        conv = cb_ref[...]
        for k in range(SSD_CONV_W):
            conv = conv + cw_ref[k:k + 1, :] * ext_scr[head0 + k:head0 + k + q, :]
        xc = _silu(conv)
        xs = xc[:, :ssd_w]
        bm = xc[:, ssd_w:ssd_w + SSD_GROUPS * n_state]
        cm = xc[:, ssd_w + SSD_GROUPS * n_state:]

        lane = lax.broadcasted_iota(I32, (q, LANES), 1)
        row = lax.broadcasted_iota(I32, (q, LANES), 0)
        dt = _softplus(dt_raw + dtb_ref[...])
        dt = jnp.where(jnp.logical_and(lane < heads, row < q_valid), dt, 0.0)
        a = dt * (-jnp.exp(alog_ref[...]))
        acum = jnp.dot(tril_ref[...], pad_t(a), precision=HI, preferred_element_type=F32)
        acum_t = pad_t(acum).T
        a_last = acum[q - 1:q, :]
        e = e_ref[...]
        expand = lambda v: jnp.dot(v.astype(BF16), e, preferred_element_type=F32)
        dt_e = expand(dt)
        dtdte_e = expand(dt * jnp.exp(a_last - acum))
        exa_e = expand(jnp.exp(acum))
        xdt = xs * dt_e
        xw_b = (xs * dtdte_e).astype(BF16)
        dlast = jnp.broadcast_to(jnp.exp(acum_t[:, q - 1:q]), (LANES, n_state))
        et_b = et_ref[...].astype(BF16)
        dcol, rest = None, dlast
        for _ in range(3):
            term = rest.astype(BF16)
            part = jnp.dot(et_b, term, preferred_element_type=F32)
            dcol = part if dcol is None else dcol + part
            rest = rest - term.astype(F32)

        tri = row >= lane
        xdt_t = pad_t(xdt)
        xw_t = pad_t(xw_b)
        lane_t = lax.broadcasted_iota(I32, (LANES, LANES), 1)
        y_groups = []
        for g in range(SSD_GROUPS):
            bm_g = bm[:, g * n_state:(g + 1) * n_state].astype(BF16)
            cm_g = cm[:, g * n_state:(g + 1) * n_state].astype(BF16)
            bm_t = pad_t(bm_g)
            cb = lax.dot_general(cm_g, bm_t, NT, preferred_element_type=F32)
            h_g = h_scr[g * gw:(g + 1) * gw, :]
            y_off = lax.dot_general(cm_g, h_g.astype(BF16), NT, preferred_element_type=F32)
            parts = []
            for pair in range(hpg // 2):
                lo = (g * hpg + 2 * pair) * SSD_HEAD_DIM
                x_pair = xdt_t[:, lo:lo + LANES]
                acc = None
                for half in range(2):
                    h = g * hpg + 2 * pair + half
                    ci = jnp.broadcast_to(acum[:, h:h + 1], (q, LANES))
                    rj = jnp.broadcast_to(acum_t[h:h + 1, :], (q, LANES))
                    dec = jnp.where(tri, jnp.exp(ci - rj), 0.0)
                    m = (cb * dec).astype(BF16)
                    own = (lane_t >= SSD_HEAD_DIM) if half else (lane_t < SSD_HEAD_DIM)
                    y_h = jnp.dot(m, jnp.where(own, x_pair, 0.0).astype(BF16), preferred_element_type=F32)
                    acc = y_h if acc is None else acc + y_h
                parts.append(acc)
            y_diag = jnp.concatenate(parts, axis=1)
            y_groups.append(y_diag + y_off * exa_e[:, g * gw:(g + 1) * gw])
            upd = lax.dot_general(xw_t[:, g * gw:(g + 1) * gw], bm_t, TN, preferred_element_type=F32)
            h_scr[g * gw:(g + 1) * gw, :] = h_g * dcol[g * gw:(g + 1) * gw, :] + upd

        y = (jnp.concatenate(y_groups, axis=1) + xs * dsk_ref[...]) * _silu(z)
        y = jnp.concatenate([_rms(y[:, g * gw:(g + 1) * gw]) for g in range(SSD_GROUPS)], axis=1)
        y_ssd = y * gssd_ref[...]

        extv_scr[SUBLANES:SUBLANES + q, :] = scc * sch
        cv = scw_ref[0:1, :] * extv_scr[headv:headv + q, :]
        for k in range(1, SC_CONV_W):
            cv = cv + scw_ref[k:k + 1, :] * extv_scr[headv + k:headv + k + q, :]
        t = scb * cv
        sc_per_group = t.shape[1] // SC_GROUPS
        gsum = jnp.dot((t * t).astype(BF16), et_ref[...].astype(BF16), preferred_element_type=F32)
        rs = lax.rsqrt(gsum * (1.0 / sc_per_group) + EPS)
        y_sc = t * expand(rs) * gsc_ref[...]

        out_rows = slice(j * rows_in, (j + 1) * rows_in)
        ymix_ref[out_rows, :ssd_w] = y_ssd[0:rows_in].astype(ymix_ref.dtype)
        ymix_ref[out_rows, ssd_w:] = y_sc[0:rows_in].astype(ymix_ref.dtype)

        if j == n_sub - 1:
            @pl.when(c == nc - 1)
            def _():
                ssm_ref[0] = h_scr[...]
                cst_ref[0] = ext_scr[SUBLANES + q_valid - (SSD_CONV_W - 1):SUBLANES + q_valid, :]
                scst_ref[0] = extv_scr[SUBLANES + q_valid - (SC_CONV_W - 1):SUBLANES + q_valid, :]

        ext_scr[0:SUBLANES, :] = ext_scr[q:q + SUBLANES, :]
        extv_scr[0:SUBLANES, :] = extv_scr[q:q + SUBLANES, :]

    for j in range(n_sub):
        scan_chunk(j, *(p[j * q:(j + 1) * q] for p in (z, xbc, scb, scc, sch, dt_raw)))


def _ssd(proj, init, consts, nb, n_chunks, q, rows_in, q_valid, ymix_dtype, widths=None, n_sub=1):
    heads = consts["heads"]
    tril = jnp.tril(jnp.ones((q, LANES), F32))
    ssd_w, conv_dim, sc_w = consts["g_ssd"].shape[1], consts["conv_w"].shape[1], consts["g_sc"].shape[1]
    has_init = init is not None
    assert n_chunks % n_sub == 0 and (n_sub == 1 or rows_in == q)
    nc = n_chunks // n_sub
    row_spec = lambda w: pl.BlockSpec((n_sub * rows_in, w), lambda b, c: (b * nc + c, 0))
    if widths is None:
        in_specs = [row_spec(ssd_w), row_spec(conv_dim), row_spec(sc_w), row_spec(sc_w), row_spec(sc_w),
                    row_spec(LANES)]
    else:
        x, mod, g, w = proj
        in_specs = [row_spec(x.shape[1]), _const_spec(mod.shape), _const_spec(g.shape),
                    _const_spec(w.shape, single=True)]
    args = list(proj)
    if has_init:
        ssm0, cst0, scst0 = init
        in_specs += [pl.BlockSpec((1,) + ssm0.shape[1:], lambda b, c: (b, 0, 0)),
                     pl.BlockSpec((1,) + cst0.shape[1:], lambda b, c: (b, 0, 0)),
                     pl.BlockSpec((1,) + scst0.shape[1:], lambda b, c: (b, 0, 0))]
        args += [ssm0, cst0, scst0]
    weights = [consts[k] for k in ("conv_w", "conv_b", "dt_bias", "a_log", "d_skip_e", "g_ssd",
                                   "sc_w", "g_sc")] + [tril, consts["e"], consts["et"]]
    in_specs += [_const_spec(w.shape) for w in weights]
    args += weights
    n_state = D_STATE
    out_shape = [jax.ShapeDtypeStruct((nb * n_chunks * rows_in, ssd_w + sc_w), ymix_dtype),
                 jax.ShapeDtypeStruct((nb, ssd_w, n_state), F32),
                 jax.ShapeDtypeStruct((nb, SSD_CONV_W - 1, conv_dim), F32),
                 jax.ShapeDtypeStruct((nb, SC_CONV_W - 1, sc_w), F32)]
    out_specs = [pl.BlockSpec((n_sub * rows_in, ssd_w + sc_w), lambda b, c: (b * nc + c, 0)),
                 pl.BlockSpec((1, ssd_w, n_state), lambda b, c: (b, 0, 0)),
                 pl.BlockSpec((1, SSD_CONV_W - 1, conv_dim), lambda b, c: (b, 0, 0)),
                 pl.BlockSpec((1, SC_CONV_W - 1, sc_w), lambda b, c: (b, 0, 0))]
    scratch = [pltpu.VMEM((ssd_w, n_state), F32),
               pltpu.VMEM((q + SUBLANES, conv_dim), F32),
               pltpu.VMEM((q + SUBLANES, sc_w), F32)]
    return pl.pallas_call(
        functools.partial(_ssd_kernel, q=q, rows_in=rows_in, q_valid=q_valid, has_init=has_init, heads=heads,
                          widths=widths, n_sub=n_sub),
        out_shape=out_shape, grid=(nb, nc), in_specs=in_specs, out_specs=out_specs,
        scratch_shapes=scratch,
        compiler_params=_params(("arbitrary", "arbitrary")),
        name="ssd",
    )(*args)


def _out_kernel(ymix_ref, x_ref, mod_ref, wout_ref, gpost_ref, gpre_ref, wr_hi_ref, wr_lo_ref,
                *rest, per_row, rows_per_batch, has_alias):
    x1_ref, u2_ref, lg_ref = rest[3:6] if has_alias else rest[0:3]
    tile, d = x_ref.shape
    gate1, shift2, scale2 = _mod_rows(mod_ref, per_row, rows_per_batch, tile, d)
    m = jnp.dot(ymix_ref[...].astype(BF16), wout_ref[...], preferred_element_type=F32)
    x1 = x_ref[...] + gate1 * (_rms(m) * gpost_ref[...])
    u2 = (_rms(x1) * gpre_ref[...]) * (1.0 + scale2) + shift2
    x1_ref[...] = x1
    u2_ref[...] = u2
    u_hi = u2.astype(BF16)
    u_lo = (u2 - u_hi.astype(F32)).astype(BF16)
    lg = lax.dot_general(wr_hi_ref[...], u_hi, NT, preferred_element_type=F32)
    lg = lg + lax.dot_general(wr_hi_ref[...], u_lo, NT, preferred_element_type=F32)
    lg = lg + lax.dot_general(wr_lo_ref[...], u_hi, NT, preferred_element_type=F32)
    lg_ref[...] = lg


def _out_proj(ymix, x, mod, consts, per_row, rows_per_batch, tile, t_total, tile_off, prev):
    t, d = x.shape
    dm = ymix.shape[1]
    ne = consts["wr_hi_t"].shape[0]
    in_specs = [pl.BlockSpec((tile, dm), lambda i: (i, 0)),
                pl.BlockSpec((tile, d), lambda i: (i, 0)), _mod_spec(mod, per_row, tile),
                _const_spec((dm, d)), _const_spec((1, d)), _const_spec((1, d)),
                _const_spec((ne, d)), _const_spec((ne, d))]
    args = [ymix, x, mod, consts["w_out"], consts["g_post_mix"], consts["g_pre_ffn"],
            consts["wr_hi_t"], consts["wr_lo_t"]]
    aliases = {}
    if prev is not None:
        in_specs += [pl.BlockSpec(memory_space=pl.ANY)] * 3
        aliases = {len(args) + k: k for k in range(3)}
        args += list(prev)
    return pl.pallas_call(
        functools.partial(_out_kernel, per_row=per_row, rows_per_batch=rows_per_batch,
                          has_alias=prev is not None),
        out_shape=[jax.ShapeDtypeStruct((t_total, d), F32), jax.ShapeDtypeStruct((t_total, d), F32),
                   jax.ShapeDtypeStruct((ne, t_total), F32)],
        grid=(t // tile,), in_specs=in_specs,
        out_specs=[pl.BlockSpec((tile, d), lambda i: (i + tile_off, 0)),
                   pl.BlockSpec((tile, d), lambda i: (i + tile_off, 0)),
                   pl.BlockSpec((ne, tile), lambda i: (0, i + tile_off))],
        input_output_aliases=aliases,
        compiler_params=_params(("arbitrary",)),
        name="out_proj",
    )(*args)


def _route_kernel(lg_ref, bias_ref, upper_ref, idx_ref, rank_ref, wtok_ref, cnt_ref, carry_scr):
    i = pl.program_id(0)
    ne, tm = lg_ref.shape
    per_group = ne // N_EXPERT_GROUPS
    neg = -jnp.inf

    @pl.when(i == 0)
    def _():
        carry_scr[...] = jnp.zeros(carry_scr.shape, F32)

    s = _sigmoid(lg_ref[...])
    biased = s + bias_ref[...]
    gl = []
    io_g = lax.broadcasted_iota(I32, (per_group, tm), 0).astype(F32)
    for g in range(N_EXPERT_GROUPS):
        blk = biased[g * per_group:(g + 1) * per_group, :]
        m1 = jnp.max(blk, axis=0, keepdims=True)
        f1 = jnp.min(jnp.where(blk == m1, io_g, float(per_group)), axis=0, keepdims=True)
        m2 = jnp.max(jnp.where(io_g == f1, neg, blk), axis=0, keepdims=True)
        gl.append(m1 + m2)
    gscore = jnp.concatenate(gl, axis=0)
    io8 = lax.broadcasted_iota(I32, (N_EXPERT_GROUPS, tm), 0).astype(F32)
    gsel = jnp.zeros((N_EXPERT_GROUPS, tm), F32)
    for _ in range(TOPK_GROUPS):
        m = jnp.max(gscore, axis=0, keepdims=True)
        f = jnp.min(jnp.where(gscore == m, io8, float(N_EXPERT_GROUPS)), axis=0, keepdims=True)
        hit = io8 == f
        gsel = jnp.where(hit, 1.0, gsel)
        gscore = jnp.where(hit, neg, gscore)
    emask = jnp.concatenate(
        [jnp.broadcast_to(gsel[g:g + 1, :], (per_group, tm)) for g in range(N_EXPERT_GROUPS)], axis=0)
    cand = jnp.where(emask > 0.5, biased, neg)
    io_e = lax.broadcasted_iota(I32, (ne, tm), 0).astype(F32)
    msel = jnp.zeros((ne, tm), F32)
    idxs, wts = [], []
    for _ in range(TOP_K):
        m = jnp.max(cand, axis=0, keepdims=True)
        f = jnp.min(jnp.where(cand == m, io_e, float(ne)), axis=0, keepdims=True)
        hit = io_e == f
        wts.append(jnp.sum(jnp.where(hit, s, 0.0), axis=0, keepdims=True))
        idxs.append(f)
        msel = jnp.where(hit, 1.0, msel)
        cand = jnp.where(hit, neg, cand)
    pref = jnp.dot(msel.astype(BF16), upper_ref[...], preferred_element_type=F32) + carry_scr[:, 0:1]
    ranks = [jnp.sum(jnp.where(io_e == f, pref, 0.0), axis=0, keepdims=True) for f in idxs]
    carry_scr[...] = carry_scr[...] + jnp.sum(msel, axis=1, keepdims=True)
    cnt_ref[...] = carry_scr[...].astype(I32)
    idx_ref[...] = jnp.concatenate(idxs, axis=0).astype(I32)
    rank_ref[...] = jnp.concatenate(ranks, axis=0).astype(I32)
    wsum = wts[0]
    for w in wts[1:]:
        wsum = wsum + w
    wn = jnp.concatenate([w / wsum * ROUTED_SCALE for w in wts]
                         + [jnp.zeros((LANES - TOP_K, tm), F32)], axis=0)
    for j in range(tm // LANES):
        wtok_ref[j * LANES:(j + 1) * LANES, :] = wn[:, j * LANES:(j + 1) * LANES].T


def _route(logits_t, bias, tile):
    ne, t = logits_t.shape
    upper = jnp.triu(jnp.ones((tile, tile), F32), 1).astype(BF16)
    return pl.pallas_call(
        _route_kernel,
        out_shape=[jax.ShapeDtypeStruct((TOP_K, t), I32), jax.ShapeDtypeStruct((TOP_K, t), I32),
                   jax.ShapeDtypeStruct((t, LANES), F32), jax.ShapeDtypeStruct((ne, LANES), I32)],
        grid=(t // tile,),
        in_specs=[pl.BlockSpec((ne, tile), lambda i: (0, i)), _const_spec((ne, 1)),
                  _const_spec((tile, tile))],
        out_specs=[pl.BlockSpec((TOP_K, tile), lambda i: (0, i)),
                   pl.BlockSpec((TOP_K, tile), lambda i: (0, i)),
                   pl.BlockSpec((tile, LANES), lambda i: (i, 0)),
                   _const_spec((ne, LANES))],
        scratch_shapes=[pltpu.VMEM((ne, LANES), F32)],
        compiler_params=_params(("arbitrary",)),
        name="route",
    )(logits_t, bias.reshape(ne, 1), upper)


META_POS, META_W, META_TOK = 0, TOP_K, 2 * TOP_K


def _pos_kernel(idx_ref, rank_ref, start_ref, wtok_ref, pos_ref, meta_ref):
    ne = start_ref.shape[0]
    tm = idx_ref.shape[1]
    io_e = lax.broadcasted_iota(I32, (ne, tm), 0)
    start = start_ref[...].astype(F32)
    rows = []
    for k in range(TOP_K):
        hit = io_e == idx_ref[k:k + 1, :]
        rows.append(jnp.sum(jnp.where(hit, start, 0.0), axis=0, keepdims=True))
    pos_f = jnp.concatenate(rows, axis=0) + rank_ref[...].astype(F32)
    pos_ref[...] = pos_f.astype(I32)
    pos_pad = jnp.concatenate([pos_f, jnp.zeros((LANES - TOP_K, tm), F32)], axis=0)
    lane = lax.broadcasted_iota(I32, (LANES, LANES), 1)
    row = lax.broadcasted_iota(I32, (LANES, LANES), 0)
    for j in range(tm // LANES):
        pos_tok = pos_pad[:, j * LANES:(j + 1) * LANES].T
        w_tok = pltpu.roll(wtok_ref[j * LANES:(j + 1) * LANES, :], META_W, axis=1)
        tok = (row + (pl.program_id(0) * tm + j * LANES)).astype(F32)
        tag = jnp.where(lane < META_W, pos_tok, jnp.where(lane == META_TOK, tok, w_tok))
        meta_ref[j * LANES:(j + 1) * LANES, :] = tag


def _positions(idx_t, rank_t, pad_start, wtok, tile):
    k, t = idx_t.shape
    ne = pad_start.shape[0]
    return pl.pallas_call(
        _pos_kernel,
        out_shape=[jax.ShapeDtypeStruct((k, t), I32), jax.ShapeDtypeStruct((t, LANES), F32)],
        grid=(t // tile,),
        in_specs=[pl.BlockSpec((k, tile), lambda i: (0, i)), pl.BlockSpec((k, tile), lambda i: (0, i)),
                  _const_spec((ne, 1)), pl.BlockSpec((tile, LANES), lambda i: (i, 0))],
        out_specs=[pl.BlockSpec((k, tile), lambda i: (0, i)), pl.BlockSpec((tile, LANES), lambda i: (i, 0))],
        compiler_params=_params(("arbitrary",)),
        name="positions",
    )(idx_t, rank_t, pad_start.reshape(ne, 1), wtok)


ZERO_BITS = (64, 32, 16, 8, 4, 2, 1)
assert sum(ZERO_BITS) * SUBLANES >= ROW_TILE


def _zero_copy(zbuf, xs_ref, sem, off, bit):
    rows = bit * SUBLANES
    return pltpu.make_async_copy(zbuf.at[pl.ds(0, rows)], xs_ref.at[pl.ds(pl.multiple_of(off, SUBLANES), rows)], sem)


def _zero_kernel(first_ref, units_ref, xs_ref, zbuf, sem):
    zbuf[...] = jnp.zeros(zbuf.shape, zbuf.dtype)
    ne = first_ref.shape[0]

    def each(e, start):
        off = first_ref[e]
        units = units_ref[e]
        for bit in ZERO_BITS:
            on = (units & bit) != 0

            @pl.when(on)
            def _():
                cp = _zero_copy(zbuf, xs_ref, sem, off, bit)
                cp.start() if start else cp.wait()
            off = off + jnp.where(on, bit * SUBLANES, 0)
        return start

    lax.fori_loop(0, ne, lambda e, c: (each(e, True), c)[1], 0)
    lax.fori_loop(0, ne, lambda e, c: (each(e, False), c)[1], 0)


def _zero_rows(first, units, n_rows, width):
    return pl.pallas_call(
        _zero_kernel,
        out_shape=jax.ShapeDtypeStruct((n_rows, width), F32),
        in_specs=[pl.BlockSpec(memory_space=pltpu.SMEM), pl.BlockSpec(memory_space=pltpu.SMEM)],
        out_specs=pl.BlockSpec(memory_space=pl.ANY),
        scratch_shapes=[pltpu.VMEM((ZERO_BITS[0] * SUBLANES, width), F32), pltpu.SemaphoreType.DMA],
        compiler_params=_params(),
        name="zero_rows",
    )(first, units)


def _row_copy(src_ref, dst_ref, sem, s, d):
    return pltpu.make_async_copy(src_ref.at[pl.ds(s, 1)], dst_ref.at[pl.ds(d, 1)], sem)


def _dispatch_kernel(pos_hbm, u_ref, meta_ref, xs_ref, pos_a, pos_b, src, psem, sem, *, nt):
    i = pl.program_id(0)
    tm, d = u_ref.shape
    per = pos_a.shape[0]
    pos_slots = (pos_a, pos_b)

    def pos_copy(tile, s):
        return pltpu.make_async_copy(pos_hbm.at[pl.ds(pl.multiple_of(tile * per, per), per)], pos_slots[s],
                                     psem.at[s])

    def drain(s):
        for _ in range(TOP_K):
            pltpu.make_async_copy(src.at[s], xs_ref.at[pl.ds(0, tm)], sem.at[s]).wait()

    def step(s):
        @pl.when(i == 0)
        def _():
            pos_copy(0, 0).start()

        @pl.when(i + 1 < nt)
        def _():
            pos_copy(i + 1, 1 - s).start()

        @pl.when(i >= 2)
        def _():
            drain(s)

        src[s, :, :d] = u_ref[...]
        src[s, :, d:] = meta_ref[...]
        pos_copy(i, s).wait()

        def issue(t, carry):
            for k in range(TOP_K):
                _row_copy(src.at[s], xs_ref, sem.at[s], t, pos_slots[s][k * tm + t]).start(
                    priority=k % DMA_THREADS)
            return carry

        lax.fori_loop(0, tm, issue, 0)

        @pl.when(i == nt - 1)
        def _():
            if nt >= 2:
                drain(1 - s)
            drain(s)

    for s in range(2):
        pl.when(i % 2 == s)(functools.partial(step, s))


def _dispatch(pos_flat, u2, meta, n_rows, tm):
    per = TOP_K * tm
    nt = pos_flat.shape[0] // per
    d = u2.shape[1]
    return pl.pallas_call(
        functools.partial(_dispatch_kernel, nt=nt),
        out_shape=jax.ShapeDtypeStruct((n_rows, d + LANES), F32),
        grid=(nt,),
        in_specs=[pl.BlockSpec(memory_space=pl.ANY), pl.BlockSpec((tm, d), lambda i: (i, 0)),
                  pl.BlockSpec((tm, LANES), lambda i: (i, 0))],
        out_specs=pl.BlockSpec(memory_space=pl.ANY),
        scratch_shapes=[pltpu.SMEM((per,), I32), pltpu.SMEM((per,), I32),
                        pltpu.VMEM((2, tm, d + LANES), F32),
                        pltpu.SemaphoreType.DMA((2,)), pltpu.SemaphoreType.DMA((2,))],
        compiler_params=_params(("arbitrary",)),
        name="dispatch",
    )(pos_flat, u2, meta)


WAIT_BITS = tuple(1 << b for b in reversed(range(ROW_TILE.bit_length())))


def _expert_kernel(te_ref, nu_ref, nv_ref, xs_ref, wg_ref, wu_ref, wd_ref, g_hbm,
                   ybuf, dst_vmem, dst_smem, sem, dsem, *, n_tok):
    del te_ref
    i = pl.program_id(0)
    nu = nu_ref[0]
    d = ybuf.shape[3]
    rows = ybuf.shape[1] * SUBLANES

    def drain(tile):
        slot = tile % 2
        n = nv_ref[tile]
        for bit in WAIT_BITS:
            @pl.when((n & bit) != 0)
            def _():
                pltpu.make_async_copy(g_hbm.at[pl.ds(0, bit)], g_hbm.at[pl.ds(0, bit)], sem.at[slot]).wait()

    @pl.when(i < nu)
    def _():
        slot = i % 2

        @pl.when(i >= 2)
        def _():
            drain(i - 2)

        dst_vmem[...] = jnp.zeros(dst_vmem.shape, I32)
        xfull = xs_ref[...]
        tag = xfull[:, d:]
        x = xfull[:, :d].astype(BF16)
        lane = lax.broadcasted_iota(I32, (rows, LANES), 1)
        p = (lax.broadcasted_iota(I32, (rows, LANES), 0) + i * rows).astype(F32)
        hit = jnp.logical_and(tag == p, lane < META_W)
        w_at = pltpu.roll(tag, LANES - META_W, axis=1)
        w = jnp.sum(jnp.where(hit, w_at, 0.0), axis=1, keepdims=True)
        kf = jnp.sum(jnp.where(hit, lane.astype(F32), 0.0), axis=1, keepdims=True)
        dest = kf * float(n_tok) + tag[:, META_TOK:META_TOK + 1]

        dest_b = jnp.broadcast_to(dest, (rows, LANES))
        n_blk = -(-rows // LANES)
        dest_b = jnp.concatenate([dest_b, jnp.zeros((n_blk * LANES - rows, LANES), F32)], axis=0)
        for b in range(n_blk):
            dst_vmem[b:b + 1, :] = dest_b[b * LANES:(b + 1) * LANES, :].T[0:1, :].astype(I32)
        to_smem = [pltpu.make_async_copy(dst_vmem.at[b], dst_smem.at[pl.ds(b * LANES, LANES)], dsem)
                   for b in range(n_blk)]
        for cp in to_smem:
            cp.start()

        g = jnp.dot(x, wg_ref[0].astype(BF16), preferred_element_type=F32)
        u = jnp.dot(x, wu_ref[0].astype(BF16), preferred_element_type=F32)
        h = (_silu(g) * u).astype(BF16)
        y = jnp.dot(h, wd_ref[0].astype(BF16), preferred_element_type=F32) * w
        ybuf[slot] = y.reshape(rows // SUBLANES, SUBLANES, d)
        for cp in to_smem:
            cp.wait()

        nv = nv_ref[i]
        full = lax.shift_right_logical(nv, 3)

        def scatter_rows(s):
            def send(r8, j, prio):
                to = dst_smem[r8 * SUBLANES + j]
                pltpu.make_async_copy(ybuf.at[s, r8, pl.ds(j, 1)], g_hbm.at[pl.ds(to, 1)],
                                      sem.at[s]).start(priority=prio)

            def issue8(r8, carry):
                for j in range(SUBLANES):
                    send(r8, j, j % DMA_THREADS)
                return carry

            def issue1(r, carry):
                send(lax.shift_right_logical(r, 3), r & (SUBLANES - 1), 0)
                return carry

            lax.fori_loop(0, full, issue8, 0)
            lax.fori_loop(full * SUBLANES, nv, issue1, 0)

        for s in range(2):
            pl.when(slot == s)(functools.partial(scatter_rows, s))

    @pl.when(i == pl.num_programs(0) - 1)
    def _():
        @pl.when(nu >= 2)
        def _():
            drain(nu - 2)

        @pl.when(nu >= 1)
        def _():
            drain(nu - 1)


def _experts(tile_expert, n_used, tile_valid, xs, w_gate, w_up, w_down, n_tok):
    n_rows, width = xs.shape
    d, de = w_gate.shape[1:]
    n_tiles = n_rows // ROW_TILE
    row_map = lambda i, te, nu, nv: (jnp.minimum(i, nu[0] - 1), 0)
    w_map = lambda i, te, nu, nv: (te[i], 0, 0)
    n_blk = -(-ROW_TILE // LANES)
    return pl.pallas_call(
        functools.partial(_expert_kernel, n_tok=n_tok),
        out_shape=jax.ShapeDtypeStruct((TOP_K * n_tok, d), F32),
        grid_spec=pltpu.PrefetchScalarGridSpec(
            num_scalar_prefetch=3, grid=(n_tiles,),
            in_specs=[pl.BlockSpec((ROW_TILE, width), row_map),
                      pl.BlockSpec((1, d, de), w_map), pl.BlockSpec((1, d, de), w_map),
                      pl.BlockSpec((1, de, d), w_map)],
            out_specs=pl.BlockSpec(memory_space=pl.ANY),
            scratch_shapes=[pltpu.VMEM((2, ROW_TILE // SUBLANES, SUBLANES, d), F32),
                            pltpu.VMEM((SUBLANES, LANES), I32),
                            pltpu.SMEM((n_blk * LANES,), I32), pltpu.SemaphoreType.DMA((2,)),
                            pltpu.SemaphoreType.DMA]),
        compiler_params=_params(("arbitrary",)),
        name="experts",
    )(tile_expert, n_used, tile_valid, xs, w_gate, w_up, w_down)


def _combine_kernel(g_ref, u_ref, x1_ref, mod_ref, wsg_ref, wsu_ref, wsd_ref, gpost_ref, o_ref,
                    *, per_row, rows_per_batch):
    tm, d = u_ref.shape
    ub = u_ref[...].astype(BF16)
    hs = _silu(jnp.dot(ub, wsg_ref[...], preferred_element_type=F32)) * jnp.dot(
        ub, wsu_ref[...], preferred_element_type=F32)
    f = jnp.dot(hs.astype(BF16), wsd_ref[...], preferred_element_type=F32)
    routed = g_ref[0]
    for k in range(1, TOP_K):
        routed = routed + g_ref[k]
    f = routed + f
    (gate2,) = _mod_rows(mod_ref, per_row, rows_per_batch, tm, d)
    o_ref[...] = x1_ref[...] + gate2 * (_rms(f) * gpost_ref[...])


def _combine(g_rows, u2, x1, mod, consts, per_row, rows_per_batch, tile_off, n_tok, tm):
    t_all, d = u2.shape
    ds_ = consts["w_sh_gate"].shape[1]
    tok_spec = lambda w: pl.BlockSpec((tm, w), lambda i: (i + tile_off, 0))
    return pl.pallas_call(
        functools.partial(_combine_kernel, per_row=per_row, rows_per_batch=rows_per_batch),
        out_shape=jax.ShapeDtypeStruct((n_tok, d), F32),
        grid=(n_tok // tm,),
        in_specs=[pl.BlockSpec((TOP_K, tm, d), lambda i: (0, i + tile_off, 0)),
                  tok_spec(d), tok_spec(d), _mod_spec(mod, per_row, tm),
                  _const_spec((d, ds_)), _const_spec((d, ds_)), _const_spec((ds_, d)), _const_spec((1, d))],
        out_specs=pl.BlockSpec((tm, d), lambda i: (i, 0)),
        compiler_params=_params(("arbitrary",)),
        name="combine",
    )(g_rows.reshape(TOP_K, t_all, d), u2, x1, mod, consts["w_sh_gate"], consts["w_sh_up"],
      consts["w_sh_down"], consts["g_post_ffn"])


def _repeat_rows(x, n):
    r, c = x.shape
    return jnp.broadcast_to(x[:, None, :], (r, n, c)).reshape(r * n, c)


def _tile_major(pos_t, tile):
    k, t = pos_t.shape
    return pos_t.reshape(k, t // tile, tile).transpose(1, 0, 2).reshape(-1)


def kernel(x_prompt, x_sample, c_prompt, c_sample, state_ssm, state_ssd_conv, state_short_conv, w_ada, b_ada, g_pre_mix, g_post_mix, g_pre_ffn, g_post_ffn, w_in, ssd_conv_w, ssd_conv_b, dt_bias, a_log, d_skip, g_ssd_norm, sc_conv_w, g_sc_norm, w_out, w_router, router_bias, w_exp_gate, w_exp_up, w_exp_down, w_sh_gate, w_sh_up, w_sh_down):
    depth = w_ada.shape[0]
    bp, seq, d = x_prompt.shape
    bs, dseq, _ = x_sample.shape
    heads = dt_bias.shape[1]
    ssd_w = heads * SSD_HEAD_DIM
    conv_dim = ssd_conv_w.shape[2]
    sc_w = sc_conv_w.shape[2]
    ne = w_router.shape[2]
    tp, ts = bp * seq, bs * dseq
    t_all = tp + ts

    assert sc_w // SC_GROUPS == SSD_HEAD_DIM and heads == SC_GROUPS
    head_of = jnp.arange(ssd_w, dtype=I32) // SSD_HEAD_DIM
    e_ind = (jnp.arange(LANES, dtype=I32)[:, None] == head_of[None, :])

    xp = x_prompt.reshape(tp, d)
    xs_pad = jnp.pad(x_sample, ((0, 0), (0, SUBLANES - dseq), (0, 0))).reshape(bs * SUBLANES, d)
    xs_tok = x_sample.reshape(ts, d)
    outs = {k: [] for k in ("ssm_p", "cst_p", "scst_p", "ssm_s", "cst_s", "scst_s")}

    for layer in range(depth):
        cuts = np.cumsum([0, ssd_w, conv_dim, heads, sc_w, sc_w, sc_w]).tolist()
        wi = w_in[layer]
        seg = lambda k: wi[:, cuts[k]:cuts[k + 1]]
        w_in_r = jnp.concatenate([seg(0), seg(1), seg(3), seg(4), seg(5),
                                  jnp.pad(seg(2), ((0, 0), (0, LANES - heads)))], axis=1).astype(BF16)
        widths = (ssd_w, conv_dim, sc_w, sc_w, sc_w, LANES)
        pad_h = lambda v: jnp.pad(v.reshape(1, heads), ((0, 0), (0, LANES - heads)))
        wr = w_router[layer].T
        wr_hi = wr.astype(BF16)
        consts = dict(
            heads=heads,
            conv_w=ssd_conv_w[layer], conv_b=ssd_conv_b[layer].reshape(1, conv_dim),
            dt_bias=pad_h(dt_bias[layer]), a_log=pad_h(a_log[layer]),
            d_skip_e=jnp.broadcast_to(d_skip[layer][:, None], (heads, SSD_HEAD_DIM)).reshape(1, ssd_w),
            g_ssd=g_ssd_norm[layer].reshape(1, ssd_w), sc_w=sc_conv_w[layer],
            g_sc=g_sc_norm[layer].reshape(1, sc_w),
            e=e_ind.astype(BF16), et=e_ind.T.astype(F32),
            w_out=w_out[layer].astype(BF16), g_post_mix=g_post_mix[layer].reshape(1, d),
            g_pre_ffn=g_pre_ffn[layer].reshape(1, d),
            wr_hi_t=wr_hi, wr_lo_t=(wr - wr_hi.astype(F32)).astype(BF16),
            w_sh_gate=w_sh_gate[layer].astype(BF16), w_sh_up=w_sh_up[layer].astype(BF16),
            w_sh_down=w_sh_down[layer].astype(BF16), g_post_ffn=g_post_ffn[layer].reshape(1, d))

        c_all = jnp.concatenate([c_prompt, c_sample], axis=0)
        m_rows = -(-c_all.shape[0] // 16) * 16
        mod = _ada(jnp.pad(c_all, ((0, m_rows - c_all.shape[0]), (0, 0))), w_ada[layer], b_ada[layer])
        mod_p = mod[:bp]
        mod_s = mod[bp:bp + bs]
        mod_s_pad = _repeat_rows(mod_s[:, :2 * d], SUBLANES)
        mod_s_tok = _repeat_rows(mod_s, dseq)

        proj_s = _in_proj(xs_pad, mod_s_pad, g_pre_mix[layer], w_in_r, widths, F32, True, 1, TOK_TILE)
        ymix_p, ssm_p, cst_p, scst_p = _ssd(
            (xp, mod_p[:, :2 * d], g_pre_mix[layer].reshape(1, d), w_in_r), None, consts, bp, seq // CHUNK,
            CHUNK, CHUNK, CHUNK, BF16, widths=widths, n_sub=SCAN_SUB)
        init = (state_ssm[layer].reshape(bs, ssd_w, D_STATE), state_ssd_conv[layer], state_short_conv[layer])
        ymix_s, ssm_s, cst_s, scst_s = _ssd(proj_s, init, consts, bs, 1, SAMPLE_CHUNK, SUBLANES, dseq, F32)
        ymix_s = ymix_s.reshape(bs, SUBLANES, ssd_w + sc_w)[:, :dseq].reshape(ts, ssd_w + sc_w)

        merged = _out_proj(ymix_p, xp, mod_p[:, 2 * d:5 * d], consts, False, seq, TOK_TILE, t_all, 0, None)
        x1, u2, logits_t = _out_proj(ymix_s, xs_tok, mod_s_tok[:, 2 * d:5 * d], consts, True, 1, TOK_TILE,
                                     t_all, tp // TOK_TILE, merged)

        idx_t, rank_t, wtok, counts = _route(logits_t, router_bias[layer], TOK_TILE)
        counts = counts[:, 0]
        padded = (counts + ROW_TILE - 1) // ROW_TILE * ROW_TILE
        pad_end = jnp.cumsum(padded)
        pad_start = pad_end - padded
        n_tiles = -(-(t_all * TOP_K) // ROW_TILE) + ne
        n_used = (pad_end[-1] // ROW_TILE).astype(I32)
        tile_ids = jnp.minimum(jnp.arange(n_tiles, dtype=I32), n_used - 1)
        tile_expert = jnp.minimum(jnp.sum(pad_end[None, :] <= (tile_ids * ROW_TILE)[:, None], axis=1), ne - 1).astype(I32)
        first_pad = (pad_start + counts) // SUBLANES * SUBLANES
        units = (pad_end - first_pad) // SUBLANES
        tile_valid = jnp.clip(counts[tile_expert] - (tile_ids * ROW_TILE - pad_start[tile_expert]), 0,
                              ROW_TILE).astype(I32)
        pos_t, meta = _positions(idx_t, rank_t, pad_start.astype(I32), wtok, TOK_TILE)
        pos_flat = _tile_major(pos_t, GATHER_TILE)

        xs_rows = _dispatch(pos_flat, u2, meta, n_tiles * ROW_TILE, GATHER_TILE)
        g_rows = _experts(tile_expert, n_used.reshape(1), tile_valid, xs_rows, w_exp_gate[layer],
                          w_exp_up[layer], w_exp_down[layer], t_all)
        xp = _combine(g_rows, u2, x1, mod_p[:, 5 * d:], consts, False, seq, 0, tp, COMBINE_TILE)
        xs_tok = _combine(g_rows, u2, x1, mod_s_tok[:, 5 * d:], consts, True, 1, tp // COMBINE_TILE, ts,
                          COMBINE_TILE)
        xs_pad = jnp.pad(xs_tok.reshape(bs, dseq, d), ((0, 0), (0, SUBLANES - dseq), (0, 0))).reshape(
            bs * SUBLANES, d)

        outs["ssm_p"].append(ssm_p.reshape(bp, heads, SSD_HEAD_DIM, D_STATE))
        outs["cst_p"].append(cst_p)
        outs["scst_p"].append(scst_p)
        outs["ssm_s"].append(ssm_s.reshape(bs, heads, SSD_HEAD_DIM, D_STATE))
        outs["cst_s"].append(cst_s)
        outs["scst_s"].append(scst_s)

    return (xp.reshape(bp, seq, d), xs_tok.reshape(bs, dseq, d),
            jnp.stack(outs["ssm_p"]), jnp.stack(outs["cst_p"]), jnp.stack(outs["scst_p"]),
            jnp.stack(outs["ssm_s"]), jnp.stack(outs["cst_s"]), jnp.stack(outs["scst_s"]))
```

```python
import functools

import jax
import jax.numpy as jnp
import numpy as np
from jax import lax
from jax.experimental import pallas as pl
from jax.experimental.pallas import tpu as pltpu

F32 = jnp.float32
BF16 = jnp.bfloat16
I32 = jnp.int32
U32 = jnp.uint32
HI = lax.Precision.HIGHEST

SSD_HEAD_DIM = 64
SSD_GROUPS = 2
D_STATE = 128
SSD_CONV_W = 4
SC_GROUPS = 16
SC_CONV_W = 3
TOP_K = 8
N_EXPERT_GROUPS = 8
TOPK_GROUPS = 4
ROUTED_SCALE = 2.5
EPS = 1e-6

LANES = 128
SUBLANES = 8
CHUNK = 128
SAMPLE_CHUNK = 16
SCAN_SUB = 4
TOK_TILE = 512
ROW_TILE = 576
GATHER_TILE = 512
COMBINE_TILE = 256
SCATTER_PRIORITY = 1
VMEM_LIMIT = 56 * 1024 * 1024
DMA_THREADS = 2

NT = (((1,), (1,)), ((), ()))
TN = (((0,), (0,)), ((), ()))


def _sigmoid(x):
    return 1.0 / (1.0 + jnp.exp(-x))


def _silu(x):
    return x * _sigmoid(x)


def _softplus(x):
    return jnp.maximum(x, 0.0) + jnp.log1p(jnp.exp(-jnp.abs(x)))


def _rms(x, eps=EPS):
    return x * lax.rsqrt(jnp.mean(x * x, axis=-1, keepdims=True) + eps)


def _params(sem=None):
    return pltpu.CompilerParams(dimension_semantics=sem, vmem_limit_bytes=VMEM_LIMIT)


def _const_spec(shape, single=False):
    nd = len(shape)
    mode = dict(pipeline_mode=pl.Buffered(1)) if single else {}
    return pl.BlockSpec(shape, lambda *_: (0,) * nd, **mode)


def _ada_kernel(c_ref, w_ref, b_ref, o_ref):
    c = c_ref[...]
    s = _silu(c).astype(BF16)
    o_ref[...] = jnp.dot(s, w_ref[...].astype(BF16), preferred_element_type=F32) + b_ref[...]


def _ada(c, w_ada, b_ada):
    m, d = c.shape
    n = w_ada.shape[1]
    tn = 512
    return pl.pallas_call(
        _ada_kernel,
        out_shape=jax.ShapeDtypeStruct((m, n), F32),
        grid=(n // tn,),
        in_specs=[_const_spec((m, d)),
                  pl.BlockSpec((d, tn), lambda j: (0, j)),
                  pl.BlockSpec((1, tn), lambda j: (0, j))],
        out_specs=pl.BlockSpec((m, tn), lambda j: (0, j)),
        compiler_params=_params(("arbitrary",)),
        name="ada",
    )(c, w_ada, b_ada.reshape(1, n))


def _mod_rows(mod_ref, per_row, rows_per_batch, tile, d):
    n = mod_ref.shape[1] // d
    if per_row:
        return [mod_ref[:, k * d:(k + 1) * d] for k in range(n)]
    b = (pl.program_id(0) * tile) // rows_per_batch
    return [mod_ref[pl.ds(b, 1), k * d:(k + 1) * d] for k in range(n)]


def _mod_spec(mod, per_row, tile):
    if per_row:
        return pl.BlockSpec((tile, mod.shape[1]), lambda i: (i, 0))
    return _const_spec(mod.shape)


def _in_kernel(x_ref, mod_ref, g_ref, w_ref, *out_refs, per_row, rows_per_batch, widths):
    tile, d = x_ref.shape
    shift, scale = _mod_rows(mod_ref, per_row, rows_per_batch, tile, d)
    u = (_rms(x_ref[...]) * g_ref[...]) * (1.0 + scale) + shift
    u = u.astype(BF16)
    col = 0
    for ref, width in zip(out_refs, widths):
        for a in range(0, width, 512):
            bw = min(512, width - a)
            r = jnp.dot(u, w_ref[:, col + a:col + a + bw], preferred_element_type=F32)
            ref[:, a:a + bw] = r.astype(ref.dtype)
        col += width


def _in_proj(x, mod, g, w_bf16, widths, out_dtype, per_row, rows_per_batch, tile):
    t, d = x.shape
    n = w_bf16.shape[1]
    mod_spec = _mod_spec(mod, per_row, tile)
    dts = [out_dtype] * (len(widths) - 1) + [F32]
    return pl.pallas_call(
        functools.partial(_in_kernel, per_row=per_row, rows_per_batch=rows_per_batch, widths=widths),
        out_shape=[jax.ShapeDtypeStruct((t, wd), dt) for wd, dt in zip(widths, dts)],
        grid=(t // tile,),
        in_specs=[pl.BlockSpec((tile, d), lambda i: (i, 0)), mod_spec,
                  _const_spec((1, d)), _const_spec((d, n), single=True)],
        out_specs=[pl.BlockSpec((tile, wd), lambda i: (i, 0)) for wd in widths],
        compiler_params=_params(("arbitrary",)),
        name="in_proj",
    )(x, mod, g.reshape(1, d), w_bf16)


def _ssd_kernel(*refs, q, rows_in, q_valid, has_init, heads, widths, n_sub):
    it = iter(refs)
    if widths is None:
        z_ref, xbc_ref, scb_ref, scc_ref, sch_ref, dt_ref = (next(it) for _ in range(6))
    else:
        x_ref, mod_ref, gpre_ref, win_ref = (next(it) for _ in range(4))
    if has_init:
        ssm0_ref, cst0_ref, scst0_ref = (next(it) for _ in range(3))
    (cw_ref, cb_ref, dtb_ref, alog_ref, dsk_ref, gssd_ref, scw_ref, gsc_ref,
     tril_ref, e_ref, et_ref) = (next(it) for _ in range(11))
    ymix_ref, ssm_ref, cst_ref, scst_ref = (next(it) for _ in range(4))
    h_scr, ext_scr, extv_scr = (next(it) for _ in range(3))

    c = pl.program_id(1)
    nc = pl.num_programs(1)
    ssd_w = dsk_ref.shape[1]
    gw = ssd_w // SSD_GROUPS
    hpg = heads // SSD_GROUPS
    n_state = D_STATE
    head0 = SUBLANES - (SSD_CONV_W - 1)
    headv = SUBLANES - (SC_CONV_W - 1)

    @pl.when(c == 0)
    def _():
        if has_init:
            h_scr[...] = ssm0_ref[0]
            ext_scr[head0:SUBLANES, :] = cst0_ref[0]
            extv_scr[headv:SUBLANES, :] = scst0_ref[0]
        else:
            h_scr[...] = jnp.zeros(h_scr.shape, F32)
            ext_scr[0:SUBLANES, :] = jnp.zeros((SUBLANES, ext_scr.shape[1]), F32)
            extv_scr[0:SUBLANES, :] = jnp.zeros((SUBLANES, extv_scr.shape[1]), F32)

    def pad_rows(v, n):
        if v.shape[0] == n:
            return v
        return jnp.concatenate([v, jnp.zeros((n - v.shape[0], v.shape[1]), v.dtype)], axis=0)

    load = lambda ref: pad_rows(ref[...].astype(F32), q)
    pad_t = lambda v: pad_rows(v, LANES)
    if widths is None:
        z, xbc, scb, scc, sch, dt_raw = (load(r) for r in (z_ref, xbc_ref, scb_ref, scc_ref, sch_ref, dt_ref))
    else:
        d_model = x_ref.shape[1]
        b = pl.program_id(0)
        shift = mod_ref[pl.ds(b, 1), 0:d_model]
        scale = mod_ref[pl.ds(b, 1), d_model:2 * d_model]
        u = ((_rms(x_ref[...]) * gpre_ref[...]) * (1.0 + scale) + shift).astype(BF16)
        pieces, col = [], 0
        for width in widths:
            parts = [jnp.dot(u, win_ref[:, col + a:col + a + min(512, width - a)], preferred_element_type=F32)
                     for a in range(0, width, 512)]
            pieces.append(parts[0] if len(parts) == 1 else jnp.concatenate(parts, axis=1))
            col += width
        z, xbc, scb, scc, sch = (p.astype(BF16).astype(F32) for p in pieces[:5])
        dt_raw = pieces[5]

    def scan_chunk(j, z, xbc, scb, scc, sch, dt_raw):
        def taps(x, scr, n_taps):
            if q != CHUNK:
                first = SUBLANES - (n_taps - 1)
                return [scr[first + k:first + k + q, :] for k in range(n_taps)]
            hist = scr[0:SUBLANES, :]
            row8 = lax.broadcasted_iota(I32, hist.shape, 0)
            out = []
            for s in range(n_taps - 1, 0, -1):
                xr = pltpu.roll(x, s, axis=0)
                top = jnp.where(row8 < s, pltpu.roll(hist, s, axis=0), xr[0:SUBLANES])
                out.append(jnp.concatenate([top, xr[SUBLANES:]], axis=0))
            return out + [x]

        ext_scr[SUBLANES:SUBLANES + q, :] = xbc
        xbc_taps = taps(xbc, ext_scr, SSD_CONV_W)
        conv = cb_ref[...]
        for k in range(SSD_CONV_W):
            conv = conv + cw_ref[k:k + 1, :] * xbc_taps[k]
        xc = _silu(conv)
        xs = xc[:, :ssd_w]
        bm = xc[:, ssd_w:ssd_w + SSD_GROUPS * n_state]
        cm = xc[:, ssd_w + SSD_GROUPS * n_state:]

        lane = lax.broadcasted_iota(I32, (q, LANES), 1)
        row = lax.broadcasted_iota(I32, (q, LANES), 0)
        dt = _softplus(dt_raw + dtb_ref[...])
        dt = jnp.where(jnp.logical_and(lane < heads, row < q_valid), dt, 0.0)
        a = dt * (-jnp.exp(alog_ref[...]))
        acum = jnp.dot(tril_ref[...], pad_t(a), precision=HI, preferred_element_type=F32)
        acum_t = pad_t(acum).T
        a_last = acum[q - 1:q, :]
        e = e_ref[...]
        expand = lambda v: jnp.dot(v.astype(BF16), e, preferred_element_type=F32)
        dt_e = expand(dt)
        dtdte_e = expand(dt * jnp.exp(a_last - acum))
        exa_e = expand(jnp.exp(acum))
        xdt = xs * dt_e
        xw_b = (xs * dtdte_e).astype(BF16)
        dlast = jnp.broadcast_to(jnp.exp(acum_t[:, q - 1:q]), (LANES, n_state))
        et_b = et_ref[...].astype(BF16)
        dcol, rest = None, dlast
        for _ in range(3):
            term = rest.astype(BF16)
            part = jnp.dot(et_b, term, preferred_element_type=F32)
            dcol = part if dcol is None else dcol + part
            rest = rest - term.astype(F32)

        tri = row >= lane
        xdt_t = pad_t(xdt)
        xw_t = pad_t(xw_b)
        lane_t = lax.broadcasted_iota(I32, (LANES, LANES), 1)
        y_groups = []
        for g in range(SSD_GROUPS):
            bm_g = bm[:, g * n_state:(g + 1) * n_state].astype(BF16)
            cm_g = cm[:, g * n_state:(g + 1) * n_state].astype(BF16)
            bm_t = pad_t(bm_g)
            cb = lax.dot_general(cm_g, bm_t, NT, preferred_element_type=F32)
            h_g = h_scr[g * gw:(g + 1) * gw, :]
            y_off = lax.dot_general(cm_g, h_g.astype(BF16), NT, preferred_element_type=F32)
            parts = []
            for pair in range(hpg // 2):
                lo = (g * hpg + 2 * pair) * SSD_HEAD_DIM
                x_pair = xdt_t[:, lo:lo + LANES]
                acc = None
                for half in range(2):
                    h = g * hpg + 2 * pair + half
                    ci = jnp.broadcast_to(acum[:, h:h + 1], (q, LANES))
                    rj = jnp.broadcast_to(acum_t[h:h + 1, :], (q, LANES))
                    dec = jnp.where(tri, jnp.exp(ci - rj), 0.0)
                    m = (cb * dec).astype(BF16)
                    own = (lane_t >= SSD_HEAD_DIM) if half else (lane_t < SSD_HEAD_DIM)
                    y_h = jnp.dot(m, jnp.where(own, x_pair, 0.0).astype(BF16), preferred_element_type=F32)
                    acc = y_h if acc is None else acc + y_h
                parts.append(acc)
            y_diag = jnp.concatenate(parts, axis=1)
            y_groups.append(y_diag + y_off * exa_e[:, g * gw:(g + 1) * gw])
            upd = lax.dot_general(xw_t[:, g * gw:(g + 1) * gw], bm_t, TN, preferred_element_type=F32)
            h_scr[g * gw:(g + 1) * gw, :] = h_g * dcol[g * gw:(g + 1) * gw, :] + upd

        y = (jnp.concatenate(y_groups, axis=1) + xs * dsk_ref[...]) * _silu(z)
        y = jnp.concatenate([_rms(y[:, g * gw:(g + 1) * gw]) for g in range(SSD_GROUPS)], axis=1)
        y_ssd = y * gssd_ref[...]

        v = scc * sch
        extv_scr[SUBLANES:SUBLANES + q, :] = v
        v_taps = taps(v, extv_scr, SC_CONV_W)
        cv =scw_ref[0:1, :] * v_taps[0]
        for k in range(1, SC_CONV_W):
            cv = cv + scw_ref[k:k + 1, :] * v_taps[k]
        t = scb * cv
        sc_per_group = t.shape[1] // SC_GROUPS
        gsum = jnp.dot((t * t).astype(BF16), et_ref[...].astype(BF16), preferred_element_type=F32)
        rs = lax.rsqrt(gsum * (1.0 / sc_per_group) + EPS)
        y_sc = t * expand(rs) * gsc_ref[...]

        out_rows = slice(j * rows_in, (j + 1) * rows_in)
        ymix_ref[out_rows, :ssd_w] = y_ssd[0:rows_in].astype(ymix_ref.dtype)
        ymix_ref[out_rows, ssd_w:] = y_sc[0:rows_in].astype(ymix_ref.dtype)

        if j == n_sub - 1:
            @pl.when(c == nc - 1)
            def _():
                ssm_ref[0] = h_scr[...]
                cst_ref[0] = ext_scr[SUBLANES + q_valid - (SSD_CONV_W - 1):SUBLANES + q_valid, :]
                scst_ref[0] = extv_scr[SUBLANES + q_valid - (SC_CONV_W - 1):SUBLANES + q_valid, :]

        ext_scr[0:SUBLANES, :] = ext_scr[q:q + SUBLANES, :]
        extv_scr[0:SUBLANES, :] = extv_scr[q:q + SUBLANES, :]

    for j in range(n_sub):
        scan_chunk(j, *(p[j * q:(j + 1) * q] for p in (z, xbc, scb, scc, sch, dt_raw)))


def _ssd(proj, init, consts, nb, n_chunks, q, rows_in, q_valid, ymix_dtype, widths=None, n_sub=1):
    heads = consts["heads"]
    tril = jnp.tril(jnp.ones((q, LANES), F32))
    ssd_w, conv_dim, sc_w = consts["g_ssd"].shape[1], consts["conv_w"].shape[1], consts["g_sc"].shape[1]
    has_init = init is not None
    assert n_chunks % n_sub == 0 and (n_sub == 1 or rows_in == q)
    nc = n_chunks // n_sub
    row_spec = lambda w: pl.BlockSpec((n_sub * rows_in, w), lambda b, c: (b * nc + c, 0))
    if widths is None:
        in_specs = [row_spec(ssd_w), row_spec(conv_dim), row_spec(sc_w), row_spec(sc_w), row_spec(sc_w),
                    row_spec(LANES)]
    else:
        x, mod, g, w = proj
        in_specs = [row_spec(x.shape[1]), _const_spec(mod.shape), _const_spec(g.shape),
                    _const_spec(w.shape, single=True)]
    args = list(proj)
    if has_init:
        ssm0, cst0, scst0 = init
        in_specs += [pl.BlockSpec((1,) + ssm0.shape[1:], lambda b, c: (b, 0, 0)),
                     pl.BlockSpec((1,) + cst0.shape[1:], lambda b, c: (b, 0, 0)),
                     pl.BlockSpec((1,) + scst0.shape[1:], lambda b, c: (b, 0, 0))]
        args += [ssm0, cst0, scst0]
    weights = [consts[k] for k in ("conv_w", "conv_b", "dt_bias", "a_log", "d_skip_e", "g_ssd",
                                   "sc_w", "g_sc")] + [tril, consts["e"], consts["et"]]
    in_specs += [_const_spec(w.shape) for w in weights]
    args += weights
    n_state = D_STATE
    out_shape = [jax.ShapeDtypeStruct((nb * n_chunks * rows_in, ssd_w + sc_w), ymix_dtype),
                 jax.ShapeDtypeStruct((nb, ssd_w, n_state), F32),
                 jax.ShapeDtypeStruct((nb, SSD_CONV_W - 1, conv_dim), F32),
                 jax.ShapeDtypeStruct((nb, SC_CONV_W - 1, sc_w), F32)]
    out_specs = [pl.BlockSpec((n_sub * rows_in, ssd_w + sc_w), lambda b, c: (b * nc + c, 0)),
                 pl.BlockSpec((1, ssd_w, n_state), lambda b, c: (b, 0, 0)),
                 pl.BlockSpec((1, SSD_CONV_W - 1, conv_dim), lambda b, c: (b, 0, 0)),
                 pl.BlockSpec((1, SC_CONV_W - 1, sc_w), lambda b, c: (b, 0, 0))]
    scratch = [pltpu.VMEM((ssd_w, n_state), F32),
               pltpu.VMEM((q + SUBLANES, conv_dim), F32),
               pltpu.VMEM((q + SUBLANES, sc_w), F32)]
    return pl.pallas_call(
        functools.partial(_ssd_kernel, q=q, rows_in=rows_in, q_valid=q_valid, has_init=has_init, heads=heads,
                          widths=widths, n_sub=n_sub),
        out_shape=out_shape, grid=(nb, nc), in_specs=in_specs, out_specs=out_specs,
        scratch_shapes=scratch,
        compiler_params=_params(("arbitrary", "arbitrary")),
        name="ssd",
    )(*args)


def _out_kernel(ymix_ref, x_ref, mod_ref, wout_ref, gpost_ref, gpre_ref, wr_hi_ref, wr_lo_ref,
                *rest, per_row, rows_per_batch, has_alias):
    x1_ref, u2_ref, lg_ref = rest[3:6] if has_alias else rest[0:3]
    tile, d = x_ref.shape
    gate1, shift2, scale2 = _mod_rows(mod_ref, per_row, rows_per_batch, tile, d)
    m = jnp.dot(ymix_ref[...].astype(BF16), wout_ref[...], preferred_element_type=F32)
    x1 = x_ref[...] + gate1 * (_rms(m) * gpost_ref[...])
    u2 = (_rms(x1) * gpre_ref[...]) * (1.0 + scale2) + shift2
    x1_ref[...] = x1
    u2_ref[...] = u2
    u_hi = u2.astype(BF16)
    u_lo = (u2 - u_hi.astype(F32)).astype(BF16)
    lg = lax.dot_general(wr_hi_ref[...], u_hi, NT, preferred_element_type=F32)
    lg = lg + lax.dot_general(wr_hi_ref[...], u_lo, NT, preferred_element_type=F32)
    lg = lg + lax.dot_general(wr_lo_ref[...], u_hi, NT, preferred_element_type=F32)
    lg_ref[...] = lg


def _out_proj(ymix, x, mod, consts, per_row, rows_per_batch, tile, t_total, tile_off, prev):
    t, d = x.shape
    dm = ymix.shape[1]
    ne = consts["wr_hi_t"].shape[0]
    in_specs = [pl.BlockSpec((tile, dm), lambda i: (i, 0)),
                pl.BlockSpec((tile, d), lambda i: (i, 0)), _mod_spec(mod, per_row, tile),
                _const_spec((dm, d)), _const_spec((1, d)), _const_spec((1, d)),
                _const_spec((ne, d)), _const_spec((ne, d))]
    args = [ymix, x, mod, consts["w_out"], consts["g_post_mix"], consts["g_pre_ffn"],
            consts["wr_hi_t"], consts["wr_lo_t"]]
    aliases = {}
    if prev is not None:
        in_specs += [pl.BlockSpec(memory_space=pl.ANY)] * 3
        aliases = {len(args) + k: k for k in range(3)}
        args += list(prev)
    return pl.pallas_call(
        functools.partial(_out_kernel, per_row=per_row, rows_per_batch=rows_per_batch,
                          has_alias=prev is not None),
        out_shape=[jax.ShapeDtypeStruct((t_total, d), F32), jax.ShapeDtypeStruct((t_total, d), F32),
                   jax.ShapeDtypeStruct((ne, t_total), F32)],
        grid=(t // tile,), in_specs=in_specs,
        out_specs=[pl.BlockSpec((tile, d), lambda i: (i + tile_off, 0)),
                   pl.BlockSpec((tile, d), lambda i: (i + tile_off, 0)),
                   pl.BlockSpec((ne, tile), lambda i: (0, i + tile_off))],
        input_output_aliases=aliases,
        compiler_params=_params(("arbitrary",)),
        name="out_proj",
    )(*args)


def _route_kernel(lg_ref, bias_ref, upper_ref, idx_ref, rank_ref, wtok_ref, cnt_ref, carry_scr):
    i = pl.program_id(0)
    ne, tm = lg_ref.shape
    per_group = ne // N_EXPERT_GROUPS
    neg = -jnp.inf

    @pl.when(i == 0)
    def _():
        carry_scr[...] = jnp.zeros(carry_scr.shape, F32)

    s = _sigmoid(lg_ref[...])
    biased = s + bias_ref[...]
    gl = []
    io_g = lax.broadcasted_iota(I32, (per_group, tm), 0).astype(F32)
    for g in range(N_EXPERT_GROUPS):
        blk = biased[g * per_group:(g + 1) * per_group, :]
        m1 = jnp.max(blk, axis=0, keepdims=True)
        f1 = jnp.min(jnp.where(blk == m1, io_g, float(per_group)), axis=0, keepdims=True)
        m2 = jnp.max(jnp.where(io_g == f1, neg, blk), axis=0, keepdims=True)
        gl.append(m1 + m2)
    gscore = jnp.concatenate(gl, axis=0)
    io8 = lax.broadcasted_iota(I32, (N_EXPERT_GROUPS, tm), 0).astype(F32)
    gsel = jnp.zeros((N_EXPERT_GROUPS, tm), F32)
    for _ in range(TOPK_GROUPS):
        m = jnp.max(gscore, axis=0, keepdims=True)
        f = jnp.min(jnp.where(gscore == m, io8, float(N_EXPERT_GROUPS)), axis=0, keepdims=True)
        hit = io8 == f
        gsel = jnp.where(hit, 1.0, gsel)
        gscore = jnp.where(hit, neg, gscore)
    emask = jnp.concatenate(
        [jnp.broadcast_to(gsel[g:g + 1, :], (per_group, tm)) for g in range(N_EXPERT_GROUPS)], axis=0)
    cand = jnp.where(emask > 0.5, biased, neg)
    io_e = lax.broadcasted_iota(I32, (ne, tm), 0).astype(F32)
    msel = jnp.zeros((ne, tm), F32)
    idxs, wts = [], []
    for _ in range(TOP_K):
        m = jnp.max(cand, axis=0, keepdims=True)
        f = jnp.min(jnp.where(cand == m, io_e, float(ne)), axis=0, keepdims=True)
        hit = io_e == f
        wts.append(jnp.sum(jnp.where(hit, s, 0.0), axis=0, keepdims=True))
        idxs.append(f)
        msel = jnp.where(hit, 1.0, msel)
        cand = jnp.where(hit, neg, cand)
    pref = jnp.dot(msel.astype(BF16), upper_ref[...], preferred_element_type=F32) + carry_scr[:, 0:1]
    ranks = [jnp.sum(jnp.where(io_e == f, pref, 0.0), axis=0, keepdims=True) for f in idxs]
    carry_scr[...] = carry_scr[...] + jnp.sum(msel, axis=1, keepdims=True)
    cnt_ref[...] = carry_scr[...].astype(I32)
    idx_ref[...] = jnp.concatenate(idxs, axis=0).astype(I32)
    rank_ref[...] = jnp.concatenate(ranks, axis=0).astype(I32)
    wsum = wts[0]
    for w in wts[1:]:
        wsum = wsum + w
    wn = jnp.concatenate([w / wsum * ROUTED_SCALE for w in wts]
                         + [jnp.zeros((LANES - TOP_K, tm), F32)], axis=0)
    for j in range(tm // LANES):
        wtok_ref[j * LANES:(j + 1) * LANES, :] = wn[:, j * LANES:(j + 1) * LANES].T


def _route(logits_t, bias, tile):
    ne, t = logits_t.shape
    upper = jnp.triu(jnp.ones((tile, tile), F32), 1).astype(BF16)
    return pl.pallas_call(
        _route_kernel,
        out_shape=[jax.ShapeDtypeStruct((TOP_K, t), I32), jax.ShapeDtypeStruct((TOP_K, t), I32),
                   jax.ShapeDtypeStruct((t, LANES), F32), jax.ShapeDtypeStruct((ne, LANES), I32)],
        grid=(t // tile,),
        in_specs=[pl.BlockSpec((ne, tile), lambda i: (0, i)), _const_spec((ne, 1)),
                  _const_spec((tile, tile))],
        out_specs=[pl.BlockSpec((TOP_K, tile), lambda i: (0, i)),
                   pl.BlockSpec((TOP_K, tile), lambda i: (0, i)),
                   pl.BlockSpec((tile, LANES), lambda i: (i, 0)),
                   _const_spec((ne, LANES))],
        scratch_shapes=[pltpu.VMEM((ne, LANES), F32)],
        compiler_params=_params(("arbitrary",)),
        name="route",
    )(logits_t, bias.reshape(ne, 1), upper)


META_POS, META_W, META_TOK = 0, TOP_K, 2 * TOP_K


def _pos_kernel(idx_ref, rank_ref, start_ref, wtok_ref, pos_ref, meta_ref):
    ne = start_ref.shape[0]
    tm = idx_ref.shape[1]
    io_e = lax.broadcasted_iota(I32, (ne, tm), 0)
    start = start_ref[...].astype(F32)
    rows = []
    for k in range(TOP_K):
        hit = io_e == idx_ref[k:k + 1, :]
        rows.append(jnp.sum(jnp.where(hit, start, 0.0), axis=0, keepdims=True))
    pos_f = jnp.concatenate(rows, axis=0) + rank_ref[...].astype(F32)
    pos_ref[...] = pos_f.astype(I32)
    pos_pad = jnp.concatenate([pos_f, jnp.zeros((LANES - TOP_K, tm), F32)], axis=0)
    lane = lax.broadcasted_iota(I32, (LANES, LANES), 1)
    row = lax.broadcasted_iota(I32, (LANES, LANES), 0)
    for j in range(tm // LANES):
        pos_tok = pos_pad[:, j * LANES:(j + 1) * LANES].T
        w_tok = pltpu.roll(wtok_ref[j * LANES:(j + 1) * LANES, :], META_W, axis=1)
        tok = (row + (pl.program_id(0) * tm + j * LANES)).astype(F32)
        tag = jnp.where(lane < META_W, pos_tok, jnp.where(lane == META_TOK, tok, w_tok))
        meta_ref[j * LANES:(j + 1) * LANES, :] = tag


def _positions(idx_t, rank_t, pad_start, wtok, tile):
    k, t = idx_t.shape
    ne = pad_start.shape[0]
    return pl.pallas_call(
        _pos_kernel,
        out_shape=[jax.ShapeDtypeStruct((k, t), I32), jax.ShapeDtypeStruct((t, LANES), F32)],
        grid=(t // tile,),
        in_specs=[pl.BlockSpec((k, tile), lambda i: (0, i)), pl.BlockSpec((k, tile), lambda i: (0, i)),
                  _const_spec((ne, 1)), pl.BlockSpec((tile, LANES), lambda i: (i, 0))],
        out_specs=[pl.BlockSpec((k, tile), lambda i: (0, i)), pl.BlockSpec((tile, LANES), lambda i: (i, 0))],
        compiler_params=_params(("arbitrary",)),
        name="positions",
    )(idx_t, rank_t, pad_start.reshape(ne, 1), wtok)


ZERO_BITS = (64, 32, 16, 8, 4, 2, 1)
assert sum(ZERO_BITS) * SUBLANES >= ROW_TILE


def _zero_copy(zbuf, xs_ref, sem, off, bit):
    rows = bit * SUBLANES
    return pltpu.make_async_copy(zbuf.at[pl.ds(0, rows)], xs_ref.at[pl.ds(pl.multiple_of(off, SUBLANES), rows)], sem)


def _zero_kernel(first_ref, units_ref, xs_ref, zbuf, sem):
    zbuf[...] = jnp.zeros(zbuf.shape, zbuf.dtype)
    ne = first_ref.shape[0]

    def each(e, start):
        off = first_ref[e]
        units = units_ref[e]
        for bit in ZERO_BITS:
            on = (units & bit) != 0

            @pl.when(on)
            def _():
                cp = _zero_copy(zbuf, xs_ref, sem, off, bit)
                cp.start() if start else cp.wait()
            off = off + jnp.where(on, bit * SUBLANES, 0)
        return start

    lax.fori_loop(0, ne, lambda e, c: (each(e, True), c)[1], 0)
    lax.fori_loop(0, ne, lambda e, c: (each(e, False), c)[1], 0)


def _zero_rows(first, units, n_rows, width):
    return pl.pallas_call(
        _zero_kernel,
        out_shape=jax.ShapeDtypeStruct((n_rows, width), F32),
        in_specs=[pl.BlockSpec(memory_space=pltpu.SMEM), pl.BlockSpec(memory_space=pltpu.SMEM)],
        out_specs=pl.BlockSpec(memory_space=pl.ANY),
        scratch_shapes=[pltpu.VMEM((ZERO_BITS[0] * SUBLANES, width), F32), pltpu.SemaphoreType.DMA],
        compiler_params=_params(),
        name="zero_rows",
    )(first, units)


def _row_copy(src_ref, dst_ref, sem, s, d):
    return pltpu.make_async_copy(src_ref.at[pl.ds(s, 1)], dst_ref.at[pl.ds(d, 1)], sem)


def _dispatch_kernel(pos_hbm, u_ref, meta_ref, xs_ref, pos_a, pos_b, src, psem, sem, *, nt):
    i = pl.program_id(0)
    tm, d = u_ref.shape
    per = pos_a.shape[0]
    pos_slots = (pos_a, pos_b)

    def pos_copy(tile, s):
        return pltpu.make_async_copy(pos_hbm.at[pl.ds(pl.multiple_of(tile * per, per), per)], pos_slots[s],
                                     psem.at[s])

    def drain(s):
        for _ in range(TOP_K):
            pltpu.make_async_copy(src.at[s], xs_ref.at[pl.ds(0, tm)], sem.at[s]).wait()

    def step(s):
        @pl.when(i == 0)
        def _():
            pos_copy(0, 0).start()

        @pl.when(i + 1 < nt)
        def _():
            pos_copy(i + 1, 1 - s).start()

        @pl.when(i >= 2)
        def _():
            drain(s)

        src[s, :, :d] = u_ref[...]
        src[s, :, d:] = meta_ref[...]
        pos_copy(i, s).wait()

        def issue(t, carry):
            for k in range(TOP_K):
                _row_copy(src.at[s], xs_ref, sem.at[s], t, pos_slots[s][k * tm + t]).start(
                    priority=k % DMA_THREADS)
            return carry

        lax.fori_loop(0, tm, issue, 0)

        @pl.when(i == nt - 1)
        def _():
            if nt >= 2:
                drain(1 - s)
            drain(s)

    for s in range(2):
        pl.when(i % 2 == s)(functools.partial(step, s))


def _dispatch(pos_flat, u2, meta, n_rows, tm):
    per = TOP_K * tm
    nt = pos_flat.shape[0] // per
    d = u2.shape[1]
    return pl.pallas_call(
        functools.partial(_dispatch_kernel, nt=nt),
        out_shape=jax.ShapeDtypeStruct((n_rows, d + LANES), F32),
        grid=(nt,),
        in_specs=[pl.BlockSpec(memory_space=pl.ANY), pl.BlockSpec((tm, d), lambda i: (i, 0)),
                  pl.BlockSpec((tm, LANES), lambda i: (i, 0))],
        out_specs=pl.BlockSpec(memory_space=pl.ANY),
        scratch_shapes=[pltpu.SMEM((per,), I32), pltpu.SMEM((per,), I32),
                        pltpu.VMEM((2, tm, d + LANES), F32),
                        pltpu.SemaphoreType.DMA((2,)), pltpu.SemaphoreType.DMA((2,))],
        compiler_params=_params(("arbitrary",)),
        name="dispatch",
    )(pos_flat, u2, meta)


WAIT_BITS = tuple(1 << b for b in reversed(range(ROW_TILE.bit_length())))


def _expert_kernel(te_ref, nu_ref, nv_ref, xs_ref, wg_ref, wu_ref, wd_ref, g_hbm,
                   ybuf, dst_vmem, dst_smem, sem, dsem, *, n_tok):
    del te_ref
    i = pl.program_id(0)
    nu = nu_ref[0]
    d = ybuf.shape[3]
    rows = ybuf.shape[1] * SUBLANES

    def drain(tile):
        slot = tile % 2
        n = nv_ref[tile]
        for bit in WAIT_BITS:
            @pl.when((n & bit) != 0)
            def _():
                pltpu.make_async_copy(g_hbm.at[pl.ds(0, bit)], g_hbm.at[pl.ds(0, bit)], sem.at[slot]).wait()

    @pl.when(i < nu)
    def _():
        slot = i % 2

        @pl.when(i >= 2)
        def _():
            drain(i - 2)

        dst_vmem[...] = jnp.zeros(dst_vmem.shape, I32)
        xfull = xs_ref[...]
        tag = xfull[:, d:]
        x = xfull[:, :d].astype(BF16)
        lane = lax.broadcasted_iota(I32, (rows, LANES), 1)
        p = (lax.broadcasted_iota(I32, (rows, LANES), 0) + i * rows).astype(F32)
        hit = jnp.logical_and(tag == p, lane < META_W)
        w_at = pltpu.roll(tag, LANES - META_W, axis=1)
        w = jnp.sum(jnp.where(hit, w_at, 0.0), axis=1, keepdims=True)
        kf = jnp.sum(jnp.where(hit, lane.astype(F32), 0.0), axis=1, keepdims=True)
        dest = kf * float(n_tok) + tag[:, META_TOK:META_TOK + 1]

        dest_b = jnp.broadcast_to(dest, (rows, LANES))
        n_blk = -(-rows // LANES)
        dest_b = jnp.concatenate([dest_b, jnp.zeros((n_blk * LANES - rows, LANES), F32)], axis=0)
        for b in range(n_blk):
            dst_vmem[b:b + 1, :] = dest_b[b * LANES:(b + 1) * LANES, :].T[0:1, :].astype(I32)
        to_smem = [pltpu.make_async_copy(dst_vmem.at[b], dst_smem.at[pl.ds(b * LANES, LANES)], dsem)
                   for b in range(n_blk)]
        for cp in to_smem:
            cp.start()

        g = jnp.dot(x, wg_ref[0].astype(BF16), preferred_element_type=F32)
        u = jnp.dot(x, wu_ref[0].astype(BF16), preferred_element_type=F32)
        h = (_silu(g) * u).astype(BF16)
        y = jnp.dot(h, wd_ref[0].astype(BF16), preferred_element_type=F32) * w
        ybuf[slot] = y.reshape(rows // SUBLANES, SUBLANES, d)
        for cp in to_smem:
            cp.wait()

        nv = nv_ref[i]
        full = lax.shift_right_logical(nv, 3)

        def scatter_rows(s):
            def send(r8, j, prio):
                to = dst_smem[r8 * SUBLANES + j]
                pltpu.make_async_copy(ybuf.at[s, r8, pl.ds(j, 1)], g_hbm.at[pl.ds(to, 1)],
                                      sem.at[s]).start(priority=prio)

            def issue8(r8, carry):
                for j in range(SUBLANES):
                    send(r8, j, SCATTER_PRIORITY)
                return carry

            def issue1(r, carry):
                send(lax.shift_right_logical(r, 3), r & (SUBLANES - 1), SCATTER_PRIORITY)
                return carry

            lax.fori_loop(0, full, issue8, 0)
            lax.fori_loop(full * SUBLANES, nv, issue1, 0)

        for s in range(2):
            pl.when(slot == s)(functools.partial(scatter_rows, s))

    @pl.when(i == pl.num_programs(0) - 1)
    def _():
        @pl.when(nu >= 2)
        def _():
            drain(nu - 2)

        @pl.when(nu >= 1)
        def _():
            drain(nu - 1)


def _experts(tile_expert, n_used, tile_valid, xs, w_gate, w_up, w_down, n_tok):
    n_rows, width = xs.shape
    d, de = w_gate.shape[1:]
    n_tiles = n_rows // ROW_TILE
    row_map = lambda i, te, nu, nv: (jnp.minimum(i, nu[0] - 1), 0)
    w_map = lambda i, te, nu, nv: (te[i], 0, 0)
    n_blk = -(-ROW_TILE // LANES)
    return pl.pallas_call(
        functools.partial(_expert_kernel, n_tok=n_tok),
        out_shape=jax.ShapeDtypeStruct((TOP_K * n_tok, d), F32),
        grid_spec=pltpu.PrefetchScalarGridSpec(
            num_scalar_prefetch=3, grid=(n_tiles,),
            in_specs=[pl.BlockSpec((ROW_TILE, width), row_map),
                      pl.BlockSpec((1, d, de), w_map), pl.BlockSpec((1, d, de), w_map),
                      pl.BlockSpec((1, de, d), w_map)],
            out_specs=pl.BlockSpec(memory_space=pl.ANY),
            scratch_shapes=[pltpu.VMEM((2, ROW_TILE // SUBLANES, SUBLANES, d), F32),
                            pltpu.VMEM((SUBLANES, LANES), I32),
                            pltpu.SMEM((n_blk * LANES,), I32), pltpu.SemaphoreType.DMA((2,)),
                            pltpu.SemaphoreType.DMA]),
        compiler_params=_params(("arbitrary",)),
        name="experts",
    )(tile_expert, n_used, tile_valid, xs, w_gate, w_up, w_down)


def _combine_kernel(g_ref, u_ref, x1_ref, mod_ref, wsg_ref, wsu_ref, wsd_ref, gpost_ref, o_ref,
                    *, per_row, rows_per_batch):
    tm, d = u_ref.shape
    ub = u_ref[...].astype(BF16)
    hs = _silu(jnp.dot(ub, wsg_ref[...], preferred_element_type=F32)) * jnp.dot(
        ub, wsu_ref[...], preferred_element_type=F32)
    f = jnp.dot(hs.astype(BF16), wsd_ref[...], preferred_element_type=F32)
    routed = g_ref[0]
    for k in range(1, TOP_K):
        routed = routed + g_ref[k]
    f = routed + f
    (gate2,) = _mod_rows(mod_ref, per_row, rows_per_batch, tm, d)
    o_ref[...] = x1_ref[...] + gate2 * (_rms(f) * gpost_ref[...])


def _combine(g_rows, u2, x1, mod, consts, per_row, rows_per_batch, tile_off, n_tok, tm):
    t_all, d = u2.shape
    ds_ = consts["w_sh_gate"].shape[1]
    tok_spec = lambda w: pl.BlockSpec((tm, w), lambda i: (i + tile_off, 0))
    return pl.pallas_call(
        functools.partial(_combine_kernel, per_row=per_row, rows_per_batch=rows_per_batch),
        out_shape=jax.ShapeDtypeStruct((n_tok, d), F32),
        grid=(n_tok // tm,),
        in_specs=[pl.BlockSpec((TOP_K, tm, d), lambda i: (0, i + tile_off, 0)),
                  tok_spec(d), tok_spec(d), _mod_spec(mod, per_row, tm),
                  _const_spec((d, ds_)), _const_spec((d, ds_)), _const_spec((ds_, d)), _const_spec((1, d))],
        out_specs=pl.BlockSpec((tm, d), lambda i: (i, 0)),
        compiler_params=_params(("arbitrary",)),
        name="combine",
    )(g_rows.reshape(TOP_K, t_all, d), u2, x1, mod, consts["w_sh_gate"], consts["w_sh_up"],
      consts["w_sh_down"], consts["g_post_ffn"])


def _repeat_rows(x, n):
    r, c = x.shape
    return jnp.broadcast_to(x[:, None, :], (r, n, c)).reshape(r * n, c)


def _tile_major(pos_t, tile):
    k, t = pos_t.shape
    return pos_t.reshape(k, t // tile, tile).transpose(1, 0, 2).reshape(-1)


def kernel(x_prompt, x_sample, c_prompt, c_sample, state_ssm, state_ssd_conv, state_short_conv, w_ada, b_ada, g_pre_mix, g_post_mix, g_pre_ffn, g_post_ffn, w_in, ssd_conv_w, ssd_conv_b, dt_bias, a_log, d_skip, g_ssd_norm, sc_conv_w, g_sc_norm, w_out, w_router, router_bias, w_exp_gate, w_exp_up, w_exp_down, w_sh_gate, w_sh_up, w_sh_down):
    depth = w_ada.shape[0]
    bp, seq, d = x_prompt.shape
    bs, dseq, _ = x_sample.shape
    heads = dt_bias.shape[1]
    ssd_w = heads * SSD_HEAD_DIM
    conv_dim = ssd_conv_w.shape[2]
    sc_w = sc_conv_w.shape[2]
    ne = w_router.shape[2]
    tp, ts = bp * seq, bs * dseq
    t_all = tp + ts

    assert sc_w // SC_GROUPS == SSD_HEAD_DIM and heads == SC_GROUPS
    head_of = jnp.arange(ssd_w, dtype=I32) // SSD_HEAD_DIM
    e_ind = (jnp.arange(LANES, dtype=I32)[:, None] == head_of[None, :])

    xp = x_prompt.reshape(tp, d)
    xs_pad = jnp.pad(x_sample, ((0, 0), (0, SUBLANES - dseq), (0, 0))).reshape(bs * SUBLANES, d)
    xs_tok = x_sample.reshape(ts, d)
    outs = {k: [] for k in ("ssm_p", "cst_p", "scst_p", "ssm_s", "cst_s", "scst_s")}

    for layer in range(depth):
        cuts = np.cumsum([0, ssd_w, conv_dim, heads, sc_w, sc_w, sc_w]).tolist()
        wi = w_in[layer]
        seg = lambda k: wi[:, cuts[k]:cuts[k + 1]]
        w_in_r = jnp.concatenate([seg(0), seg(1), seg(3), seg(4), seg(5),
                                  jnp.pad(seg(2), ((0, 0), (0, LANES - heads)))], axis=1).astype(BF16)
        widths = (ssd_w, conv_dim, sc_w, sc_w, sc_w, LANES)
        pad_h = lambda v: jnp.pad(v.reshape(1, heads), ((0, 0), (0, LANES - heads)))
        wr = w_router[layer].T
        wr_hi = wr.astype(BF16)
        consts = dict(
            heads=heads,
            conv_w=ssd_conv_w[layer], conv_b=ssd_conv_b[layer].reshape(1, conv_dim),
            dt_bias=pad_h(dt_bias[layer]), a_log=pad_h(a_log[layer]),
            d_skip_e=jnp.broadcast_to(d_skip[layer][:, None], (heads, SSD_HEAD_DIM)).reshape(1, ssd_w),
            g_ssd=g_ssd_norm[layer].reshape(1, ssd_w), sc_w=sc_conv_w[layer],
            g_sc=g_sc_norm[layer].reshape(1, sc_w),
            e=e_ind.astype(BF16), et=e_ind.T.astype(F32),
            w_out=w_out[layer].astype(BF16), g_post_mix=g_post_mix[layer].reshape(1, d),
            g_pre_ffn=g_pre_ffn[layer].reshape(1, d),
            wr_hi_t=wr_hi, wr_lo_t=(wr - wr_hi.astype(F32)).astype(BF16),
            w_sh_gate=w_sh_gate[layer].astype(BF16), w_sh_up=w_sh_up[layer].astype(BF16),
            w_sh_down=w_sh_down[layer].astype(BF16), g_post_ffn=g_post_ffn[layer].reshape(1, d))

        c_all = jnp.concatenate([c_prompt, c_sample], axis=0)
        m_rows = -(-c_all.shape[0] // 16) * 16
        mod = _ada(jnp.pad(c_all, ((0, m_rows - c_all.shape[0]), (0, 0))), w_ada[layer], b_ada[layer])
        mod_p = mod[:bp]
        mod_s = mod[bp:bp + bs]
        mod_s_pad = _repeat_rows(mod_s[:, :2 * d], SUBLANES)
        mod_s_tok = _repeat_rows(mod_s, dseq)

        proj_s = _in_proj(xs_pad, mod_s_pad, g_pre_mix[layer], w_in_r, widths, F32, True, 1, TOK_TILE)
        ymix_p, ssm_p, cst_p, scst_p = _ssd(
            (xp, mod_p[:, :2 * d], g_pre_mix[layer].reshape(1, d), w_in_r), None, consts, bp, seq // CHUNK,
            CHUNK, CHUNK, CHUNK, BF16, widths=widths, n_sub=SCAN_SUB)
        init = (state_ssm[layer].reshape(bs, ssd_w, D_STATE), state_ssd_conv[layer], state_short_conv[layer])
        ymix_s, ssm_s, cst_s, scst_s = _ssd(proj_s, init, consts, bs, 1, SAMPLE_CHUNK, SUBLANES, dseq, F32)
        ymix_s = ymix_s.reshape(bs, SUBLANES, ssd_w + sc_w)[:, :dseq].reshape(ts, ssd_w + sc_w)

        merged = _out_proj(ymix_p, xp, mod_p[:, 2 * d:5 * d], consts, False, seq, TOK_TILE, t_all, 0, None)
        x1, u2, logits_t = _out_proj(ymix_s, xs_tok, mod_s_tok[:, 2 * d:5 * d], consts, True, 1, TOK_TILE,
                                     t_all, tp // TOK_TILE, merged)

        idx_t, rank_t, wtok, counts = _route(logits_t, router_bias[layer], TOK_TILE)
        counts = counts[:, 0]
        padded = (counts + ROW_TILE - 1) // ROW_TILE * ROW_TILE
        pad_end = jnp.cumsum(padded)
        pad_start = pad_end - padded
        n_tiles = -(-(t_all * TOP_K) // ROW_TILE) + ne
        n_used = (pad_end[-1] // ROW_TILE).astype(I32)
        tile_ids = jnp.minimum(jnp.arange(n_tiles, dtype=I32), n_used - 1)
        tile_expert = jnp.minimum(jnp.sum(pad_end[None, :] <= (tile_ids * ROW_TILE)[:, None], axis=1), ne - 1).astype(I32)
        first_pad = (pad_start + counts) // SUBLANES * SUBLANES
        units = (pad_end - first_pad) // SUBLANES
        tile_valid = jnp.clip(counts[tile_expert] - (tile_ids * ROW_TILE - pad_start[tile_expert]), 0,
                              ROW_TILE).astype(I32)
        pos_t, meta = _positions(idx_t, rank_t, pad_start.astype(I32), wtok, TOK_TILE)
        pos_flat = _tile_major(pos_t, GATHER_TILE)

        xs_rows = _dispatch(pos_flat, u2, meta, n_tiles * ROW_TILE, GATHER_TILE)
        g_rows = _experts(tile_expert, n_used.reshape(1), tile_valid, xs_rows, w_exp_gate[layer],
                          w_exp_up[layer], w_exp_down[layer], t_all)
        xp = _combine(g_rows, u2, x1, mod_p[:, 5 * d:], consts, False, seq, 0, tp, COMBINE_TILE)
        xs_tok = _combine(g_rows, u2, x1, mod_s_tok[:, 5 * d:], consts, True, 1, tp // COMBINE_TILE, ts,
                          COMBINE_TILE)
        xs_pad = jnp.pad(xs_tok.reshape(bs, dseq, d), ((0, 0), (0, SUBLANES - dseq), (0, 0))).reshape(
            bs * SUBLANES, d)

        outs["ssm_p"].append(ssm_p.reshape(bp, heads, SSD_HEAD_DIM, D_STATE))
        outs["cst_p"].append(cst_p)
        outs["scst_p"].append(scst_p)
        outs["ssm_s"].append(ssm_s.reshape(bs, heads, SSD_HEAD_DIM, D_STATE))
        outs["cst_s"].append(cst_s)
        outs["scst_s"].append(scst_s)

    return (xp.reshape(bp, seq, d), xs_tok.reshape(bs, dseq, d),
            jnp.stack(outs["ssm_p"]), jnp.stack(outs["cst_p"]), jnp.stack(outs["scst_p"]),
            jnp.stack(outs["ssm_s"]), jnp.stack(outs["cst_s"]), jnp.stack(outs["scst_s"]))
```

```python
import functools

import jax
import jax.numpy as jnp
import numpy as np
from jax import lax
from jax.experimental import pallas as pl
from jax.experimental.pallas import tpu as pltpu

F32 = jnp.float32
BF16 = jnp.bfloat16
I32 = jnp.int32
U32 = jnp.uint32

SSD_HEAD_DIM = 64
SSD_GROUPS = 2
D_STATE = 128
SSD_CONV_W = 4
SC_GROUPS = 16
SC_CONV_W = 3
TOP_K = 8
N_EXPERT_GROUPS = 8
TOPK_GROUPS = 4
ROUTED_SCALE = 2.5
EPS = 1e-6

LANES = 128
SUBLANES = 8
CHUNK = 128
SAMPLE_CHUNK = 16
SCAN_SUB = 4
TOK_TILE = 512
ROW_TILE = 576
GATHER_TILE = 512
COMBINE_TILE = 512
SCATTER_PRIORITY = 1
VMEM_LIMIT = 56 * 1024 * 1024
DMA_THREADS = 2

NT = (((1,), (1,)), ((), ()))
TN = (((0,), (0,)), ((), ()))


def _sigmoid(x):
    return 1.0 / (1.0 + jnp.exp(-x))


def _silu(x):
    return x * _sigmoid(x)


def _softplus(x):
    return jnp.maximum(x, 0.0) + jnp.log1p(jnp.exp(-jnp.abs(x)))


def _rms(x, eps=EPS):
    return x * lax.rsqrt(jnp.mean(x * x, axis=-1, keepdims=True) + eps)


def _params(sem=None):
    return pltpu.CompilerParams(dimension_semantics=sem, vmem_limit_bytes=VMEM_LIMIT)


def _const_spec(shape, single=False):
    nd = len(shape)
    mode = dict(pipeline_mode=pl.Buffered(1)) if single else {}
    return pl.BlockSpec(shape, lambda *_: (0,) * nd, **mode)


def _ada_kernel(c_ref, w_ref, b_ref, o_ref):
    c = c_ref[...]
    s = _silu(c).astype(BF16)
    o_ref[...] = jnp.dot(s, w_ref[...].astype(BF16), preferred_element_type=F32) + b_ref[...]


def _ada(c, w_ada, b_ada):
    m, d = c.shape
    n = w_ada.shape[1]
    tn = 512
    return pl.pallas_call(
        _ada_kernel,
        out_shape=jax.ShapeDtypeStruct((m, n), F32),
        grid=(n // tn,),
        in_specs=[_const_spec((m, d)),
                  pl.BlockSpec((d, tn), lambda j: (0, j)),
                  pl.BlockSpec((1, tn), lambda j: (0, j))],
        out_specs=pl.BlockSpec((m, tn), lambda j: (0, j)),
        compiler_params=_params(("arbitrary",)),
        name="ada",
    )(c, w_ada, b_ada.reshape(1, n))


def _mod_rows(mod_ref, per_row, rows_per_batch, tile, d):
    n = mod_ref.shape[1] // d
    if per_row:
        return [mod_ref[:, k * d:(k + 1) * d] for k in range(n)]
    b = (pl.program_id(0) * tile) // rows_per_batch
    return [mod_ref[pl.ds(b, 1), k * d:(k + 1) * d] for k in range(n)]


def _mod_spec(mod, per_row, tile):
    if per_row:
        return pl.BlockSpec((tile, mod.shape[1]), lambda i: (i, 0))
    return _const_spec(mod.shape)


def _in_kernel(x_ref, mod_ref, g_ref, w_ref, *out_refs, per_row, rows_per_batch, widths):
    tile, d = x_ref.shape
    shift, scale = _mod_rows(mod_ref, per_row, rows_per_batch, tile, d)
    u = (_rms(x_ref[...]) * g_ref[...]) * (1.0 + scale) + shift
    u = u.astype(BF16)
    col = 0
    for ref, width in zip(out_refs, widths):
        for a in range(0, width, 512):
            bw = min(512, width - a)
            r = jnp.dot(u, w_ref[:, col + a:col + a + bw], preferred_element_type=F32)
            ref[:, a:a + bw] = r.astype(ref.dtype)
        col += width


def _in_proj(x, mod, g, w_bf16, widths, out_dtype, per_row, rows_per_batch, tile):
    t, d = x.shape
    n = w_bf16.shape[1]
    mod_spec = _mod_spec(mod, per_row, tile)
    dts = [out_dtype] * (len(widths) - 1) + [F32]
    return pl.pallas_call(
        functools.partial(_in_kernel, per_row=per_row, rows_per_batch=rows_per_batch, widths=widths),
        out_shape=[jax.ShapeDtypeStruct((t, wd), dt) for wd, dt in zip(widths, dts)],
        grid=(t // tile,),
        in_specs=[pl.BlockSpec((tile, d), lambda i: (i, 0)), mod_spec,
                  _const_spec((1, d)), _const_spec((d, n), single=True)],
        out_specs=[pl.BlockSpec((tile, wd), lambda i: (i, 0)) for wd in widths],
        compiler_params=_params(("arbitrary",)),
        name="in_proj",
    )(x, mod, g.reshape(1, d), w_bf16)


def _ssd_kernel(*refs, q, rows_in, q_valid, has_init, heads, widths, n_sub):
    it = iter(refs)
    if widths is None:
        z_ref, xbc_ref, scb_ref, scc_ref, sch_ref, dt_ref = (next(it) for _ in range(6))
    else:
        x_ref, mod_ref, gpre_ref, win_ref = (next(it) for _ in range(4))
    if has_init:
        ssm0_ref, cst0_ref, scst0_ref = (next(it) for _ in range(3))
    (cw_ref, cb_ref, dtb_ref, alog_ref, dsk_ref, gssd_ref, scw_ref, gsc_ref,
     tril_ref, e_ref, et_ref) = (next(it) for _ in range(11))
    ymix_ref, ssm_ref, cst_ref, scst_ref = (next(it) for _ in range(4))
    h_scr, ext_scr, extv_scr = (next(it) for _ in range(3))

    c = pl.program_id(1)
    nc = pl.num_programs(1)
    ssd_w = dsk_ref.shape[1]
    gw = ssd_w // SSD_GROUPS
    hpg = heads // SSD_GROUPS
    n_state = D_STATE
    head0 = SUBLANES - (SSD_CONV_W - 1)
    headv = SUBLANES - (SC_CONV_W - 1)

    @pl.when(c == 0)
    def _():
        if has_init:
            h_scr[...] = ssm0_ref[0]
            ext_scr[head0:SUBLANES, :] = cst0_ref[0]
            extv_scr[headv:SUBLANES, :] = scst0_ref[0]
        else:
            h_scr[...] = jnp.zeros(h_scr.shape, F32)
            ext_scr[0:SUBLANES, :] = jnp.zeros((SUBLANES, ext_scr.shape[1]), F32)
            extv_scr[0:SUBLANES, :] = jnp.zeros((SUBLANES, extv_scr.shape[1]), F32)

    def pad_rows(v, n):
        if v.shape[0] == n:
            return v
        return jnp.concatenate([v, jnp.zeros((n - v.shape[0], v.shape[1]), v.dtype)], axis=0)

    load = lambda ref: pad_rows(ref[...].astype(F32), q)
    pad_t = lambda v: pad_rows(v, LANES)
    if widths is None:
        z, xbc, scb, scc, sch, dt_raw = (load(r) for r in (z_ref, xbc_ref, scb_ref, scc_ref, sch_ref, dt_ref))
    else:
        d_model = x_ref.shape[1]
        b = pl.program_id(0)
        shift = mod_ref[pl.ds(b, 1), 0:d_model]
        scale = mod_ref[pl.ds(b, 1), d_model:2 * d_model]
        u = ((_rms(x_ref[...]) * gpre_ref[...]) * (1.0 + scale) + shift).astype(BF16)
        pieces, col = [], 0
        for width in widths:
            parts = [jnp.dot(u, win_ref[:, col + a:col + a + min(512, width - a)], preferred_element_type=F32)
                     for a in range(0, width, 512)]
            pieces.append(parts[0] if len(parts) == 1 else jnp.concatenate(parts, axis=1))
            col += width
        z, xbc, scb, scc, sch = (p.astype(BF16).astype(F32) for p in pieces[:5])
        dt_raw = pieces[5]

    def scan_chunk(j, z, xbc, scb, scc, sch, dt_raw):
        def taps(x, scr, n_taps):
            if q != CHUNK:
                first = SUBLANES - (n_taps - 1)
                return [scr[first + k:first + k + q, :] for k in range(n_taps)]
            hist = scr[0:SUBLANES, :]
            row8 = lax.broadcasted_iota(I32, hist.shape, 0)
            out = []
            for s in range(n_taps - 1, 0, -1):
                xr = pltpu.roll(x, s, axis=0)
                top = jnp.where(row8 < s, pltpu.roll(hist, s, axis=0), xr[0:SUBLANES])
                out.append(jnp.concatenate([top, xr[SUBLANES:]], axis=0))
            return out + [x]

        ext_scr[SUBLANES:SUBLANES + q, :] = xbc
        xbc_taps = taps(xbc, ext_scr, SSD_CONV_W)
        conv = cb_ref[...]
        for k in range(SSD_CONV_W):
            conv = conv + cw_ref[k:k + 1, :] * xbc_taps[k]
        xc = _silu(conv)
        xs = xc[:, :ssd_w]
        bm = xc[:, ssd_w:ssd_w + SSD_GROUPS * n_state]
        cm = xc[:, ssd_w + SSD_GROUPS * n_state:]

        lane = lax.broadcasted_iota(I32, (q, LANES), 1)
        row = lax.broadcasted_iota(I32, (q, LANES), 0)
        dt = _softplus(dt_raw + dtb_ref[...])
        dt = jnp.where(jnp.logical_and(lane < heads, row < q_valid), dt, 0.0)
        a = dt * (-jnp.exp(alog_ref[...]))
        tril_b = tril_ref[...].astype(BF16)
        acum, rest = None, pad_t(a)
        for _ in range(3):
            term = rest.astype(BF16)
            part = jnp.dot(tril_b, term, preferred_element_type=F32)
            acum = part if acum is None else acum + part
            rest = rest - term.astype(F32)
        acum_t = pad_t(acum).T
        a_last = acum[q - 1:q, :]
        e = e_ref[...]
        expand = lambda v: jnp.dot(v.astype(BF16), e, preferred_element_type=F32)
        dt_e = expand(dt)
        dtdte_e = expand(dt * jnp.exp(a_last - acum))
        exa_e = expand(jnp.exp(acum))
        xdt = xs * dt_e
        xw_b = (xs * dtdte_e).astype(BF16)
        dlast = jnp.broadcast_to(jnp.exp(acum_t[:, q - 1:q]), (LANES, n_state))
        et_b = et_ref[...].astype(BF16)
        dcol, rest = None, dlast
        for _ in range(3):
            term = rest.astype(BF16)
            part = jnp.dot(et_b, term, preferred_element_type=F32)
            dcol = part if dcol is None else dcol + part
            rest = rest - term.astype(F32)

        tri = row >= lane
        xdt_t = pad_t(xdt)
        xw_t = pad_t(xw_b)
        lane_t = lax.broadcasted_iota(I32, (LANES, LANES), 1)
        y_groups = []
        for g in range(SSD_GROUPS):
            bm_g = bm[:, g * n_state:(g + 1) * n_state].astype(BF16)
            cm_g = cm[:, g * n_state:(g + 1) * n_state].astype(BF16)
            bm_t = pad_t(bm_g)
            cb = lax.dot_general(cm_g, bm_t, NT, preferred_element_type=F32)
            h_g = h_scr[g * gw:(g + 1) * gw, :]
            y_off = lax.dot_general(cm_g, h_g.astype(BF16), NT, preferred_element_type=F32)
            parts = []
            for pair in range(hpg // 2):
                lo = (g * hpg + 2 * pair) * SSD_HEAD_DIM
                x_pair = xdt_t[:, lo:lo + LANES]
                acc = None
                for half in range(2):
                    h = g * hpg + 2 * pair + half
                    ci = jnp.broadcast_to(acum[:, h:h + 1], (q, LANES))
                    rj = jnp.broadcast_to(acum_t[h:h + 1, :], (q, LANES))
                    dec = jnp.where(tri, jnp.exp(ci - rj), 0.0)
                    m = (cb * dec).astype(BF16)
                    own = (lane_t >= SSD_HEAD_DIM) if half else (lane_t < SSD_HEAD_DIM)
                    y_h = jnp.dot(m, jnp.where(own, x_pair, 0.0).astype(BF16), preferred_element_type=F32)
                    acc = y_h if acc is None else acc + y_h
                parts.append(acc)
            y_diag = jnp.concatenate(parts, axis=1)
            y_groups.append(y_diag + y_off * exa_e[:, g * gw:(g + 1) * gw])
            upd = lax.dot_general(xw_t[:, g * gw:(g + 1) * gw], bm_t, TN, preferred_element_type=F32)
            h_scr[g * gw:(g + 1) * gw, :] = h_g * dcol[g * gw:(g + 1) * gw, :] + upd

        y = (jnp.concatenate(y_groups, axis=1) + xs * dsk_ref[...]) * _silu(z)
        y = jnp.concatenate([_rms(y[:, g * gw:(g + 1) * gw]) for g in range(SSD_GROUPS)], axis=1)
        y_ssd = y * gssd_ref[...]

        v = scc * sch
        extv_scr[SUBLANES:SUBLANES + q, :] = v
        v_taps = taps(v, extv_scr, SC_CONV_W)
        cv =scw_ref[0:1, :] * v_taps[0]
        for k in range(1, SC_CONV_W):
            cv = cv + scw_ref[k:k + 1, :] * v_taps[k]
        t = scb * cv
        sc_per_group = t.shape[1] // SC_GROUPS
        gsum = jnp.dot((t * t).astype(BF16), et_ref[...].astype(BF16), preferred_element_type=F32)
        rs = lax.rsqrt(gsum * (1.0 / sc_per_group) + EPS)
        y_sc = t * expand(rs) * gsc_ref[...]

        out_rows = slice(j * rows_in, (j + 1) * rows_in)
        ymix_ref[out_rows, :ssd_w] = y_ssd[0:rows_in].astype(ymix_ref.dtype)
        ymix_ref[out_rows, ssd_w:] = y_sc[0:rows_in].astype(ymix_ref.dtype)

        if j == n_sub - 1:
            @pl.when(c == nc - 1)
            def _():
                ssm_ref[0] = h_scr[...]
                cst_ref[0] = ext_scr[SUBLANES + q_valid - (SSD_CONV_W - 1):SUBLANES + q_valid, :]
                scst_ref[0] = extv_scr[SUBLANES + q_valid - (SC_CONV_W - 1):SUBLANES + q_valid, :]

        ext_scr[0:SUBLANES, :] = ext_scr[q:q + SUBLANES, :]
        extv_scr[0:SUBLANES, :] = extv_scr[q:q + SUBLANES, :]

    for j in range(n_sub):
        scan_chunk(j, *(p[j * q:(j + 1) * q] for p in (z, xbc, scb, scc, sch, dt_raw)))


def _ssd(proj, init, consts, nb, n_chunks, q, rows_in, q_valid, ymix_dtype, widths=None, n_sub=1):
    heads = consts["heads"]
    tril = jnp.tril(jnp.ones((q, LANES), F32))
    ssd_w, conv_dim, sc_w = consts["g_ssd"].shape[1], consts["conv_w"].shape[1], consts["g_sc"].shape[1]
    has_init = init is not None
    assert n_chunks % n_sub == 0 and (n_sub == 1 or rows_in == q)
    nc = n_chunks // n_sub
    row_spec = lambda w: pl.BlockSpec((n_sub * rows_in, w), lambda b, c: (b * nc + c, 0))
    if widths is None:
        in_specs = [row_spec(ssd_w), row_spec(conv_dim), row_spec(sc_w), row_spec(sc_w), row_spec(sc_w),
                    row_spec(LANES)]
    else:
        x, mod, g, w = proj
        in_specs = [row_spec(x.shape[1]), _const_spec(mod.shape), _const_spec(g.shape),
                    _const_spec(w.shape, single=True)]
    args = list(proj)
    if has_init:
        ssm0, cst0, scst0 = init
        in_specs += [pl.BlockSpec((1,) + ssm0.shape[1:], lambda b, c: (b, 0, 0)),
                     pl.BlockSpec((1,) + cst0.shape[1:], lambda b, c: (b, 0, 0)),
                     pl.BlockSpec((1,) + scst0.shape[1:], lambda b, c: (b, 0, 0))]
        args += [ssm0, cst0, scst0]
    weights = [consts[k] for k in ("conv_w", "conv_b", "dt_bias", "a_log", "d_skip_e", "g_ssd",
                                   "sc_w", "g_sc")] + [tril, consts["e"], consts["et"]]
    in_specs += [_const_spec(w.shape) for w in weights]
    args += weights
    n_state = D_STATE
    out_shape = [jax.ShapeDtypeStruct((nb * n_chunks * rows_in, ssd_w + sc_w), ymix_dtype),
                 jax.ShapeDtypeStruct((nb, ssd_w, n_state), F32),
                 jax.ShapeDtypeStruct((nb, SSD_CONV_W - 1, conv_dim), F32),
                 jax.ShapeDtypeStruct((nb, SC_CONV_W - 1, sc_w), F32)]
    out_specs = [pl.BlockSpec((n_sub * rows_in, ssd_w + sc_w), lambda b, c: (b * nc + c, 0)),
                 pl.BlockSpec((1, ssd_w, n_state), lambda b, c: (b, 0, 0)),
                 pl.BlockSpec((1, SSD_CONV_W - 1, conv_dim), lambda b, c: (b, 0, 0)),
                 pl.BlockSpec((1, SC_CONV_W - 1, sc_w), lambda b, c: (b, 0, 0))]
    scratch = [pltpu.VMEM((ssd_w, n_state), F32),
               pltpu.VMEM((q + SUBLANES, conv_dim), F32),
               pltpu.VMEM((q + SUBLANES, sc_w), F32)]
    return pl.pallas_call(
        functools.partial(_ssd_kernel, q=q, rows_in=rows_in, q_valid=q_valid, has_init=has_init, heads=heads,
                          widths=widths, n_sub=n_sub),
        out_shape=out_shape, grid=(nb, nc), in_specs=in_specs, out_specs=out_specs,
        scratch_shapes=scratch,
        compiler_params=_params(("arbitrary", "arbitrary")),
        name="ssd",
    )(*args)


def _out_kernel(ymix_ref, x_ref, mod_ref, wout_ref, gpost_ref, gpre_ref, wr_hi_ref, wr_lo_ref,
                *rest, per_row, rows_per_batch, has_alias):
    x1_ref, u2_ref, lg_ref = rest[3:6] if has_alias else rest[0:3]
    tile, d = x_ref.shape
    gate1, shift2, scale2 = _mod_rows(mod_ref, per_row, rows_per_batch, tile, d)
    m = jnp.dot(ymix_ref[...].astype(BF16), wout_ref[...], preferred_element_type=F32)
    x1 = x_ref[...] + gate1 * (_rms(m) * gpost_ref[...])
    u2 = (_rms(x1) * gpre_ref[...]) * (1.0 + scale2) + shift2
    x1_ref[...] = x1
    u2_ref[...] = u2
    u_hi = u2.astype(BF16)
    u_lo = (u2 - u_hi.astype(F32)).astype(BF16)
    lg = lax.dot_general(wr_hi_ref[...], u_hi, NT, preferred_element_type=F32)
    lg = lg + lax.dot_general(wr_hi_ref[...], u_lo, NT, preferred_element_type=F32)
    lg = lg + lax.dot_general(wr_lo_ref[...], u_hi, NT, preferred_element_type=F32)
    lg_ref[...] = lg


def _out_proj(ymix, x, mod, consts, per_row, rows_per_batch, tile, t_total, tile_off, prev):
    t, d = x.shape
    dm = ymix.shape[1]
    ne = consts["wr_hi_t"].shape[0]
    in_specs = [pl.BlockSpec((tile, dm), lambda i: (i, 0)),
                pl.BlockSpec((tile, d), lambda i: (i, 0)), _mod_spec(mod, per_row, tile),
                _const_spec((dm, d)), _const_spec((1, d)), _const_spec((1, d)),
                _const_spec((ne, d)), _const_spec((ne, d))]
    args = [ymix, x, mod, consts["w_out"], consts["g_post_mix"], consts["g_pre_ffn"],
            consts["wr_hi_t"], consts["wr_lo_t"]]
    aliases = {}
    if prev is not None:
        in_specs += [pl.BlockSpec(memory_space=pl.ANY)] * 3
        aliases = {len(args) + k: k for k in range(3)}
        args += list(prev)
    return pl.pallas_call(
        functools.partial(_out_kernel, per_row=per_row, rows_per_batch=rows_per_batch,
                          has_alias=prev is not None),
        out_shape=[jax.ShapeDtypeStruct((t_total, d), F32), jax.ShapeDtypeStruct((t_total, d), F32),
                   jax.ShapeDtypeStruct((ne, t_total), F32)],
        grid=(t // tile,), in_specs=in_specs,
        out_specs=[pl.BlockSpec((tile, d), lambda i: (i + tile_off, 0)),
                   pl.BlockSpec((tile, d), lambda i: (i + tile_off, 0)),
                   pl.BlockSpec((ne, tile), lambda i: (0, i + tile_off))],
        input_output_aliases=aliases,
        compiler_params=_params(("arbitrary",)),
        name="out_proj",
    )(*args)


def _route_kernel(lg_ref, bias_ref, upper_ref, idx_ref, rank_ref, wtok_ref, cnt_ref, carry_scr):
    i = pl.program_id(0)
    ne, tm = lg_ref.shape
    per_group = ne // N_EXPERT_GROUPS
    neg = -jnp.inf

    @pl.when(i == 0)
    def _():
        carry_scr[...] = jnp.zeros(carry_scr.shape, F32)

    s = _sigmoid(lg_ref[...])
    biased = s + bias_ref[...]
    gl = []
    io_g = lax.broadcasted_iota(I32, (per_group, tm), 0).astype(F32)
    for g in range(N_EXPERT_GROUPS):
        blk = biased[g * per_group:(g + 1) * per_group, :]
        m1 = jnp.max(blk, axis=0, keepdims=True)
        f1 = jnp.min(jnp.where(blk == m1, io_g, float(per_group)), axis=0, keepdims=True)
        m2 = jnp.max(jnp.where(io_g == f1, neg, blk), axis=0, keepdims=True)
        gl.append(m1 + m2)
    gscore = jnp.concatenate(gl, axis=0)
    io8 = lax.broadcasted_iota(I32, (N_EXPERT_GROUPS, tm), 0).astype(F32)
    gsel = jnp.zeros((N_EXPERT_GROUPS, tm), F32)
    for _ in range(TOPK_GROUPS):
        m = jnp.max(gscore, axis=0, keepdims=True)
        f = jnp.min(jnp.where(gscore == m, io8, float(N_EXPERT_GROUPS)), axis=0, keepdims=True)
        hit = io8 == f
        gsel = jnp.where(hit, 1.0, gsel)
        gscore = jnp.where(hit, neg, gscore)
    emask = jnp.concatenate(
        [jnp.broadcast_to(gsel[g:g + 1, :], (per_group, tm)) for g in range(N_EXPERT_GROUPS)], axis=0)
    cand = jnp.where(emask > 0.5, biased, neg)
    io_e = lax.broadcasted_iota(I32, (ne, tm), 0).astype(F32)
    msel = jnp.zeros((ne, tm), F32)
    idxs, wts = [], []
    for _ in range(TOP_K):
        m = jnp.max(cand, axis=0, keepdims=True)
        f = jnp.min(jnp.where(cand == m, io_e, float(ne)), axis=0, keepdims=True)
        hit = io_e == f
        wts.append(jnp.sum(jnp.where(hit, s, 0.0), axis=0, keepdims=True))
        idxs.append(f)
        msel = jnp.where(hit, 1.0, msel)
        cand = jnp.where(hit, neg, cand)
    pref = jnp.dot(msel.astype(BF16), upper_ref[...], preferred_element_type=F32) + carry_scr[:, 0:1]
    ranks = [jnp.sum(jnp.where(io_e == f, pref, 0.0), axis=0, keepdims=True) for f in idxs]
    carry_scr[...] = carry_scr[...] + jnp.sum(msel, axis=1, keepdims=True)
    cnt_ref[...] = carry_scr[...].astype(I32)
    idx_ref[...] = jnp.concatenate(idxs, axis=0).astype(I32)
    rank_ref[...] = jnp.concatenate(ranks, axis=0).astype(I32)
    wsum = wts[0]
    for w in wts[1:]:
        wsum = wsum + w
    wn = jnp.concatenate([w / wsum * ROUTED_SCALE for w in wts]
                         + [jnp.zeros((LANES - TOP_K, tm), F32)], axis=0)
    for j in range(tm // LANES):
        wtok_ref[j * LANES:(j + 1) * LANES, :] = wn[:, j * LANES:(j + 1) * LANES].T


def _route(logits_t, bias, tile):
    ne, t = logits_t.shape
    upper = jnp.triu(jnp.ones((tile, tile), F32), 1).astype(BF16)
    return pl.pallas_call(
        _route_kernel,
        out_shape=[jax.ShapeDtypeStruct((TOP_K, t), I32), jax.ShapeDtypeStruct((TOP_K, t), I32),
                   jax.ShapeDtypeStruct((t, LANES), F32), jax.ShapeDtypeStruct((ne, LANES), I32)],
        grid=(t // tile,),
        in_specs=[pl.BlockSpec((ne, tile), lambda i: (0, i)), _const_spec((ne, 1)),
                  _const_spec((tile, tile))],
        out_specs=[pl.BlockSpec((TOP_K, tile), lambda i: (0, i)),
                   pl.BlockSpec((TOP_K, tile), lambda i: (0, i)),
                   pl.BlockSpec((tile, LANES), lambda i: (i, 0)),
                   _const_spec((ne, LANES))],
        scratch_shapes=[pltpu.VMEM((ne, LANES), F32)],
        compiler_params=_params(("arbitrary",)),
        name="route",
    )(logits_t, bias.reshape(ne, 1), upper)


META_POS, META_W, META_TOK = 0, TOP_K, 2 * TOP_K


def _pos_kernel(idx_ref, rank_ref, start_ref, wtok_ref, pos_ref, meta_ref):
    ne = start_ref.shape[0]
    tm = idx_ref.shape[1]
    io_e = lax.broadcasted_iota(I32, (ne, tm), 0)
    start = start_ref[...].astype(F32)
    rows = []
    for k in range(TOP_K):
        hit = io_e == idx_ref[k:k + 1, :]
        rows.append(jnp.sum(jnp.where(hit, start, 0.0), axis=0, keepdims=True))
    pos_f = jnp.concatenate(rows, axis=0) + rank_ref[...].astype(F32)
    pos_ref[...] = pos_f.astype(I32)
    pos_pad = jnp.concatenate([pos_f, jnp.zeros((LANES - TOP_K, tm), F32)], axis=0)
    lane = lax.broadcasted_iota(I32, (LANES, LANES), 1)
    row = lax.broadcasted_iota(I32, (LANES, LANES), 0)
    for j in range(tm // LANES):
        pos_tok = pos_pad[:, j * LANES:(j + 1) * LANES].T
        w_tok = pltpu.roll(wtok_ref[j * LANES:(j + 1) * LANES, :], META_W, axis=1)
        tok = (row + (pl.program_id(0) * tm + j * LANES)).astype(F32)
        tag = jnp.where(lane < META_W, pos_tok, jnp.where(lane == META_TOK, tok, w_tok))
        meta_ref[j * LANES:(j + 1) * LANES, :] = tag


def _positions(idx_t, rank_t, pad_start, wtok, tile):
    k, t = idx_t.shape
    ne = pad_start.shape[0]
    return pl.pallas_call(
        _pos_kernel,
        out_shape=[jax.ShapeDtypeStruct((k, t), I32), jax.ShapeDtypeStruct((t, LANES), F32)],
        grid=(t // tile,),
        in_specs=[pl.BlockSpec((k, tile), lambda i: (0, i)), pl.BlockSpec((k, tile), lambda i: (0, i)),
                  _const_spec((ne, 1)), pl.BlockSpec((tile, LANES), lambda i: (i, 0))],
        out_specs=[pl.BlockSpec((k, tile), lambda i: (0, i)), pl.BlockSpec((tile, LANES), lambda i: (i, 0))],
        compiler_params=_params(("arbitrary",)),
        name="positions",
    )(idx_t, rank_t, pad_start.reshape(ne, 1), wtok)


ZERO_BITS = (64, 32, 16, 8, 4, 2, 1)
assert sum(ZERO_BITS) * SUBLANES >= ROW_TILE


def _zero_copy(zbuf, xs_ref, sem, off, bit):
    rows = bit * SUBLANES
    return pltpu.make_async_copy(zbuf.at[pl.ds(0, rows)], xs_ref.at[pl.ds(pl.multiple_of(off, SUBLANES), rows)], sem)


def _zero_kernel(first_ref, units_ref, xs_ref, zbuf, sem):
    zbuf[...] = jnp.zeros(zbuf.shape, zbuf.dtype)
    ne = first_ref.shape[0]

    def each(e, start):
        off = first_ref[e]
        units = units_ref[e]
        for bit in ZERO_BITS:
            on = (units & bit) != 0

            @pl.when(on)
            def _():
                cp = _zero_copy(zbuf, xs_ref, sem, off, bit)
                cp.start() if start else cp.wait()
            off = off + jnp.where(on, bit * SUBLANES, 0)
        return start

    lax.fori_loop(0, ne, lambda e, c: (each(e, True), c)[1], 0)
    lax.fori_loop(0, ne, lambda e, c: (each(e, False), c)[1], 0)


def _zero_rows(first, units, n_rows, width):
    return pl.pallas_call(
        _zero_kernel,
        out_shape=jax.ShapeDtypeStruct((n_rows, width), F32),
        in_specs=[pl.BlockSpec(memory_space=pltpu.SMEM), pl.BlockSpec(memory_space=pltpu.SMEM)],
        out_specs=pl.BlockSpec(memory_space=pl.ANY),
        scratch_shapes=[pltpu.VMEM((ZERO_BITS[0] * SUBLANES, width), F32), pltpu.SemaphoreType.DMA],
        compiler_params=_params(),
        name="zero_rows",
    )(first, units)


def _row_copy(src_ref, dst_ref, sem, s, d):
    return pltpu.make_async_copy(src_ref.at[pl.ds(s, 1)], dst_ref.at[pl.ds(d, 1)], sem)


def _dispatch_kernel(pos_hbm, u_ref, meta_ref, xs_ref, pos_a, pos_b, src, psem, sem, *, nt):
    i = pl.program_id(0)
    tm, d = u_ref.shape
    per = pos_a.shape[0]
    pos_slots = (pos_a, pos_b)

    def pos_copy(tile, s):
        return pltpu.make_async_copy(pos_hbm.at[pl.ds(pl.multiple_of(tile * per, per), per)], pos_slots[s],
                                     psem.at[s])

    def drain(s):
        for _ in range(TOP_K):
            pltpu.make_async_copy(src.at[s], xs_ref.at[pl.ds(0, tm)], sem.at[s]).wait()

    def step(s):
        @pl.when(i == 0)
        def _():
            pos_copy(0, 0).start()

        @pl.when(i + 1 < nt)
        def _():
            pos_copy(i + 1, 1 - s).start()

        @pl.when(i >= 2)
        def _():
            drain(s)

        src[s, :, :d] = u_ref[...]
        src[s, :, d:] = meta_ref[...]
        pos_copy(i, s).wait()

        def issue(t, carry):
            for k in range(TOP_K):
                _row_copy(src.at[s], xs_ref, sem.at[s], t, pos_slots[s][k * tm + t]).start(
                    priority=k % DMA_THREADS)
            return carry

        lax.fori_loop(0, tm, issue, 0)

        @pl.when(i == nt - 1)
        def _():
            if nt >= 2:
                drain(1 - s)
            drain(s)

    for s in range(2):
        pl.when(i % 2 == s)(functools.partial(step, s))


def _dispatch(pos_flat, u2, meta, n_rows, tm):
    per = TOP_K * tm
    nt = pos_flat.shape[0] // per
    d = u2.shape[1]
    return pl.pallas_call(
        functools.partial(_dispatch_kernel, nt=nt),
        out_shape=jax.ShapeDtypeStruct((n_rows, d + LANES), F32),
        grid=(nt,),
        in_specs=[pl.BlockSpec(memory_space=pl.ANY), pl.BlockSpec((tm, d), lambda i: (i, 0)),
                  pl.BlockSpec((tm, LANES), lambda i: (i, 0))],
        out_specs=pl.BlockSpec(memory_space=pl.ANY),
        scratch_shapes=[pltpu.SMEM((per,), I32), pltpu.SMEM((per,), I32),
                        pltpu.VMEM((2, tm, d + LANES), F32),
                        pltpu.SemaphoreType.DMA((2,)), pltpu.SemaphoreType.DMA((2,))],
        compiler_params=_params(("arbitrary",)),
        name="dispatch",
    )(pos_flat, u2, meta)


WAIT_BITS = tuple(1 << b for b in reversed(range(ROW_TILE.bit_length())))


def _expert_kernel(te_ref, nu_ref, nv_ref, xs_ref, wg_ref, wu_ref, wd_ref, g_hbm,
                   ybuf, dst_vmem, dst_smem, sem, dsem, *, n_tok):
    del te_ref
    i = pl.program_id(0)
    nu = nu_ref[0]
    d = ybuf.shape[3]
    rows = ybuf.shape[1] * SUBLANES

    def drain(tile):
        slot = tile % 2
        n = nv_ref[tile]
        for bit in WAIT_BITS:
            @pl.when((n & bit) != 0)
            def _():
                pltpu.make_async_copy(g_hbm.at[pl.ds(0, bit)], g_hbm.at[pl.ds(0, bit)], sem.at[slot]).wait()

    @pl.when(i < nu)
    def _():
        slot = i % 2

        @pl.when(i >= 2)
        def _():
            drain(i - 2)

        dst_vmem[...] = jnp.zeros(dst_vmem.shape, I32)
        xfull = xs_ref[...]
        tag = xfull[:, d:]
        x = xfull[:, :d].astype(BF16)
        lane = lax.broadcasted_iota(I32, (rows, LANES), 1)
        p = (lax.broadcasted_iota(I32, (rows, LANES), 0) + i * rows).astype(F32)
        hit = jnp.logical_and(tag == p, lane < META_W)
        w_at = pltpu.roll(tag, LANES - META_W, axis=1)
        w = jnp.sum(jnp.where(hit, w_at, 0.0), axis=1, keepdims=True)
        kf = jnp.sum(jnp.where(hit, lane.astype(F32), 0.0), axis=1, keepdims=True)
        dest = kf * float(n_tok) + tag[:, META_TOK:META_TOK + 1]

        dest_b = jnp.broadcast_to(dest, (rows, LANES))
        n_blk = -(-rows // LANES)
        dest_b = jnp.concatenate([dest_b, jnp.zeros((n_blk * LANES - rows, LANES), F32)], axis=0)
        for b in range(n_blk):
            dst_vmem[b:b + 1, :] = dest_b[b * LANES:(b + 1) * LANES, :].T[0:1, :].astype(I32)
        to_smem = [pltpu.make_async_copy(dst_vmem.at[b], dst_smem.at[pl.ds(b * LANES, LANES)], dsem)
                   for b in range(n_blk)]
        for cp in to_smem:
            cp.start()

        g = jnp.dot(x, wg_ref[0].astype(BF16), preferred_element_type=F32)
        u = jnp.dot(x, wu_ref[0].astype(BF16), preferred_element_type=F32)
        h = (_silu(g) * u).astype(BF16)
        y = jnp.dot(h, wd_ref[0].astype(BF16), preferred_element_type=F32) * w
        ybuf[slot] = y.reshape(rows // SUBLANES, SUBLANES, d)
        for cp in to_smem:
            cp.wait()

        nv = nv_ref[i]
        full = lax.shift_right_logical(nv, 3)

        def scatter_rows(s):
            def send(r8, j, prio):
                to = dst_smem[r8 * SUBLANES + j]
                pltpu.make_async_copy(ybuf.at[s, r8, pl.ds(j, 1)], g_hbm.at[pl.ds(to, 1)],
                                      sem.at[s]).start(priority=prio)

            def issue8(r8, carry):
                for j in range(SUBLANES):
                    send(r8, j, SCATTER_PRIORITY)
                return carry

            def issue1(r, carry):
                send(lax.shift_right_logical(r, 3), r & (SUBLANES - 1), SCATTER_PRIORITY)
                return carry

            lax.fori_loop(0, full, issue8, 0)
            lax.fori_loop(full * SUBLANES, nv, issue1, 0)

        for s in range(2):
            pl.when(slot == s)(functools.partial(scatter_rows, s))

    @pl.when(i == pl.num_programs(0) - 1)
    def _():
        @pl.when(nu >= 2)
        def _():
            drain(nu - 2)

        @pl.when(nu >= 1)
        def _():
            drain(nu - 1)


def _experts(tile_expert, n_used, tile_valid, xs, w_gate, w_up, w_down, n_tok):
    n_rows, width = xs.shape
    d, de = w_gate.shape[1:]
    n_tiles = n_rows // ROW_TILE
    row_map = lambda i, te, nu, nv: (jnp.minimum(i, nu[0] - 1), 0)
    w_map = lambda i, te, nu, nv: (te[i], 0, 0)
    n_blk = -(-ROW_TILE // LANES)
    return pl.pallas_call(
        functools.partial(_expert_kernel, n_tok=n_tok),
        out_shape=jax.ShapeDtypeStruct((TOP_K * n_tok, d), F32),
        grid_spec=pltpu.PrefetchScalarGridSpec(
            num_scalar_prefetch=3, grid=(n_tiles,),
            in_specs=[pl.BlockSpec((ROW_TILE, width), row_map),
                      pl.BlockSpec((1, d, de), w_map), pl.BlockSpec((1, d, de), w_map),
                      pl.BlockSpec((1, de, d), w_map)],
            out_specs=pl.BlockSpec(memory_space=pl.ANY),
            scratch_shapes=[pltpu.VMEM((2, ROW_TILE // SUBLANES, SUBLANES, d), F32),
                            pltpu.VMEM((SUBLANES, LANES), I32),
                            pltpu.SMEM((n_blk * LANES,), I32), pltpu.SemaphoreType.DMA((2,)),
                            pltpu.SemaphoreType.DMA]),
        compiler_params=_params(("arbitrary",)),
        name="experts",
    )(tile_expert, n_used, tile_valid, xs, w_gate, w_up, w_down)


def _combine_kernel(g_ref, u_ref, x1_ref, mod_ref, wsg_ref, wsu_ref, wsd_ref, gpost_ref, o_ref,
                    *, per_row, rows_per_batch):
    tm, d = u_ref.shape
    ub = u_ref[...].astype(BF16)
    hs = _silu(jnp.dot(ub, wsg_ref[...], preferred_element_type=F32)) * jnp.dot(
        ub, wsu_ref[...], preferred_element_type=F32)
    f = jnp.dot(hs.astype(BF16), wsd_ref[...], preferred_element_type=F32)
    routed = g_ref[0]
    for k in range(1, TOP_K):
        routed = routed + g_ref[k]
    f = routed + f
    (gate2,) = _mod_rows(mod_ref, per_row, rows_per_batch, tm, d)
    o_ref[...] = x1_ref[...] + gate2 * (_rms(f) * gpost_ref[...])


def _combine(g_rows, u2, x1, mod, consts, per_row, rows_per_batch, tile_off, n_tok, tm):
    t_all, d = u2.shape
    ds_ = consts["w_sh_gate"].shape[1]
    tok_spec = lambda w: pl.BlockSpec((tm, w), lambda i: (i + tile_off, 0))
    return pl.pallas_call(
        functools.partial(_combine_kernel, per_row=per_row, rows_per_batch=rows_per_batch),
        out_shape=jax.ShapeDtypeStruct((n_tok, d), F32),
        grid=(n_tok // tm,),
        in_specs=[pl.BlockSpec((TOP_K, tm, d), lambda i: (0, i + tile_off, 0)),
                  tok_spec(d), tok_spec(d), _mod_spec(mod, per_row, tm),
                  _const_spec((d, ds_)), _const_spec((d, ds_)), _const_spec((ds_, d)), _const_spec((1, d))],
        out_specs=pl.BlockSpec((tm, d), lambda i: (i, 0)),
        compiler_params=_params(("arbitrary",)),
        name="combine",
    )(g_rows.reshape(TOP_K, t_all, d), u2, x1, mod, consts["w_sh_gate"], consts["w_sh_up"],
      consts["w_sh_down"], consts["g_post_ffn"])


def _repeat_rows(x, n):
    r, c = x.shape
    return jnp.broadcast_to(x[:, None, :], (r, n, c)).reshape(r * n, c)


def _tile_major(pos_t, tile):
    k, t = pos_t.shape
    return pos_t.reshape(k, t // tile, tile).transpose(1, 0, 2).reshape(-1)


def kernel(x_prompt, x_sample, c_prompt, c_sample, state_ssm, state_ssd_conv, state_short_conv, w_ada, b_ada, g_pre_mix, g_post_mix, g_pre_ffn, g_post_ffn, w_in, ssd_conv_w, ssd_conv_b, dt_bias, a_log, d_skip, g_ssd_norm, sc_conv_w, g_sc_norm, w_out, w_router, router_bias, w_exp_gate, w_exp_up, w_exp_down, w_sh_gate, w_sh_up, w_sh_down):
    depth = w_ada.shape[0]
    bp, seq, d = x_prompt.shape
    bs, dseq, _ = x_sample.shape
    heads = dt_bias.shape[1]
    ssd_w = heads * SSD_HEAD_DIM
    conv_dim = ssd_conv_w.shape[2]
    sc_w = sc_conv_w.shape[2]
    ne = w_router.shape[2]
    tp, ts = bp * seq, bs * dseq
    t_all = tp + ts

    assert sc_w // SC_GROUPS == SSD_HEAD_DIM and heads == SC_GROUPS
    head_of = jnp.arange(ssd_w, dtype=I32) // SSD_HEAD_DIM
    e_ind = (jnp.arange(LANES, dtype=I32)[:, None] == head_of[None, :])

    xp = x_prompt.reshape(tp, d)
    xs_pad = jnp.pad(x_sample, ((0, 0), (0, SUBLANES - dseq), (0, 0))).reshape(bs * SUBLANES, d)
    xs_tok = x_sample.reshape(ts, d)
    outs = {k: [] for k in ("ssm_p", "cst_p", "scst_p", "ssm_s", "cst_s", "scst_s")}

    for layer in range(depth):
        cuts = np.cumsum([0, ssd_w, conv_dim, heads, sc_w, sc_w, sc_w]).tolist()
        wi = w_in[layer]
        seg = lambda k: wi[:, cuts[k]:cuts[k + 1]]
        w_in_r = jnp.concatenate([seg(0), seg(1), seg(3), seg(4), seg(5),
                                  jnp.pad(seg(2), ((0, 0), (0, LANES - heads)))], axis=1).astype(BF16)
        widths = (ssd_w, conv_dim, sc_w, sc_w, sc_w, LANES)
        pad_h = lambda v: jnp.pad(v.reshape(1, heads), ((0, 0), (0, LANES - heads)))
        wr = w_router[layer].T
        wr_hi = wr.astype(BF16)
        consts = dict(
            heads=heads,
            conv_w=ssd_conv_w[layer], conv_b=ssd_conv_b[layer].reshape(1, conv_dim),
            dt_bias=pad_h(dt_bias[layer]), a_log=pad_h(a_log[layer]),
            d_skip_e=jnp.broadcast_to(d_skip[layer][:, None], (heads, SSD_HEAD_DIM)).reshape(1, ssd_w),
            g_ssd=g_ssd_norm[layer].reshape(1, ssd_w), sc_w=sc_conv_w[layer],
            g_sc=g_sc_norm[layer].reshape(1, sc_w),
            e=e_ind.astype(BF16), et=e_ind.T.astype(F32),
            w_out=w_out[layer].astype(BF16), g_post_mix=g_post_mix[layer].reshape(1, d),
            g_pre_ffn=g_pre_ffn[layer].reshape(1, d),
            wr_hi_t=wr_hi, wr_lo_t=(wr - wr_hi.astype(F32)).astype(BF16),
            w_sh_gate=w_sh_gate[layer].astype(BF16), w_sh_up=w_sh_up[layer].astype(BF16),
            w_sh_down=w_sh_down[layer].astype(BF16), g_post_ffn=g_post_ffn[layer].reshape(1, d))

        c_all = jnp.concatenate([c_prompt, c_sample], axis=0)
        m_rows = -(-c_all.shape[0] // 16) * 16
        mod = _ada(jnp.pad(c_all, ((0, m_rows - c_all.shape[0]), (0, 0))), w_ada[layer], b_ada[layer])
        mod_p = mod[:bp]
        mod_s = mod[bp:bp + bs]
        mod_s_pad = _repeat_rows(mod_s[:, :2 * d], SUBLANES)
        mod_s_tok = _repeat_rows(mod_s, dseq)

        proj_s = _in_proj(xs_pad, mod_s_pad, g_pre_mix[layer], w_in_r, widths, F32, True, 1, TOK_TILE)
        ymix_p, ssm_p, cst_p, scst_p = _ssd(
            (xp, mod_p[:, :2 * d], g_pre_mix[layer].reshape(1, d), w_in_r), None, consts, bp, seq // CHUNK,
            CHUNK, CHUNK, CHUNK, BF16, widths=widths, n_sub=SCAN_SUB)
        init = (state_ssm[layer].reshape(bs, ssd_w, D_STATE), state_ssd_conv[layer], state_short_conv[layer])
        ymix_s, ssm_s, cst_s, scst_s = _ssd(proj_s, init, consts, bs, 1, SAMPLE_CHUNK, SUBLANES, dseq, F32)
        ymix_s = ymix_s.reshape(bs, SUBLANES, ssd_w + sc_w)[:, :dseq].reshape(ts, ssd_w + sc_w)

        merged = _out_proj(ymix_p, xp, mod_p[:, 2 * d:5 * d], consts, False, seq, TOK_TILE, t_all, 0, None)
        x1, u2, logits_t = _out_proj(ymix_s, xs_tok, mod_s_tok[:, 2 * d:5 * d], consts, True, 1, TOK_TILE,
                                     t_all, tp // TOK_TILE, merged)

        idx_t, rank_t, wtok, counts = _route(logits_t, router_bias[layer], TOK_TILE)
        counts = counts[:, 0]
        padded = (counts + ROW_TILE - 1) // ROW_TILE * ROW_TILE
        pad_end = jnp.cumsum(padded)
        pad_start = pad_end - padded
        n_tiles = -(-(t_all * TOP_K) // ROW_TILE) + ne
        n_used = (pad_end[-1] // ROW_TILE).astype(I32)
        tile_ids = jnp.minimum(jnp.arange(n_tiles, dtype=I32), n_used - 1)
        tile_expert = jnp.minimum(jnp.sum(pad_end[None, :] <= (tile_ids * ROW_TILE)[:, None], axis=1), ne - 1).astype(I32)
        first_pad = (pad_start + counts) // SUBLANES * SUBLANES
        units = (pad_end - first_pad) // SUBLANES
        tile_valid = jnp.clip(counts[tile_expert] - (tile_ids * ROW_TILE - pad_start[tile_expert]), 0,
                              ROW_TILE).astype(I32)
        pos_t, meta = _positions(idx_t, rank_t, pad_start.astype(I32), wtok, TOK_TILE)
        pos_flat = _tile_major(pos_t, GATHER_TILE)

        xs_rows = _dispatch(pos_flat, u2, meta, n_tiles * ROW_TILE, GATHER_TILE)
        g_rows = _experts(tile_expert, n_used.reshape(1), tile_valid, xs_rows, w_exp_gate[layer],
                          w_exp_up[layer], w_exp_down[layer], t_all)
        xp = _combine(g_rows, u2, x1, mod_p[:, 5 * d:], consts, False, seq, 0, tp, COMBINE_TILE)
        xs_tok = _combine(g_rows, u2, x1, mod_s_tok[:, 5 * d:], consts, True, 1, tp // COMBINE_TILE, ts,
                          COMBINE_TILE)
        xs_pad = jnp.pad(xs_tok.reshape(bs, dseq, d), ((0, 0), (0, SUBLANES - dseq), (0, 0))).reshape(
            bs * SUBLANES, d)

        outs["ssm_p"].append(ssm_p.reshape(bp, heads, SSD_HEAD_DIM, D_STATE))
        outs["cst_p"].append(cst_p)
        outs["scst_p"].append(scst_p)
        outs["ssm_s"].append(ssm_s.reshape(bs, heads, SSD_HEAD_DIM, D_STATE))
        outs["cst_s"].append(cst_s)
        outs["scst_s"].append(scst_s)

    return (xp.reshape(bp, seq, d), xs_tok.reshape(bs, dseq, d),
            jnp.stack(outs["ssm_p"]), jnp.stack(outs["cst_p"]), jnp.stack(outs["scst_p"]),
            jnp.stack(outs["ssm_s"]), jnp.stack(outs["cst_s"]), jnp.stack(outs["scst_s"]))
```
